```python
import math
import jax, jax.numpy as jnp
from jax import lax
import numpy as np

D_MODEL = 2048
BATCH = 2
SEQ = 8192
DEPTH = 4

EPS = 1e-6
CONV_K = 4
D_FF = ((8 * D_MODEL // 3 + 127) // 128) * 128

GDN_HEADS = 4
GDN_DK = 128
GDN_DV = 128
GDN_CHUNK = 64
GDN_CONV_CH = GDN_HEADS * (2 * GDN_DK + GDN_DV)

NSA_HEADS = 4
NSA_KV_GROUPS = 2
NSA_DK = 128
NSA_DV = 128
CMP_BLOCK = 32
CMP_STRIDE = 16
SEL_BLOCK = 64
N_SELECT = 16
WINDOW = 512
Q_BLOCK = 128
SEL_OVERLAP_WEIGHTS = (1.0, 2.0, 2.0, 2.0, 1.0)
FORCE_SCORE = 1e4
NEG = -1e30

SSM_HEADS = 16
SSM_HEAD_DIM = 64
SSM_GROUPS = 2
SSM_STATE = 128
SSM_CHUNK = 128
SSM_INNER = SSM_HEADS * SSM_HEAD_DIM
SSM_CONV_CH = SSM_INNER + 2 * SSM_GROUPS * SSM_STATE

REL_BUCKETS = 32
REL_MAX_EXACT = 16
REL_MAX_DIST = 1024

IN_SIZES = (GDN_HEADS * GDN_DK, GDN_HEADS * GDN_DK, GDN_HEADS * GDN_DV, GDN_HEADS * GDN_DV, GDN_HEADS, GDN_HEADS,
            NSA_HEADS * NSA_DK, NSA_KV_GROUPS * NSA_DK, NSA_KV_GROUPS * NSA_DV, NSA_KV_GROUPS * NSA_DK,
            NSA_KV_GROUPS * NSA_DV, NSA_KV_GROUPS * NSA_DK, NSA_KV_GROUPS * NSA_DV, 3 * NSA_HEADS,
            SSM_INNER, SSM_CONV_CH, SSM_HEADS,
            3 * D_MODEL)
N_IN = sum(IN_SIZES)

kernel_name = "hybrid_gdn_nsa_ssd_macaron"


def rmsnorm(x, g):
    xf = x.astype(jnp.float32)
    y = xf * lax.rsqrt(jnp.mean(xf * xf, axis=-1, keepdims=True) + EPS)
    return (y * g.astype(jnp.float32)).astype(x.dtype)


def l2norm(t):
    return t * lax.rsqrt(jnp.sum(t * t, axis=-1, keepdims=True) + EPS)


def swiglu(h, w_up, w_down):
    a, b = jnp.split(h @ w_up, 2, axis=-1)
    return (jax.nn.silu(a) * b) @ w_down


def causal_dwconv(x, w):
    return lax.conv_general_dilated(x, w[:, None, :], window_strides=(1,), padding=[(CONV_K - 1, 0)],
                                    dimension_numbers=('NWC', 'WIO', 'NWC'), feature_group_count=x.shape[-1])


def rel_bucket(dist):
    n = jnp.maximum(dist, 0)
    nf = jnp.maximum(n, 1).astype(jnp.float32)
    large = REL_MAX_EXACT + (jnp.log(nf / REL_MAX_EXACT) / math.log(REL_MAX_DIST / REL_MAX_EXACT)
                             * (REL_BUCKETS - REL_MAX_EXACT)).astype(jnp.int32)
    large = jnp.minimum(large, REL_BUCKETS - 1)
    return jnp.where(n < REL_MAX_EXACT, n, large)


def gated_deltanet(q, k, v, z, a, b, conv_w, a_log, dt_bias, norm_g):
    f32 = jnp.float32
    Bsz, T, _ = q.shape
    H, C = GDN_HEADS, GDN_CHUNK
    n_c = T // C
    qk_w = H * GDN_DK
    qkv = jax.nn.silu(causal_dwconv(jnp.concatenate([q, k, v], axis=-1), conv_w)).astype(f32)

    def heads(t):
        return t.reshape(Bsz, n_c, C, H, -1).transpose(0, 3, 1, 2, 4)

    qh = l2norm(heads(qkv[..., :qk_w])) * GDN_DK ** -0.5
    kh = l2norm(heads(qkv[..., qk_w:2 * qk_w]))
    vh = heads(qkv[..., 2 * qk_w:])
    beta = jax.nn.sigmoid(b.astype(f32)).reshape(Bsz, n_c, C, H).transpose(0, 3, 1, 2)
    g = -jnp.exp(a_log.astype(f32)) * jax.nn.softplus(a.astype(f32) + dt_bias.astype(f32))
    gc = jnp.cumsum(g.reshape(Bsz, n_c, C, H).transpose(0, 3, 1, 2), axis=-1)
    causal = jnp.tril(jnp.ones((C, C), bool))
    strict = jnp.tril(jnp.ones((C, C), bool), -1)
    diff = gc[..., :, None] - gc[..., None, :]
    decay = jnp.where(causal, jnp.exp(jnp.where(causal, diff, 0.0)), 0.0)
    kb = kh * beta[..., None]
    m = jnp.where(strict, jnp.einsum('bhncd,bhnsd->bhncs', kb, kh) * decay, 0.0) + jnp.eye(C, dtype=f32)
    rhs = jnp.concatenate([vh * beta[..., None], kb * jnp.exp(gc)[..., None]], axis=-1)
    sol = lax.linalg.triangular_solve(m, rhs, left_side=True, lower=True, unit_diagonal=True)
    u, w = sol[..., :GDN_DV], sol[..., GDN_DV:]
    attn = jnp.einsum('bhncd,bhnsd->bhncs', qh, kh) * decay
    q_dec = qh * jnp.exp(gc)[..., None]
    k_dec = kh * jnp.exp(gc[..., -1:] - gc)[..., None]
    g_last = jnp.exp(gc[..., -1])

    def step(state, inp):
        u_c, w_c, a_c, qd_c, kd_c, gl_c = inp
        v_new = u_c - jnp.einsum('bhck,bhkv->bhcv', w_c, state)
        o_c = jnp.einsum('bhck,bhkv->bhcv', qd_c, state) + jnp.einsum('bhcs,bhsv->bhcv', a_c, v_new)
        state = state * gl_c[..., None, None] + jnp.einsum('bhck,bhcv->bhkv', kd_c, v_new)
        return state, o_c

    xs = (jnp.moveaxis(u, 2, 0), jnp.moveaxis(w, 2, 0), jnp.moveaxis(attn, 2, 0),
          jnp.moveaxis(q_dec, 2, 0), jnp.moveaxis(k_dec, 2, 0), jnp.moveaxis(g_last, 2, 0))
    _, o = lax.scan(step, jnp.zeros((Bsz, H, GDN_DK, GDN_DV), f32), xs)
    o = o.transpose(1, 0, 3, 2, 4).reshape(Bsz, T, H, GDN_DV)
    o = rmsnorm(o, norm_g) * jax.nn.silu(z.astype(f32).reshape(Bsz, T, H, GDN_DV))
    return o.reshape(Bsz, T, H * GDN_DV).astype(z.dtype)


def nsa_attention(q_in, kc, vc, ks, vs, kw, vw, gate_logits, q_norm, k_norm, pe_k, pe_v, w_ck, w_cv, rel_table):
    f32 = jnp.float32
    Bsz, T, _ = q_in.shape
    G, Hg = NSA_KV_GROUPS, NSA_HEADS // NSA_KV_GROUPS
    n_cmp = T // CMP_STRIDE - 1
    n_sel = T // SEL_BLOCK
    n_top = min(N_SELECT, n_sel)
    n_qb = T // Q_BLOCK
    ratio = SEL_BLOCK // CMP_STRIDE

    q = rmsnorm(q_in.reshape(Bsz, T, NSA_HEADS, NSA_DK), q_norm).astype(f32) * NSA_DK ** -0.5
    q_blocks = q.reshape(Bsz, n_qb, Q_BLOCK, G, Hg, NSA_DK).transpose(1, 0, 3, 4, 2, 5)

    def groups(t):
        return t.reshape(Bsz, T, G, -1).astype(f32)

    def compress(t, pe, w):
        c = groups(t).reshape(Bsz, T // CMP_STRIDE, CMP_STRIDE, G, -1)
        blocks = jnp.concatenate([c[:, :-1], c[:, 1:]], axis=2)
        return jnp.einsum('bnlgd,lde->bgne', blocks + pe.astype(f32)[:, None, :], w.astype(f32))

    k_cmp = rmsnorm(compress(kc, pe_k, w_ck), k_norm)
    v_cmp = compress(vc, pe_v, w_cv)
    k_sel = rmsnorm(groups(ks), k_norm).reshape(Bsz, n_sel, SEL_BLOCK, G, NSA_DK).transpose(0, 3, 1, 2, 4)
    v_sel = groups(vs).reshape(Bsz, n_sel, SEL_BLOCK, G, NSA_DV).transpose(0, 3, 1, 2, 4)
    pad = ((0, 0), (0, 0), (WINDOW, 0), (0, 0))
    k_win = jnp.pad(rmsnorm(groups(kw), k_norm).transpose(0, 2, 1, 3), pad)
    v_win = jnp.pad(groups(vw).transpose(0, 2, 1, 3), pad)

    table = rel_table.astype(f32).T.reshape(G, Hg, REL_BUCKETS)
    cmp_end = jnp.arange(n_cmp) * CMP_STRIDE + CMP_BLOCK - 1
    blk = jnp.arange(n_sel)
    bi = jnp.arange(Bsz)[:, None, None, None]
    gi = jnp.arange(G)[None, :, None, None]
    gi6 = jnp.arange(G)[None, :, None, None, None, None]
    hi6 = jnp.arange(Hg)[None, None, :, None, None, None]

    def block(args):
        qb, qi = args
        t = qi * Q_BLOCK + jnp.arange(Q_BLOCK)
        dist = t[:, None] - cmp_end[None, :]
        ok = dist >= 0
        s = jnp.einsum('bghqd,bgnd->bghqn', qb, k_cmp) + table[:, :, rel_bucket(dist)]
        p_cmp = jnp.where(ok, jax.nn.softmax(jnp.where(ok, s, NEG), axis=-1), 0.0)
        o_cmp = jnp.einsum('bghqn,bgnd->bghqd', p_cmp, v_cmp)
        imp_c = jnp.pad(p_cmp.sum(axis=2), ((0, 0), (0, 0), (0, 0), (1, ratio * n_sel - n_cmp)))
        imp = sum(wt * imp_c[..., j:j + ratio * n_sel:ratio] for j, wt in enumerate(SEL_OVERLAP_WEIGHTS))
        cur = t // SEL_BLOCK
        ok_s = blk[None, :] <= cur[:, None]
        forced = ok_s & ((blk[None, :] == 0) | (blk[None, :] >= cur[:, None] - 1))
        score = jnp.where(forced, FORCE_SCORE, jnp.where(ok_s, imp, -FORCE_SCORE))
        _, idx = lax.top_k(score, n_top)
        kg = k_sel[bi, gi, idx]
        vg = v_sel[bi, gi, idx]
        dist = t[:, None, None] - (idx[..., None] * SEL_BLOCK + jnp.arange(SEL_BLOCK))
        ok = (dist >= 0)[:, :, None]
        s = jnp.einsum('bghqd,bgqkld->bghqkl', qb, kg) + table[gi6, hi6, rel_bucket(dist)[:, :, None]]
        s = jnp.where(ok, s, NEG).reshape(Bsz, G, Hg, Q_BLOCK, n_top * SEL_BLOCK)
        p = jax.nn.softmax(s, axis=-1).reshape(Bsz, G, Hg, Q_BLOCK, n_top, SEL_BLOCK)
        o_sel = jnp.einsum('bghqkl,bgqkld->bghqd', p, vg)
        kwb = lax.dynamic_slice_in_dim(k_win, qi * Q_BLOCK, Q_BLOCK + WINDOW, axis=2)
        vwb = lax.dynamic_slice_in_dim(v_win, qi * Q_BLOCK, Q_BLOCK + WINDOW, axis=2)
        pos = qi * Q_BLOCK - WINDOW + jnp.arange(Q_BLOCK + WINDOW)
        dist = t[:, None] - pos[None, :]
        ok = (dist >= 0) & (dist < WINDOW) & (pos >= 0)[None, :]
        s = jnp.einsum('bghqd,bgkd->bghqk', qb, kwb) + table[:, :, rel_bucket(dist)]
        p = jax.nn.softmax(jnp.where(ok, s, NEG), axis=-1)
        o_win = jnp.einsum('bghqk,bgkd->bghqd', p, vwb)
        return jnp.stack([o_cmp, o_sel, o_win], axis=-2)

    out = lax.map(block, (q_blocks, jnp.arange(n_qb)))
    out = out.transpose(1, 0, 4, 2, 3, 5, 6).reshape(Bsz, T, NSA_HEADS, 3, NSA_DV)
    gates = jax.nn.sigmoid(gate_logits.astype(f32)).reshape(Bsz, T, NSA_HEADS, 3)
    o = jnp.einsum('bthr,bthrd->bthd', gates, out)
    return o.reshape(Bsz, T, NSA_HEADS * NSA_DV).astype(q_in.dtype)


def mamba2_ssd(z, xbc, dt, conv_w, conv_b, dt_bias, a_log, d_skip, norm_g):
    f32 = jnp.float32
    Bsz, T, _ = z.shape
    H, P, G, N, L = SSM_HEADS, SSM_HEAD_DIM, SSM_GROUPS, SSM_STATE, SSM_CHUNK
    n_c = T // L
    xbc = jax.nn.silu(causal_dwconv(xbc, conv_w) + conv_b).astype(f32)
    xs = xbc[..., :SSM_INNER].reshape(Bsz, n_c, L, H, P)
    bm = jnp.repeat(xbc[..., SSM_INNER:SSM_INNER + G * N].reshape(Bsz, n_c, L, G, N), H // G, axis=3)
    cm = jnp.repeat(xbc[..., SSM_INNER + G * N:].reshape(Bsz, n_c, L, G, N), H // G, axis=3)
    dt = jax.nn.softplus(dt.astype(f32) + dt_bias.astype(f32)).reshape(Bsz, n_c, L, H)
    ac = jnp.cumsum(dt * (-jnp.exp(a_log.astype(f32))), axis=2)
    xdt = xs * dt[..., None]
    ach = jnp.swapaxes(ac, 2, 3)
    causal = jnp.tril(jnp.ones((L, L), bool))
    seg = ach[..., :, None] - ach[..., None, :]
    decay = jnp.where(causal, jnp.exp(jnp.where(causal, seg, 0.0)), 0.0)
    scores = jnp.einsum('bclhn,bcshn->bchls', cm, bm) * decay
    y = jnp.einsum('bchls,bcshp->bclhp', scores, xdt)
    states = jnp.einsum('bclhn,bclhp->bchpn', bm * jnp.exp(ac[:, :, -1:, :] - ac)[..., None], xdt)
    chunk_decay = jnp.exp(ac[:, :, -1, :])

    def step(s, inp):
        st, cd = inp
        return s * cd[..., None, None] + st, s

    _, s_in = lax.scan(step, jnp.zeros((Bsz, H, P, N), f32),
                       (jnp.moveaxis(states, 1, 0), jnp.moveaxis(chunk_decay, 1, 0)))
    s_in = jnp.moveaxis(s_in, 0, 1)
    y = y + jnp.einsum('bclhn,bchpn->bclhp', cm * jnp.exp(ac)[..., None], s_in) + xs * d_skip.astype(f32)[:, None]
    y = y.reshape(Bsz, T, SSM_INNER) * jax.nn.silu(z.astype(f32))
    y = y.reshape(Bsz, T, G, SSM_INNER // G)
    y = y * lax.rsqrt(jnp.mean(y * y, axis=-1, keepdims=True) + EPS)
    return (y.reshape(Bsz, T, SSM_INNER) * norm_g.astype(f32)).astype(z.dtype)


def setup_inputs(seed: int = 0) -> dict:
    key = jax.random.key(seed)
    keys = iter(jax.random.split(key, 40))
    f32 = jnp.float32
    Ld = DEPTH

    def nrm(shape, scale):
        return scale * jax.random.normal(next(keys), shape, f32)

    def gain(shape):
        return 1.0 + 0.02 * jax.random.normal(next(keys), shape, f32)

    def a_log(n):
        return jnp.log(jax.random.uniform(next(keys), (Ld, n), f32, 1.0, 16.0))

    def dt_bias(n):
        dtv = jnp.exp(jax.random.uniform(next(keys), (Ld, n), f32, math.log(1e-3), math.log(1e-1)))
        return dtv + jnp.log(-jnp.expm1(-dtv))

    return {
        "x": nrm((BATCH, SEQ, D_MODEL), 1.0),
        "rel_table": nrm((REL_BUCKETS, NSA_HEADS), 0.5),
        "g_ffn1": gain((Ld, D_MODEL)),
        "w_up1": nrm((Ld, D_MODEL, 2 * D_FF), D_MODEL ** -0.5),
        "w_down1": nrm((Ld, D_FF, D_MODEL), D_FF ** -0.5),
        "g_mix": gain((Ld, D_MODEL)),
        "w_in": nrm((Ld, D_MODEL, N_IN), D_MODEL ** -0.5),
        "gdn_conv": nrm((Ld, CONV_K, GDN_CONV_CH), CONV_K ** -0.5),
        "gdn_a_log": a_log(GDN_HEADS),
        "gdn_dt_bias": dt_bias(GDN_HEADS),
        "gdn_norm": gain((Ld, GDN_DV)),
        "nsa_q_norm": gain((Ld, NSA_DK)),
        "nsa_k_norm": gain((Ld, NSA_DK)),
        "nsa_pe_k": nrm((Ld, CMP_BLOCK, NSA_DK), 0.1),
        "nsa_pe_v": nrm((Ld, CMP_BLOCK, NSA_DV), 0.1),
        "nsa_w_ck": nrm((Ld, CMP_BLOCK, NSA_DK, NSA_DK), (CMP_BLOCK * NSA_DK) ** -0.5),
        "nsa_w_cv": nrm((Ld, CMP_BLOCK, NSA_DV, NSA_DV), (CMP_BLOCK * NSA_DV) ** -0.5),
        "ssm_conv_w": nrm((Ld, CONV_K, SSM_CONV_CH), CONV_K ** -0.5),
        "ssm_conv_b": nrm((Ld, SSM_CONV_CH), 0.02),
        "ssm_dt_bias": dt_bias(SSM_HEADS),
        "ssm_a_log": a_log(SSM_HEADS),
        "ssm_d": 1.0 + 0.1 * jax.random.normal(next(keys), (Ld, SSM_HEADS), f32),
        "ssm_norm": gain((Ld, SSM_INNER)),
        "p_a": nrm((Ld, GDN_HEADS * GDN_DV, D_MODEL), (GDN_HEADS * GDN_DV) ** -0.5),
        "p_b": nrm((Ld, NSA_HEADS * NSA_DV, D_MODEL), (NSA_HEADS * NSA_DV) ** -0.5),
        "p_c": nrm((Ld, SSM_INNER, D_MODEL), SSM_INNER ** -0.5),
        "w_o": nrm((Ld, D_MODEL, D_MODEL), D_MODEL ** -0.5),
        "g_ffn2": gain((Ld, D_MODEL)),
        "w_up2": nrm((Ld, D_MODEL, 2 * D_FF), D_MODEL ** -0.5),
        "w_down2": nrm((Ld, D_FF, D_MODEL), D_FF ** -0.5),
    }


def reference(x, rel_table, g_ffn1, w_up1, w_down1, g_mix, w_in, gdn_conv, gdn_a_log, gdn_dt_bias, gdn_norm,
              nsa_q_norm, nsa_k_norm, nsa_pe_k, nsa_pe_v, nsa_w_ck, nsa_w_cv, ssm_conv_w, ssm_conv_b,
              ssm_dt_bias, ssm_a_log, ssm_d, ssm_norm, p_a, p_b, p_c, w_o, g_ffn2, w_up2, w_down2):
    Bsz, T, _ = x.shape
    split_at = np.cumsum(IN_SIZES)[:-1].tolist()
    for l in range(DEPTH):
        x = x + 0.5 * swiglu(rmsnorm(x, g_ffn1[l]), w_up1[l], w_down1[l])
        h = rmsnorm(x, g_mix[l])
        (a_q, a_k, a_v, a_z, a_a, a_b, b_q, b_kc, b_vc, b_ks, b_vs, b_kw, b_vw, b_g,
         c_z, c_xbc, c_dt, m_gate) = jnp.split(h @ w_in[l], split_at, axis=-1)
        y_a = gated_deltanet(a_q, a_k, a_v, a_z, a_a, a_b, gdn_conv[l], gdn_a_log[l], gdn_dt_bias[l], gdn_norm[l])
        y_b = nsa_attention(b_q, b_kc, b_vc, b_ks, b_vs, b_kw, b_vw, b_g, nsa_q_norm[l], nsa_k_norm[l],
                            nsa_pe_k[l], nsa_pe_v[l], nsa_w_ck[l], nsa_w_cv[l], rel_table)
        y_c = mamba2_ssd(c_z, c_xbc, c_dt, ssm_conv_w[l], ssm_conv_b[l], ssm_dt_bias[l], ssm_a_log[l],
                         ssm_d[l], ssm_norm[l])
        gates = jax.nn.sigmoid(m_gate).reshape(Bsz, T, 3, D_MODEL)
        merged = (gates[:, :, 0] * (y_a @ p_a[l]) + gates[:, :, 1] * (y_b @ p_b[l])
                  + gates[:, :, 2] * (y_c @ p_c[l]))
        x = x + merged @ w_o[l]
        x = x + 0.5 * swiglu(rmsnorm(x, g_ffn2[l]), w_up2[l], w_down2[l])
    return x
```

```python
import functools
import math

import jax
import jax.numpy as jnp
import numpy as np
from jax import lax
from jax.experimental import pallas as pl
from jax.experimental.pallas import tpu as pltpu

EPS = 1e-6
CONV_K = 4

GDN_HEADS = 4
GDN_DK = 128
GDN_DV = 128
GDN_CHUNK = 64

NSA_HEADS = 4
NSA_KV_GROUPS = 2
NSA_DK = 128
NSA_DV = 128
CMP_BLOCK = 32
CMP_STRIDE = 16
SEL_BLOCK = 64
N_SELECT = 16
WINDOW = 512
Q_BLOCK = 128
SEL_OVERLAP_WEIGHTS = (1.0, 2.0, 2.0, 2.0, 1.0)
FORCE_SCORE = 1e4
NEG = -1e30

SSM_HEADS = 16
SSM_HEAD_DIM = 64
SSM_GROUPS = 2
SSM_STATE = 128
SSM_CHUNK = 128
SSM_INNER = SSM_HEADS * SSM_HEAD_DIM
SSM_CONV_CH = SSM_INNER + 2 * SSM_GROUPS * SSM_STATE

REL_BUCKETS = 32
REL_MAX_EXACT = 16
REL_MAX_DIST = 1024

V7X_VMEM_LIMIT_BYTES = 56 * 1024 * 1024
LANE = 128

bf16 = jnp.bfloat16
f32 = jnp.float32


def _round_up(n, m):
    return (n + m - 1) // m * m


def _ffn_kernel(x_ref, g_ref, wa_ref, wb_ref, wd_ref, o_ref, h_ref):
    j = pl.program_id(1)

    @pl.when(j == 0)
    def _():
        x = x_ref[...]
        h = x * lax.rsqrt(jnp.mean(x * x, axis=-1, keepdims=True) + EPS) * g_ref[...]
        h_ref[...] = h.astype(bf16)
        o_ref[...] = x

    h = h_ref[...]
    a = jnp.dot(h, wa_ref[...], preferred_element_type=f32)
    b = jnp.dot(h, wb_ref[...], preferred_element_type=f32)
    act = (0.5 * a * jax.nn.sigmoid(a) * b).astype(bf16)
    o_ref[...] += jnp.dot(act, wd_ref[...], preferred_element_type=f32)


def _ffn(x2, g, wa, wb, wd, *, tm=512, tf=512):
    m, d = x2.shape
    ffp = wa.shape[1]
    return pl.pallas_call(
        _ffn_kernel,
        grid=(m // tm, ffp // tf),
        in_specs=[
            pl.BlockSpec((tm, d), lambda i, j: (i, 0)),
            pl.BlockSpec((1, d), lambda i, j: (0, 0)),
            pl.BlockSpec((d, tf), lambda i, j: (0, j)),
            pl.BlockSpec((d, tf), lambda i, j: (0, j)),
            pl.BlockSpec((tf, d), lambda i, j: (j, 0)),
        ],
        out_specs=pl.BlockSpec((tm, d), lambda i, j: (i, 0)),
        out_shape=jax.ShapeDtypeStruct((m, d), f32),
        scratch_shapes=[pltpu.VMEM((tm, d), bf16)],
        compiler_params=pltpu.CompilerParams(
            dimension_semantics=("parallel", "arbitrary"),
            vmem_limit_bytes=V7X_VMEM_LIMIT_BYTES),
        name="ffn_swiglu",
    )(x2, g, wa, wb, wd)


def _prep_ffn_weights(w_up, w_down, tf=512):
    d, two_ff = w_up.shape
    ff = two_ff // 2
    ffp = _round_up(ff, tf)
    pad = ffp - ff
    wa = jnp.pad(w_up[:, :ff].astype(bf16), ((0, 0), (0, pad)))
    wb = jnp.pad(w_up[:, ff:].astype(bf16), ((0, 0), (0, pad)))
    wd = jnp.pad(w_down.astype(bf16), ((0, pad), (0, 0)))
    return wa, wb, wd


def _norm_matmul_kernel(x_ref, g_ref, w_ref, o_ref, h_ref):
    j = pl.program_id(1)

    @pl.when(j == 0)
    def _():
        x = x_ref[...]
        h = x * lax.rsqrt(jnp.mean(x * x, axis=-1, keepdims=True) + EPS) * g_ref[...]
        h_ref[...] = h.astype(bf16)

    o_ref[...] = jnp.dot(h_ref[...], w_ref[...], preferred_element_type=f32)


def _norm_matmul(x2, g, w, *, tm=512, tn=512):
    m, d = x2.shape
    n = w.shape[1]
    return pl.pallas_call(
        _norm_matmul_kernel,
        grid=(m // tm, n // tn),
        in_specs=[
            pl.BlockSpec((tm, d), lambda i, j: (i, 0)),
            pl.BlockSpec((1, d), lambda i, j: (0, 0)),
            pl.BlockSpec((d, tn), lambda i, j: (0, j)),
        ],
        out_specs=pl.BlockSpec((tm, tn), lambda i, j: (i, j)),
        out_shape=jax.ShapeDtypeStruct((m, n), f32),
        scratch_shapes=[pltpu.VMEM((tm, d), bf16)],
        compiler_params=pltpu.CompilerParams(
            dimension_semantics=("parallel", "arbitrary"),
            vmem_limit_bytes=V7X_VMEM_LIMIT_BYTES),
        name="norm_in_proj",
    )(x2, g, w)


def _merge_kernel(x_ref, ya_ref, yb_ref, yc_ref, ga_ref, gb_ref, gc_ref, pa_ref, pb_ref, pc_ref, wo_ref, o_ref):
    ma = jnp.dot(ya_ref[...].astype(bf16), pa_ref[...], preferred_element_type=f32)
    mb = jnp.dot(yb_ref[...].astype(bf16), pb_ref[...], preferred_element_type=f32)
    mc = jnp.dot(yc_ref[...].astype(bf16), pc_ref[...], preferred_element_type=f32)
    merged = (jax.nn.sigmoid(ga_ref[...]) * ma + jax.nn.sigmoid(gb_ref[...]) * mb
              + jax.nn.sigmoid(gc_ref[...]) * mc)
    o_ref[...] = x_ref[...] + jnp.dot(merged.astype(bf16), wo_ref[...], preferred_element_type=f32)


def _merge(x2, ya, yb, yc, gates, gate_col0, pa, pb, pc, wo, *, tm=256):
    m, d = x2.shape
    gb0 = gate_col0 // d
    const = dict(pipeline_mode=pl.Buffered(1))
    return pl.pallas_call(
        _merge_kernel,
        grid=(m // tm,),
        in_specs=[
            pl.BlockSpec((tm, d), lambda i: (i, 0)),
            pl.BlockSpec((tm, ya.shape[1]), lambda i: (i, 0)),
            pl.BlockSpec((tm, yb.shape[1]), lambda i: (i, 0)),
            pl.BlockSpec((tm, yc.shape[1]), lambda i: (i, 0)),
            pl.BlockSpec((tm, d), lambda i: (i, gb0)),
            pl.BlockSpec((tm, d), lambda i: (i, gb0 + 1)),
            pl.BlockSpec((tm, d), lambda i: (i, gb0 + 2)),
            pl.BlockSpec(pa.shape, lambda i: (0, 0), **const),
            pl.BlockSpec(pb.shape, lambda i: (0, 0), **const),
            pl.BlockSpec(pc.shape, lambda i: (0, 0), **const),
            pl.BlockSpec(wo.shape, lambda i: (0, 0), **const),
        ],
        out_specs=pl.BlockSpec((tm, d), lambda i: (i, 0)),
        out_shape=jax.ShapeDtypeStruct((m, d), f32),
        compiler_params=pltpu.CompilerParams(
            dimension_semantics=("parallel",),
            vmem_limit_bytes=V7X_VMEM_LIMIT_BYTES),
        name="merge_out_proj",
    )(x2, ya, yb, yc, gates, gates, gates, pa, pb, pc, wo)


def _jx_rmsnorm(x, g):
    xf = x.astype(jnp.float32)
    y = xf * lax.rsqrt(jnp.mean(xf * xf, axis=-1, keepdims=True) + EPS)
    return (y * g.astype(jnp.float32)).astype(x.dtype)


def _jx_l2norm(t):
    return t * lax.rsqrt(jnp.sum(t * t, axis=-1, keepdims=True) + EPS)


def _jx_causal_dwconv(x, w):
    return lax.conv_general_dilated(x, w[:, None, :], window_strides=(1,), padding=[(CONV_K - 1, 0)],
                                    dimension_numbers=('NWC', 'WIO', 'NWC'), feature_group_count=x.shape[-1])


def _jx_rel_bucket(dist):
    n = jnp.maximum(dist, 0)
    nf = jnp.maximum(n, 1).astype(jnp.float32)
    large = REL_MAX_EXACT + (jnp.log(nf / REL_MAX_EXACT) / math.log(REL_MAX_DIST / REL_MAX_EXACT)
                             * (REL_BUCKETS - REL_MAX_EXACT)).astype(jnp.int32)
    large = jnp.minimum(large, REL_BUCKETS - 1)
    return jnp.where(n < REL_MAX_EXACT, n, large)


def _jx_gated_deltanet(q, k, v, z, a, b, conv_w, a_log, dt_bias, norm_g):
    Bsz, T, _ = q.shape
    H, C = GDN_HEADS, GDN_CHUNK
    n_c = T // C
    qk_w = H * GDN_DK
    qkv = jax.nn.silu(_jx_causal_dwconv(jnp.concatenate([q, k, v], axis=-1), conv_w)).astype(f32)

    def heads(t):
        return t.reshape(Bsz, n_c, C, H, -1).transpose(0, 3, 1, 2, 4)

    qh = _jx_l2norm(heads(qkv[..., :qk_w])) * GDN_DK ** -0.5
    kh = _jx_l2norm(heads(qkv[..., qk_w:2 * qk_w]))
    vh = heads(qkv[..., 2 * qk_w:])
    beta = jax.nn.sigmoid(b.astype(f32)).reshape(Bsz, n_c, C, H).transpose(0, 3, 1, 2)
    g = -jnp.exp(a_log.astype(f32)) * jax.nn.softplus(a.astype(f32) + dt_bias.astype(f32))
    gc = jnp.cumsum(g.reshape(Bsz, n_c, C, H).transpose(0, 3, 1, 2), axis=-1)
    causal = jnp.tril(jnp.ones((C, C), bool))
    strict = jnp.tril(jnp.ones((C, C), bool), -1)
    diff = gc[..., :, None] - gc[..., None, :]
    decay = jnp.where(causal, jnp.exp(jnp.where(causal, diff, 0.0)), 0.0)
    kb = kh * beta[..., None]
    m = jnp.where(strict, jnp.einsum('bhncd,bhnsd->bhncs', kb, kh) * decay, 0.0) + jnp.eye(C, dtype=f32)
    rhs = jnp.concatenate([vh * beta[..., None], kb * jnp.exp(gc)[..., None]], axis=-1)
    sol = lax.linalg.triangular_solve(m, rhs, left_side=True, lower=True, unit_diagonal=True)
    u, w = sol[..., :GDN_DV], sol[..., GDN_DV:]
    attn = jnp.einsum('bhncd,bhnsd->bhncs', qh, kh) * decay
    q_dec = qh * jnp.exp(gc)[..., None]
    k_dec = kh * jnp.exp(gc[..., -1:] - gc)[..., None]
    g_last = jnp.exp(gc[..., -1])

    def step(state, inp):
        u_c, w_c, a_c, qd_c, kd_c, gl_c = inp
        v_new = u_c - jnp.einsum('bhck,bhkv->bhcv', w_c, state)
        o_c = jnp.einsum('bhck,bhkv->bhcv', qd_c, state) + jnp.einsum('bhcs,bhsv->bhcv', a_c, v_new)
        state = state * gl_c[..., None, None] + jnp.einsum('bhck,bhcv->bhkv', kd_c, v_new)
        return state, o_c

    xs = (jnp.moveaxis(u, 2, 0), jnp.moveaxis(w, 2, 0), jnp.moveaxis(attn, 2, 0),
          jnp.moveaxis(q_dec, 2, 0), jnp.moveaxis(k_dec, 2, 0), jnp.moveaxis(g_last, 2, 0))
    _, o = lax.scan(step, jnp.zeros((Bsz, H, GDN_DK, GDN_DV), f32), xs)
    o = o.transpose(1, 0, 3, 2, 4).reshape(Bsz, T, H, GDN_DV)
    o = _jx_rmsnorm(o, norm_g) * jax.nn.silu(z.astype(f32).reshape(Bsz, T, H, GDN_DV))
    return o.reshape(Bsz, T, H * GDN_DV).astype(z.dtype)


def _jx_nsa_attention(q_in, kc, vc, ks, vs, kw, vw, gate_logits, q_norm, k_norm, pe_k, pe_v, w_ck, w_cv, rel_table):
    Bsz, T, _ = q_in.shape
    G, Hg = NSA_KV_GROUPS, NSA_HEADS // NSA_KV_GROUPS
    n_cmp = T // CMP_STRIDE - 1
    n_sel = T // SEL_BLOCK
    n_top = min(N_SELECT, n_sel)
    n_qb = T // Q_BLOCK
    ratio = SEL_BLOCK // CMP_STRIDE

    q = _jx_rmsnorm(q_in.reshape(Bsz, T, NSA_HEADS, NSA_DK), q_norm).astype(f32) * NSA_DK ** -0.5
    q_blocks = q.reshape(Bsz, n_qb, Q_BLOCK, G, Hg, NSA_DK).transpose(1, 0, 3, 4, 2, 5)

    def groups(t):
        return t.reshape(Bsz, T, G, -1).astype(f32)

    def compress(t, pe, w):
        c = groups(t).reshape(Bsz, T // CMP_STRIDE, CMP_STRIDE, G, -1)
        blocks = jnp.concatenate([c[:, :-1], c[:, 1:]], axis=2)
        return jnp.einsum('bnlgd,lde->bgne', blocks + pe.astype(f32)[:, None, :], w.astype(f32))

    k_cmp = _jx_rmsnorm(compress(kc, pe_k, w_ck), k_norm)
    v_cmp = compress(vc, pe_v, w_cv)
    k_sel = _jx_rmsnorm(groups(ks), k_norm).reshape(Bsz, n_sel, SEL_BLOCK, G, NSA_DK).transpose(0, 3, 1, 2, 4)
    v_sel = groups(vs).reshape(Bsz, n_sel, SEL_BLOCK, G, NSA_DV).transpose(0, 3, 1, 2, 4)
    pad = ((0, 0), (0, 0), (WINDOW, 0), (0, 0))
    k_win = jnp.pad(_jx_rmsnorm(groups(kw), k_norm).transpose(0, 2, 1, 3), pad)
    v_win = jnp.pad(groups(vw).transpose(0, 2, 1, 3), pad)

    table = rel_table.astype(f32).T.reshape(G, Hg, REL_BUCKETS)
    cmp_end = jnp.arange(n_cmp) * CMP_STRIDE + CMP_BLOCK - 1
    blk = jnp.arange(n_sel)
    bi = jnp.arange(Bsz)[:, None, None, None]
    gi = jnp.arange(G)[None, :, None, None]
    gi6 = jnp.arange(G)[None, :, None, None, None, None]
    hi6 = jnp.arange(Hg)[None, None, :, None, None, None]

    def block(args):
        qb, qi = args
        t = qi * Q_BLOCK + jnp.arange(Q_BLOCK)
        dist = t[:, None] - cmp_end[None, :]
        ok = dist >= 0
        s = jnp.einsum('bghqd,bgnd->bghqn', qb, k_cmp) + table[:, :, _jx_rel_bucket(dist)]
        p_cmp = jnp.where(ok, jax.nn.softmax(jnp.where(ok, s, NEG), axis=-1), 0.0)
        o_cmp = jnp.einsum('bghqn,bgnd->bghqd', p_cmp, v_cmp)
        imp_c = jnp.pad(p_cmp.sum(axis=2), ((0, 0), (0, 0), (0, 0), (1, ratio * n_sel - n_cmp)))
        imp = sum(wt * imp_c[..., j:j + ratio * n_sel:ratio] for j, wt in enumerate(SEL_OVERLAP_WEIGHTS))
        cur = t // SEL_BLOCK
        ok_s = blk[None, :] <= cur[:, None]
        forced = ok_s & ((blk[None, :] == 0) | (blk[None, :] >= cur[:, None] - 1))
        score = jnp.where(forced, FORCE_SCORE, jnp.where(ok_s, imp, -FORCE_SCORE))
        _, idx = lax.top_k(score, n_top)
        kg = k_sel[bi, gi, idx]
        vg = v_sel[bi, gi, idx]
        dist = t[:, None, None] - (idx[..., None] * SEL_BLOCK + jnp.arange(SEL_BLOCK))
        ok = (dist >= 0)[:, :, None]
        s = jnp.einsum('bghqd,bgqkld->bghqkl', qb, kg) + table[gi6, hi6, _jx_rel_bucket(dist)[:, :, None]]
        s = jnp.where(ok, s, NEG).reshape(Bsz, G, Hg, Q_BLOCK, n_top * SEL_BLOCK)
        p = jax.nn.softmax(s, axis=-1).reshape(Bsz, G, Hg, Q_BLOCK, n_top, SEL_BLOCK)
        o_sel = jnp.einsum('bghqkl,bgqkld->bghqd', p, vg)
        kwb = lax.dynamic_slice_in_dim(k_win, qi * Q_BLOCK, Q_BLOCK + WINDOW, axis=2)
        vwb = lax.dynamic_slice_in_dim(v_win, qi * Q_BLOCK, Q_BLOCK + WINDOW, axis=2)
        pos = qi * Q_BLOCK - WINDOW + jnp.arange(Q_BLOCK + WINDOW)
        dist = t[:, None] - pos[None, :]
        ok = (dist >= 0) & (dist < WINDOW) & (pos >= 0)[None, :]
        s = jnp.einsum('bghqd,bgkd->bghqk', qb, kwb) + table[:, :, _jx_rel_bucket(dist)]
        p = jax.nn.softmax(jnp.where(ok, s, NEG), axis=-1)
        o_win = jnp.einsum('bghqk,bgkd->bghqd', p, vwb)
        return jnp.stack([o_cmp, o_sel, o_win], axis=-2)

    out = lax.map(block, (q_blocks, jnp.arange(n_qb)))
    out = out.transpose(1, 0, 4, 2, 3, 5, 6).reshape(Bsz, T, NSA_HEADS, 3, NSA_DV)
    gates = jax.nn.sigmoid(gate_logits.astype(f32)).reshape(Bsz, T, NSA_HEADS, 3)
    o = jnp.einsum('bthr,bthrd->bthd', gates, out)
    return o.reshape(Bsz, T, NSA_HEADS * NSA_DV).astype(q_in.dtype)


def _jx_mamba2_ssd(z, xbc, dt, conv_w, conv_b, dt_bias, a_log, d_skip, norm_g):
    Bsz, T, _ = z.shape
    H, P, G, N, L = SSM_HEADS, SSM_HEAD_DIM, SSM_GROUPS, SSM_STATE, SSM_CHUNK
    n_c = T // L
    xbc = jax.nn.silu(_jx_causal_dwconv(xbc, conv_w) + conv_b).astype(f32)
    xs = xbc[..., :SSM_INNER].reshape(Bsz, n_c, L, H, P)
    bm = jnp.repeat(xbc[..., SSM_INNER:SSM_INNER + G * N].reshape(Bsz, n_c, L, G, N), H // G, axis=3)
    cm = jnp.repeat(xbc[..., SSM_INNER + G * N:].reshape(Bsz, n_c, L, G, N), H // G, axis=3)
    dt = jax.nn.softplus(dt.astype(f32) + dt_bias.astype(f32)).reshape(Bsz, n_c, L, H)
    ac = jnp.cumsum(dt * (-jnp.exp(a_log.astype(f32))), axis=2)
    xdt = xs * dt[..., None]
    ach = jnp.swapaxes(ac, 2, 3)
    causal = jnp.tril(jnp.ones((L, L), bool))
    seg = ach[..., :, None] - ach[..., None, :]
    decay = jnp.where(causal, jnp.exp(jnp.where(causal, seg, 0.0)), 0.0)
    scores = jnp.einsum('bclhn,bcshn->bchls', cm, bm) * decay
    y = jnp.einsum('bchls,bcshp->bclhp', scores, xdt)
    states = jnp.einsum('bclhn,bclhp->bchpn', bm * jnp.exp(ac[:, :, -1:, :] - ac)[..., None], xdt)
    chunk_decay = jnp.exp(ac[:, :, -1, :])

    def step(s, inp):
        st, cd = inp
        return s * cd[..., None, None] + st, s

    _, s_in = lax.scan(step, jnp.zeros((Bsz, H, P, N), f32),
                       (jnp.moveaxis(states, 1, 0), jnp.moveaxis(chunk_decay, 1, 0)))
    s_in = jnp.moveaxis(s_in, 0, 1)
    y = y + jnp.einsum('bclhn,bchpn->bclhp', cm * jnp.exp(ac)[..., None], s_in) + xs * d_skip.astype(f32)[:, None]
    y = y.reshape(Bsz, T, SSM_INNER) * jax.nn.silu(z.astype(f32))
    y = y.reshape(Bsz, T, G, SSM_INNER // G)
    y = y * lax.rsqrt(jnp.mean(y * y, axis=-1, keepdims=True) + EPS)
    return (y.reshape(Bsz, T, SSM_INNER) * norm_g.astype(f32)).astype(z.dtype)


_IN_NAMES = ("a_q", "a_k", "a_v", "a_z", "a_a", "a_b", "b_q", "b_kc", "b_vc", "b_ks", "b_vs", "b_kw", "b_vw",
             "b_g", "c_z", "c_xbc", "c_dt", "m_gate")


def _in_sizes(d_model):
    return (GDN_HEADS * GDN_DK, GDN_HEADS * GDN_DK, GDN_HEADS * GDN_DV, GDN_HEADS * GDN_DV, GDN_HEADS, GDN_HEADS,
            NSA_HEADS * NSA_DK, NSA_KV_GROUPS * NSA_DK, NSA_KV_GROUPS * NSA_DV, NSA_KV_GROUPS * NSA_DK,
            NSA_KV_GROUPS * NSA_DV, NSA_KV_GROUPS * NSA_DK, NSA_KV_GROUPS * NSA_DV, 3 * NSA_HEADS,
            SSM_INNER, SSM_CONV_CH, SSM_HEADS, 3 * d_model)


_BIG_ORDER = ("m_gate", "a_q", "a_k", "a_v", "c_xbc", "c_z", "a_z", "b_q", "b_kc", "b_vc", "b_ks", "b_vs",
              "b_kw", "b_vw")
_SMALL_ORDER = ("a_a", "a_b", "b_g", "c_dt")


def _in_layout(d_model, tn=512):
    sizes = dict(zip(_IN_NAMES, _in_sizes(d_model)))
    src_off = dict(zip(_IN_NAMES, np.cumsum((0,) + _in_sizes(d_model))[:-1].tolist()))
    cols, offs = [], {}
    for name in _BIG_ORDER:
        offs[name] = len(cols)
        cols.extend(range(src_off[name], src_off[name] + sizes[name]))
    offs["small"] = len(cols)
    n_small = 0
    for name in _SMALL_ORDER:
        offs[name] = len(cols)
        cols.extend(range(src_off[name], src_off[name] + sizes[name]))
        n_small += sizes[name]
    cols.extend([-1] * (LANE - n_small))
    n_pad = _round_up(len(cols), tn)
    cols.extend([-1] * (n_pad - len(cols)))
    return np.asarray(cols, np.int32), offs, sizes, n_pad


def _prep_w_in(w_in_l, cols):
    w = jnp.take(w_in_l, jnp.asarray(np.maximum(cols, 0)), axis=1)
    return jnp.where(jnp.asarray(cols >= 0)[None, :], w, 0.0).astype(bf16)


def kernel(x, rel_table, g_ffn1, w_up1, w_down1, g_mix, w_in, gdn_conv, gdn_a_log, gdn_dt_bias, gdn_norm,
           nsa_q_norm, nsa_k_norm, nsa_pe_k, nsa_pe_v, nsa_w_ck, nsa_w_cv, ssm_conv_w, ssm_conv_b,
           ssm_dt_bias, ssm_a_log, ssm_d, ssm_norm, p_a, p_b, p_c, w_o, g_ffn2, w_up2, w_down2):
    bsz, seq, d = x.shape
    depth = w_in.shape[0]
    m = bsz * seq
    cols, offs, sizes, _ = _in_layout(d)
    x2 = x.reshape(m, d)

    def seg(proj, name):
        return proj[:, offs[name]:offs[name] + sizes[name]].reshape(bsz, seq, sizes[name])

    for l in range(depth):
        wa, wb, wd = _prep_ffn_weights(w_up1[l], w_down1[l])
        x2 = _ffn(x2, g_ffn1[l].reshape(1, d), wa, wb, wd)

        proj = _norm_matmul(x2, g_mix[l].reshape(1, d), _prep_w_in(w_in[l], cols))
        y_a = _jx_gated_deltanet(seg(proj, "a_q"), seg(proj, "a_k"), seg(proj, "a_v"), seg(proj, "a_z"),
                                 seg(proj, "a_a"), seg(proj, "a_b"), gdn_conv[l], gdn_a_log[l], gdn_dt_bias[l],
                                 gdn_norm[l])
        y_b = _jx_nsa_attention(seg(proj, "b_q"), seg(proj, "b_kc"), seg(proj, "b_vc"), seg(proj, "b_ks"),
                                seg(proj, "b_vs"), seg(proj, "b_kw"), seg(proj, "b_vw"), seg(proj, "b_g"),
                                nsa_q_norm[l], nsa_k_norm[l], nsa_pe_k[l], nsa_pe_v[l], nsa_w_ck[l], nsa_w_cv[l],
                                rel_table)
        y_c = _jx_mamba2_ssd(seg(proj, "c_z"), seg(proj, "c_xbc"), seg(proj, "c_dt"), ssm_conv_w[l], ssm_conv_b[l],
                             ssm_dt_bias[l], ssm_a_log[l], ssm_d[l], ssm_norm[l])
        x2 = _merge(x2, y_a.reshape(m, -1), y_b.reshape(m, -1), y_c.reshape(m, -1), proj, offs["m_gate"],
                    p_a[l].astype(bf16), p_b[l].astype(bf16), p_c[l].astype(bf16), w_o[l].astype(bf16))

        wa, wb, wd = _prep_ffn_weights(w_up2[l], w_down2[l])
        x2 = _ffn(x2, g_ffn2[l].reshape(1, d), wa, wb, wd)
    return x2.reshape(bsz, seq, d)
```

```python
import functools
import math

import jax
import jax.numpy as jnp
import numpy as np
from jax import lax
from jax.experimental import pallas as pl
from jax.experimental.pallas import tpu as pltpu

EPS = 1e-6
CONV_K = 4

GDN_HEADS = 4
GDN_DK = 128
GDN_DV = 128
GDN_CHUNK = 64

NSA_HEADS = 4
NSA_KV_GROUPS = 2
NSA_DK = 128
NSA_DV = 128
CMP_BLOCK = 32
CMP_STRIDE = 16
SEL_BLOCK = 64
N_SELECT = 16
WINDOW = 512
Q_BLOCK = 128
SEL_OVERLAP_WEIGHTS = (1.0, 2.0, 2.0, 2.0, 1.0)
FORCE_SCORE = 1e4
NEG = -1e30

SSM_HEADS = 16
SSM_HEAD_DIM = 64
SSM_GROUPS = 2
SSM_STATE = 128
SSM_CHUNK = 128
SSM_INNER = SSM_HEADS * SSM_HEAD_DIM
SSM_CONV_CH = SSM_INNER + 2 * SSM_GROUPS * SSM_STATE

REL_BUCKETS = 32
REL_MAX_EXACT = 16
REL_MAX_DIST = 1024

V7X_VMEM_LIMIT_BYTES = 56 * 1024 * 1024
LANE = 128

bf16 = jnp.bfloat16
f32 = jnp.float32


def _round_up(n, m):
    return (n + m - 1) // m * m


def _ffn_kernel(x_ref, g_ref, wa_ref, wb_ref, wd_ref, o_ref, h_ref):
    j = pl.program_id(1)

    @pl.when(j == 0)
    def _():
        x = x_ref[...]
        h = x * lax.rsqrt(jnp.mean(x * x, axis=-1, keepdims=True) + EPS) * g_ref[...]
        h_ref[...] = h.astype(bf16)
        o_ref[...] = x

    h = h_ref[...]
    a = jnp.dot(h, wa_ref[...], preferred_element_type=f32)
    b = jnp.dot(h, wb_ref[...], preferred_element_type=f32)
    act = (0.5 * a * jax.nn.sigmoid(a) * b).astype(bf16)
    o_ref[...] += jnp.dot(act, wd_ref[...], preferred_element_type=f32)


def _ffn(x2, g, wa, wb, wd, *, tm=512, tf=512):
    m, d = x2.shape
    ffp = wa.shape[1]
    return pl.pallas_call(
        _ffn_kernel,
        grid=(m // tm, ffp // tf),
        in_specs=[
            pl.BlockSpec((tm, d), lambda i, j: (i, 0)),
            pl.BlockSpec((1, d), lambda i, j: (0, 0)),
            pl.BlockSpec((d, tf), lambda i, j: (0, j)),
            pl.BlockSpec((d, tf), lambda i, j: (0, j)),
            pl.BlockSpec((tf, d), lambda i, j: (j, 0)),
        ],
        out_specs=pl.BlockSpec((tm, d), lambda i, j: (i, 0)),
        out_shape=jax.ShapeDtypeStruct((m, d), f32),
        scratch_shapes=[pltpu.VMEM((tm, d), bf16)],
        compiler_params=pltpu.CompilerParams(
            dimension_semantics=("parallel", "arbitrary"),
            vmem_limit_bytes=V7X_VMEM_LIMIT_BYTES),
        name="ffn_swiglu",
    )(x2, g, wa, wb, wd)


def _prep_ffn_weights(w_up, w_down, tf=512):
    d, two_ff = w_up.shape
    ff = two_ff // 2
    ffp = _round_up(ff, tf)
    pad = ffp - ff
    wa = jnp.pad(w_up[:, :ff].astype(bf16), ((0, 0), (0, pad)))
    wb = jnp.pad(w_up[:, ff:].astype(bf16), ((0, 0), (0, pad)))
    wd = jnp.pad(w_down.astype(bf16), ((0, pad), (0, 0)))
    return wa, wb, wd


def _norm_matmul_kernel(x_ref, g_ref, w_ref, o_ref, h_ref):
    j = pl.program_id(1)

    @pl.when(j == 0)
    def _():
        x = x_ref[...]
        h = x * lax.rsqrt(jnp.mean(x * x, axis=-1, keepdims=True) + EPS) * g_ref[...]
        h_ref[...] = h.astype(bf16)

    o_ref[...] = jnp.dot(h_ref[...], w_ref[...], preferred_element_type=f32)


def _norm_matmul(x2, g, w, *, tm=512, tn=512):
    m, d = x2.shape
    n = w.shape[1]
    return pl.pallas_call(
        _norm_matmul_kernel,
        grid=(m // tm, n // tn),
        in_specs=[
            pl.BlockSpec((tm, d), lambda i, j: (i, 0)),
            pl.BlockSpec((1, d), lambda i, j: (0, 0)),
            pl.BlockSpec((d, tn), lambda i, j: (0, j)),
        ],
        out_specs=pl.BlockSpec((tm, tn), lambda i, j: (i, j)),
        out_shape=jax.ShapeDtypeStruct((m, n), f32),
        scratch_shapes=[pltpu.VMEM((tm, d), bf16)],
        compiler_params=pltpu.CompilerParams(
            dimension_semantics=("parallel", "arbitrary"),
            vmem_limit_bytes=V7X_VMEM_LIMIT_BYTES),
        name="norm_in_proj",
    )(x2, g, w)


def _merge_kernel(x_ref, ya_ref, yb_ref, yc_ref, ga_ref, gb_ref, gc_ref, pa_ref, pb_ref, pc_ref, wo_ref, o_ref):
    ma = jnp.dot(ya_ref[...].astype(bf16), pa_ref[...], preferred_element_type=f32)
    mb = jnp.dot(yb_ref[...].astype(bf16), pb_ref[...], preferred_element_type=f32)
    mc = jnp.dot(yc_ref[...].astype(bf16), pc_ref[...], preferred_element_type=f32)
    merged = (jax.nn.sigmoid(ga_ref[...]) * ma + jax.nn.sigmoid(gb_ref[...]) * mb
              + jax.nn.sigmoid(gc_ref[...]) * mc)
    o_ref[...] = x_ref[...] + jnp.dot(merged.astype(bf16), wo_ref[...], preferred_element_type=f32)


def _merge(x2, ya, yb, yc, gates, gate_col0, pa, pb, pc, wo, *, tm=256):
    m, d = x2.shape
    gb0 = gate_col0 // d
    const = dict(pipeline_mode=pl.Buffered(1))
    return pl.pallas_call(
        _merge_kernel,
        grid=(m // tm,),
        in_specs=[
            pl.BlockSpec((tm, d), lambda i: (i, 0)),
            pl.BlockSpec((tm, ya.shape[1]), lambda i: (i, 0)),
            pl.BlockSpec((tm, yb.shape[1]), lambda i: (i, 0)),
            pl.BlockSpec((tm, yc.shape[1]), lambda i: (i, 0)),
            pl.BlockSpec((tm, d), lambda i: (i, gb0)),
            pl.BlockSpec((tm, d), lambda i: (i, gb0 + 1)),
            pl.BlockSpec((tm, d), lambda i: (i, gb0 + 2)),
            pl.BlockSpec(pa.shape, lambda i: (0, 0), **const),
            pl.BlockSpec(pb.shape, lambda i: (0, 0), **const),
            pl.BlockSpec(pc.shape, lambda i: (0, 0), **const),
            pl.BlockSpec(wo.shape, lambda i: (0, 0), **const),
        ],
        out_specs=pl.BlockSpec((tm, d), lambda i: (i, 0)),
        out_shape=jax.ShapeDtypeStruct((m, d), f32),
        compiler_params=pltpu.CompilerParams(
            dimension_semantics=("parallel",),
            vmem_limit_bytes=V7X_VMEM_LIMIT_BYTES),
        name="merge_out_proj",
    )(x2, ya, yb, yc, gates, gates, gates, pa, pb, pc, wo)


def _jx_rmsnorm(x, g):
    xf = x.astype(jnp.float32)
    y = xf * lax.rsqrt(jnp.mean(xf * xf, axis=-1, keepdims=True) + EPS)
    return (y * g.astype(jnp.float32)).astype(x.dtype)


def _jx_l2norm(t):
    return t * lax.rsqrt(jnp.sum(t * t, axis=-1, keepdims=True) + EPS)


def _jx_causal_dwconv(x, w):
    return lax.conv_general_dilated(x, w[:, None, :], window_strides=(1,), padding=[(CONV_K - 1, 0)],
                                    dimension_numbers=('NWC', 'WIO', 'NWC'), feature_group_count=x.shape[-1])


def _jx_rel_bucket(dist):
    n = jnp.maximum(dist, 0)
    nf = jnp.maximum(n, 1).astype(jnp.float32)
    large = REL_MAX_EXACT + (jnp.log(nf / REL_MAX_EXACT) / math.log(REL_MAX_DIST / REL_MAX_EXACT)
                             * (REL_BUCKETS - REL_MAX_EXACT)).astype(jnp.int32)
    large = jnp.minimum(large, REL_BUCKETS - 1)
    return jnp.where(n < REL_MAX_EXACT, n, large)


def _jx_gated_deltanet(q, k, v, z, a, b, conv_w, a_log, dt_bias, norm_g):
    Bsz, T, _ = q.shape
    H, C = GDN_HEADS, GDN_CHUNK
    n_c = T // C
    qk_w = H * GDN_DK
    qkv = jax.nn.silu(_jx_causal_dwconv(jnp.concatenate([q, k, v], axis=-1), conv_w)).astype(f32)

    def heads(t):
        return t.reshape(Bsz, n_c, C, H, -1).transpose(0, 3, 1, 2, 4)

    qh = _jx_l2norm(heads(qkv[..., :qk_w])) * GDN_DK ** -0.5
    kh = _jx_l2norm(heads(qkv[..., qk_w:2 * qk_w]))
    vh = heads(qkv[..., 2 * qk_w:])
    beta = jax.nn.sigmoid(b.astype(f32)).reshape(Bsz, n_c, C, H).transpose(0, 3, 1, 2)
    g = -jnp.exp(a_log.astype(f32)) * jax.nn.softplus(a.astype(f32) + dt_bias.astype(f32))
    gc = jnp.cumsum(g.reshape(Bsz, n_c, C, H).transpose(0, 3, 1, 2), axis=-1)
    causal = jnp.tril(jnp.ones((C, C), bool))
    strict = jnp.tril(jnp.ones((C, C), bool), -1)
    diff = gc[..., :, None] - gc[..., None, :]
    decay = jnp.where(causal, jnp.exp(jnp.where(causal, diff, 0.0)), 0.0)
    kb = kh * beta[..., None]
    m = jnp.where(strict, jnp.einsum('bhncd,bhnsd->bhncs', kb, kh) * decay, 0.0) + jnp.eye(C, dtype=f32)
    rhs = jnp.concatenate([vh * beta[..., None], kb * jnp.exp(gc)[..., None]], axis=-1)
    sol = lax.linalg.triangular_solve(m, rhs, left_side=True, lower=True, unit_diagonal=True)
    u, w = sol[..., :GDN_DV], sol[..., GDN_DV:]
    attn = jnp.einsum('bhncd,bhnsd->bhncs', qh, kh) * decay
    q_dec = qh * jnp.exp(gc)[..., None]
    k_dec = kh * jnp.exp(gc[..., -1:] - gc)[..., None]
    g_last = jnp.exp(gc[..., -1])

    def step(state, inp):
        u_c, w_c, a_c, qd_c, kd_c, gl_c = inp
        v_new = u_c - jnp.einsum('bhck,bhkv->bhcv', w_c, state)
        o_c = jnp.einsum('bhck,bhkv->bhcv', qd_c, state) + jnp.einsum('bhcs,bhsv->bhcv', a_c, v_new)
        state = state * gl_c[..., None, None] + jnp.einsum('bhck,bhcv->bhkv', kd_c, v_new)
        return state, o_c

    xs = (jnp.moveaxis(u, 2, 0), jnp.moveaxis(w, 2, 0), jnp.moveaxis(attn, 2, 0),
          jnp.moveaxis(q_dec, 2, 0), jnp.moveaxis(k_dec, 2, 0), jnp.moveaxis(g_last, 2, 0))
    _, o = lax.scan(step, jnp.zeros((Bsz, H, GDN_DK, GDN_DV), f32), xs)
    o = o.transpose(1, 0, 3, 2, 4).reshape(Bsz, T, H, GDN_DV)
    o = _jx_rmsnorm(o, norm_g) * jax.nn.silu(z.astype(f32).reshape(Bsz, T, H, GDN_DV))
    return o.reshape(Bsz, T, H * GDN_DV).astype(z.dtype)


def _jx_nsa_attention(q_in, kc, vc, ks, vs, kw, vw, gate_logits, q_norm, k_norm, pe_k, pe_v, w_ck, w_cv, rel_table):
    Bsz, T, _ = q_in.shape
    G, Hg = NSA_KV_GROUPS, NSA_HEADS // NSA_KV_GROUPS
    n_cmp = T // CMP_STRIDE - 1
    n_sel = T // SEL_BLOCK
    n_top = min(N_SELECT, n_sel)
    n_qb = T // Q_BLOCK
    ratio = SEL_BLOCK // CMP_STRIDE

    q = _jx_rmsnorm(q_in.reshape(Bsz, T, NSA_HEADS, NSA_DK), q_norm).astype(f32) * NSA_DK ** -0.5
    q_blocks = q.reshape(Bsz, n_qb, Q_BLOCK, G, Hg, NSA_DK).transpose(1, 0, 3, 4, 2, 5)

    def groups(t):
        return t.reshape(Bsz, T, G, -1).astype(f32)

    def compress(t, pe, w):
        c = groups(t).reshape(Bsz, T // CMP_STRIDE, CMP_STRIDE, G, -1)
        blocks = jnp.concatenate([c[:, :-1], c[:, 1:]], axis=2)
        return jnp.einsum('bnlgd,lde->bgne', blocks + pe.astype(f32)[:, None, :], w.astype(f32))

    k_cmp = _jx_rmsnorm(compress(kc, pe_k, w_ck), k_norm)
    v_cmp = compress(vc, pe_v, w_cv)
    k_sel = _jx_rmsnorm(groups(ks), k_norm).reshape(Bsz, n_sel, SEL_BLOCK, G, NSA_DK).transpose(0, 3, 1, 2, 4)
    v_sel = groups(vs).reshape(Bsz, n_sel, SEL_BLOCK, G, NSA_DV).transpose(0, 3, 1, 2, 4)
    pad = ((0, 0), (0, 0), (WINDOW, 0), (0, 0))
    k_win = jnp.pad(_jx_rmsnorm(groups(kw), k_norm).transpose(0, 2, 1, 3), pad)
    v_win = jnp.pad(groups(vw).transpose(0, 2, 1, 3), pad)

    table = rel_table.astype(f32).T.reshape(G, Hg, REL_BUCKETS)
    cmp_end = jnp.arange(n_cmp) * CMP_STRIDE + CMP_BLOCK - 1
    blk = jnp.arange(n_sel)
    bi = jnp.arange(Bsz)[:, None, None, None]
    gi = jnp.arange(G)[None, :, None, None]
    gi6 = jnp.arange(G)[None, :, None, None, None, None]
    hi6 = jnp.arange(Hg)[None, None, :, None, None, None]

    def block(args):
        qb, qi = args
        t = qi * Q_BLOCK + jnp.arange(Q_BLOCK)
        dist = t[:, None] - cmp_end[None, :]
        ok = dist >= 0
        s = jnp.einsum('bghqd,bgnd->bghqn', qb, k_cmp) + table[:, :, _jx_rel_bucket(dist)]
        p_cmp = jnp.where(ok, jax.nn.softmax(jnp.where(ok, s, NEG), axis=-1), 0.0)
        o_cmp = jnp.einsum('bghqn,bgnd->bghqd', p_cmp, v_cmp)
        imp_c = jnp.pad(p_cmp.sum(axis=2), ((0, 0), (0, 0), (0, 0), (1, ratio * n_sel - n_cmp)))
        imp = sum(wt * imp_c[..., j:j + ratio * n_sel:ratio] for j, wt in enumerate(SEL_OVERLAP_WEIGHTS))
        cur = t // SEL_BLOCK
        ok_s = blk[None, :] <= cur[:, None]
        forced = ok_s & ((blk[None, :] == 0) | (blk[None, :] >= cur[:, None] - 1))
        score = jnp.where(forced, FORCE_SCORE, jnp.where(ok_s, imp, -FORCE_SCORE))
        _, idx = lax.top_k(score, n_top)
        kg = k_sel[bi, gi, idx]
        vg = v_sel[bi, gi, idx]
        dist = t[:, None, None] - (idx[..., None] * SEL_BLOCK + jnp.arange(SEL_BLOCK))
        ok = (dist >= 0)[:, :, None]
        s = jnp.einsum('bghqd,bgqkld->bghqkl', qb, kg) + table[gi6, hi6, _jx_rel_bucket(dist)[:, :, None]]
        s = jnp.where(ok, s, NEG).reshape(Bsz, G, Hg, Q_BLOCK, n_top * SEL_BLOCK)
        p = jax.nn.softmax(s, axis=-1).reshape(Bsz, G, Hg, Q_BLOCK, n_top, SEL_BLOCK)
        o_sel = jnp.einsum('bghqkl,bgqkld->bghqd', p, vg)
        kwb = lax.dynamic_slice_in_dim(k_win, qi * Q_BLOCK, Q_BLOCK + WINDOW, axis=2)
        vwb = lax.dynamic_slice_in_dim(v_win, qi * Q_BLOCK, Q_BLOCK + WINDOW, axis=2)
        pos = qi * Q_BLOCK - WINDOW + jnp.arange(Q_BLOCK + WINDOW)
        dist = t[:, None] - pos[None, :]
        ok = (dist >= 0) & (dist < WINDOW) & (pos >= 0)[None, :]
        s = jnp.einsum('bghqd,bgkd->bghqk', qb, kwb) + table[:, :, _jx_rel_bucket(dist)]
        p = jax.nn.softmax(jnp.where(ok, s, NEG), axis=-1)
        o_win = jnp.einsum('bghqk,bgkd->bghqd', p, vwb)
        return jnp.stack([o_cmp, o_sel, o_win], axis=-2)

    out = lax.map(block, (q_blocks, jnp.arange(n_qb)))
    out = out.transpose(1, 0, 4, 2, 3, 5, 6).reshape(Bsz, T, NSA_HEADS, 3, NSA_DV)
    gates = jax.nn.sigmoid(gate_logits.astype(f32)).reshape(Bsz, T, NSA_HEADS, 3)
    o = jnp.einsum('bthr,bthrd->bthd', gates, out)
    return o.reshape(Bsz, T, NSA_HEADS * NSA_DV).astype(q_in.dtype)


def _jx_mamba2_ssd(z, xbc, dt, conv_w, conv_b, dt_bias, a_log, d_skip, norm_g):
    Bsz, T, _ = z.shape
    H, P, G, N, L = SSM_HEADS, SSM_HEAD_DIM, SSM_GROUPS, SSM_STATE, SSM_CHUNK
    n_c = T // L
    xbc = jax.nn.silu(_jx_causal_dwconv(xbc, conv_w) + conv_b).astype(f32)
    xs = xbc[..., :SSM_INNER].reshape(Bsz, n_c, L, H, P)
    bm = jnp.repeat(xbc[..., SSM_INNER:SSM_INNER + G * N].reshape(Bsz, n_c, L, G, N), H // G, axis=3)
    cm = jnp.repeat(xbc[..., SSM_INNER + G * N:].reshape(Bsz, n_c, L, G, N), H // G, axis=3)
    dt = jax.nn.softplus(dt.astype(f32) + dt_bias.astype(f32)).reshape(Bsz, n_c, L, H)
    ac = jnp.cumsum(dt * (-jnp.exp(a_log.astype(f32))), axis=2)
    xdt = xs * dt[..., None]
    ach = jnp.swapaxes(ac, 2, 3)
    causal = jnp.tril(jnp.ones((L, L), bool))
    seg = ach[..., :, None] - ach[..., None, :]
    decay = jnp.where(causal, jnp.exp(jnp.where(causal, seg, 0.0)), 0.0)
    scores = jnp.einsum('bclhn,bcshn->bchls', cm, bm) * decay
    y = jnp.einsum('bchls,bcshp->bclhp', scores, xdt)
    states = jnp.einsum('bclhn,bclhp->bchpn', bm * jnp.exp(ac[:, :, -1:, :] - ac)[..., None], xdt)
    chunk_decay = jnp.exp(ac[:, :, -1, :])

    def step(s, inp):
        st, cd = inp
        return s * cd[..., None, None] + st, s

    _, s_in = lax.scan(step, jnp.zeros((Bsz, H, P, N), f32),
                       (jnp.moveaxis(states, 1, 0), jnp.moveaxis(chunk_decay, 1, 0)))
    s_in = jnp.moveaxis(s_in, 0, 1)
    y = y + jnp.einsum('bclhn,bchpn->bclhp', cm * jnp.exp(ac)[..., None], s_in) + xs * d_skip.astype(f32)[:, None]
    y = y.reshape(Bsz, T, SSM_INNER) * jax.nn.silu(z.astype(f32))
    y = y.reshape(Bsz, T, G, SSM_INNER // G)
    y = y * lax.rsqrt(jnp.mean(y * y, axis=-1, keepdims=True) + EPS)
    return (y.reshape(Bsz, T, SSM_INNER) * norm_g.astype(f32)).astype(z.dtype)


NSA_HG = NSA_HEADS // NSA_KV_GROUPS
SEL_FAR_TILE = 512
NEAR_TILES = 11
STRIP_W = NEAR_TILES * Q_BLOCK
STRIP_D0 = (NEAR_TILES - 1) * Q_BLOCK
WIN_TILES = WINDOW // Q_BLOCK + 1
CMP_PER_Q = Q_BLOCK // CMP_STRIDE
SEL_PER_Q = Q_BLOCK // SEL_BLOCK
MASK_PEN = -1e30


def _dot_nt(a, b):
    return lax.dot_general(a, b, (((1,), (1,)), ((), ())), preferred_element_type=f32)


def _lane_rms(x, gain):
    return x * lax.rsqrt(jnp.mean(x * x, axis=-1, keepdims=True) + EPS) * gain


def _nsa_prep_kernel(q_ref, ks_ref, kw_ref, qg_ref, kg_ref, qo_ref, kso_ref, kwo_ref):
    qg = qg_ref[...] * (NSA_DK ** -0.5)
    kg = kg_ref[...]
    for h in range(NSA_HEADS):
        sl = slice(h * NSA_DK, (h + 1) * NSA_DK)
        qo_ref[:, sl] = _lane_rms(q_ref[:, sl], qg).astype(bf16)
    kw_ = NSA_KV_GROUPS * NSA_DK
    for src, dst in ((ks_ref, kso_ref), (kw_ref, kwo_ref)):
        for g in range(NSA_KV_GROUPS):
            sl = slice(g * NSA_DK, (g + 1) * NSA_DK)
            dst[:, sl] = _lane_rms(src[:, sl], kg).astype(bf16)
        dst[:, kw_:] = src[:, kw_:].astype(bf16)


def _nsa_prep(proj, offs, q_gain, k_gain, *, tm=512):
    m = proj.shape[0]
    w = NSA_HEADS * NSA_DK
    blk = lambda name: pl.BlockSpec((tm, w), lambda i, c=offs[name] // w: (i, c))
    out = jax.ShapeDtypeStruct((m, w), bf16)
    return pl.pallas_call(
        _nsa_prep_kernel,
        grid=(m // tm,),
        in_specs=[blk("b_q"), blk("b_ks"), blk("b_kw"),
                  pl.BlockSpec((1, NSA_DK), lambda i: (0, 0)), pl.BlockSpec((1, NSA_DK), lambda i: (0, 0))],
        out_specs=[pl.BlockSpec((tm, w), lambda i: (i, 0))] * 3,
        out_shape=[out, out, out],
        compiler_params=pltpu.CompilerParams(dimension_semantics=("parallel",),
                                             vmem_limit_bytes=V7X_VMEM_LIMIT_BYTES),
        name="nsa_prep",
    )(proj, proj, proj, q_gain, k_gain)


def _nsa_compress_kernel(x_ref, w_ref, pe_ref, kg_ref, o_ref, hi_ref):
    j = pl.program_id(1)
    nc = o_ref.shape[2]
    lo = jnp.zeros((nc, NSA_DK), f32)
    hi = jnp.zeros((nc, NSA_DK), f32)
    for l in range(CMP_STRIDE):
        rows = x_ref[pl.ds(l, nc, stride=CMP_STRIDE), :]
        lo += jnp.dot((rows + pe_ref[0, l:l + 1, :]).astype(bf16), w_ref[0, l], preferred_element_type=f32)
        hi += jnp.dot((rows + pe_ref[0, CMP_STRIDE + l:CMP_STRIDE + l + 1, :]).astype(bf16),
                      w_ref[0, CMP_STRIDE + l], preferred_element_type=f32)
    hi_ref[pl.ds(0, nc), :] = hi
    hi_ref[pl.ds(nc, 8), :] = jnp.zeros((8, NSA_DK), f32)
    c = lo + hi_ref[pl.ds(1, nc), :]
    normed = _lane_rms(c, kg_ref[...])
    o_ref[0, 0] = jnp.where(j < NSA_KV_GROUPS, normed, c).astype(bf16)


def _nsa_compress(proj, offs, w_stack, pe_stack, k_gain, bsz, seq):
    nc = seq // CMP_STRIDE
    c0 = offs["b_kc"] // NSA_DK
    return pl.pallas_call(
        _nsa_compress_kernel,
        grid=(bsz, 2 * NSA_KV_GROUPS),
        in_specs=[
            pl.BlockSpec((seq, NSA_DK), lambda b, j: (b, c0 + j)),
            pl.BlockSpec((1, CMP_BLOCK, NSA_DK, NSA_DK), lambda b, j: (j // NSA_KV_GROUPS, 0, 0, 0)),
            pl.BlockSpec((1, CMP_BLOCK, NSA_DK), lambda b, j: (j // NSA_KV_GROUPS, 0, 0)),
            pl.BlockSpec((1, NSA_DK), lambda b, j: (0, 0)),
        ],
        out_specs=pl.BlockSpec((1, 1, nc, NSA_DK), lambda b, j: (b, j, 0, 0)),
        out_shape=jax.ShapeDtypeStruct((bsz, 2 * NSA_KV_GROUPS, nc, NSA_DK), bf16),
        scratch_shapes=[pltpu.VMEM((nc + 8, NSA_DK), f32)],
        compiler_params=pltpu.CompilerParams(dimension_semantics=("parallel", "parallel"),
                                             vmem_limit_bytes=V7X_VMEM_LIMIT_BYTES),
        name="nsa_compress",
    )(proj, w_stack, pe_stack, k_gain)


def _split3(x):
    hi = x.astype(bf16)
    r1 = x - hi.astype(f32)
    mid = r1.astype(bf16)
    lo = (r1 - mid.astype(f32)).astype(bf16)
    return hi, mid, lo


def _nsa_attn_kernel(tbl_ref, q_ref, kc_ref, vc_ref, ks_ref, vs_ref, kw_ref, vw_ref, et_ref, mt_ref, bs_ref,
                     bc_ref, gl_ref, o_ref, strip_ref, sc_ref, score_ref, m_ref, l_ref, acc_ref):
    g = pl.program_id(1)
    qi = pl.program_id(2)
    nc = kc_ref.shape[2]
    qb = Q_BLOCK

    @pl.when(qi == 0)
    def _():
        def lookup(bucket):
            v0 = jnp.zeros(bucket.shape, f32)
            v1 = jnp.zeros(bucket.shape, f32)
            for k in range(REL_BUCKETS):
                eq = bucket == k
                v0 = jnp.where(eq, tbl_ref[NSA_HG * g, k], v0)
                v1 = jnp.where(eq, tbl_ref[NSA_HG * g + 1, k], v1)
            return v0, v1

        for c in range(NEAR_TILES):
            sl = slice(c * qb, (c + 1) * qb)
            v0, v1 = lookup(bs_ref[:, sl])
            strip_ref[0, :, sl] = v0
            strip_ref[1, :, sl] = v1
        v0, v1 = lookup(bc_ref[...])
        for hh, v in enumerate((v0, v1)):
            d = v - tbl_ref[NSA_HG * g + hh, REL_BUCKETS - 1]
            hi = d.astype(bf16)
            sc_ref[hh, 0] = hi
            sc_ref[hh, 1] = (d - hi.astype(f32)).astype(bf16)

    far_bias = [tbl_ref[NSA_HG * g + hh, REL_BUCKETS - 1] for hh in range(NSA_HG)]
    q = q_ref[...]
    q2 = jnp.concatenate([q[:, :NSA_DK], q[:, NSA_DK:]], axis=0)

    r_i = lax.broadcasted_iota(jnp.int32, (qb, nc), 0)
    n_i = lax.broadcasted_iota(jnp.int32, (qb, nc), 1)
    ok_c = CMP_STRIDE * n_i <= qb * qi + r_i - (CMP_BLOCK - 1)
    shift = jnp.where(r_i + n_i == CMP_PER_Q * qi + CMP_PER_Q - 1, 1.0, 0.0).astype(bf16)
    kc = kc_ref[0, 0]
    vc = vc_ref[0, 0]
    psum = jnp.zeros((qb, nc), f32)
    o_cmp = []
    for hh in range(NSA_HG):
        s = _dot_nt(q2[hh * qb:(hh + 1) * qb], kc)
        bias = (far_bias[hh] + jnp.dot(sc_ref[hh, 0], shift, preferred_element_type=f32)
                + jnp.dot(sc_ref[hh, 1], shift, preferred_element_type=f32))
        s = jnp.where(ok_c, s + bias, NEG)
        e = jnp.exp(s - jnp.max(s, axis=-1, keepdims=True))
        p = jnp.where(ok_c, e / jnp.sum(e, axis=-1, keepdims=True), 0.0)
        o_cmp.append(jnp.dot(p.astype(bf16), vc, preferred_element_type=f32))
        psum += p

    mt = mt_ref[...]
    imp_t = sum(_dot_nt(mt, part) for part in _split3(psum))
    nblk = imp_t.shape[0]
    j_i = lax.broadcasted_iota(jnp.int32, (nblk, qb), 0)
    l_i = lax.broadcasted_iota(jnp.int32, (nblk, qb), 1)
    cur = SEL_PER_Q * qi + l_i // SEL_BLOCK
    ok_s = j_i <= cur
    forced = ok_s & ((j_i == 0) | (j_i >= cur - 1))
    score = jnp.where(forced, FORCE_SCORE, jnp.where(ok_s, imp_t, -FORCE_SCORE))
    score_ref[...] = score

    def rank_body(jp, cnt):
        row = score_ref[pl.ds(jp, 1), :]
        beats = (row > score) | ((row == score) & (j_i > jp))
        return cnt + jnp.where(beats, 1.0, 0.0)

    cnt = lax.fori_loop(0, SEL_PER_Q * qi + SEL_PER_Q, rank_body, jnp.zeros((nblk, qb), f32))
    pen = jnp.where(cnt < N_SELECT, 0.0, MASK_PEN).T.astype(bf16)
    pen2 = jnp.concatenate([pen, pen], axis=0)

    m_ref[...] = jnp.full(m_ref.shape, MASK_PEN, f32)
    l_ref[...] = jnp.zeros(l_ref.shape, f32)
    acc_ref[...] = jnp.zeros(acc_ref.shape, f32)

    def flash_update(s, v):
        m_old = m_ref[...]
        m_new = jnp.maximum(m_old, jnp.max(s, axis=-1, keepdims=True))
        p = jnp.exp(s - m_new)
        alpha = jnp.exp(m_old - m_new)
        l_ref[...] = alpha * l_ref[...] + jnp.sum(p, axis=-1, keepdims=True)
        acc_ref[...] = alpha * acc_ref[...] + jnp.dot(p.astype(bf16), v, preferred_element_type=f32)
        m_ref[...] = m_new

    row2 = lax.broadcasted_iota(jnp.int32, (NSA_HG * qb, 1), 0)
    far_col = jnp.where(row2 < qb, far_bias[0], far_bias[1])
    near_per_far = SEL_FAR_TILE // qb
    n_far = jnp.maximum(qi - (NEAR_TILES - near_per_far), 0) // near_per_far

    def far_body(kt, carry):
        rows = pl.ds(pl.multiple_of(kt * SEL_FAR_TILE, SEL_FAR_TILE), SEL_FAR_TILE)
        s = _dot_nt(q2, ks_ref[rows, :]) + _dot_nt(pen2, et_ref[rows, :]) + far_col
        flash_update(s, vs_ref[rows, :])
        return carry

    lax.fori_loop(0, n_far, far_body, 0)

    r2 = lax.broadcasted_iota(jnp.int32, (NSA_HG * qb, qb), 0) % qb
    c2 = lax.broadcasted_iota(jnp.int32, (NSA_HG * qb, qb), 1)

    def strip_bias(w):
        sl = slice(w * qb, (w + 1) * qb)
        return jnp.concatenate([strip_ref[0, :, sl], strip_ref[1, :, sl]], axis=0)

    for w in range(NEAR_TILES):
        kt = qi - (NEAR_TILES - 1) + w

        @pl.when(kt >= near_per_far * n_far)
        def _():
            rows = pl.ds(pl.multiple_of(kt * qb, qb), qb)
            s = _dot_nt(q2, ks_ref[rows, :]) + _dot_nt(pen2, et_ref[rows, :]) + strip_bias(w)
            if w == NEAR_TILES - 1:
                s = jnp.where(c2 <= r2, s, NEG)
            flash_update(s, vs_ref[rows, :])

    o_sel = acc_ref[...] / l_ref[...]

    s_win, v_win = [], []
    for w in range(NEAR_TILES - WIN_TILES, NEAR_TILES):
        kt = qi - (NEAR_TILES - 1) + w
        rows = pl.ds(pl.multiple_of(jnp.maximum(kt, 0) * qb, qb), qb)
        s = _dot_nt(q2, kw_ref[rows, :]) + strip_bias(w)
        if w == NEAR_TILES - WIN_TILES:
            valid = (c2 > r2) & (kt >= 0)
        elif w == NEAR_TILES - 1:
            valid = c2 <= r2
        else:
            valid = jnp.broadcast_to(kt >= 0, c2.shape)
        s_win.append(jnp.where(valid, s, NEG))
        v_win.append(vw_ref[rows, :])
    m_w = functools.reduce(jnp.maximum, [jnp.max(s, axis=-1, keepdims=True) for s in s_win])
    e_win = [jnp.exp(s - m_w) for s in s_win]
    den = sum(jnp.sum(e, axis=-1, keepdims=True) for e in e_win)
    o_win = sum(jnp.dot(e.astype(bf16), v, preferred_element_type=f32) for e, v in zip(e_win, v_win)) / den

    gate = jax.nn.sigmoid(gl_ref[...])
    for hh in range(NSA_HG):
        rs = slice(hh * qb, (hh + 1) * qb)
        o_ref[:, hh * NSA_DV:(hh + 1) * NSA_DV] = (gate[:, 3 * hh:3 * hh + 1] * o_cmp[hh]
                                                   + gate[:, 3 * hh + 1:3 * hh + 2] * o_sel[rs]
                                                   + gate[:, 3 * hh + 2:3 * hh + 3] * o_win[rs])


def _rel_bucket(dist):
    n = jnp.maximum(dist, 0)
    nf = jnp.maximum(n, 1).astype(f32)
    large = REL_MAX_EXACT + (jnp.log(nf / REL_MAX_EXACT) / math.log(REL_MAX_DIST / REL_MAX_EXACT)
                             * (REL_BUCKETS - REL_MAX_EXACT)).astype(jnp.int32)
    large = jnp.minimum(large, REL_BUCKETS - 1)
    return jnp.where(n < REL_MAX_EXACT, n, large)


def _nsa_constants(seq):
    nc = seq // CMP_STRIDE
    nblk = max(seq // SEL_BLOCK, LANE)
    r = jnp.arange(Q_BLOCK)[:, None]
    bs = _rel_bucket(STRIP_D0 + r - jnp.arange(STRIP_W)[None, :])
    bc = _rel_bucket(r + CMP_STRIDE * jnp.arange(LANE)[None, :] - (CMP_STRIDE * (CMP_PER_Q - 1) + CMP_BLOCK - 1))
    et = (np.arange(seq)[:, None] // SEL_BLOCK == np.arange(nblk)[None, :]).astype(np.float32)
    ratio = SEL_BLOCK // CMP_STRIDE
    mt = np.zeros((nblk, nc), np.float32)
    for j in range(nblk):
        for i, wt in enumerate(SEL_OVERLAP_WEIGHTS):
            mcol = ratio * j - 1 + i
            if 0 <= mcol < nc:
                mt[j, mcol] = wt
    return bs.astype(jnp.int32), bc.astype(jnp.int32), jnp.asarray(et, bf16), jnp.asarray(mt, bf16)


def _nsa_attention(qn, cmp_kv, ksvs, kwvw, proj, offs, tbl, consts, bsz, seq):
    bs, bc, et, mt = consts
    m = bsz * seq
    n_qb = seq // Q_BLOCK
    nc = seq // CMP_STRIDE
    nblk = et.shape[1]
    g_ = NSA_KV_GROUPS
    kv = lambda col0: pl.BlockSpec((seq, NSA_DK), lambda b, g, i: (b, col0 + g))
    cmp_spec = lambda j0: pl.BlockSpec((1, 1, nc, NSA_DK), lambda b, g, i: (b, j0 + g, 0, 0))
    const2 = lambda shape: pl.BlockSpec(shape, lambda b, g, i: (0, 0))
    gcol = offs["b_g"] // LANE
    return pl.pallas_call(
        _nsa_attn_kernel,
        grid=(bsz, g_, n_qb),
        in_specs=[
            pl.BlockSpec(memory_space=pltpu.SMEM),
            pl.BlockSpec((Q_BLOCK, NSA_HG * NSA_DK), lambda b, g, i: (b * n_qb + i, g)),
            cmp_spec(0), cmp_spec(g_),
            kv(0), kv(g_), kv(0), kv(g_),
            const2((seq, nblk)), const2((nblk, nc)), const2((Q_BLOCK, STRIP_W)), const2((Q_BLOCK, LANE)),
            pl.BlockSpec((Q_BLOCK, LANE), lambda b, g, i: (b * n_qb + i, gcol + g)),
        ],
        out_specs=pl.BlockSpec((Q_BLOCK, NSA_HG * NSA_DV), lambda b, g, i: (b * n_qb + i, g)),
        out_shape=jax.ShapeDtypeStruct((m, NSA_HEADS * NSA_DV), f32),
        scratch_shapes=[
            pltpu.VMEM((NSA_HG, Q_BLOCK, STRIP_W), f32),
            pltpu.VMEM((NSA_HG, 2, Q_BLOCK, LANE), bf16),
            pltpu.VMEM((nblk, Q_BLOCK), f32),
            pltpu.VMEM((NSA_HG * Q_BLOCK, 1), f32),
            pltpu.VMEM((NSA_HG * Q_BLOCK, 1), f32),
            pltpu.VMEM((NSA_HG * Q_BLOCK, NSA_DV), f32),
        ],
        compiler_params=pltpu.CompilerParams(dimension_semantics=("arbitrary", "arbitrary", "arbitrary"),
                                             vmem_limit_bytes=V7X_VMEM_LIMIT_BYTES),
        name="nsa_attention",
    )(tbl, qn, cmp_kv, cmp_kv, ksvs, ksvs, kwvw, kwvw, et, mt, bs, bc, proj)


def _nsa(proj, offs, q_norm, k_norm, pe_k, pe_v, w_ck, w_cv, rel_table, consts, bsz, seq):
    q_gain = q_norm.reshape(1, NSA_DK)
    k_gain = k_norm.reshape(1, NSA_DK)
    qn, ksvs, kwvw = _nsa_prep(proj, offs, q_gain, k_gain)
    cmp_kv = _nsa_compress(proj, offs, jnp.stack([w_ck, w_cv]).astype(bf16), jnp.stack([pe_k, pe_v]), k_gain,
                           bsz, seq)
    return _nsa_attention(qn, cmp_kv, ksvs, kwvw, proj, offs, rel_table.T, consts, bsz, seq)


_IN_NAMES = ("a_q", "a_k", "a_v", "a_z", "a_a", "a_b", "b_q", "b_kc", "b_vc", "b_ks", "b_vs", "b_kw", "b_vw",
             "b_g", "c_z", "c_xbc", "c_dt", "m_gate")


def _in_sizes(d_model):
    return (GDN_HEADS * GDN_DK, GDN_HEADS * GDN_DK, GDN_HEADS * GDN_DV, GDN_HEADS * GDN_DV, GDN_HEADS, GDN_HEADS,
            NSA_HEADS * NSA_DK, NSA_KV_GROUPS * NSA_DK, NSA_KV_GROUPS * NSA_DV, NSA_KV_GROUPS * NSA_DK,
            NSA_KV_GROUPS * NSA_DV, NSA_KV_GROUPS * NSA_DK, NSA_KV_GROUPS * NSA_DV, 3 * NSA_HEADS,
            SSM_INNER, SSM_CONV_CH, SSM_HEADS, 3 * d_model)


_BIG_ORDER = ("m_gate", "a_q", "a_k", "a_v", "c_xbc", "c_z", "a_z", "b_q", "b_kc", "b_vc", "b_ks", "b_vs",
              "b_kw", "b_vw")
_SMALL_ORDER = ("a_a", "a_b", "c_dt")


def _in_layout(d_model, tn=512):
    sizes = dict(zip(_IN_NAMES, _in_sizes(d_model)))
    src_off = dict(zip(_IN_NAMES, np.cumsum((0,) + _in_sizes(d_model))[:-1].tolist()))
    cols, offs = [], {}

    def pad_to_lane():
        cols.extend([-1] * (_round_up(len(cols), LANE) - len(cols)))

    for name in _BIG_ORDER:
        offs[name] = len(cols)
        cols.extend(range(src_off[name], src_off[name] + sizes[name]))
    offs["small"] = len(cols)
    for name in _SMALL_ORDER:
        offs[name] = len(cols)
        cols.extend(range(src_off[name], src_off[name] + sizes[name]))
    pad_to_lane()
    offs["b_g"] = len(cols)
    per_group = 3 * NSA_HG
    for g in range(NSA_KV_GROUPS):
        cols.extend(range(src_off["b_g"] + g * per_group, src_off["b_g"] + (g + 1) * per_group))
        pad_to_lane()
    n_pad = _round_up(len(cols), tn)
    cols.extend([-1] * (n_pad - len(cols)))
    return np.asarray(cols, np.int32), offs, sizes, n_pad


def _prep_w_in(w_in_l, cols):
    w = jnp.take(w_in_l, jnp.asarray(np.maximum(cols, 0)), axis=1)
    return jnp.where(jnp.asarray(cols >= 0)[None, :], w, 0.0).astype(bf16)


def kernel(x, rel_table, g_ffn1, w_up1, w_down1, g_mix, w_in, gdn_conv, gdn_a_log, gdn_dt_bias, gdn_norm,
           nsa_q_norm, nsa_k_norm, nsa_pe_k, nsa_pe_v, nsa_w_ck, nsa_w_cv, ssm_conv_w, ssm_conv_b,
           ssm_dt_bias, ssm_a_log, ssm_d, ssm_norm, p_a, p_b, p_c, w_o, g_ffn2, w_up2, w_down2):
    bsz, seq, d = x.shape
    depth = w_in.shape[0]
    m = bsz * seq
    cols, offs, sizes, _ = _in_layout(d)
    x2 = x.reshape(m, d)
    nsa_consts = _nsa_constants(seq)

    def seg(proj, name):
        return proj[:, offs[name]:offs[name] + sizes[name]].reshape(bsz, seq, sizes[name])

    for l in range(depth):
        wa, wb, wd = _prep_ffn_weights(w_up1[l], w_down1[l])
        x2 = _ffn(x2, g_ffn1[l].reshape(1, d), wa, wb, wd)

        proj = _norm_matmul(x2, g_mix[l].reshape(1, d), _prep_w_in(w_in[l], cols))
        y_a = _jx_gated_deltanet(seg(proj, "a_q"), seg(proj, "a_k"), seg(proj, "a_v"), seg(proj, "a_z"),
                                 seg(proj, "a_a"), seg(proj, "a_b"), gdn_conv[l], gdn_a_log[l], gdn_dt_bias[l],
                                 gdn_norm[l])
        y_b = _nsa(proj, offs, nsa_q_norm[l], nsa_k_norm[l], nsa_pe_k[l], nsa_pe_v[l], nsa_w_ck[l], nsa_w_cv[l],
                   rel_table, nsa_consts, bsz, seq)
        y_c = _jx_mamba2_ssd(seg(proj, "c_z"), seg(proj, "c_xbc"), seg(proj, "c_dt"), ssm_conv_w[l], ssm_conv_b[l],
                             ssm_dt_bias[l], ssm_a_log[l], ssm_d[l], ssm_norm[l])
        x2 = _merge(x2, y_a.reshape(m, -1), y_b.reshape(m, -1), y_c.reshape(m, -1), proj, offs["m_gate"],
                    p_a[l].astype(bf16), p_b[l].astype(bf16), p_c[l].astype(bf16), w_o[l].astype(bf16))

        wa, wb, wd = _prep_ffn_weights(w_up2[l], w_down2[l])
        x2 = _ffn(x2, g_ffn2[l].reshape(1, d), wa, wb, wd)
    return x2.reshape(bsz, seq, d)
```

```python
import functools
import math

import jax
import jax.numpy as jnp
import numpy as np
from jax import lax
from jax.experimental import pallas as pl
from jax.experimental.pallas import tpu as pltpu

EPS = 1e-6
CONV_K = 4

GDN_HEADS = 4
GDN_DK = 128
GDN_DV = 128
GDN_CHUNK = 64

NSA_HEADS = 4
NSA_KV_GROUPS = 2
NSA_DK = 128
NSA_DV = 128
CMP_BLOCK = 32
CMP_STRIDE = 16
SEL_BLOCK = 64
N_SELECT = 16
WINDOW = 512
Q_BLOCK = 128
SEL_OVERLAP_WEIGHTS = (1.0, 2.0, 2.0, 2.0, 1.0)
FORCE_SCORE = 1e4
NEG = -1e30

SSM_HEADS = 16
SSM_HEAD_DIM = 64
SSM_GROUPS = 2
SSM_STATE = 128
SSM_CHUNK = 128
SSM_INNER = SSM_HEADS * SSM_HEAD_DIM
SSM_CONV_CH = SSM_INNER + 2 * SSM_GROUPS * SSM_STATE

REL_BUCKETS = 32
REL_MAX_EXACT = 16
REL_MAX_DIST = 1024

V7X_VMEM_LIMIT_BYTES = 56 * 1024 * 1024
LANE = 128

bf16 = jnp.bfloat16
f32 = jnp.float32


def _round_up(n, m):
    return (n + m - 1) // m * m


def _ffn_kernel(x_ref, g_ref, wa_ref, wb_ref, wd_ref, o_ref, h_ref):
    j = pl.program_id(1)

    @pl.when(j == 0)
    def _():
        x = x_ref[...]
        h = x * lax.rsqrt(jnp.mean(x * x, axis=-1, keepdims=True) + EPS) * g_ref[...]
        h_ref[...] = h.astype(bf16)
        o_ref[...] = x

    h = h_ref[...]
    a = jnp.dot(h, wa_ref[...], preferred_element_type=f32)
    b = jnp.dot(h, wb_ref[...], preferred_element_type=f32)
    act = (0.5 * a * jax.nn.sigmoid(a) * b).astype(bf16)
    o_ref[...] += jnp.dot(act, wd_ref[...], preferred_element_type=f32)


def _ffn(x2, g, wa, wb, wd, *, tm=512, tf=512):
    m, d = x2.shape
    ffp = wa.shape[1]
    return pl.pallas_call(
        _ffn_kernel,
        grid=(m // tm, ffp // tf),
        in_specs=[
            pl.BlockSpec((tm, d), lambda i, j: (i, 0)),
            pl.BlockSpec((1, d), lambda i, j: (0, 0)),
            pl.BlockSpec((d, tf), lambda i, j: (0, j)),
            pl.BlockSpec((d, tf), lambda i, j: (0, j)),
            pl.BlockSpec((tf, d), lambda i, j: (j, 0)),
        ],
        out_specs=pl.BlockSpec((tm, d), lambda i, j: (i, 0)),
        out_shape=jax.ShapeDtypeStruct((m, d), f32),
        scratch_shapes=[pltpu.VMEM((tm, d), bf16)],
        compiler_params=pltpu.CompilerParams(
            dimension_semantics=("parallel", "arbitrary"),
            vmem_limit_bytes=V7X_VMEM_LIMIT_BYTES),
        name="ffn_swiglu",
    )(x2, g, wa, wb, wd)


def _prep_ffn_weights(w_up, w_down, tf=512):
    d, two_ff = w_up.shape
    ff = two_ff // 2
    ffp = _round_up(ff, tf)
    pad = ffp - ff
    wa = jnp.pad(w_up[:, :ff].astype(bf16), ((0, 0), (0, pad)))
    wb = jnp.pad(w_up[:, ff:].astype(bf16), ((0, 0), (0, pad)))
    wd = jnp.pad(w_down.astype(bf16), ((0, pad), (0, 0)))
    return wa, wb, wd


def _norm_matmul_kernel(x_ref, g_ref, w_ref, o_ref, h_ref):
    j = pl.program_id(1)

    @pl.when(j == 0)
    def _():
        x = x_ref[...]
        h = x * lax.rsqrt(jnp.mean(x * x, axis=-1, keepdims=True) + EPS) * g_ref[...]
        h_ref[...] = h.astype(bf16)

    o_ref[...] = jnp.dot(h_ref[...], w_ref[...], preferred_element_type=f32)


def _norm_matmul(x2, g, w, *, tm=512, tn=512):
    m, d = x2.shape
    n = w.shape[1]
    return pl.pallas_call(
        _norm_matmul_kernel,
        grid=(m // tm, n // tn),
        in_specs=[
            pl.BlockSpec((tm, d), lambda i, j: (i, 0)),
            pl.BlockSpec((1, d), lambda i, j: (0, 0)),
            pl.BlockSpec((d, tn), lambda i, j: (0, j)),
        ],
        out_specs=pl.BlockSpec((tm, tn), lambda i, j: (i, j)),
        out_shape=jax.ShapeDtypeStruct((m, n), f32),
        scratch_shapes=[pltpu.VMEM((tm, d), bf16)],
        compiler_params=pltpu.CompilerParams(
            dimension_semantics=("parallel", "arbitrary"),
            vmem_limit_bytes=V7X_VMEM_LIMIT_BYTES),
        name="norm_in_proj",
    )(x2, g, w)


def _merge_kernel(x_ref, ya_ref, yb_ref, yc_ref, ga_ref, gb_ref, gc_ref, pa_ref, pb_ref, pc_ref, wo_ref, o_ref):
    ma = jnp.dot(ya_ref[...].astype(bf16), pa_ref[...], preferred_element_type=f32)
    mb = jnp.dot(yb_ref[...].astype(bf16), pb_ref[...], preferred_element_type=f32)
    mc = jnp.dot(yc_ref[...].astype(bf16), pc_ref[...], preferred_element_type=f32)
    merged = (jax.nn.sigmoid(ga_ref[...]) * ma + jax.nn.sigmoid(gb_ref[...]) * mb
              + jax.nn.sigmoid(gc_ref[...]) * mc)
    o_ref[...] = x_ref[...] + jnp.dot(merged.astype(bf16), wo_ref[...], preferred_element_type=f32)


def _merge(x2, ya, yb, yc, gates, gate_col0, pa, pb, pc, wo, *, tm=256):
    m, d = x2.shape
    gb0 = gate_col0 // d
    const = dict(pipeline_mode=pl.Buffered(1))
    return pl.pallas_call(
        _merge_kernel,
        grid=(m // tm,),
        in_specs=[
            pl.BlockSpec((tm, d), lambda i: (i, 0)),
            pl.BlockSpec((tm, ya.shape[1]), lambda i: (i, 0)),
            pl.BlockSpec((tm, yb.shape[1]), lambda i: (i, 0)),
            pl.BlockSpec((tm, yc.shape[1]), lambda i: (i, 0)),
            pl.BlockSpec((tm, d), lambda i: (i, gb0)),
            pl.BlockSpec((tm, d), lambda i: (i, gb0 + 1)),
            pl.BlockSpec((tm, d), lambda i: (i, gb0 + 2)),
            pl.BlockSpec(pa.shape, lambda i: (0, 0), **const),
            pl.BlockSpec(pb.shape, lambda i: (0, 0), **const),
            pl.BlockSpec(pc.shape, lambda i: (0, 0), **const),
            pl.BlockSpec(wo.shape, lambda i: (0, 0), **const),
        ],
        out_specs=pl.BlockSpec((tm, d), lambda i: (i, 0)),
        out_shape=jax.ShapeDtypeStruct((m, d), f32),
        compiler_params=pltpu.CompilerParams(
            dimension_semantics=("parallel",),
            vmem_limit_bytes=V7X_VMEM_LIMIT_BYTES),
        name="merge_out_proj",
    )(x2, ya, yb, yc, gates, gates, gates, pa, pb, pc, wo)


def _jx_rmsnorm(x, g):
    xf = x.astype(jnp.float32)
    y = xf * lax.rsqrt(jnp.mean(xf * xf, axis=-1, keepdims=True) + EPS)
    return (y * g.astype(jnp.float32)).astype(x.dtype)


def _jx_l2norm(t):
    return t * lax.rsqrt(jnp.sum(t * t, axis=-1, keepdims=True) + EPS)


def _jx_causal_dwconv(x, w):
    return lax.conv_general_dilated(x, w[:, None, :], window_strides=(1,), padding=[(CONV_K - 1, 0)],
                                    dimension_numbers=('NWC', 'WIO', 'NWC'), feature_group_count=x.shape[-1])


def _jx_rel_bucket(dist):
    n = jnp.maximum(dist, 0)
    nf = jnp.maximum(n, 1).astype(jnp.float32)
    large = REL_MAX_EXACT + (jnp.log(nf / REL_MAX_EXACT) / math.log(REL_MAX_DIST / REL_MAX_EXACT)
                             * (REL_BUCKETS - REL_MAX_EXACT)).astype(jnp.int32)
    large = jnp.minimum(large, REL_BUCKETS - 1)
    return jnp.where(n < REL_MAX_EXACT, n, large)


def _jx_gated_deltanet(q, k, v, z, a, b, conv_w, a_log, dt_bias, norm_g):
    Bsz, T, _ = q.shape
    H, C = GDN_HEADS, GDN_CHUNK
    n_c = T // C
    qk_w = H * GDN_DK
    qkv = jax.nn.silu(_jx_causal_dwconv(jnp.concatenate([q, k, v], axis=-1), conv_w)).astype(f32)

    def heads(t):
        return t.reshape(Bsz, n_c, C, H, -1).transpose(0, 3, 1, 2, 4)

    qh = _jx_l2norm(heads(qkv[..., :qk_w])) * GDN_DK ** -0.5
    kh = _jx_l2norm(heads(qkv[..., qk_w:2 * qk_w]))
    vh = heads(qkv[..., 2 * qk_w:])
    beta = jax.nn.sigmoid(b.astype(f32)).reshape(Bsz, n_c, C, H).transpose(0, 3, 1, 2)
    g = -jnp.exp(a_log.astype(f32)) * jax.nn.softplus(a.astype(f32) + dt_bias.astype(f32))
    gc = jnp.cumsum(g.reshape(Bsz, n_c, C, H).transpose(0, 3, 1, 2), axis=-1)
    causal = jnp.tril(jnp.ones((C, C), bool))
    strict = jnp.tril(jnp.ones((C, C), bool), -1)
    diff = gc[..., :, None] - gc[..., None, :]
    decay = jnp.where(causal, jnp.exp(jnp.where(causal, diff, 0.0)), 0.0)
    kb = kh * beta[..., None]
    m = jnp.where(strict, jnp.einsum('bhncd,bhnsd->bhncs', kb, kh) * decay, 0.0) + jnp.eye(C, dtype=f32)
    rhs = jnp.concatenate([vh * beta[..., None], kb * jnp.exp(gc)[..., None]], axis=-1)
    sol = lax.linalg.triangular_solve(m, rhs, left_side=True, lower=True, unit_diagonal=True)
    u, w = sol[..., :GDN_DV], sol[..., GDN_DV:]
    attn = jnp.einsum('bhncd,bhnsd->bhncs', qh, kh) * decay
    q_dec = qh * jnp.exp(gc)[..., None]
    k_dec = kh * jnp.exp(gc[..., -1:] - gc)[..., None]
    g_last = jnp.exp(gc[..., -1])

    def step(state, inp):
        u_c, w_c, a_c, qd_c, kd_c, gl_c = inp
        v_new = u_c - jnp.einsum('bhck,bhkv->bhcv', w_c, state)
        o_c = jnp.einsum('bhck,bhkv->bhcv', qd_c, state) + jnp.einsum('bhcs,bhsv->bhcv', a_c, v_new)
        state = state * gl_c[..., None, None] + jnp.einsum('bhck,bhcv->bhkv', kd_c, v_new)
        return state, o_c

    xs = (jnp.moveaxis(u, 2, 0), jnp.moveaxis(w, 2, 0), jnp.moveaxis(attn, 2, 0),
          jnp.moveaxis(q_dec, 2, 0), jnp.moveaxis(k_dec, 2, 0), jnp.moveaxis(g_last, 2, 0))
    _, o = lax.scan(step, jnp.zeros((Bsz, H, GDN_DK, GDN_DV), f32), xs)
    o = o.transpose(1, 0, 3, 2, 4).reshape(Bsz, T, H, GDN_DV)
    o = _jx_rmsnorm(o, norm_g) * jax.nn.silu(z.astype(f32).reshape(Bsz, T, H, GDN_DV))
    return o.reshape(Bsz, T, H * GDN_DV).astype(z.dtype)


def _jx_nsa_attention(q_in, kc, vc, ks, vs, kw, vw, gate_logits, q_norm, k_norm, pe_k, pe_v, w_ck, w_cv, rel_table):
    Bsz, T, _ = q_in.shape
    G, Hg = NSA_KV_GROUPS, NSA_HEADS // NSA_KV_GROUPS
    n_cmp = T // CMP_STRIDE - 1
    n_sel = T // SEL_BLOCK
    n_top = min(N_SELECT, n_sel)
    n_qb = T // Q_BLOCK
    ratio = SEL_BLOCK // CMP_STRIDE

    q = _jx_rmsnorm(q_in.reshape(Bsz, T, NSA_HEADS, NSA_DK), q_norm).astype(f32) * NSA_DK ** -0.5
    q_blocks = q.reshape(Bsz, n_qb, Q_BLOCK, G, Hg, NSA_DK).transpose(1, 0, 3, 4, 2, 5)

    def groups(t):
        return t.reshape(Bsz, T, G, -1).astype(f32)

    def compress(t, pe, w):
        c = groups(t).reshape(Bsz, T // CMP_STRIDE, CMP_STRIDE, G, -1)
        blocks = jnp.concatenate([c[:, :-1], c[:, 1:]], axis=2)
        return jnp.einsum('bnlgd,lde->bgne', blocks + pe.astype(f32)[:, None, :], w.astype(f32))

    k_cmp = _jx_rmsnorm(compress(kc, pe_k, w_ck), k_norm)
    v_cmp = compress(vc, pe_v, w_cv)
    k_sel = _jx_rmsnorm(groups(ks), k_norm).reshape(Bsz, n_sel, SEL_BLOCK, G, NSA_DK).transpose(0, 3, 1, 2, 4)
    v_sel = groups(vs).reshape(Bsz, n_sel, SEL_BLOCK, G, NSA_DV).transpose(0, 3, 1, 2, 4)
    pad = ((0, 0), (0, 0), (WINDOW, 0), (0, 0))
    k_win = jnp.pad(_jx_rmsnorm(groups(kw), k_norm).transpose(0, 2, 1, 3), pad)
    v_win = jnp.pad(groups(vw).transpose(0, 2, 1, 3), pad)

    table = rel_table.astype(f32).T.reshape(G, Hg, REL_BUCKETS)
    cmp_end = jnp.arange(n_cmp) * CMP_STRIDE + CMP_BLOCK - 1
    blk = jnp.arange(n_sel)
    bi = jnp.arange(Bsz)[:, None, None, None]
    gi = jnp.arange(G)[None, :, None, None]
    gi6 = jnp.arange(G)[None, :, None, None, None, None]
    hi6 = jnp.arange(Hg)[None, None, :, None, None, None]

    def block(args):
        qb, qi = args
        t = qi * Q_BLOCK + jnp.arange(Q_BLOCK)
        dist = t[:, None] - cmp_end[None, :]
        ok = dist >= 0
        s = jnp.einsum('bghqd,bgnd->bghqn', qb, k_cmp) + table[:, :, _jx_rel_bucket(dist)]
        p_cmp = jnp.where(ok, jax.nn.softmax(jnp.where(ok, s, NEG), axis=-1), 0.0)
        o_cmp = jnp.einsum('bghqn,bgnd->bghqd', p_cmp, v_cmp)
        imp_c = jnp.pad(p_cmp.sum(axis=2), ((0, 0), (0, 0), (0, 0), (1, ratio * n_sel - n_cmp)))
        imp = sum(wt * imp_c[..., j:j + ratio * n_sel:ratio] for j, wt in enumerate(SEL_OVERLAP_WEIGHTS))
        cur = t // SEL_BLOCK
        ok_s = blk[None, :] <= cur[:, None]
        forced = ok_s & ((blk[None, :] == 0) | (blk[None, :] >= cur[:, None] - 1))
        score = jnp.where(forced, FORCE_SCORE, jnp.where(ok_s, imp, -FORCE_SCORE))
        _, idx = lax.top_k(score, n_top)
        kg = k_sel[bi, gi, idx]
        vg = v_sel[bi, gi, idx]
        dist = t[:, None, None] - (idx[..., None] * SEL_BLOCK + jnp.arange(SEL_BLOCK))
        ok = (dist >= 0)[:, :, None]
        s = jnp.einsum('bghqd,bgqkld->bghqkl', qb, kg) + table[gi6, hi6, _jx_rel_bucket(dist)[:, :, None]]
        s = jnp.where(ok, s, NEG).reshape(Bsz, G, Hg, Q_BLOCK, n_top * SEL_BLOCK)
        p = jax.nn.softmax(s, axis=-1).reshape(Bsz, G, Hg, Q_BLOCK, n_top, SEL_BLOCK)
        o_sel = jnp.einsum('bghqkl,bgqkld->bghqd', p, vg)
        kwb = lax.dynamic_slice_in_dim(k_win, qi * Q_BLOCK, Q_BLOCK + WINDOW, axis=2)
        vwb = lax.dynamic_slice_in_dim(v_win, qi * Q_BLOCK, Q_BLOCK + WINDOW, axis=2)
        pos = qi * Q_BLOCK - WINDOW + jnp.arange(Q_BLOCK + WINDOW)
        dist = t[:, None] - pos[None, :]
        ok = (dist >= 0) & (dist < WINDOW) & (pos >= 0)[None, :]
        s = jnp.einsum('bghqd,bgkd->bghqk', qb, kwb) + table[:, :, _jx_rel_bucket(dist)]
        p = jax.nn.softmax(jnp.where(ok, s, NEG), axis=-1)
        o_win = jnp.einsum('bghqk,bgkd->bghqd', p, vwb)
        return jnp.stack([o_cmp, o_sel, o_win], axis=-2)

    out = lax.map(block, (q_blocks, jnp.arange(n_qb)))
    out = out.transpose(1, 0, 4, 2, 3, 5, 6).reshape(Bsz, T, NSA_HEADS, 3, NSA_DV)
    gates = jax.nn.sigmoid(gate_logits.astype(f32)).reshape(Bsz, T, NSA_HEADS, 3)
    o = jnp.einsum('bthr,bthrd->bthd', gates, out)
    return o.reshape(Bsz, T, NSA_HEADS * NSA_DV).astype(q_in.dtype)


def _jx_mamba2_ssd(z, xbc, dt, conv_w, conv_b, dt_bias, a_log, d_skip, norm_g):
    Bsz, T, _ = z.shape
    H, P, G, N, L = SSM_HEADS, SSM_HEAD_DIM, SSM_GROUPS, SSM_STATE, SSM_CHUNK
    n_c = T // L
    xbc = jax.nn.silu(_jx_causal_dwconv(xbc, conv_w) + conv_b).astype(f32)
    xs = xbc[..., :SSM_INNER].reshape(Bsz, n_c, L, H, P)
    bm = jnp.repeat(xbc[..., SSM_INNER:SSM_INNER + G * N].reshape(Bsz, n_c, L, G, N), H // G, axis=3)
    cm = jnp.repeat(xbc[..., SSM_INNER + G * N:].reshape(Bsz, n_c, L, G, N), H // G, axis=3)
    dt = jax.nn.softplus(dt.astype(f32) + dt_bias.astype(f32)).reshape(Bsz, n_c, L, H)
    ac = jnp.cumsum(dt * (-jnp.exp(a_log.astype(f32))), axis=2)
    xdt = xs * dt[..., None]
    ach = jnp.swapaxes(ac, 2, 3)
    causal = jnp.tril(jnp.ones((L, L), bool))
    seg = ach[..., :, None] - ach[..., None, :]
    decay = jnp.where(causal, jnp.exp(jnp.where(causal, seg, 0.0)), 0.0)
    scores = jnp.einsum('bclhn,bcshn->bchls', cm, bm) * decay
    y = jnp.einsum('bchls,bcshp->bclhp', scores, xdt)
    states = jnp.einsum('bclhn,bclhp->bchpn', bm * jnp.exp(ac[:, :, -1:, :] - ac)[..., None], xdt)
    chunk_decay = jnp.exp(ac[:, :, -1, :])

    def step(s, inp):
        st, cd = inp
        return s * cd[..., None, None] + st, s

    _, s_in = lax.scan(step, jnp.zeros((Bsz, H, P, N), f32),
                       (jnp.moveaxis(states, 1, 0), jnp.moveaxis(chunk_decay, 1, 0)))
    s_in = jnp.moveaxis(s_in, 0, 1)
    y = y + jnp.einsum('bclhn,bchpn->bclhp', cm * jnp.exp(ac)[..., None], s_in) + xs * d_skip.astype(f32)[:, None]
    y = y.reshape(Bsz, T, SSM_INNER) * jax.nn.silu(z.astype(f32))
    y = y.reshape(Bsz, T, G, SSM_INNER // G)
    y = y * lax.rsqrt(jnp.mean(y * y, axis=-1, keepdims=True) + EPS)
    return (y.reshape(Bsz, T, SSM_INNER) * norm_g.astype(f32)).astype(z.dtype)


NSA_HG = NSA_HEADS // NSA_KV_GROUPS
SEL_FAR_TILE = 512
NEAR_TILES = 11
STRIP_W = NEAR_TILES * Q_BLOCK
STRIP_D0 = (NEAR_TILES - 1) * Q_BLOCK
WIN_TILES = WINDOW // Q_BLOCK + 1
CMP_PER_Q = Q_BLOCK // CMP_STRIDE
SEL_PER_Q = Q_BLOCK // SEL_BLOCK
MASK_PEN = -1e30


def _dot_nt(a, b):
    return lax.dot_general(a, b, (((1,), (1,)), ((), ())), preferred_element_type=f32)


def _lane_rms(x, gain):
    return x * lax.rsqrt(jnp.mean(x * x, axis=-1, keepdims=True) + EPS) * gain


def _nsa_prep_kernel(q_ref, ks_ref, kw_ref, qg_ref, kg_ref, qo_ref, kso_ref, kwo_ref):
    qg = qg_ref[...] * (NSA_DK ** -0.5)
    kg = kg_ref[...]
    for h in range(NSA_HEADS):
        sl = slice(h * NSA_DK, (h + 1) * NSA_DK)
        qo_ref[:, sl] = _lane_rms(q_ref[:, sl], qg).astype(bf16)
    kw_ = NSA_KV_GROUPS * NSA_DK
    for src, dst in ((ks_ref, kso_ref), (kw_ref, kwo_ref)):
        for g in range(NSA_KV_GROUPS):
            sl = slice(g * NSA_DK, (g + 1) * NSA_DK)
            dst[:, sl] = _lane_rms(src[:, sl], kg).astype(bf16)
        dst[:, kw_:] = src[:, kw_:].astype(bf16)


def _nsa_prep(proj, offs, q_gain, k_gain, *, tm=512):
    m = proj.shape[0]
    w = NSA_HEADS * NSA_DK
    blk = lambda name: pl.BlockSpec((tm, w), lambda i, c=offs[name] // w: (i, c))
    out = jax.ShapeDtypeStruct((m, w), bf16)
    return pl.pallas_call(
        _nsa_prep_kernel,
        grid=(m // tm,),
        in_specs=[blk("b_q"), blk("b_ks"), blk("b_kw"),
                  pl.BlockSpec((1, NSA_DK), lambda i: (0, 0)), pl.BlockSpec((1, NSA_DK), lambda i: (0, 0))],
        out_specs=[pl.BlockSpec((tm, w), lambda i: (i, 0))] * 3,
        out_shape=[out, out, out],
        compiler_params=pltpu.CompilerParams(dimension_semantics=("parallel",),
                                             vmem_limit_bytes=V7X_VMEM_LIMIT_BYTES),
        name="nsa_prep",
    )(proj, proj, proj, q_gain, k_gain)


def _nsa_compress_kernel(x_ref, w_ref, pe_ref, kg_ref, o_ref, hi_ref):
    j = pl.program_id(1)
    nc = o_ref.shape[2]
    lo = jnp.zeros((nc, NSA_DK), f32)
    hi = jnp.zeros((nc, NSA_DK), f32)
    for l in range(CMP_STRIDE):
        rows = x_ref[pl.ds(l, nc, stride=CMP_STRIDE), :]
        lo += jnp.dot((rows + pe_ref[0, l:l + 1, :]).astype(bf16), w_ref[0, l], preferred_element_type=f32)
        hi += jnp.dot((rows + pe_ref[0, CMP_STRIDE + l:CMP_STRIDE + l + 1, :]).astype(bf16),
                      w_ref[0, CMP_STRIDE + l], preferred_element_type=f32)
    hi_ref[pl.ds(0, nc), :] = hi
    hi_ref[pl.ds(nc, 8), :] = jnp.zeros((8, NSA_DK), f32)
    c = lo + hi_ref[pl.ds(1, nc), :]
    normed = _lane_rms(c, kg_ref[...])
    o_ref[0, 0] = jnp.where(j < NSA_KV_GROUPS, normed, c).astype(bf16)


def _nsa_compress(proj, offs, w_stack, pe_stack, k_gain, bsz, seq):
    nc = seq // CMP_STRIDE
    c0 = offs["b_kc"] // NSA_DK
    return pl.pallas_call(
        _nsa_compress_kernel,
        grid=(bsz, 2 * NSA_KV_GROUPS),
        in_specs=[
            pl.BlockSpec((seq, NSA_DK), lambda b, j: (b, c0 + j)),
            pl.BlockSpec((1, CMP_BLOCK, NSA_DK, NSA_DK), lambda b, j: (j // NSA_KV_GROUPS, 0, 0, 0)),
            pl.BlockSpec((1, CMP_BLOCK, NSA_DK), lambda b, j: (j // NSA_KV_GROUPS, 0, 0)),
            pl.BlockSpec((1, NSA_DK), lambda b, j: (0, 0)),
        ],
        out_specs=pl.BlockSpec((1, 1, nc, NSA_DK), lambda b, j: (b, j, 0, 0)),
        out_shape=jax.ShapeDtypeStruct((bsz, 2 * NSA_KV_GROUPS, nc, NSA_DK), bf16),
        scratch_shapes=[pltpu.VMEM((nc + 8, NSA_DK), f32)],
        compiler_params=pltpu.CompilerParams(dimension_semantics=("parallel", "parallel"),
                                             vmem_limit_bytes=V7X_VMEM_LIMIT_BYTES),
        name="nsa_compress",
    )(proj, w_stack, pe_stack, k_gain)


def _split3(x):
    hi = x.astype(bf16)
    r1 = x - hi.astype(f32)
    mid = r1.astype(bf16)
    lo = (r1 - mid.astype(f32)).astype(bf16)
    return hi, mid, lo


def _nsa_attn_kernel(tbl_ref, q_ref, kc_ref, vc_ref, ks_ref, vs_ref, kw_ref, vw_ref, et_ref, mt_ref, bs_ref,
                     bc_ref, gl_ref, o_ref, strip_ref, sc_ref, score_ref, m_ref, l_ref, acc_ref):
    g = pl.program_id(1)
    qi = pl.program_id(2)
    nc = kc_ref.shape[2]
    qb = Q_BLOCK

    @pl.when(qi == 0)
    def _():
        def lookup(bucket):
            v0 = jnp.zeros(bucket.shape, f32)
            v1 = jnp.zeros(bucket.shape, f32)
            for k in range(REL_BUCKETS):
                eq = bucket == k
                v0 = jnp.where(eq, tbl_ref[NSA_HG * g, k], v0)
                v1 = jnp.where(eq, tbl_ref[NSA_HG * g + 1, k], v1)
            return v0, v1

        for c in range(NEAR_TILES):
            sl = slice(c * qb, (c + 1) * qb)
            v0, v1 = lookup(bs_ref[:, sl])
            strip_ref[0, :, sl] = v0
            strip_ref[1, :, sl] = v1
        v0, v1 = lookup(bc_ref[...])
        for hh, v in enumerate((v0, v1)):
            d = v - tbl_ref[NSA_HG * g + hh, REL_BUCKETS - 1]
            hi = d.astype(bf16)
            sc_ref[hh, 0] = hi
            sc_ref[hh, 1] = (d - hi.astype(f32)).astype(bf16)

    far_bias = [tbl_ref[NSA_HG * g + hh, REL_BUCKETS - 1] for hh in range(NSA_HG)]
    q = q_ref[...]
    q2 = jnp.concatenate([q[:, :NSA_DK], q[:, NSA_DK:]], axis=0)

    r_i = lax.broadcasted_iota(jnp.int32, (qb, nc), 0)
    n_i = lax.broadcasted_iota(jnp.int32, (qb, nc), 1)
    ok_c = CMP_STRIDE * n_i <= qb * qi + r_i - (CMP_BLOCK - 1)
    shift = jnp.where(r_i + n_i == CMP_PER_Q * qi + CMP_PER_Q - 1, 1.0, 0.0).astype(bf16)
    kc = kc_ref[0, 0]
    vc = vc_ref[0, 0]
    psum = jnp.zeros((qb, nc), f32)
    o_cmp = []
    for hh in range(NSA_HG):
        s = _dot_nt(q2[hh * qb:(hh + 1) * qb], kc)
        bias = (far_bias[hh] + jnp.dot(sc_ref[hh, 0], shift, preferred_element_type=f32)
                + jnp.dot(sc_ref[hh, 1], shift, preferred_element_type=f32))
        s = jnp.where(ok_c, s + bias, NEG)
        e = jnp.exp(s - jnp.max(s, axis=-1, keepdims=True))
        p = jnp.where(ok_c, e / jnp.sum(e, axis=-1, keepdims=True), 0.0)
        o_cmp.append(jnp.dot(p.astype(bf16), vc, preferred_element_type=f32))
        psum += p

    mt = mt_ref[...]
    imp_t = sum(_dot_nt(mt, part) for part in _split3(psum))
    nblk = imp_t.shape[0]
    j_i = lax.broadcasted_iota(jnp.int32, (nblk, qb), 0)
    l_i = lax.broadcasted_iota(jnp.int32, (nblk, qb), 1)
    cur = SEL_PER_Q * qi + l_i // SEL_BLOCK
    ok_s = j_i <= cur
    forced = ok_s & ((j_i == 0) | (j_i >= cur - 1))
    score = jnp.where(forced, FORCE_SCORE, jnp.where(ok_s, imp_t, -FORCE_SCORE))
    score_ref[...] = score

    def rank_body(jp, cnt):
        row = score_ref[pl.ds(jp, 1), :]
        beats = (row > score) | ((row == score) & (j_i > jp))
        return cnt + jnp.where(beats, 1.0, 0.0)

    cnt = lax.fori_loop(0, SEL_PER_Q * qi + SEL_PER_Q, rank_body, jnp.zeros((nblk, qb), f32))
    pen = jnp.where(cnt < N_SELECT, 0.0, MASK_PEN).T.astype(bf16)
    pen2 = jnp.concatenate([pen, pen], axis=0)

    m_ref[...] = jnp.full(m_ref.shape, MASK_PEN, f32)
    l_ref[...] = jnp.zeros(l_ref.shape, f32)
    acc_ref[...] = jnp.zeros(acc_ref.shape, f32)

    def flash_update(s, v):
        m_old = m_ref[...]
        m_new = jnp.maximum(m_old, jnp.max(s, axis=-1, keepdims=True))
        p = jnp.exp(s - m_new)
        alpha = jnp.exp(m_old - m_new)
        l_ref[...] = alpha * l_ref[...] + jnp.sum(p, axis=-1, keepdims=True)
        acc_ref[...] = alpha * acc_ref[...] + jnp.dot(p.astype(bf16), v, preferred_element_type=f32)
        m_ref[...] = m_new

    row2 = lax.broadcasted_iota(jnp.int32, (NSA_HG * qb, 1), 0)
    far_col = jnp.where(row2 < qb, far_bias[0], far_bias[1])
    near_per_far = SEL_FAR_TILE // qb
    n_far = jnp.maximum(qi - (NEAR_TILES - near_per_far), 0) // near_per_far

    def far_body(kt, carry):
        rows = pl.ds(pl.multiple_of(kt * SEL_FAR_TILE, SEL_FAR_TILE), SEL_FAR_TILE)
        s = _dot_nt(q2, ks_ref[rows, :]) + _dot_nt(pen2, et_ref[rows, :]) + far_col
        flash_update(s, vs_ref[rows, :])
        return carry

    lax.fori_loop(0, n_far, far_body, 0)

    r2 = lax.broadcasted_iota(jnp.int32, (NSA_HG * qb, qb), 0) % qb
    c2 = lax.broadcasted_iota(jnp.int32, (NSA_HG * qb, qb), 1)

    def strip_bias(w):
        sl = slice(w * qb, (w + 1) * qb)
        return jnp.concatenate([strip_ref[0, :, sl], strip_ref[1, :, sl]], axis=0)

    for w in range(NEAR_TILES):
        kt = qi - (NEAR_TILES - 1) + w

        @pl.when(kt >= near_per_far * n_far)
        def _():
            rows = pl.ds(pl.multiple_of(kt * qb, qb), qb)
            s = _dot_nt(q2, ks_ref[rows, :]) + _dot_nt(pen2, et_ref[rows, :]) + strip_bias(w)
            if w == NEAR_TILES - 1:
                s = jnp.where(c2 <= r2, s, NEG)
            flash_update(s, vs_ref[rows, :])

    o_sel = acc_ref[...] / l_ref[...]

    s_win, v_win = [], []
    for w in range(NEAR_TILES - WIN_TILES, NEAR_TILES):
        kt = qi - (NEAR_TILES - 1) + w
        rows = pl.ds(pl.multiple_of(jnp.maximum(kt, 0) * qb, qb), qb)
        s = _dot_nt(q2, kw_ref[rows, :]) + strip_bias(w)
        if w == NEAR_TILES - WIN_TILES:
            valid = (c2 > r2) & (kt >= 0)
        elif w == NEAR_TILES - 1:
            valid = c2 <= r2
        else:
            valid = jnp.broadcast_to(kt >= 0, c2.shape)
        s_win.append(jnp.where(valid, s, NEG))
        v_win.append(vw_ref[rows, :])
    m_w = functools.reduce(jnp.maximum, [jnp.max(s, axis=-1, keepdims=True) for s in s_win])
    e_win = [jnp.exp(s - m_w) for s in s_win]
    den = sum(jnp.sum(e, axis=-1, keepdims=True) for e in e_win)
    o_win = sum(jnp.dot(e.astype(bf16), v, preferred_element_type=f32) for e, v in zip(e_win, v_win)) / den

    gate = jax.nn.sigmoid(gl_ref[...])
    for hh in range(NSA_HG):
        rs = slice(hh * qb, (hh + 1) * qb)
        o_ref[:, hh * NSA_DV:(hh + 1) * NSA_DV] = (gate[:, 3 * hh:3 * hh + 1] * o_cmp[hh]
                                                   + gate[:, 3 * hh + 1:3 * hh + 2] * o_sel[rs]
                                                   + gate[:, 3 * hh + 2:3 * hh + 3] * o_win[rs])


def _rel_bucket(dist):
    n = jnp.maximum(dist, 0)
    nf = jnp.maximum(n, 1).astype(f32)
    large = REL_MAX_EXACT + (jnp.log(nf / REL_MAX_EXACT) / math.log(REL_MAX_DIST / REL_MAX_EXACT)
                             * (REL_BUCKETS - REL_MAX_EXACT)).astype(jnp.int32)
    large = jnp.minimum(large, REL_BUCKETS - 1)
    return jnp.where(n < REL_MAX_EXACT, n, large)


def _nsa_constants(seq):
    nc = seq // CMP_STRIDE
    nblk = max(seq // SEL_BLOCK, LANE)
    r = jnp.arange(Q_BLOCK)[:, None]
    bs = _rel_bucket(STRIP_D0 + r - jnp.arange(STRIP_W)[None, :])
    bc = _rel_bucket(r + CMP_STRIDE * jnp.arange(LANE)[None, :] - (CMP_STRIDE * (CMP_PER_Q - 1) + CMP_BLOCK - 1))
    et = (np.arange(seq)[:, None] // SEL_BLOCK == np.arange(nblk)[None, :]).astype(np.float32)
    ratio = SEL_BLOCK // CMP_STRIDE
    mt = np.zeros((nblk, nc), np.float32)
    for j in range(nblk):
        for i, wt in enumerate(SEL_OVERLAP_WEIGHTS):
            mcol = ratio * j - 1 + i
            if 0 <= mcol < nc:
                mt[j, mcol] = wt
    return bs.astype(jnp.int32), bc.astype(jnp.int32), jnp.asarray(et, bf16), jnp.asarray(mt, bf16)


def _nsa_attention(qn, cmp_kv, ksvs, kwvw, proj, offs, tbl, consts, bsz, seq):
    bs, bc, et, mt = consts
    m = bsz * seq
    n_qb = seq // Q_BLOCK
    nc = seq // CMP_STRIDE
    nblk = et.shape[1]
    g_ = NSA_KV_GROUPS
    kv = lambda col0: pl.BlockSpec((seq, NSA_DK), lambda b, g, i: (b, col0 + g))
    cmp_spec = lambda j0: pl.BlockSpec((1, 1, nc, NSA_DK), lambda b, g, i: (b, j0 + g, 0, 0))
    const2 = lambda shape: pl.BlockSpec(shape, lambda b, g, i: (0, 0))
    gcol = offs["b_g"] // LANE
    return pl.pallas_call(
        _nsa_attn_kernel,
        grid=(bsz, g_, n_qb),
        in_specs=[
            pl.BlockSpec(memory_space=pltpu.SMEM),
            pl.BlockSpec((Q_BLOCK, NSA_HG * NSA_DK), lambda b, g, i: (b * n_qb + i, g)),
            cmp_spec(0), cmp_spec(g_),
            kv(0), kv(g_), kv(0), kv(g_),
            const2((seq, nblk)), const2((nblk, nc)), const2((Q_BLOCK, STRIP_W)), const2((Q_BLOCK, LANE)),
            pl.BlockSpec((Q_BLOCK, LANE), lambda b, g, i: (b * n_qb + i, gcol + g)),
        ],
        out_specs=pl.BlockSpec((Q_BLOCK, NSA_HG * NSA_DV), lambda b, g, i: (b * n_qb + i, g)),
        out_shape=jax.ShapeDtypeStruct((m, NSA_HEADS * NSA_DV), f32),
        scratch_shapes=[
            pltpu.VMEM((NSA_HG, Q_BLOCK, STRIP_W), f32),
            pltpu.VMEM((NSA_HG, 2, Q_BLOCK, LANE), bf16),
            pltpu.VMEM((nblk, Q_BLOCK), f32),
            pltpu.VMEM((NSA_HG * Q_BLOCK, 1), f32),
            pltpu.VMEM((NSA_HG * Q_BLOCK, 1), f32),
            pltpu.VMEM((NSA_HG * Q_BLOCK, NSA_DV), f32),
        ],
        compiler_params=pltpu.CompilerParams(dimension_semantics=("arbitrary", "arbitrary", "arbitrary"),
                                             vmem_limit_bytes=V7X_VMEM_LIMIT_BYTES),
        name="nsa_attention",
    )(tbl, qn, cmp_kv, cmp_kv, ksvs, ksvs, kwvw, kwvw, et, mt, bs, bc, proj)


def _nsa(proj, offs, q_norm, k_norm, pe_k, pe_v, w_ck, w_cv, rel_table, consts, bsz, seq):
    q_gain = q_norm.reshape(1, NSA_DK)
    k_gain = k_norm.reshape(1, NSA_DK)
    qn, ksvs, kwvw = _nsa_prep(proj, offs, q_gain, k_gain)
    cmp_kv = _nsa_compress(proj, offs, jnp.stack([w_ck, w_cv]).astype(bf16), jnp.stack([pe_k, pe_v]), k_gain,
                           bsz, seq)
    return _nsa_attention(qn, cmp_kv, ksvs, kwvw, proj, offs, rel_table.T, consts, bsz, seq)


SMALL_A_LANE = 0
SMALL_B_LANE = GDN_HEADS
SMALL_DT_LANE = 2 * GDN_HEADS
CONV_HALO = 8


def _softplus(x):
    return jnp.maximum(x, 0.0) + jnp.log1p(jnp.exp(-jnp.abs(x)))


def _exact_dot(a_f32, b_01):
    return sum(jnp.dot(part, b_01, preferred_element_type=f32) for part in _split3(a_f32))


def _causal_conv_silu(x_ref, tail_ref, buf_ref, w_ref, bias):
    rows = x_ref.shape[0]
    buf_ref[pl.ds(0, CONV_HALO), :] = tail_ref[...]
    buf_ref[pl.ds(CONV_HALO, rows), :] = x_ref[...]
    tail_ref[...] = x_ref[pl.ds(rows - CONV_HALO, CONV_HALO), :]
    acc = bias
    for k in range(CONV_K):
        acc = acc + w_ref[k:k + 1, :] * buf_ref[pl.ds(CONV_HALO - (CONV_K - 1) + k, rows), :]
    return acc * jax.nn.sigmoid(acc)


SSM_PAIR = LANE // SSM_HEAD_DIM
SSM_HEADS_PER_GROUP = SSM_HEADS // SSM_GROUPS


def _ssd_kernel(xbc_ref, z_ref, sm_ref, cw_ref, cb_ref, dtb_ref, alog_ref, dvec_ref, ng_ref, tri_ref, exp_ref,
                o_ref, tail_ref, buf_ref, state_ref):
    L = SSM_CHUNK
    N = SSM_STATE

    @pl.when(pl.program_id(1) == 0)
    def _():
        tail_ref[...] = jnp.zeros(tail_ref.shape, f32)
        state_ref[...] = jnp.zeros(state_ref.shape, f32)

    xbc = _causal_conv_silu(xbc_ref, tail_ref, buf_ref, cw_ref, cb_ref[...])
    xs = xbc[:, :SSM_INNER]
    dt = _softplus(sm_ref[...] + dtb_ref[...])
    a = dt * (-jnp.exp(alog_ref[...]))
    tri = tri_ref[...]
    ac = _exact_dot_left(tri, a)
    ac_t = ac.T
    eac = jnp.exp(ac)
    w_t = jnp.exp(ac_t[:, L - 1:L] - ac_t)
    chunk_decay = jnp.exp(ac[L - 1:L, :])
    xdt = xs * _exact_dot(dt, exp_ref[...])

    row = lax.broadcasted_iota(jnp.int32, (L, L), 0)
    col = lax.broadcasted_iota(jnp.int32, (L, L), 1)
    causal = row >= col
    lane = lax.broadcasted_iota(jnp.int32, (L, LANE), 1)
    first_half = lane < SSM_HEAD_DIM

    ys = []
    for g in range(SSM_GROUPS):
        bg = xbc[:, SSM_INNER + g * N:SSM_INNER + (g + 1) * N]
        cg = xbc[:, SSM_INNER + SSM_GROUPS * N + g * N:SSM_INNER + SSM_GROUPS * N + (g + 1) * N]
        cb = _dot_nt(cg.astype(bf16), bg.astype(bf16))
        bg_t = bg.T
        for i in range(g * SSM_HEADS_PER_GROUP // SSM_PAIR, (g + 1) * SSM_HEADS_PER_GROUP // SSM_PAIR):
            xdt_pair = xdt[:, i * LANE:(i + 1) * LANE].astype(bf16)
            outs = []
            for hh in range(SSM_PAIR):
                h = SSM_PAIR * i + hh
                c = SMALL_DT_LANE + h
                seg = ac[:, c:c + 1] - ac_t[c:c + 1, :]
                sc = (cb * jnp.exp(jnp.where(causal, seg, NEG))).astype(bf16)
                y = jnp.dot(sc, xdt_pair, preferred_element_type=f32)
                s_in = state_ref[h]
                y += jnp.dot((cg * eac[:, c:c + 1]).astype(bf16), s_in.astype(bf16), preferred_element_type=f32)
                st = jnp.dot((bg_t * w_t[c:c + 1, :]).astype(bf16), xdt_pair, preferred_element_type=f32)
                state_ref[h] = s_in * chunk_decay[:, c:c + 1] + st
                outs.append(y)
            y_pair = jnp.where(first_half, outs[0], outs[1])
            sl = slice(i * LANE, (i + 1) * LANE)
            y_pair = y_pair + xs[:, sl] * dvec_ref[:, sl]
            z = z_ref[:, sl]
            ys.append(y_pair * (z * jax.nn.sigmoid(z)))

    per_group = len(ys) // SSM_GROUPS
    gw = SSM_INNER // SSM_GROUPS
    for g in range(SSM_GROUPS):
        tiles = ys[g * per_group:(g + 1) * per_group]
        ms = sum(jnp.sum(t * t, axis=-1, keepdims=True) for t in tiles) / gw
        scale = lax.rsqrt(ms + EPS)
        for k, t in enumerate(tiles):
            sl = slice(g * gw + k * LANE, g * gw + (k + 1) * LANE)
            o_ref[:, sl] = (t * scale * ng_ref[:, sl]).astype(o_ref.dtype)


def _exact_dot_left(b_01, a_f32):
    return sum(jnp.dot(b_01, part, preferred_element_type=f32) for part in _split3(a_f32))


def _lane_vec(values, lane0):
    return jnp.zeros((1, LANE), f32).at[0, lane0:lane0 + values.shape[0]].set(values.astype(f32))


def _ssd(proj, offs, conv_w, conv_b, dt_bias, a_log, d_skip, norm_g, bsz, seq):
    m = bsz * seq
    L = SSM_CHUNK
    n_c = seq // L
    tri = jnp.asarray(np.tril(np.ones((L, L), np.float32)), bf16)
    expand = np.zeros((LANE, SSM_INNER), np.float32)
    for h in range(SSM_HEADS):
        expand[SMALL_DT_LANE + h, h * SSM_HEAD_DIM:(h + 1) * SSM_HEAD_DIM] = 1.0
    row = lambda w: pl.BlockSpec((1, w), lambda b, c: (0, 0))
    return pl.pallas_call(
        _ssd_kernel,
        grid=(bsz, n_c),
        in_specs=[
            pl.BlockSpec((L, SSM_CONV_CH), lambda b, c: (b * n_c + c, offs["c_xbc"] // SSM_CONV_CH)),
            pl.BlockSpec((L, SSM_INNER), lambda b, c: (b * n_c + c, offs["c_z"] // SSM_INNER)),
            pl.BlockSpec((L, LANE), lambda b, c: (b * n_c + c, offs["small"] // LANE)),
            pl.BlockSpec((CONV_K, SSM_CONV_CH), lambda b, c: (0, 0)),
            row(SSM_CONV_CH), row(LANE), row(LANE), row(SSM_INNER), row(SSM_INNER),
            pl.BlockSpec((L, L), lambda b, c: (0, 0)),
            pl.BlockSpec((LANE, SSM_INNER), lambda b, c: (0, 0)),
        ],
        out_specs=pl.BlockSpec((L, SSM_INNER), lambda b, c: (b * n_c + c, 0)),
        out_shape=jax.ShapeDtypeStruct((m, SSM_INNER), bf16),
        scratch_shapes=[
            pltpu.VMEM((CONV_HALO, SSM_CONV_CH), f32),
            pltpu.VMEM((CONV_HALO + L, SSM_CONV_CH), f32),
            pltpu.VMEM((SSM_HEADS, SSM_STATE, LANE), f32),
        ],
        compiler_params=pltpu.CompilerParams(dimension_semantics=("arbitrary", "arbitrary"),
                                             vmem_limit_bytes=V7X_VMEM_LIMIT_BYTES),
        name="ssd_mixer",
    )(proj, proj, proj, conv_w, conv_b.reshape(1, -1), _lane_vec(dt_bias, SMALL_DT_LANE),
      _lane_vec(a_log, SMALL_DT_LANE), jnp.repeat(d_skip, SSM_HEAD_DIM).reshape(1, -1), norm_g.reshape(1, -1),
      tri, jnp.asarray(expand, bf16))


GDN_STEP_ROWS = 128
GDN_QK_W = GDN_HEADS * GDN_DK
GDN_CONV_CH = GDN_HEADS * (2 * GDN_DK + GDN_DV)


def _dot_f32(a, b):
    return jnp.dot(a, b, precision=lax.Precision.HIGHEST, preferred_element_type=f32)


def _unit_lower_solve(a_strict, rhs):
    p = -a_strict
    x = rhs + _dot_f32(p, rhs)
    k = 2
    while k < a_strict.shape[0]:
        p = _dot_f32(p, p)
        x = x + _dot_f32(p, x)
        k *= 2
    return x


def _gdn_kernel(qkv_ref, z_ref, sm_ref, cw_ref, dtb_ref, alog_ref, ng_ref, tri_ref, o_ref, tail_ref, buf_ref,
                state_ref):
    C = GDN_CHUNK

    @pl.when(pl.program_id(1) == 0)
    def _():
        tail_ref[...] = jnp.zeros(tail_ref.shape, f32)
        state_ref[...] = jnp.zeros(state_ref.shape, f32)

    qkv = _causal_conv_silu(qkv_ref, tail_ref, buf_ref, cw_ref, 0.0)
    sm = sm_ref[...]
    beta = jax.nn.sigmoid(sm)
    g_log = -jnp.exp(alog_ref[...]) * _softplus(sm + dtb_ref[...])
    gc = _exact_dot_left(tri_ref[...], g_log)
    gc_t = gc.T
    egc = jnp.exp(gc)

    row = lax.broadcasted_iota(jnp.int32, (C, C), 0)
    col = lax.broadcasted_iota(jnp.int32, (C, C), 1)
    causal = row >= col
    strict = row > col

    for ch in range(GDN_STEP_ROWS // C):
        rs = slice(ch * C, (ch + 1) * C)
        last = slice((ch + 1) * C - 1, (ch + 1) * C)
        for h in range(GDN_HEADS):
            q = qkv[rs, h * GDN_DK:(h + 1) * GDN_DK]
            k = qkv[rs, GDN_QK_W + h * GDN_DK:GDN_QK_W + (h + 1) * GDN_DK]
            v = qkv[rs, 2 * GDN_QK_W + h * GDN_DV:2 * GDN_QK_W + (h + 1) * GDN_DV]
            q = q * lax.rsqrt(jnp.sum(q * q, axis=-1, keepdims=True) + EPS) * (GDN_DK ** -0.5)
            k = k * lax.rsqrt(jnp.sum(k * k, axis=-1, keepdims=True) + EPS)
            ca = SMALL_A_LANE + h
            b = beta[rs, SMALL_B_LANE + h:SMALL_B_LANE + h + 1]
            g_col = gc[rs, ca:ca + 1]
            g_last = gc[last, ca:ca + 1]
            decay = jnp.exp(jnp.where(causal, g_col - gc_t[ca:ca + 1, rs], NEG))
            kb = k * b
            k16 = k.astype(bf16)
            a_strict = jnp.where(strict, _dot_nt(kb.astype(bf16), k16) * decay, 0.0)
            sol = _unit_lower_solve(a_strict, jnp.concatenate([v * b, kb * egc[rs, ca:ca + 1]], axis=1))
            u, w = sol[:, :GDN_DV], sol[:, GDN_DV:]
            attn = _dot_nt(q.astype(bf16), k16) * decay
            q_dec = q * egc[rs, ca:ca + 1]
            k_dec = k * jnp.exp(g_last - g_col)

            s_in = state_ref[h]
            s16 = s_in.astype(bf16)
            v_new = u - jnp.dot(w.astype(bf16), s16, preferred_element_type=f32)
            v16 = v_new.astype(bf16)
            o = (jnp.dot(q_dec.astype(bf16), s16, preferred_element_type=f32)
                 + jnp.dot(attn.astype(bf16), v16, preferred_element_type=f32))
            state_ref[h] = s_in * jnp.exp(g_last) + jnp.dot(k_dec.T.astype(bf16), v16, preferred_element_type=f32)

            o = o * lax.rsqrt(jnp.mean(o * o, axis=-1, keepdims=True) + EPS) * ng_ref[...]
            z = z_ref[rs, h * GDN_DV:(h + 1) * GDN_DV]
            o_ref[rs, h * GDN_DV:(h + 1) * GDN_DV] = (o * (z * jax.nn.sigmoid(z))).astype(o_ref.dtype)


def _gdn(proj, offs, conv_w, a_log, dt_bias, norm_g, bsz, seq):
    m = bsz * seq
    R_ = GDN_STEP_ROWS
    n_s = seq // R_
    idx = np.arange(R_)
    tri = ((idx[:, None] >= idx[None, :]) & (idx[:, None] // GDN_CHUNK == idx[None, :] // GDN_CHUNK))
    row = lambda w: pl.BlockSpec((1, w), lambda b, c: (0, 0))
    return pl.pallas_call(
        _gdn_kernel,
        grid=(bsz, n_s),
        in_specs=[
            pl.BlockSpec((R_, GDN_CONV_CH), lambda b, c: (b * n_s + c, offs["a_q"] // GDN_CONV_CH)),
            pl.BlockSpec((R_, GDN_HEADS * GDN_DV), lambda b, c: (b * n_s + c, offs["a_z"] // (GDN_HEADS * GDN_DV))),
            pl.BlockSpec((R_, LANE), lambda b, c: (b * n_s + c, offs["small"] // LANE)),
            pl.BlockSpec((CONV_K, GDN_CONV_CH), lambda b, c: (0, 0)),
            row(LANE), row(LANE), row(GDN_DV),
            pl.BlockSpec((R_, R_), lambda b, c: (0, 0)),
        ],
        out_specs=pl.BlockSpec((R_, GDN_HEADS * GDN_DV), lambda b, c: (b * n_s + c, 0)),
        out_shape=jax.ShapeDtypeStruct((m, GDN_HEADS * GDN_DV), bf16),
        scratch_shapes=[
            pltpu.VMEM((CONV_HALO, GDN_CONV_CH), f32),
            pltpu.VMEM((CONV_HALO + R_, GDN_CONV_CH), f32),
            pltpu.VMEM((GDN_HEADS, GDN_DK, GDN_DV), f32),
        ],
        compiler_params=pltpu.CompilerParams(dimension_semantics=("arbitrary", "arbitrary"),
                                             vmem_limit_bytes=V7X_VMEM_LIMIT_BYTES),
        name="gdn_mixer",
    )(proj, proj, proj, conv_w, _lane_vec(dt_bias, SMALL_A_LANE), _lane_vec(a_log, SMALL_A_LANE),
      norm_g.reshape(1, -1), jnp.asarray(tri.astype(np.float32), bf16))


_IN_NAMES = ("a_q", "a_k", "a_v", "a_z", "a_a", "a_b", "b_q", "b_kc", "b_vc", "b_ks", "b_vs", "b_kw", "b_vw",
             "b_g", "c_z", "c_xbc", "c_dt", "m_gate")


def _in_sizes(d_model):
    return (GDN_HEADS * GDN_DK, GDN_HEADS * GDN_DK, GDN_HEADS * GDN_DV, GDN_HEADS * GDN_DV, GDN_HEADS, GDN_HEADS,
            NSA_HEADS * NSA_DK, NSA_KV_GROUPS * NSA_DK, NSA_KV_GROUPS * NSA_DV, NSA_KV_GROUPS * NSA_DK,
            NSA_KV_GROUPS * NSA_DV, NSA_KV_GROUPS * NSA_DK, NSA_KV_GROUPS * NSA_DV, 3 * NSA_HEADS,
            SSM_INNER, SSM_CONV_CH, SSM_HEADS, 3 * d_model)


_BIG_ORDER = ("m_gate", "a_q", "a_k", "a_v", "c_xbc", "c_z", "a_z", "b_q", "b_kc", "b_vc", "b_ks", "b_vs",
              "b_kw", "b_vw")
_SMALL_ORDER = ("a_a", "a_b", "c_dt")


def _in_layout(d_model, tn=512):
    sizes = dict(zip(_IN_NAMES, _in_sizes(d_model)))
    src_off = dict(zip(_IN_NAMES, np.cumsum((0,) + _in_sizes(d_model))[:-1].tolist()))
    cols, offs = [], {}

    def pad_to_lane():
        cols.extend([-1] * (_round_up(len(cols), LANE) - len(cols)))

    for name in _BIG_ORDER:
        offs[name] = len(cols)
        cols.extend(range(src_off[name], src_off[name] + sizes[name]))
    offs["small"] = len(cols)
    for name in _SMALL_ORDER:
        offs[name] = len(cols)
        cols.extend(range(src_off[name], src_off[name] + sizes[name]))
    pad_to_lane()
    offs["b_g"] = len(cols)
    per_group = 3 * NSA_HG
    for g in range(NSA_KV_GROUPS):
        cols.extend(range(src_off["b_g"] + g * per_group, src_off["b_g"] + (g + 1) * per_group))
        pad_to_lane()
    n_pad = _round_up(len(cols), tn)
    cols.extend([-1] * (n_pad - len(cols)))
    return np.asarray(cols, np.int32), offs, sizes, n_pad


def _prep_w_in(w_in_l, cols):
    w = jnp.take(w_in_l, jnp.asarray(np.maximum(cols, 0)), axis=1)
    return jnp.where(jnp.asarray(cols >= 0)[None, :], w, 0.0).astype(bf16)


def kernel(x, rel_table, g_ffn1, w_up1, w_down1, g_mix, w_in, gdn_conv, gdn_a_log, gdn_dt_bias, gdn_norm,
           nsa_q_norm, nsa_k_norm, nsa_pe_k, nsa_pe_v, nsa_w_ck, nsa_w_cv, ssm_conv_w, ssm_conv_b,
           ssm_dt_bias, ssm_a_log, ssm_d, ssm_norm, p_a, p_b, p_c, w_o, g_ffn2, w_up2, w_down2):
    bsz, seq, d = x.shape
    depth = w_in.shape[0]
    m = bsz * seq
    cols, offs, sizes, _ = _in_layout(d)
    x2 = x.reshape(m, d)
    nsa_consts = _nsa_constants(seq)

    def seg(proj, name):
        return proj[:, offs[name]:offs[name] + sizes[name]].reshape(bsz, seq, sizes[name])

    for l in range(depth):
        wa, wb, wd = _prep_ffn_weights(w_up1[l], w_down1[l])
        x2 = _ffn(x2, g_ffn1[l].reshape(1, d), wa, wb, wd)

        proj = _norm_matmul(x2, g_mix[l].reshape(1, d), _prep_w_in(w_in[l], cols))
        y_a = _gdn(proj, offs, gdn_conv[l], gdn_a_log[l], gdn_dt_bias[l], gdn_norm[l], bsz, seq)
        y_b = _nsa(proj, offs, nsa_q_norm[l], nsa_k_norm[l], nsa_pe_k[l], nsa_pe_v[l], nsa_w_ck[l], nsa_w_cv[l],
                   rel_table, nsa_consts, bsz, seq)
        y_c = _ssd(proj, offs, ssm_conv_w[l], ssm_conv_b[l], ssm_dt_bias[l], ssm_a_log[l], ssm_d[l], ssm_norm[l],
                   bsz, seq)
        x2 = _merge(x2, y_a, y_b, y_c, proj, offs["m_gate"],
                    p_a[l].astype(bf16), p_b[l].astype(bf16), p_c[l].astype(bf16), w_o[l].astype(bf16))

        wa, wb, wd = _prep_ffn_weights(w_up2[l], w_down2[l])
        x2 = _ffn(x2, g_ffn2[l].reshape(1, d), wa, wb, wd)
    return x2.reshape(bsz, seq, d)
```

```python
import functools
import math

import jax
import jax.numpy as jnp
import numpy as np
from jax import lax
from jax.experimental import pallas as pl
from jax.experimental.pallas import tpu as pltpu

EPS = 1e-6
CONV_K = 4

GDN_HEADS = 4
GDN_DK = 128
GDN_DV = 128
GDN_CHUNK = 64

NSA_HEADS = 4
NSA_KV_GROUPS = 2
NSA_DK = 128
NSA_DV = 128
CMP_BLOCK = 32
CMP_STRIDE = 16
SEL_BLOCK = 64
N_SELECT = 16
WINDOW = 512
Q_BLOCK = 128
SEL_OVERLAP_WEIGHTS = (1.0, 2.0, 2.0, 2.0, 1.0)
FORCE_SCORE = 1e4
NEG = -1e30

SSM_HEADS = 16
SSM_HEAD_DIM = 64
SSM_GROUPS = 2
SSM_STATE = 128
SSM_CHUNK = 128
SSM_INNER = SSM_HEADS * SSM_HEAD_DIM
SSM_CONV_CH = SSM_INNER + 2 * SSM_GROUPS * SSM_STATE

REL_BUCKETS = 32
REL_MAX_EXACT = 16
REL_MAX_DIST = 1024

V7X_VMEM_LIMIT_BYTES = 56 * 1024 * 1024
LANE = 128

bf16 = jnp.bfloat16
f32 = jnp.float32


def _round_up(n, m):
    return (n + m - 1) // m * m


def _ffn_kernel(x_ref, g_ref, wa_ref, wb_ref, wd_ref, o_ref, h_ref):
    j = pl.program_id(1)

    @pl.when(j == 0)
    def _():
        x = x_ref[...]
        h = x * lax.rsqrt(jnp.mean(x * x, axis=-1, keepdims=True) + EPS) * g_ref[...]
        h_ref[...] = h.astype(bf16)
        o_ref[...] = x

    h = h_ref[...]
    a = jnp.dot(h, wa_ref[...], preferred_element_type=f32)
    b = jnp.dot(h, wb_ref[...], preferred_element_type=f32)
    act = (0.5 * a * jax.nn.sigmoid(a) * b).astype(bf16)
    o_ref[...] += jnp.dot(act, wd_ref[...], preferred_element_type=f32)


def _ffn(x2, g, wa, wb, wd, *, tm=512, tf=512):
    m, d = x2.shape
    ffp = wa.shape[1]
    return pl.pallas_call(
        _ffn_kernel,
        grid=(m // tm, ffp // tf),
        in_specs=[
            pl.BlockSpec((tm, d), lambda i, j: (i, 0)),
            pl.BlockSpec((1, d), lambda i, j: (0, 0)),
            pl.BlockSpec((d, tf), lambda i, j: (0, j)),
            pl.BlockSpec((d, tf), lambda i, j: (0, j)),
            pl.BlockSpec((tf, d), lambda i, j: (j, 0)),
        ],
        out_specs=pl.BlockSpec((tm, d), lambda i, j: (i, 0)),
        out_shape=jax.ShapeDtypeStruct((m, d), f32),
        scratch_shapes=[pltpu.VMEM((tm, d), bf16)],
        compiler_params=pltpu.CompilerParams(
            dimension_semantics=("parallel", "arbitrary"),
            vmem_limit_bytes=V7X_VMEM_LIMIT_BYTES),
        name="ffn_swiglu",
    )(x2, g, wa, wb, wd)


def _prep_ffn_weights(w_up, w_down, tf=512):
    d, two_ff = w_up.shape
    ff = two_ff // 2
    ffp = _round_up(ff, tf)
    pad = ffp - ff
    wa = jnp.pad(w_up[:, :ff].astype(bf16), ((0, 0), (0, pad)))
    wb = jnp.pad(w_up[:, ff:].astype(bf16), ((0, 0), (0, pad)))
    wd = jnp.pad(w_down.astype(bf16), ((0, pad), (0, 0)))
    return wa, wb, wd


def _norm_matmul_kernel(x_ref, g_ref, w_ref, o_ref, h_ref):
    j = pl.program_id(1)

    @pl.when(j == 0)
    def _():
        x = x_ref[...]
        h = x * lax.rsqrt(jnp.mean(x * x, axis=-1, keepdims=True) + EPS) * g_ref[...]
        h_ref[...] = h.astype(bf16)

    o_ref[...] = jnp.dot(h_ref[...], w_ref[...], preferred_element_type=f32)


def _norm_matmul(x2, g, w, *, tm=512, tn=512):
    m, d = x2.shape
    n = w.shape[1]
    return pl.pallas_call(
        _norm_matmul_kernel,
        grid=(m // tm, n // tn),
        in_specs=[
            pl.BlockSpec((tm, d), lambda i, j: (i, 0)),
            pl.BlockSpec((1, d), lambda i, j: (0, 0)),
            pl.BlockSpec((d, tn), lambda i, j: (0, j)),
        ],
        out_specs=pl.BlockSpec((tm, tn), lambda i, j: (i, j)),
        out_shape=jax.ShapeDtypeStruct((m, n), f32),
        scratch_shapes=[pltpu.VMEM((tm, d), bf16)],
        compiler_params=pltpu.CompilerParams(
            dimension_semantics=("parallel", "arbitrary"),
            vmem_limit_bytes=V7X_VMEM_LIMIT_BYTES),
        name="norm_in_proj",
    )(x2, g, w)


def _merge_kernel(x_ref, ya_ref, yb_ref, yc_ref, ga_ref, gb_ref, gc_ref, pa_ref, pb_ref, pc_ref, wo_ref, o_ref):
    ma = jnp.dot(ya_ref[...].astype(bf16), pa_ref[...], preferred_element_type=f32)
    mb = jnp.dot(yb_ref[...].astype(bf16), pb_ref[...], preferred_element_type=f32)
    mc = jnp.dot(yc_ref[...].astype(bf16), pc_ref[...], preferred_element_type=f32)
    merged = (jax.nn.sigmoid(ga_ref[...]) * ma + jax.nn.sigmoid(gb_ref[...]) * mb
              + jax.nn.sigmoid(gc_ref[...]) * mc)
    o_ref[...] = x_ref[...] + jnp.dot(merged.astype(bf16), wo_ref[...], preferred_element_type=f32)


def _merge(x2, ya, yb, yc, gates, gate_col0, pa, pb, pc, wo, *, tm=256):
    m, d = x2.shape
    gb0 = gate_col0 // d
    const = dict(pipeline_mode=pl.Buffered(1))
    return pl.pallas_call(
        _merge_kernel,
        grid=(m // tm,),
        in_specs=[
            pl.BlockSpec((tm, d), lambda i: (i, 0)),
            pl.BlockSpec((tm, ya.shape[1]), lambda i: (i, 0)),
            pl.BlockSpec((tm, yb.shape[1]), lambda i: (i, 0)),
            pl.BlockSpec((tm, yc.shape[1]), lambda i: (i, 0)),
            pl.BlockSpec((tm, d), lambda i: (i, gb0)),
            pl.BlockSpec((tm, d), lambda i: (i, gb0 + 1)),
            pl.BlockSpec((tm, d), lambda i: (i, gb0 + 2)),
            pl.BlockSpec(pa.shape, lambda i: (0, 0), **const),
            pl.BlockSpec(pb.shape, lambda i: (0, 0), **const),
            pl.BlockSpec(pc.shape, lambda i: (0, 0), **const),
            pl.BlockSpec(wo.shape, lambda i: (0, 0), **const),
        ],
        out_specs=pl.BlockSpec((tm, d), lambda i: (i, 0)),
        out_shape=jax.ShapeDtypeStruct((m, d), f32),
        compiler_params=pltpu.CompilerParams(
            dimension_semantics=("parallel",),
            vmem_limit_bytes=V7X_VMEM_LIMIT_BYTES),
        name="merge_out_proj",
    )(x2, ya, yb, yc, gates, gates, gates, pa, pb, pc, wo)


def _jx_rmsnorm(x, g):
    xf = x.astype(jnp.float32)
    y = xf * lax.rsqrt(jnp.mean(xf * xf, axis=-1, keepdims=True) + EPS)
    return (y * g.astype(jnp.float32)).astype(x.dtype)


def _jx_l2norm(t):
    return t * lax.rsqrt(jnp.sum(t * t, axis=-1, keepdims=True) + EPS)


def _jx_causal_dwconv(x, w):
    return lax.conv_general_dilated(x, w[:, None, :], window_strides=(1,), padding=[(CONV_K - 1, 0)],
                                    dimension_numbers=('NWC', 'WIO', 'NWC'), feature_group_count=x.shape[-1])


def _jx_rel_bucket(dist):
    n = jnp.maximum(dist, 0)
    nf = jnp.maximum(n, 1).astype(jnp.float32)
    large = REL_MAX_EXACT + (jnp.log(nf / REL_MAX_EXACT) / math.log(REL_MAX_DIST / REL_MAX_EXACT)
                             * (REL_BUCKETS - REL_MAX_EXACT)).astype(jnp.int32)
    large = jnp.minimum(large, REL_BUCKETS - 1)
    return jnp.where(n < REL_MAX_EXACT, n, large)


def _jx_gated_deltanet(q, k, v, z, a, b, conv_w, a_log, dt_bias, norm_g):
    Bsz, T, _ = q.shape
    H, C = GDN_HEADS, GDN_CHUNK
    n_c = T // C
    qk_w = H * GDN_DK
    qkv = jax.nn.silu(_jx_causal_dwconv(jnp.concatenate([q, k, v], axis=-1), conv_w)).astype(f32)

    def heads(t):
        return t.reshape(Bsz, n_c, C, H, -1).transpose(0, 3, 1, 2, 4)

    qh = _jx_l2norm(heads(qkv[..., :qk_w])) * GDN_DK ** -0.5
    kh = _jx_l2norm(heads(qkv[..., qk_w:2 * qk_w]))
    vh = heads(qkv[..., 2 * qk_w:])
    beta = jax.nn.sigmoid(b.astype(f32)).reshape(Bsz, n_c, C, H).transpose(0, 3, 1, 2)
    g = -jnp.exp(a_log.astype(f32)) * jax.nn.softplus(a.astype(f32) + dt_bias.astype(f32))
    gc = jnp.cumsum(g.reshape(Bsz, n_c, C, H).transpose(0, 3, 1, 2), axis=-1)
    causal = jnp.tril(jnp.ones((C, C), bool))
    strict = jnp.tril(jnp.ones((C, C), bool), -1)
    diff = gc[..., :, None] - gc[..., None, :]
    decay = jnp.where(causal, jnp.exp(jnp.where(causal, diff, 0.0)), 0.0)
    kb = kh * beta[..., None]
    m = jnp.where(strict, jnp.einsum('bhncd,bhnsd->bhncs', kb, kh) * decay, 0.0) + jnp.eye(C, dtype=f32)
    rhs = jnp.concatenate([vh * beta[..., None], kb * jnp.exp(gc)[..., None]], axis=-1)
    sol = lax.linalg.triangular_solve(m, rhs, left_side=True, lower=True, unit_diagonal=True)
    u, w = sol[..., :GDN_DV], sol[..., GDN_DV:]
    attn = jnp.einsum('bhncd,bhnsd->bhncs', qh, kh) * decay
    q_dec = qh * jnp.exp(gc)[..., None]
    k_dec = kh * jnp.exp(gc[..., -1:] - gc)[..., None]
    g_last = jnp.exp(gc[..., -1])

    def step(state, inp):
        u_c, w_c, a_c, qd_c, kd_c, gl_c = inp
        v_new = u_c - jnp.einsum('bhck,bhkv->bhcv', w_c, state)
        o_c = jnp.einsum('bhck,bhkv->bhcv', qd_c, state) + jnp.einsum('bhcs,bhsv->bhcv', a_c, v_new)
        state = state * gl_c[..., None, None] + jnp.einsum('bhck,bhcv->bhkv', kd_c, v_new)
        return state, o_c

    xs = (jnp.moveaxis(u, 2, 0), jnp.moveaxis(w, 2, 0), jnp.moveaxis(attn, 2, 0),
          jnp.moveaxis(q_dec, 2, 0), jnp.moveaxis(k_dec, 2, 0), jnp.moveaxis(g_last, 2, 0))
    _, o = lax.scan(step, jnp.zeros((Bsz, H, GDN_DK, GDN_DV), f32), xs)
    o = o.transpose(1, 0, 3, 2, 4).reshape(Bsz, T, H, GDN_DV)
    o = _jx_rmsnorm(o, norm_g) * jax.nn.silu(z.astype(f32).reshape(Bsz, T, H, GDN_DV))
    return o.reshape(Bsz, T, H * GDN_DV).astype(z.dtype)


def _jx_nsa_attention(q_in, kc, vc, ks, vs, kw, vw, gate_logits, q_norm, k_norm, pe_k, pe_v, w_ck, w_cv, rel_table):
    Bsz, T, _ = q_in.shape
    G, Hg = NSA_KV_GROUPS, NSA_HEADS // NSA_KV_GROUPS
    n_cmp = T // CMP_STRIDE - 1
    n_sel = T // SEL_BLOCK
    n_top = min(N_SELECT, n_sel)
    n_qb = T // Q_BLOCK
    ratio = SEL_BLOCK // CMP_STRIDE

    q = _jx_rmsnorm(q_in.reshape(Bsz, T, NSA_HEADS, NSA_DK), q_norm).astype(f32) * NSA_DK ** -0.5
    q_blocks = q.reshape(Bsz, n_qb, Q_BLOCK, G, Hg, NSA_DK).transpose(1, 0, 3, 4, 2, 5)

    def groups(t):
        return t.reshape(Bsz, T, G, -1).astype(f32)

    def compress(t, pe, w):
        c = groups(t).reshape(Bsz, T // CMP_STRIDE, CMP_STRIDE, G, -1)
        blocks = jnp.concatenate([c[:, :-1], c[:, 1:]], axis=2)
        return jnp.einsum('bnlgd,lde->bgne', blocks + pe.astype(f32)[:, None, :], w.astype(f32))

    k_cmp = _jx_rmsnorm(compress(kc, pe_k, w_ck), k_norm)
    v_cmp = compress(vc, pe_v, w_cv)
    k_sel = _jx_rmsnorm(groups(ks), k_norm).reshape(Bsz, n_sel, SEL_BLOCK, G, NSA_DK).transpose(0, 3, 1, 2, 4)
    v_sel = groups(vs).reshape(Bsz, n_sel, SEL_BLOCK, G, NSA_DV).transpose(0, 3, 1, 2, 4)
    pad = ((0, 0), (0, 0), (WINDOW, 0), (0, 0))
    k_win = jnp.pad(_jx_rmsnorm(groups(kw), k_norm).transpose(0, 2, 1, 3), pad)
    v_win = jnp.pad(groups(vw).transpose(0, 2, 1, 3), pad)

    table = rel_table.astype(f32).T.reshape(G, Hg, REL_BUCKETS)
    cmp_end = jnp.arange(n_cmp) * CMP_STRIDE + CMP_BLOCK - 1
    blk = jnp.arange(n_sel)
    bi = jnp.arange(Bsz)[:, None, None, None]
    gi = jnp.arange(G)[None, :, None, None]
    gi6 = jnp.arange(G)[None, :, None, None, None, None]
    hi6 = jnp.arange(Hg)[None, None, :, None, None, None]

    def block(args):
        qb, qi = args
        t = qi * Q_BLOCK + jnp.arange(Q_BLOCK)
        dist = t[:, None] - cmp_end[None, :]
        ok = dist >= 0
        s = jnp.einsum('bghqd,bgnd->bghqn', qb, k_cmp) + table[:, :, _jx_rel_bucket(dist)]
        p_cmp = jnp.where(ok, jax.nn.softmax(jnp.where(ok, s, NEG), axis=-1), 0.0)
        o_cmp = jnp.einsum('bghqn,bgnd->bghqd', p_cmp, v_cmp)
        imp_c = jnp.pad(p_cmp.sum(axis=2), ((0, 0), (0, 0), (0, 0), (1, ratio * n_sel - n_cmp)))
        imp = sum(wt * imp_c[..., j:j + ratio * n_sel:ratio] for j, wt in enumerate(SEL_OVERLAP_WEIGHTS))
        cur = t // SEL_BLOCK
        ok_s = blk[None, :] <= cur[:, None]
        forced = ok_s & ((blk[None, :] == 0) | (blk[None, :] >= cur[:, None] - 1))
        score = jnp.where(forced, FORCE_SCORE, jnp.where(ok_s, imp, -FORCE_SCORE))
        _, idx = lax.top_k(score, n_top)
        kg = k_sel[bi, gi, idx]
        vg = v_sel[bi, gi, idx]
        dist = t[:, None, None] - (idx[..., None] * SEL_BLOCK + jnp.arange(SEL_BLOCK))
        ok = (dist >= 0)[:, :, None]
        s = jnp.einsum('bghqd,bgqkld->bghqkl', qb, kg) + table[gi6, hi6, _jx_rel_bucket(dist)[:, :, None]]
        s = jnp.where(ok, s, NEG).reshape(Bsz, G, Hg, Q_BLOCK, n_top * SEL_BLOCK)
        p = jax.nn.softmax(s, axis=-1).reshape(Bsz, G, Hg, Q_BLOCK, n_top, SEL_BLOCK)
        o_sel = jnp.einsum('bghqkl,bgqkld->bghqd', p, vg)
        kwb = lax.dynamic_slice_in_dim(k_win, qi * Q_BLOCK, Q_BLOCK + WINDOW, axis=2)
        vwb = lax.dynamic_slice_in_dim(v_win, qi * Q_BLOCK, Q_BLOCK + WINDOW, axis=2)
        pos = qi * Q_BLOCK - WINDOW + jnp.arange(Q_BLOCK + WINDOW)
        dist = t[:, None] - pos[None, :]
        ok = (dist >= 0) & (dist < WINDOW) & (pos >= 0)[None, :]
        s = jnp.einsum('bghqd,bgkd->bghqk', qb, kwb) + table[:, :, _jx_rel_bucket(dist)]
        p = jax.nn.softmax(jnp.where(ok, s, NEG), axis=-1)
        o_win = jnp.einsum('bghqk,bgkd->bghqd', p, vwb)
        return jnp.stack([o_cmp, o_sel, o_win], axis=-2)

    out = lax.map(block, (q_blocks, jnp.arange(n_qb)))
    out = out.transpose(1, 0, 4, 2, 3, 5, 6).reshape(Bsz, T, NSA_HEADS, 3, NSA_DV)
    gates = jax.nn.sigmoid(gate_logits.astype(f32)).reshape(Bsz, T, NSA_HEADS, 3)
    o = jnp.einsum('bthr,bthrd->bthd', gates, out)
    return o.reshape(Bsz, T, NSA_HEADS * NSA_DV).astype(q_in.dtype)


def _jx_mamba2_ssd(z, xbc, dt, conv_w, conv_b, dt_bias, a_log, d_skip, norm_g):
    Bsz, T, _ = z.shape
    H, P, G, N, L = SSM_HEADS, SSM_HEAD_DIM, SSM_GROUPS, SSM_STATE, SSM_CHUNK
    n_c = T // L
    xbc = jax.nn.silu(_jx_causal_dwconv(xbc, conv_w) + conv_b).astype(f32)
    xs = xbc[..., :SSM_INNER].reshape(Bsz, n_c, L, H, P)
    bm = jnp.repeat(xbc[..., SSM_INNER:SSM_INNER + G * N].reshape(Bsz, n_c, L, G, N), H // G, axis=3)
    cm = jnp.repeat(xbc[..., SSM_INNER + G * N:].reshape(Bsz, n_c, L, G, N), H // G, axis=3)
    dt = jax.nn.softplus(dt.astype(f32) + dt_bias.astype(f32)).reshape(Bsz, n_c, L, H)
    ac = jnp.cumsum(dt * (-jnp.exp(a_log.astype(f32))), axis=2)
    xdt = xs * dt[..., None]
    ach = jnp.swapaxes(ac, 2, 3)
    causal = jnp.tril(jnp.ones((L, L), bool))
    seg = ach[..., :, None] - ach[..., None, :]
    decay = jnp.where(causal, jnp.exp(jnp.where(causal, seg, 0.0)), 0.0)
    scores = jnp.einsum('bclhn,bcshn->bchls', cm, bm) * decay
    y = jnp.einsum('bchls,bcshp->bclhp', scores, xdt)
    states = jnp.einsum('bclhn,bclhp->bchpn', bm * jnp.exp(ac[:, :, -1:, :] - ac)[..., None], xdt)
    chunk_decay = jnp.exp(ac[:, :, -1, :])

    def step(s, inp):
        st, cd = inp
        return s * cd[..., None, None] + st, s

    _, s_in = lax.scan(step, jnp.zeros((Bsz, H, P, N), f32),
                       (jnp.moveaxis(states, 1, 0), jnp.moveaxis(chunk_decay, 1, 0)))
    s_in = jnp.moveaxis(s_in, 0, 1)
    y = y + jnp.einsum('bclhn,bchpn->bclhp', cm * jnp.exp(ac)[..., None], s_in) + xs * d_skip.astype(f32)[:, None]
    y = y.reshape(Bsz, T, SSM_INNER) * jax.nn.silu(z.astype(f32))
    y = y.reshape(Bsz, T, G, SSM_INNER // G)
    y = y * lax.rsqrt(jnp.mean(y * y, axis=-1, keepdims=True) + EPS)
    return (y.reshape(Bsz, T, SSM_INNER) * norm_g.astype(f32)).astype(z.dtype)


NSA_HG = NSA_HEADS // NSA_KV_GROUPS
SEL_FAR_TILE = 512
SEL_FAR_SUB = 2
NEAR_TILES = 8
STRIP_W = NEAR_TILES * Q_BLOCK
STRIP_D0 = (NEAR_TILES - 1) * Q_BLOCK
WIN_TILES = WINDOW // Q_BLOCK + 1
CMP_PER_Q = Q_BLOCK // CMP_STRIDE
SEL_PER_Q = Q_BLOCK // SEL_BLOCK
MASK_PEN = -1e30


def _dot_nt(a, b):
    return lax.dot_general(a, b, (((1,), (1,)), ((), ())), preferred_element_type=f32)


def _lane_rms(x, gain):
    return x * lax.rsqrt(jnp.mean(x * x, axis=-1, keepdims=True) + EPS) * gain


def _nsa_prep_kernel(q_ref, ks_ref, kw_ref, qg_ref, kg_ref, qo_ref, kso_ref, kwo_ref):
    qg = qg_ref[...] * (NSA_DK ** -0.5)
    kg = kg_ref[...]
    for h in range(NSA_HEADS):
        sl = slice(h * NSA_DK, (h + 1) * NSA_DK)
        qo_ref[:, sl] = _lane_rms(q_ref[:, sl], qg).astype(bf16)
    kw_ = NSA_KV_GROUPS * NSA_DK
    for src, dst in ((ks_ref, kso_ref), (kw_ref, kwo_ref)):
        for g in range(NSA_KV_GROUPS):
            sl = slice(g * NSA_DK, (g + 1) * NSA_DK)
            dst[:, sl] = _lane_rms(src[:, sl], kg).astype(bf16)
        dst[:, kw_:] = src[:, kw_:].astype(bf16)


def _nsa_prep(proj, offs, q_gain, k_gain, *, tm=512):
    m = proj.shape[0]
    w = NSA_HEADS * NSA_DK
    blk = lambda name: pl.BlockSpec((tm, w), lambda i, c=offs[name] // w: (i, c))
    out = jax.ShapeDtypeStruct((m, w), bf16)
    return pl.pallas_call(
        _nsa_prep_kernel,
        grid=(m // tm,),
        in_specs=[blk("b_q"), blk("b_ks"), blk("b_kw"),
                  pl.BlockSpec((1, NSA_DK), lambda i: (0, 0)), pl.BlockSpec((1, NSA_DK), lambda i: (0, 0))],
        out_specs=[pl.BlockSpec((tm, w), lambda i: (i, 0))] * 3,
        out_shape=[out, out, out],
        compiler_params=pltpu.CompilerParams(dimension_semantics=("parallel",),
                                             vmem_limit_bytes=V7X_VMEM_LIMIT_BYTES),
        name="nsa_prep",
    )(proj, proj, proj, q_gain, k_gain)


def _nsa_compress_kernel(x_ref, w_ref, pe_ref, kg_ref, o_ref, hi_ref):
    j = pl.program_id(1)
    nc = o_ref.shape[2]
    lo = jnp.zeros((nc, NSA_DK), f32)
    hi = jnp.zeros((nc, NSA_DK), f32)
    for l in range(CMP_STRIDE):
        rows = x_ref[pl.ds(l, nc, stride=CMP_STRIDE), :]
        lo += jnp.dot((rows + pe_ref[0, l:l + 1, :]).astype(bf16), w_ref[0, l], preferred_element_type=f32)
        hi += jnp.dot((rows + pe_ref[0, CMP_STRIDE + l:CMP_STRIDE + l + 1, :]).astype(bf16),
                      w_ref[0, CMP_STRIDE + l], preferred_element_type=f32)
    hi_ref[pl.ds(0, nc), :] = hi
    hi_ref[pl.ds(nc, 8), :] = jnp.zeros((8, NSA_DK), f32)
    c = lo + hi_ref[pl.ds(1, nc), :]
    normed = _lane_rms(c, kg_ref[...])
    o_ref[0, 0] = jnp.where(j < NSA_KV_GROUPS, normed, c).astype(bf16)


def _nsa_compress(proj, offs, w_stack, pe_stack, k_gain, bsz, seq):
    nc = seq // CMP_STRIDE
    c0 = offs["b_kc"] // NSA_DK
    return pl.pallas_call(
        _nsa_compress_kernel,
        grid=(bsz, 2 * NSA_KV_GROUPS),
        in_specs=[
            pl.BlockSpec((seq, NSA_DK), lambda b, j: (b, c0 + j)),
            pl.BlockSpec((1, CMP_BLOCK, NSA_DK, NSA_DK), lambda b, j: (j // NSA_KV_GROUPS, 0, 0, 0)),
            pl.BlockSpec((1, CMP_BLOCK, NSA_DK), lambda b, j: (j // NSA_KV_GROUPS, 0, 0)),
            pl.BlockSpec((1, NSA_DK), lambda b, j: (0, 0)),
        ],
        out_specs=pl.BlockSpec((1, 1, nc, NSA_DK), lambda b, j: (b, j, 0, 0)),
        out_shape=jax.ShapeDtypeStruct((bsz, 2 * NSA_KV_GROUPS, nc, NSA_DK), bf16),
        scratch_shapes=[pltpu.VMEM((nc + 8, NSA_DK), f32)],
        compiler_params=pltpu.CompilerParams(dimension_semantics=("parallel", "parallel"),
                                             vmem_limit_bytes=V7X_VMEM_LIMIT_BYTES),
        name="nsa_compress",
    )(proj, w_stack, pe_stack, k_gain)


def _split3(x):
    hi = x.astype(bf16)
    r1 = x - hi.astype(f32)
    mid = r1.astype(bf16)
    lo = (r1 - mid.astype(f32)).astype(bf16)
    return hi, mid, lo


def _nsa_attn_kernel(tbl_ref, q_ref, kc_ref, vc_ref, ks_ref, vs_ref, kw_ref, vw_ref, et_ref, mt_ref, bs_ref,
                     bc_ref, gl_ref, o_ref, strip_ref, sc_ref, score_ref, stage_ref):
    g = pl.program_id(1)
    qi = pl.program_id(2)
    nc = kc_ref.shape[2]
    qb = Q_BLOCK

    @pl.when(qi == 0)
    def _():
        def lookup(bucket):
            v0 = jnp.zeros(bucket.shape, f32)
            v1 = jnp.zeros(bucket.shape, f32)
            for k in range(REL_BUCKETS):
                eq = bucket == k
                v0 = jnp.where(eq, tbl_ref[NSA_HG * g, k], v0)
                v1 = jnp.where(eq, tbl_ref[NSA_HG * g + 1, k], v1)
            return v0, v1

        for c in range(NEAR_TILES):
            sl = slice(c * qb, (c + 1) * qb)
            v0, v1 = lookup(bs_ref[:, sl])
            strip_ref[0, :, sl] = v0
            strip_ref[1, :, sl] = v1
        v0, v1 = lookup(bc_ref[...])
        for hh, v in enumerate((v0, v1)):
            d = v - tbl_ref[NSA_HG * g + hh, REL_BUCKETS - 1]
            hi = d.astype(bf16)
            sc_ref[hh, 0] = hi
            sc_ref[hh, 1] = (d - hi.astype(f32)).astype(bf16)

    far_bias = [tbl_ref[NSA_HG * g + hh, REL_BUCKETS - 1] for hh in range(NSA_HG)]
    q = q_ref[...]
    q2 = jnp.concatenate([q[:, :NSA_DK], q[:, NSA_DK:]], axis=0)

    r_i = lax.broadcasted_iota(jnp.int32, (qb, nc), 0)
    n_i = lax.broadcasted_iota(jnp.int32, (qb, nc), 1)
    ok_c = CMP_STRIDE * n_i <= qb * qi + r_i - (CMP_BLOCK - 1)
    shift = jnp.where(r_i + n_i == CMP_PER_Q * qi + CMP_PER_Q - 1, 1.0, 0.0).astype(bf16)
    kc = kc_ref[0, 0]
    vc = vc_ref[0, 0]
    psum = jnp.zeros((qb, nc), f32)
    o_cmp = []
    for hh in range(NSA_HG):
        s = _dot_nt(q2[hh * qb:(hh + 1) * qb], kc)
        bias = (far_bias[hh] + jnp.dot(sc_ref[hh, 0], shift, preferred_element_type=f32)
                + jnp.dot(sc_ref[hh, 1], shift, preferred_element_type=f32))
        s = jnp.where(ok_c, s + bias, NEG)
        e = jnp.exp(s - jnp.max(s, axis=-1, keepdims=True))
        p = jnp.where(ok_c, e / jnp.sum(e, axis=-1, keepdims=True), 0.0)
        o_cmp.append(jnp.dot(p.astype(bf16), vc, preferred_element_type=f32))
        psum += p

    mt = mt_ref[...]
    imp_t = sum(_dot_nt(mt, part) for part in _split3(psum))
    nblk = imp_t.shape[0]
    j_i = lax.broadcasted_iota(jnp.int32, (nblk, qb), 0)
    l_i = lax.broadcasted_iota(jnp.int32, (nblk, qb), 1)
    cur = SEL_PER_Q * qi + l_i // SEL_BLOCK
    ok_s = j_i <= cur
    forced = ok_s & ((j_i == 0) | (j_i >= cur - 1))
    score = jnp.where(forced, FORCE_SCORE, jnp.where(ok_s, imp_t, -FORCE_SCORE))
    score_ref[...] = score

    def rank_body(i, cnt):
        for u in range(SEL_PER_Q):
            jp = SEL_PER_Q * i + u
            row = score_ref[pl.ds(jp, 1), :]
            beats = (row > score) | ((row == score) & (j_i > jp))
            cnt = cnt + jnp.where(beats, 1.0, 0.0)
        return cnt

    cnt = lax.fori_loop(0, qi + 1, rank_body, jnp.zeros((nblk, qb), f32))
    pen = jnp.where(cnt < N_SELECT, 0.0, MASK_PEN).T.astype(bf16)
    pen2 = jnp.concatenate([pen, pen], axis=0)

    row2 = lax.broadcasted_iota(jnp.int32, (NSA_HG * qb, 1), 0)
    far_col = jnp.where(row2 < qb, far_bias[0], far_bias[1])
    n_far_keys = qb * jnp.maximum(qi - (NEAR_TILES - 1), 0)
    far_step = SEL_FAR_SUB * SEL_FAR_TILE
    c_far = lax.broadcasted_iota(jnp.int32, (NSA_HG * qb, SEL_FAR_TILE), 1)

    def far_body(kt, carry):
        m_old, l_old, acc_old = carry
        ss, vv = [], []
        for sub in range(SEL_FAR_SUB):
            k0 = kt * far_step + sub * SEL_FAR_TILE
            rows = pl.ds(pl.multiple_of(k0, SEL_FAR_TILE), SEL_FAR_TILE)
            s = _dot_nt(q2, ks_ref[rows, :]) + _dot_nt(pen2, et_ref[rows, :]) + far_col
            ss.append(jnp.where(c_far < n_far_keys - k0, s, MASK_PEN))
            vv.append(vs_ref[rows, :])
        m_new = functools.reduce(jnp.maximum, [m_old] + [jnp.max(s, axis=-1, keepdims=True) for s in ss])
        ps = [jnp.exp(s - m_new) for s in ss]
        alpha = jnp.exp(m_old - m_new)
        l_new = alpha * l_old + sum(jnp.sum(p, axis=-1, keepdims=True) for p in ps)
        acc_new = alpha * acc_old + sum(jnp.dot(p.astype(bf16), v, preferred_element_type=f32)
                                        for p, v in zip(ps, vv))
        return m_new, l_new, acc_new

    m_far, l_far, acc_far = lax.fori_loop(
        0, (n_far_keys + far_step - 1) // far_step, far_body,
        (jnp.full((NSA_HG * qb, 1), MASK_PEN, f32), jnp.zeros((NSA_HG * qb, 1), f32),
         jnp.zeros((NSA_HG * qb, NSA_DV), f32)))

    r2 = lax.broadcasted_iota(jnp.int32, (NSA_HG * qb, qb), 0) % qb
    c2 = lax.broadcasted_iota(jnp.int32, (NSA_HG * qb, qb), 1)

    def strip_bias(w):
        sl = slice(w * qb, (w + 1) * qb)
        return jnp.concatenate([strip_ref[0, :, sl], strip_ref[1, :, sl]], axis=0)

    def key_rows(w):
        kt = qi - (NEAR_TILES - 1) + w
        return kt, pl.ds(pl.multiple_of(jnp.maximum(kt, 0) * qb, qb), qb)

    m_sel = m_far
    for w in range(NEAR_TILES):
        kt, rows = key_rows(w)
        s = _dot_nt(q2, ks_ref[rows, :]) + _dot_nt(pen2, et_ref[rows, :]) + strip_bias(w)
        if w == NEAR_TILES - 1:
            s = jnp.where(c2 <= r2, s, NEG)
        else:
            s = jnp.where(kt >= 0, s, NEG)
        stage_ref[w] = s
        m_sel = jnp.maximum(m_sel, jnp.max(s, axis=-1, keepdims=True))
    alpha = jnp.exp(m_far - m_sel)
    l_sel = alpha * l_far
    acc_sel = alpha * acc_far
    for w in range(NEAR_TILES):
        _, rows = key_rows(w)
        e = jnp.exp(stage_ref[w] - m_sel)
        l_sel = l_sel + jnp.sum(e, axis=-1, keepdims=True)
        acc_sel = acc_sel + jnp.dot(e.astype(bf16), vs_ref[rows, :], preferred_element_type=f32)
    o_sel = acc_sel / l_sel

    m_w = jnp.full((NSA_HG * qb, 1), NEG, f32)
    for w in range(NEAR_TILES - WIN_TILES, NEAR_TILES):
        kt, rows = key_rows(w)
        s = _dot_nt(q2, kw_ref[rows, :]) + strip_bias(w)
        if w == NEAR_TILES - WIN_TILES:
            valid = (c2 > r2) & (kt >= 0)
        elif w == NEAR_TILES - 1:
            valid = c2 <= r2
        else:
            valid = jnp.broadcast_to(kt >= 0, c2.shape)
        s = jnp.where(valid, s, NEG)
        stage_ref[w] = s
        m_w = jnp.maximum(m_w, jnp.max(s, axis=-1, keepdims=True))
    den = jnp.zeros((NSA_HG * qb, 1), f32)
    acc_w = jnp.zeros((NSA_HG * qb, NSA_DV), f32)
    for w in range(NEAR_TILES - WIN_TILES, NEAR_TILES):
        _, rows = key_rows(w)
        e = jnp.exp(stage_ref[w] - m_w)
        den = den + jnp.sum(e, axis=-1, keepdims=True)
        acc_w = acc_w + jnp.dot(e.astype(bf16), vw_ref[rows, :], preferred_element_type=f32)
    o_win = acc_w / den

    gate = jax.nn.sigmoid(gl_ref[...])
    for hh in range(NSA_HG):
        rs = slice(hh * qb, (hh + 1) * qb)
        o_ref[:, hh * NSA_DV:(hh + 1) * NSA_DV] = (gate[:, 3 * hh:3 * hh + 1] * o_cmp[hh]
                                                   + gate[:, 3 * hh + 1:3 * hh + 2] * o_sel[rs]
                                                   + gate[:, 3 * hh + 2:3 * hh + 3] * o_win[rs])


def _rel_bucket(dist):
    n = jnp.maximum(dist, 0)
    nf = jnp.maximum(n, 1).astype(f32)
    large = REL_MAX_EXACT + (jnp.log(nf / REL_MAX_EXACT) / math.log(REL_MAX_DIST / REL_MAX_EXACT)
                             * (REL_BUCKETS - REL_MAX_EXACT)).astype(jnp.int32)
    large = jnp.minimum(large, REL_BUCKETS - 1)
    return jnp.where(n < REL_MAX_EXACT, n, large)


def _nsa_constants(seq):
    nc = seq // CMP_STRIDE
    nblk = max(seq // SEL_BLOCK, LANE)
    r = jnp.arange(Q_BLOCK)[:, None]
    bs = _rel_bucket(STRIP_D0 + r - jnp.arange(STRIP_W)[None, :])
    bc = _rel_bucket(r + CMP_STRIDE * jnp.arange(LANE)[None, :] - (CMP_STRIDE * (CMP_PER_Q - 1) + CMP_BLOCK - 1))
    et = (np.arange(seq)[:, None] // SEL_BLOCK == np.arange(nblk)[None, :]).astype(np.float32)
    ratio = SEL_BLOCK // CMP_STRIDE
    mt = np.zeros((nblk, nc), np.float32)
    for j in range(nblk):
        for i, wt in enumerate(SEL_OVERLAP_WEIGHTS):
            mcol = ratio * j - 1 + i
            if 0 <= mcol < nc:
                mt[j, mcol] = wt
    return bs.astype(jnp.int32), bc.astype(jnp.int32), jnp.asarray(et, bf16), jnp.asarray(mt, bf16)


def _nsa_attention(qn, cmp_kv, ksvs, kwvw, proj, offs, tbl, consts, bsz, seq):
    bs, bc, et, mt = consts
    m = bsz * seq
    n_qb = seq // Q_BLOCK
    nc = seq // CMP_STRIDE
    nblk = et.shape[1]
    g_ = NSA_KV_GROUPS
    kv = lambda col0: pl.BlockSpec((seq, NSA_DK), lambda b, g, i: (b, col0 + g))
    cmp_spec = lambda j0: pl.BlockSpec((1, 1, nc, NSA_DK), lambda b, g, i: (b, j0 + g, 0, 0))
    const2 = lambda shape: pl.BlockSpec(shape, lambda b, g, i: (0, 0))
    gcol = offs["b_g"] // LANE
    return pl.pallas_call(
        _nsa_attn_kernel,
        grid=(bsz, g_, n_qb),
        in_specs=[
            pl.BlockSpec(memory_space=pltpu.SMEM),
            pl.BlockSpec((Q_BLOCK, NSA_HG * NSA_DK), lambda b, g, i: (b * n_qb + i, g)),
            cmp_spec(0), cmp_spec(g_),
            kv(0), kv(g_), kv(0), kv(g_),
            const2((seq, nblk)), const2((nblk, nc)), const2((Q_BLOCK, STRIP_W)), const2((Q_BLOCK, LANE)),
            pl.BlockSpec((Q_BLOCK, LANE), lambda b, g, i: (b * n_qb + i, gcol + g)),
        ],
        out_specs=pl.BlockSpec((Q_BLOCK, NSA_HG * NSA_DV), lambda b, g, i: (b * n_qb + i, g)),
        out_shape=jax.ShapeDtypeStruct((m, NSA_HEADS * NSA_DV), f32),
        scratch_shapes=[
            pltpu.VMEM((NSA_HG, Q_BLOCK, STRIP_W), f32),
            pltpu.VMEM((NSA_HG, 2, Q_BLOCK, LANE), bf16),
            pltpu.VMEM((nblk, Q_BLOCK), f32),
            pltpu.VMEM((NEAR_TILES, NSA_HG * Q_BLOCK, Q_BLOCK), f32),
        ],
        compiler_params=pltpu.CompilerParams(dimension_semantics=("arbitrary", "arbitrary", "arbitrary"),
                                             vmem_limit_bytes=V7X_VMEM_LIMIT_BYTES),
        name="nsa_attention",
    )(tbl, qn, cmp_kv, cmp_kv, ksvs, ksvs, kwvw, kwvw, et, mt, bs, bc, proj)


def _nsa(proj, offs, q_norm, k_norm, pe_k, pe_v, w_ck, w_cv, rel_table, consts, bsz, seq):
    q_gain = q_norm.reshape(1, NSA_DK)
    k_gain = k_norm.reshape(1, NSA_DK)
    qn, ksvs, kwvw = _nsa_prep(proj, offs, q_gain, k_gain)
    cmp_kv = _nsa_compress(proj, offs, jnp.stack([w_ck, w_cv]).astype(bf16), jnp.stack([pe_k, pe_v]), k_gain,
                           bsz, seq)
    return _nsa_attention(qn, cmp_kv, ksvs, kwvw, proj, offs, rel_table.T, consts, bsz, seq)


SMALL_A_LANE = 0
SMALL_B_LANE = GDN_HEADS
SMALL_DT_LANE = 2 * GDN_HEADS
CONV_HALO = 8


def _softplus(x):
    return jnp.maximum(x, 0.0) + jnp.log1p(jnp.exp(-jnp.abs(x)))


def _exact_dot(a_f32, b_01):
    return sum(jnp.dot(part, b_01, preferred_element_type=f32) for part in _split3(a_f32))


def _causal_conv_silu(x_ref, tail_ref, buf_ref, w_ref, bias):
    rows = x_ref.shape[0]
    buf_ref[pl.ds(0, CONV_HALO), :] = tail_ref[...]
    buf_ref[pl.ds(CONV_HALO, rows), :] = x_ref[...]
    tail_ref[...] = x_ref[pl.ds(rows - CONV_HALO, CONV_HALO), :]
    acc = bias
    for k in range(CONV_K):
        acc = acc + w_ref[k:k + 1, :] * buf_ref[pl.ds(CONV_HALO - (CONV_K - 1) + k, rows), :]
    return acc * jax.nn.sigmoid(acc)


SSM_PAIR = LANE // SSM_HEAD_DIM
SSM_HEADS_PER_GROUP = SSM_HEADS // SSM_GROUPS


def _ssd_kernel(xbc_ref, z_ref, sm_ref, cw_ref, cb_ref, dtb_ref, alog_ref, dvec_ref, ng_ref, tri_ref, exp_ref,
                o_ref, tail_ref, buf_ref, state_ref):
    L = SSM_CHUNK
    N = SSM_STATE

    @pl.when(pl.program_id(1) == 0)
    def _():
        tail_ref[...] = jnp.zeros(tail_ref.shape, f32)
        state_ref[...] = jnp.zeros(state_ref.shape, f32)

    xbc = _causal_conv_silu(xbc_ref, tail_ref, buf_ref, cw_ref, cb_ref[...])
    xs = xbc[:, :SSM_INNER]
    dt = _softplus(sm_ref[...] + dtb_ref[...])
    a = dt * (-jnp.exp(alog_ref[...]))
    tri = tri_ref[...]
    ac = _exact_dot_left(tri, a)
    ac_t = ac.T
    eac = jnp.exp(ac)
    w_t = jnp.exp(ac_t[:, L - 1:L] - ac_t)
    chunk_decay = jnp.exp(ac[L - 1:L, :])
    xdt = xs * _exact_dot(dt, exp_ref[...])

    row = lax.broadcasted_iota(jnp.int32, (L, L), 0)
    col = lax.broadcasted_iota(jnp.int32, (L, L), 1)
    causal = row >= col
    lane = lax.broadcasted_iota(jnp.int32, (L, LANE), 1)
    first_half = lane < SSM_HEAD_DIM

    ys = []
    for g in range(SSM_GROUPS):
        bg = xbc[:, SSM_INNER + g * N:SSM_INNER + (g + 1) * N]
        cg = xbc[:, SSM_INNER + SSM_GROUPS * N + g * N:SSM_INNER + SSM_GROUPS * N + (g + 1) * N]
        cb = _dot_nt(cg.astype(bf16), bg.astype(bf16))
        bg_t = bg.T
        for i in range(g * SSM_HEADS_PER_GROUP // SSM_PAIR, (g + 1) * SSM_HEADS_PER_GROUP // SSM_PAIR):
            xdt_pair = xdt[:, i * LANE:(i + 1) * LANE].astype(bf16)
            outs = []
            for hh in range(SSM_PAIR):
                h = SSM_PAIR * i + hh
                c = SMALL_DT_LANE + h
                seg = ac[:, c:c + 1] - ac_t[c:c + 1, :]
                sc = (cb * jnp.exp(jnp.where(causal, seg, NEG))).astype(bf16)
                y = jnp.dot(sc, xdt_pair, preferred_element_type=f32)
                s_in = state_ref[h]
                y += jnp.dot((cg * eac[:, c:c + 1]).astype(bf16), s_in.astype(bf16), preferred_element_type=f32)
                st = jnp.dot((bg_t * w_t[c:c + 1, :]).astype(bf16), xdt_pair, preferred_element_type=f32)
                state_ref[h] = s_in * chunk_decay[:, c:c + 1] + st
                outs.append(y)
            y_pair = jnp.where(first_half, outs[0], outs[1])
            sl = slice(i * LANE, (i + 1) * LANE)
            y_pair = y_pair + xs[:, sl] * dvec_ref[:, sl]
            z = z_ref[:, sl]
            ys.append(y_pair * (z * jax.nn.sigmoid(z)))

    per_group = len(ys) // SSM_GROUPS
    gw = SSM_INNER // SSM_GROUPS
    for g in range(SSM_GROUPS):
        tiles = ys[g * per_group:(g + 1) * per_group]
        ms = sum(jnp.sum(t * t, axis=-1, keepdims=True) for t in tiles) / gw
        scale = lax.rsqrt(ms + EPS)
        for k, t in enumerate(tiles):
            sl = slice(g * gw + k * LANE, g * gw + (k + 1) * LANE)
            o_ref[:, sl] = (t * scale * ng_ref[:, sl]).astype(o_ref.dtype)


def _exact_dot_left(b_01, a_f32):
    return sum(jnp.dot(b_01, part, preferred_element_type=f32) for part in _split3(a_f32))


def _lane_vec(values, lane0):
    return jnp.zeros((1, LANE), f32).at[0, lane0:lane0 + values.shape[0]].set(values.astype(f32))


def _ssd(proj, offs, conv_w, conv_b, dt_bias, a_log, d_skip, norm_g, bsz, seq):
    m = bsz * seq
    L = SSM_CHUNK
    n_c = seq // L
    tri = jnp.asarray(np.tril(np.ones((L, L), np.float32)), bf16)
    expand = np.zeros((LANE, SSM_INNER), np.float32)
    for h in range(SSM_HEADS):
        expand[SMALL_DT_LANE + h, h * SSM_HEAD_DIM:(h + 1) * SSM_HEAD_DIM] = 1.0
    row = lambda w: pl.BlockSpec((1, w), lambda b, c: (0, 0))
    return pl.pallas_call(
        _ssd_kernel,
        grid=(bsz, n_c),
        in_specs=[
            pl.BlockSpec((L, SSM_CONV_CH), lambda b, c: (b * n_c + c, offs["c_xbc"] // SSM_CONV_CH)),
            pl.BlockSpec((L, SSM_INNER), lambda b, c: (b * n_c + c, offs["c_z"] // SSM_INNER)),
            pl.BlockSpec((L, LANE), lambda b, c: (b * n_c + c, offs["small"] // LANE)),
            pl.BlockSpec((CONV_K, SSM_CONV_CH), lambda b, c: (0, 0)),
            row(SSM_CONV_CH), row(LANE), row(LANE), row(SSM_INNER), row(SSM_INNER),
            pl.BlockSpec((L, L), lambda b, c: (0, 0)),
            pl.BlockSpec((LANE, SSM_INNER), lambda b, c: (0, 0)),
        ],
        out_specs=pl.BlockSpec((L, SSM_INNER), lambda b, c: (b * n_c + c, 0)),
        out_shape=jax.ShapeDtypeStruct((m, SSM_INNER), bf16),
        scratch_shapes=[
            pltpu.VMEM((CONV_HALO, SSM_CONV_CH), f32),
            pltpu.VMEM((CONV_HALO + L, SSM_CONV_CH), f32),
            pltpu.VMEM((SSM_HEADS, SSM_STATE, LANE), f32),
        ],
        compiler_params=pltpu.CompilerParams(dimension_semantics=("arbitrary", "arbitrary"),
                                             vmem_limit_bytes=V7X_VMEM_LIMIT_BYTES),
        name="ssd_mixer",
    )(proj, proj, proj, conv_w, conv_b.reshape(1, -1), _lane_vec(dt_bias, SMALL_DT_LANE),
      _lane_vec(a_log, SMALL_DT_LANE), jnp.repeat(d_skip, SSM_HEAD_DIM).reshape(1, -1), norm_g.reshape(1, -1),
      tri, jnp.asarray(expand, bf16))


GDN_STEP_ROWS = 128
GDN_QK_W = GDN_HEADS * GDN_DK
GDN_CONV_CH = GDN_HEADS * (2 * GDN_DK + GDN_DV)


def _split2(x):
    hi = x.astype(bf16)
    return hi, (x - hi.astype(f32)).astype(bf16)


def _dot_split(a_parts, b_parts):
    (ah, al), (bh, bl) = a_parts, b_parts
    return (jnp.dot(ah, bh, preferred_element_type=f32) + jnp.dot(ah, bl, preferred_element_type=f32)
            + jnp.dot(al, bh, preferred_element_type=f32))


def _unit_lower_solve_many(a_list, rhs_list):
    n = a_list[0].shape[0]
    ps = [_split2(-a) for a in a_list]
    xs = [rhs + _dot_split(p, _split2(rhs)) for p, rhs in zip(ps, rhs_list)]
    k = 2
    while k < n:
        ps = [_split2(_dot_split(p, p)) for p in ps]
        xs = [x + _dot_split(p, _split2(x)) for p, x in zip(ps, xs)]
        k *= 2
    return xs


def _gdn_kernel(qkv_ref, z_ref, sm_ref, cw_ref, dtb_ref, alog_ref, ng_ref, tri_ref, o_ref, tail_ref, buf_ref,
                state_ref):
    C = GDN_CHUNK

    @pl.when(pl.program_id(1) == 0)
    def _():
        tail_ref[...] = jnp.zeros(tail_ref.shape, f32)
        state_ref[...] = jnp.zeros(state_ref.shape, f32)

    qkv = _causal_conv_silu(qkv_ref, tail_ref, buf_ref, cw_ref, 0.0)
    sm = sm_ref[...]
    beta = jax.nn.sigmoid(sm)
    g_log = -jnp.exp(alog_ref[...]) * _softplus(sm + dtb_ref[...])
    gc = _exact_dot_left(tri_ref[...], g_log)
    gc_t = gc.T
    egc = jnp.exp(gc)

    row = lax.broadcasted_iota(jnp.int32, (C, C), 0)
    col = lax.broadcasted_iota(jnp.int32, (C, C), 1)
    causal = row >= col
    strict = row > col

    units = [(ch, h) for ch in range(GDN_STEP_ROWS // C) for h in range(GDN_HEADS)]
    a_list, rhs_list, pre = [], [], []
    for ch, h in units:
        rs = slice(ch * C, (ch + 1) * C)
        last = slice((ch + 1) * C - 1, (ch + 1) * C)
        q = qkv[rs, h * GDN_DK:(h + 1) * GDN_DK]
        k = qkv[rs, GDN_QK_W + h * GDN_DK:GDN_QK_W + (h + 1) * GDN_DK]
        v = qkv[rs, 2 * GDN_QK_W + h * GDN_DV:2 * GDN_QK_W + (h + 1) * GDN_DV]
        q = q * lax.rsqrt(jnp.sum(q * q, axis=-1, keepdims=True) + EPS) * (GDN_DK ** -0.5)
        k = k * lax.rsqrt(jnp.sum(k * k, axis=-1, keepdims=True) + EPS)
        ca = SMALL_A_LANE + h
        b = beta[rs, SMALL_B_LANE + h:SMALL_B_LANE + h + 1]
        g_col = gc[rs, ca:ca + 1]
        g_last = gc[last, ca:ca + 1]
        e_col = egc[rs, ca:ca + 1]
        decay = jnp.exp(jnp.where(causal, g_col - gc_t[ca:ca + 1, rs], NEG))
        kb = k * b
        k16 = k.astype(bf16)
        a_list.append(jnp.where(strict, _dot_nt(kb.astype(bf16), k16) * decay, 0.0))
        rhs_list.append(jnp.concatenate([v * b, kb * e_col], axis=1))
        attn = (_dot_nt(q.astype(bf16), k16) * decay).astype(bf16)
        q_dec = (q * e_col).astype(bf16)
        k_dec_t = (k * jnp.exp(g_last - g_col)).T.astype(bf16)
        pre.append((attn, q_dec, k_dec_t, jnp.exp(g_last)))

    sols = _unit_lower_solve_many(a_list, rhs_list)

    for (ch, h), sol, (attn, q_dec, k_dec_t, g_end) in zip(units, sols, pre):
        rs = slice(ch * C, (ch + 1) * C)
        u, w = sol[:, :GDN_DV], sol[:, GDN_DV:]
        s_in = state_ref[h]
        s16 = s_in.astype(bf16)
        v_new = u - jnp.dot(w.astype(bf16), s16, preferred_element_type=f32)
        v16 = v_new.astype(bf16)
        o = (jnp.dot(q_dec, s16, preferred_element_type=f32) + jnp.dot(attn, v16, preferred_element_type=f32))
        state_ref[h] = s_in * g_end + jnp.dot(k_dec_t, v16, preferred_element_type=f32)

        o = o * lax.rsqrt(jnp.mean(o * o, axis=-1, keepdims=True) + EPS) * ng_ref[...]
        z = z_ref[rs, h * GDN_DV:(h + 1) * GDN_DV]
        o_ref[rs, h * GDN_DV:(h + 1) * GDN_DV] = (o * (z * jax.nn.sigmoid(z))).astype(o_ref.dtype)


def _gdn(proj, offs, conv_w, a_log, dt_bias, norm_g, bsz, seq):
    m = bsz * seq
    R_ = GDN_STEP_ROWS
    n_s = seq // R_
    idx = np.arange(R_)
    tri = ((idx[:, None] >= idx[None, :]) & (idx[:, None] // GDN_CHUNK == idx[None, :] // GDN_CHUNK))
    row = lambda w: pl.BlockSpec((1, w), lambda b, c: (0, 0))
    return pl.pallas_call(
        _gdn_kernel,
        grid=(bsz, n_s),
        in_specs=[
            pl.BlockSpec((R_, GDN_CONV_CH), lambda b, c: (b * n_s + c, offs["a_q"] // GDN_CONV_CH)),
            pl.BlockSpec((R_, GDN_HEADS * GDN_DV), lambda b, c: (b * n_s + c, offs["a_z"] // (GDN_HEADS * GDN_DV))),
            pl.BlockSpec((R_, LANE), lambda b, c: (b * n_s + c, offs["small"] // LANE)),
            pl.BlockSpec((CONV_K, GDN_CONV_CH), lambda b, c: (0, 0)),
            row(LANE), row(LANE), row(GDN_DV),
            pl.BlockSpec((R_, R_), lambda b, c: (0, 0)),
        ],
        out_specs=pl.BlockSpec((R_, GDN_HEADS * GDN_DV), lambda b, c: (b * n_s + c, 0)),
        out_shape=jax.ShapeDtypeStruct((m, GDN_HEADS * GDN_DV), bf16),
        scratch_shapes=[
            pltpu.VMEM((CONV_HALO, GDN_CONV_CH), f32),
            pltpu.VMEM((CONV_HALO + R_, GDN_CONV_CH), f32),
            pltpu.VMEM((GDN_HEADS, GDN_DK, GDN_DV), f32),
        ],
        compiler_params=pltpu.CompilerParams(dimension_semantics=("arbitrary", "arbitrary"),
                                             vmem_limit_bytes=V7X_VMEM_LIMIT_BYTES),
        name="gdn_mixer",
    )(proj, proj, proj, conv_w, _lane_vec(dt_bias, SMALL_A_LANE), _lane_vec(a_log, SMALL_A_LANE),
      norm_g.reshape(1, -1), jnp.asarray(tri.astype(np.float32), bf16))


_IN_NAMES = ("a_q", "a_k", "a_v", "a_z", "a_a", "a_b", "b_q", "b_kc", "b_vc", "b_ks", "b_vs", "b_kw", "b_vw",
             "b_g", "c_z", "c_xbc", "c_dt", "m_gate")


def _in_sizes(d_model):
    return (GDN_HEADS * GDN_DK, GDN_HEADS * GDN_DK, GDN_HEADS * GDN_DV, GDN_HEADS * GDN_DV, GDN_HEADS, GDN_HEADS,
            NSA_HEADS * NSA_DK, NSA_KV_GROUPS * NSA_DK, NSA_KV_GROUPS * NSA_DV, NSA_KV_GROUPS * NSA_DK,
            NSA_KV_GROUPS * NSA_DV, NSA_KV_GROUPS * NSA_DK, NSA_KV_GROUPS * NSA_DV, 3 * NSA_HEADS,
            SSM_INNER, SSM_CONV_CH, SSM_HEADS, 3 * d_model)


_BIG_ORDER = ("m_gate", "a_q", "a_k", "a_v", "c_xbc", "c_z", "a_z", "b_q", "b_kc", "b_vc", "b_ks", "b_vs",
              "b_kw", "b_vw")
_SMALL_ORDER = ("a_a", "a_b", "c_dt")


def _in_layout(d_model, tn=512):
    sizes = dict(zip(_IN_NAMES, _in_sizes(d_model)))
    src_off = dict(zip(_IN_NAMES, np.cumsum((0,) + _in_sizes(d_model))[:-1].tolist()))
    cols, offs = [], {}

    def pad_to_lane():
        cols.extend([-1] * (_round_up(len(cols), LANE) - len(cols)))

    for name in _BIG_ORDER:
        offs[name] = len(cols)
        cols.extend(range(src_off[name], src_off[name] + sizes[name]))
    offs["small"] = len(cols)
    for name in _SMALL_ORDER:
        offs[name] = len(cols)
        cols.extend(range(src_off[name], src_off[name] + sizes[name]))
    pad_to_lane()
    offs["b_g"] = len(cols)
    per_group = 3 * NSA_HG
    for g in range(NSA_KV_GROUPS):
        cols.extend(range(src_off["b_g"] + g * per_group, src_off["b_g"] + (g + 1) * per_group))
        pad_to_lane()
    n_pad = _round_up(len(cols), tn)
    cols.extend([-1] * (n_pad - len(cols)))
    return np.asarray(cols, np.int32), offs, sizes, n_pad


def _prep_w_in(w_in_l, cols):
    pieces, start = [], 0
    for i in range(1, len(cols) + 1):
        same_run = i < len(cols) and ((cols[i] < 0 and cols[i - 1] < 0)
                                      or (cols[i - 1] >= 0 and cols[i] == cols[i - 1] + 1))
        if not same_run:
            if cols[start] < 0:
                pieces.append(jnp.zeros((w_in_l.shape[0], i - start), bf16))
            else:
                pieces.append(w_in_l[:, int(cols[start]):int(cols[start]) + i - start].astype(bf16))
            start = i
    return jnp.concatenate(pieces, axis=1)


def kernel(x, rel_table, g_ffn1, w_up1, w_down1, g_mix, w_in, gdn_conv, gdn_a_log, gdn_dt_bias, gdn_norm,
           nsa_q_norm, nsa_k_norm, nsa_pe_k, nsa_pe_v, nsa_w_ck, nsa_w_cv, ssm_conv_w, ssm_conv_b,
           ssm_dt_bias, ssm_a_log, ssm_d, ssm_norm, p_a, p_b, p_c, w_o, g_ffn2, w_up2, w_down2):
    bsz, seq, d = x.shape
    depth = w_in.shape[0]
    m = bsz * seq
    cols, offs, sizes, _ = _in_layout(d)
    x2 = x.reshape(m, d)
    nsa_consts = _nsa_constants(seq)

    def seg(proj, name):
        return proj[:, offs[name]:offs[name] + sizes[name]].reshape(bsz, seq, sizes[name])

    for l in range(depth):
        wa, wb, wd = _prep_ffn_weights(w_up1[l], w_down1[l])
        x2 = _ffn(x2, g_ffn1[l].reshape(1, d), wa, wb, wd)

        proj = _norm_matmul(x2, g_mix[l].reshape(1, d), _prep_w_in(w_in[l], cols))
        y_a = _gdn(proj, offs, gdn_conv[l], gdn_a_log[l], gdn_dt_bias[l], gdn_norm[l], bsz, seq)
        y_b = _nsa(proj, offs, nsa_q_norm[l], nsa_k_norm[l], nsa_pe_k[l], nsa_pe_v[l], nsa_w_ck[l], nsa_w_cv[l],
                   rel_table, nsa_consts, bsz, seq)
        y_c = _ssd(proj, offs, ssm_conv_w[l], ssm_conv_b[l], ssm_dt_bias[l], ssm_a_log[l], ssm_d[l], ssm_norm[l],
                   bsz, seq)
        x2 = _merge(x2, y_a, y_b, y_c, proj, offs["m_gate"],
                    p_a[l].astype(bf16), p_b[l].astype(bf16), p_c[l].astype(bf16), w_o[l].astype(bf16))

        wa, wb, wd = _prep_ffn_weights(w_up2[l], w_down2[l])
        x2 = _ffn(x2, g_ffn2[l].reshape(1, d), wa, wb, wd)
    return x2.reshape(bsz, seq, d)
```

```python
import functools
import math

import jax
import jax.numpy as jnp
import numpy as np
from jax import lax
from jax.experimental import pallas as pl
from jax.experimental.pallas import tpu as pltpu

EPS = 1e-6
CONV_K = 4

GDN_HEADS = 4
GDN_DK = 128
GDN_DV = 128
GDN_CHUNK = 64

NSA_HEADS = 4
NSA_KV_GROUPS = 2
NSA_DK = 128
NSA_DV = 128
CMP_BLOCK = 32
CMP_STRIDE = 16
SEL_BLOCK = 64
N_SELECT = 16
WINDOW = 512
Q_BLOCK = 128
SEL_OVERLAP_WEIGHTS = (1.0, 2.0, 2.0, 2.0, 1.0)
FORCE_SCORE = 1e4
NEG = -1e30

SSM_HEADS = 16
SSM_HEAD_DIM = 64
SSM_GROUPS = 2
SSM_STATE = 128
SSM_CHUNK = 128
SSM_INNER = SSM_HEADS * SSM_HEAD_DIM
SSM_CONV_CH = SSM_INNER + 2 * SSM_GROUPS * SSM_STATE

REL_BUCKETS = 32
REL_MAX_EXACT = 16
REL_MAX_DIST = 1024

V7X_VMEM_LIMIT_BYTES = 56 * 1024 * 1024
LANE = 128

bf16 = jnp.bfloat16
f32 = jnp.float32


def _round_up(n, m):
    return (n + m - 1) // m * m


def _ffn_kernel(x_ref, g_ref, wa_ref, wb_ref, wd_ref, o_ref, h_ref):
    j = pl.program_id(1)

    @pl.when(j == 0)
    def _():
        x = x_ref[...]
        h = x * lax.rsqrt(jnp.mean(x * x, axis=-1, keepdims=True) + EPS) * g_ref[...]
        h_ref[...] = h.astype(bf16)
        o_ref[...] = x

    h = h_ref[...]
    a = jnp.dot(h, wa_ref[...], preferred_element_type=f32)
    b = jnp.dot(h, wb_ref[...], preferred_element_type=f32)
    act = (0.5 * a * jax.nn.sigmoid(a) * b).astype(bf16)
    o_ref[...] += jnp.dot(act, wd_ref[...], preferred_element_type=f32)


def _ffn(x2, g, wa, wb, wd, *, tm=512, tf=512):
    m, d = x2.shape
    ffp = wa.shape[1]
    return pl.pallas_call(
        _ffn_kernel,
        grid=(m // tm, ffp // tf),
        in_specs=[
            pl.BlockSpec((tm, d), lambda i, j: (i, 0)),
            pl.BlockSpec((1, d), lambda i, j: (0, 0)),
            pl.BlockSpec((d, tf), lambda i, j: (0, j)),
            pl.BlockSpec((d, tf), lambda i, j: (0, j)),
            pl.BlockSpec((tf, d), lambda i, j: (j, 0)),
        ],
        out_specs=pl.BlockSpec((tm, d), lambda i, j: (i, 0)),
        out_shape=jax.ShapeDtypeStruct((m, d), f32),
        scratch_shapes=[pltpu.VMEM((tm, d), bf16)],
        compiler_params=pltpu.CompilerParams(
            dimension_semantics=("parallel", "arbitrary"),
            vmem_limit_bytes=V7X_VMEM_LIMIT_BYTES),
        name="ffn_swiglu",
    )(x2, g, wa, wb, wd)


def _norm_matmul_kernel(x_ref, g_ref, w_ref, o_ref, h_ref):
    j = pl.program_id(1)

    @pl.when(j == 0)
    def _():
        x = x_ref[...]
        h = x * lax.rsqrt(jnp.mean(x * x, axis=-1, keepdims=True) + EPS) * g_ref[...]
        h_ref[...] = h.astype(bf16)

    o_ref[...] = jnp.dot(h_ref[...], w_ref[...], preferred_element_type=f32)


def _norm_matmul(x2, g, w, *, tm=1024, tn=1024):
    m, d = x2.shape
    n = w.shape[1]
    return pl.pallas_call(
        _norm_matmul_kernel,
        grid=(m // tm, n // tn),
        in_specs=[
            pl.BlockSpec((tm, d), lambda i, j: (i, 0)),
            pl.BlockSpec((1, d), lambda i, j: (0, 0)),
            pl.BlockSpec((d, tn), lambda i, j: (0, j)),
        ],
        out_specs=pl.BlockSpec((tm, tn), lambda i, j: (i, j)),
        out_shape=jax.ShapeDtypeStruct((m, n), f32),
        scratch_shapes=[pltpu.VMEM((tm, d), bf16)],
        compiler_params=pltpu.CompilerParams(
            dimension_semantics=("parallel", "arbitrary"),
            vmem_limit_bytes=V7X_VMEM_LIMIT_BYTES),
        name="norm_in_proj",
    )(x2, g, w)


def _merge_kernel(x_ref, ya_ref, yb_ref, yc_ref, ga_ref, gb_ref, gc_ref, pa_ref, pb_ref, pc_ref, wo_ref, o_ref):
    ma = jnp.dot(ya_ref[...].astype(bf16), pa_ref[...], preferred_element_type=f32)
    mb = jnp.dot(yb_ref[...].astype(bf16), pb_ref[...], preferred_element_type=f32)
    mc = jnp.dot(yc_ref[...].astype(bf16), pc_ref[...], preferred_element_type=f32)
    merged = (jax.nn.sigmoid(ga_ref[...]) * ma + jax.nn.sigmoid(gb_ref[...]) * mb
              + jax.nn.sigmoid(gc_ref[...]) * mc)
    o_ref[...] = x_ref[...] + jnp.dot(merged.astype(bf16), wo_ref[...], preferred_element_type=f32)


def _merge(x2, ya, yb, yc, gates, gate_col0, pa, pb, pc, wo, *, tm=256):
    m, d = x2.shape
    gb0 = gate_col0 // d
    const = dict(pipeline_mode=pl.Buffered(1))
    return pl.pallas_call(
        _merge_kernel,
        grid=(m // tm,),
        in_specs=[
            pl.BlockSpec((tm, d), lambda i: (i, 0)),
            pl.BlockSpec((tm, ya.shape[1]), lambda i: (i, 0)),
            pl.BlockSpec((tm, yb.shape[1]), lambda i: (i, 0)),
            pl.BlockSpec((tm, yc.shape[1]), lambda i: (i, 0)),
            pl.BlockSpec((tm, d), lambda i: (i, gb0)),
            pl.BlockSpec((tm, d), lambda i: (i, gb0 + 1)),
            pl.BlockSpec((tm, d), lambda i: (i, gb0 + 2)),
            pl.BlockSpec(pa.shape, lambda i: (0, 0), **const),
            pl.BlockSpec(pb.shape, lambda i: (0, 0), **const),
            pl.BlockSpec(pc.shape, lambda i: (0, 0), **const),
            pl.BlockSpec(wo.shape, lambda i: (0, 0), **const),
        ],
        out_specs=pl.BlockSpec((tm, d), lambda i: (i, 0)),
        out_shape=jax.ShapeDtypeStruct((m, d), f32),
        compiler_params=pltpu.CompilerParams(
            dimension_semantics=("parallel",),
            vmem_limit_bytes=V7X_VMEM_LIMIT_BYTES),
        name="merge_out_proj",
    )(x2, ya, yb, yc, gates, gates, gates, pa, pb, pc, wo)


def _jx_rmsnorm(x, g):
    xf = x.astype(jnp.float32)
    y = xf * lax.rsqrt(jnp.mean(xf * xf, axis=-1, keepdims=True) + EPS)
    return (y * g.astype(jnp.float32)).astype(x.dtype)


def _jx_l2norm(t):
    return t * lax.rsqrt(jnp.sum(t * t, axis=-1, keepdims=True) + EPS)


def _jx_causal_dwconv(x, w):
    return lax.conv_general_dilated(x, w[:, None, :], window_strides=(1,), padding=[(CONV_K - 1, 0)],
                                    dimension_numbers=('NWC', 'WIO', 'NWC'), feature_group_count=x.shape[-1])


def _jx_rel_bucket(dist):
    n = jnp.maximum(dist, 0)
    nf = jnp.maximum(n, 1).astype(jnp.float32)
    large = REL_MAX_EXACT + (jnp.log(nf / REL_MAX_EXACT) / math.log(REL_MAX_DIST / REL_MAX_EXACT)
                             * (REL_BUCKETS - REL_MAX_EXACT)).astype(jnp.int32)
    large = jnp.minimum(large, REL_BUCKETS - 1)
    return jnp.where(n < REL_MAX_EXACT, n, large)


def _jx_gated_deltanet(q, k, v, z, a, b, conv_w, a_log, dt_bias, norm_g):
    Bsz, T, _ = q.shape
    H, C = GDN_HEADS, GDN_CHUNK
    n_c = T // C
    qk_w = H * GDN_DK
    qkv = jax.nn.silu(_jx_causal_dwconv(jnp.concatenate([q, k, v], axis=-1), conv_w)).astype(f32)

    def heads(t):
        return t.reshape(Bsz, n_c, C, H, -1).transpose(0, 3, 1, 2, 4)

    qh = _jx_l2norm(heads(qkv[..., :qk_w])) * GDN_DK ** -0.5
    kh = _jx_l2norm(heads(qkv[..., qk_w:2 * qk_w]))
    vh = heads(qkv[..., 2 * qk_w:])
    beta = jax.nn.sigmoid(b.astype(f32)).reshape(Bsz, n_c, C, H).transpose(0, 3, 1, 2)
    g = -jnp.exp(a_log.astype(f32)) * jax.nn.softplus(a.astype(f32) + dt_bias.astype(f32))
    gc = jnp.cumsum(g.reshape(Bsz, n_c, C, H).transpose(0, 3, 1, 2), axis=-1)
    causal = jnp.tril(jnp.ones((C, C), bool))
    strict = jnp.tril(jnp.ones((C, C), bool), -1)
    diff = gc[..., :, None] - gc[..., None, :]
    decay = jnp.where(causal, jnp.exp(jnp.where(causal, diff, 0.0)), 0.0)
    kb = kh * beta[..., None]
    m = jnp.where(strict, jnp.einsum('bhncd,bhnsd->bhncs', kb, kh) * decay, 0.0) + jnp.eye(C, dtype=f32)
    rhs = jnp.concatenate([vh * beta[..., None], kb * jnp.exp(gc)[..., None]], axis=-1)
    sol = lax.linalg.triangular_solve(m, rhs, left_side=True, lower=True, unit_diagonal=True)
    u, w = sol[..., :GDN_DV], sol[..., GDN_DV:]
    attn = jnp.einsum('bhncd,bhnsd->bhncs', qh, kh) * decay
    q_dec = qh * jnp.exp(gc)[..., None]
    k_dec = kh * jnp.exp(gc[..., -1:] - gc)[..., None]
    g_last = jnp.exp(gc[..., -1])

    def step(state, inp):
        u_c, w_c, a_c, qd_c, kd_c, gl_c = inp
        v_new = u_c - jnp.einsum('bhck,bhkv->bhcv', w_c, state)
        o_c = jnp.einsum('bhck,bhkv->bhcv', qd_c, state) + jnp.einsum('bhcs,bhsv->bhcv', a_c, v_new)
        state = state * gl_c[..., None, None] + jnp.einsum('bhck,bhcv->bhkv', kd_c, v_new)
        return state, o_c

    xs = (jnp.moveaxis(u, 2, 0), jnp.moveaxis(w, 2, 0), jnp.moveaxis(attn, 2, 0),
          jnp.moveaxis(q_dec, 2, 0), jnp.moveaxis(k_dec, 2, 0), jnp.moveaxis(g_last, 2, 0))
    _, o = lax.scan(step, jnp.zeros((Bsz, H, GDN_DK, GDN_DV), f32), xs)
    o = o.transpose(1, 0, 3, 2, 4).reshape(Bsz, T, H, GDN_DV)
    o = _jx_rmsnorm(o, norm_g) * jax.nn.silu(z.astype(f32).reshape(Bsz, T, H, GDN_DV))
    return o.reshape(Bsz, T, H * GDN_DV).astype(z.dtype)


def _jx_nsa_attention(q_in, kc, vc, ks, vs, kw, vw, gate_logits, q_norm, k_norm, pe_k, pe_v, w_ck, w_cv, rel_table):
    Bsz, T, _ = q_in.shape
    G, Hg = NSA_KV_GROUPS, NSA_HEADS // NSA_KV_GROUPS
    n_cmp = T // CMP_STRIDE - 1
    n_sel = T // SEL_BLOCK
    n_top = min(N_SELECT, n_sel)
    n_qb = T // Q_BLOCK
    ratio = SEL_BLOCK // CMP_STRIDE

    q = _jx_rmsnorm(q_in.reshape(Bsz, T, NSA_HEADS, NSA_DK), q_norm).astype(f32) * NSA_DK ** -0.5
    q_blocks = q.reshape(Bsz, n_qb, Q_BLOCK, G, Hg, NSA_DK).transpose(1, 0, 3, 4, 2, 5)

    def groups(t):
        return t.reshape(Bsz, T, G, -1).astype(f32)

    def compress(t, pe, w):
        c = groups(t).reshape(Bsz, T // CMP_STRIDE, CMP_STRIDE, G, -1)
        blocks = jnp.concatenate([c[:, :-1], c[:, 1:]], axis=2)
        return jnp.einsum('bnlgd,lde->bgne', blocks + pe.astype(f32)[:, None, :], w.astype(f32))

    k_cmp = _jx_rmsnorm(compress(kc, pe_k, w_ck), k_norm)
    v_cmp = compress(vc, pe_v, w_cv)
    k_sel = _jx_rmsnorm(groups(ks), k_norm).reshape(Bsz, n_sel, SEL_BLOCK, G, NSA_DK).transpose(0, 3, 1, 2, 4)
    v_sel = groups(vs).reshape(Bsz, n_sel, SEL_BLOCK, G, NSA_DV).transpose(0, 3, 1, 2, 4)
    pad = ((0, 0), (0, 0), (WINDOW, 0), (0, 0))
    k_win = jnp.pad(_jx_rmsnorm(groups(kw), k_norm).transpose(0, 2, 1, 3), pad)
    v_win = jnp.pad(groups(vw).transpose(0, 2, 1, 3), pad)

    table = rel_table.astype(f32).T.reshape(G, Hg, REL_BUCKETS)
    cmp_end = jnp.arange(n_cmp) * CMP_STRIDE + CMP_BLOCK - 1
    blk = jnp.arange(n_sel)
    bi = jnp.arange(Bsz)[:, None, None, None]
    gi = jnp.arange(G)[None, :, None, None]
    gi6 = jnp.arange(G)[None, :, None, None, None, None]
    hi6 = jnp.arange(Hg)[None, None, :, None, None, None]

    def block(args):
        qb, qi = args
        t = qi * Q_BLOCK + jnp.arange(Q_BLOCK)
        dist = t[:, None] - cmp_end[None, :]
        ok = dist >= 0
        s = jnp.einsum('bghqd,bgnd->bghqn', qb, k_cmp) + table[:, :, _jx_rel_bucket(dist)]
        p_cmp = jnp.where(ok, jax.nn.softmax(jnp.where(ok, s, NEG), axis=-1), 0.0)
        o_cmp = jnp.einsum('bghqn,bgnd->bghqd', p_cmp, v_cmp)
        imp_c = jnp.pad(p_cmp.sum(axis=2), ((0, 0), (0, 0), (0, 0), (1, ratio * n_sel - n_cmp)))
        imp = sum(wt * imp_c[..., j:j + ratio * n_sel:ratio] for j, wt in enumerate(SEL_OVERLAP_WEIGHTS))
        cur = t // SEL_BLOCK
        ok_s = blk[None, :] <= cur[:, None]
        forced = ok_s & ((blk[None, :] == 0) | (blk[None, :] >= cur[:, None] - 1))
        score = jnp.where(forced, FORCE_SCORE, jnp.where(ok_s, imp, -FORCE_SCORE))
        _, idx = lax.top_k(score, n_top)
        kg = k_sel[bi, gi, idx]
        vg = v_sel[bi, gi, idx]
        dist = t[:, None, None] - (idx[..., None] * SEL_BLOCK + jnp.arange(SEL_BLOCK))
        ok = (dist >= 0)[:, :, None]
        s = jnp.einsum('bghqd,bgqkld->bghqkl', qb, kg) + table[gi6, hi6, _jx_rel_bucket(dist)[:, :, None]]
        s = jnp.where(ok, s, NEG).reshape(Bsz, G, Hg, Q_BLOCK, n_top * SEL_BLOCK)
        p = jax.nn.softmax(s, axis=-1).reshape(Bsz, G, Hg, Q_BLOCK, n_top, SEL_BLOCK)
        o_sel = jnp.einsum('bghqkl,bgqkld->bghqd', p, vg)
        kwb = lax.dynamic_slice_in_dim(k_win, qi * Q_BLOCK, Q_BLOCK + WINDOW, axis=2)
        vwb = lax.dynamic_slice_in_dim(v_win, qi * Q_BLOCK, Q_BLOCK + WINDOW, axis=2)
        pos = qi * Q_BLOCK - WINDOW + jnp.arange(Q_BLOCK + WINDOW)
        dist = t[:, None] - pos[None, :]
        ok = (dist >= 0) & (dist < WINDOW) & (pos >= 0)[None, :]
        s = jnp.einsum('bghqd,bgkd->bghqk', qb, kwb) + table[:, :, _jx_rel_bucket(dist)]
        p = jax.nn.softmax(jnp.where(ok, s, NEG), axis=-1)
        o_win = jnp.einsum('bghqk,bgkd->bghqd', p, vwb)
        return jnp.stack([o_cmp, o_sel, o_win], axis=-2)

    out = lax.map(block, (q_blocks, jnp.arange(n_qb)))
    out = out.transpose(1, 0, 4, 2, 3, 5, 6).reshape(Bsz, T, NSA_HEADS, 3, NSA_DV)
    gates = jax.nn.sigmoid(gate_logits.astype(f32)).reshape(Bsz, T, NSA_HEADS, 3)
    o = jnp.einsum('bthr,bthrd->bthd', gates, out)
    return o.reshape(Bsz, T, NSA_HEADS * NSA_DV).astype(q_in.dtype)


def _jx_mamba2_ssd(z, xbc, dt, conv_w, conv_b, dt_bias, a_log, d_skip, norm_g):
    Bsz, T, _ = z.shape
    H, P, G, N, L = SSM_HEADS, SSM_HEAD_DIM, SSM_GROUPS, SSM_STATE, SSM_CHUNK
    n_c = T // L
    xbc = jax.nn.silu(_jx_causal_dwconv(xbc, conv_w) + conv_b).astype(f32)
    xs = xbc[..., :SSM_INNER].reshape(Bsz, n_c, L, H, P)
    bm = jnp.repeat(xbc[..., SSM_INNER:SSM_INNER + G * N].reshape(Bsz, n_c, L, G, N), H // G, axis=3)
    cm = jnp.repeat(xbc[..., SSM_INNER + G * N:].reshape(Bsz, n_c, L, G, N), H // G, axis=3)
    dt = jax.nn.softplus(dt.astype(f32) + dt_bias.astype(f32)).reshape(Bsz, n_c, L, H)
    ac = jnp.cumsum(dt * (-jnp.exp(a_log.astype(f32))), axis=2)
    xdt = xs * dt[..., None]
    ach = jnp.swapaxes(ac, 2, 3)
    causal = jnp.tril(jnp.ones((L, L), bool))
    seg = ach[..., :, None] - ach[..., None, :]
    decay = jnp.where(causal, jnp.exp(jnp.where(causal, seg, 0.0)), 0.0)
    scores = jnp.einsum('bclhn,bcshn->bchls', cm, bm) * decay
    y = jnp.einsum('bchls,bcshp->bclhp', scores, xdt)
    states = jnp.einsum('bclhn,bclhp->bchpn', bm * jnp.exp(ac[:, :, -1:, :] - ac)[..., None], xdt)
    chunk_decay = jnp.exp(ac[:, :, -1, :])

    def step(s, inp):
        st, cd = inp
        return s * cd[..., None, None] + st, s

    _, s_in = lax.scan(step, jnp.zeros((Bsz, H, P, N), f32),
                       (jnp.moveaxis(states, 1, 0), jnp.moveaxis(chunk_decay, 1, 0)))
    s_in = jnp.moveaxis(s_in, 0, 1)
    y = y + jnp.einsum('bclhn,bchpn->bclhp', cm * jnp.exp(ac)[..., None], s_in) + xs * d_skip.astype(f32)[:, None]
    y = y.reshape(Bsz, T, SSM_INNER) * jax.nn.silu(z.astype(f32))
    y = y.reshape(Bsz, T, G, SSM_INNER // G)
    y = y * lax.rsqrt(jnp.mean(y * y, axis=-1, keepdims=True) + EPS)
    return (y.reshape(Bsz, T, SSM_INNER) * norm_g.astype(f32)).astype(z.dtype)


NSA_HG = NSA_HEADS // NSA_KV_GROUPS
SEL_FAR_TILE = 512
SEL_FAR_SUB = 2
NEAR_TILES = 8
STRIP_W = NEAR_TILES * Q_BLOCK
STRIP_D0 = (NEAR_TILES - 1) * Q_BLOCK
WIN_TILES = WINDOW // Q_BLOCK + 1
CMP_PER_Q = Q_BLOCK // CMP_STRIDE
SEL_PER_Q = Q_BLOCK // SEL_BLOCK
MASK_PEN = -1e30


def _dot_nt(a, b):
    return lax.dot_general(a, b, (((1,), (1,)), ((), ())), preferred_element_type=f32)


def _lane_rms(x, gain):
    return x * lax.rsqrt(jnp.mean(x * x, axis=-1, keepdims=True) + EPS) * gain


def _nsa_prep_kernel(q_ref, ks_ref, kw_ref, qg_ref, kg_ref, qo_ref, kso_ref, kwo_ref):
    qg = qg_ref[...] * (NSA_DK ** -0.5)
    kg = kg_ref[...]
    for h in range(NSA_HEADS):
        sl = slice(h * NSA_DK, (h + 1) * NSA_DK)
        qo_ref[:, sl] = _lane_rms(q_ref[:, sl], qg).astype(bf16)
    kw_ = NSA_KV_GROUPS * NSA_DK
    for src, dst in ((ks_ref, kso_ref), (kw_ref, kwo_ref)):
        for g in range(NSA_KV_GROUPS):
            sl = slice(g * NSA_DK, (g + 1) * NSA_DK)
            dst[:, sl] = _lane_rms(src[:, sl], kg).astype(bf16)
        dst[:, kw_:] = src[:, kw_:].astype(bf16)


def _nsa_prep(proj, offs, q_gain, k_gain, *, tm=512):
    m = proj.shape[0]
    w = NSA_HEADS * NSA_DK
    blk = lambda name: pl.BlockSpec((tm, w), lambda i, c=offs[name] // w: (i, c))
    out = jax.ShapeDtypeStruct((m, w), bf16)
    return pl.pallas_call(
        _nsa_prep_kernel,
        grid=(m // tm,),
        in_specs=[blk("b_q"), blk("b_ks"), blk("b_kw"),
                  pl.BlockSpec((1, NSA_DK), lambda i: (0, 0)), pl.BlockSpec((1, NSA_DK), lambda i: (0, 0))],
        out_specs=[pl.BlockSpec((tm, w), lambda i: (i, 0))] * 3,
        out_shape=[out, out, out],
        compiler_params=pltpu.CompilerParams(dimension_semantics=("parallel",),
                                             vmem_limit_bytes=V7X_VMEM_LIMIT_BYTES),
        name="nsa_prep",
    )(proj, proj, proj, q_gain, k_gain)


def _nsa_compress_kernel(x_ref, w_ref, pe_ref, kg_ref, o_ref, hi_ref):
    j = pl.program_id(1)
    nc = o_ref.shape[2]
    lo = jnp.zeros((nc, NSA_DK), f32)
    hi = jnp.zeros((nc, NSA_DK), f32)
    for l in range(CMP_STRIDE):
        rows = x_ref[pl.ds(l, nc, stride=CMP_STRIDE), :]
        lo += jnp.dot((rows + pe_ref[0, l:l + 1, :]).astype(bf16), w_ref[0, l], preferred_element_type=f32)
        hi += jnp.dot((rows + pe_ref[0, CMP_STRIDE + l:CMP_STRIDE + l + 1, :]).astype(bf16),
                      w_ref[0, CMP_STRIDE + l], preferred_element_type=f32)
    hi_ref[pl.ds(0, nc), :] = hi
    hi_ref[pl.ds(nc, 8), :] = jnp.zeros((8, NSA_DK), f32)
    c = lo + hi_ref[pl.ds(1, nc), :]
    normed = _lane_rms(c, kg_ref[...])
    o_ref[0, 0] = jnp.where(j < NSA_KV_GROUPS, normed, c).astype(bf16)


def _nsa_compress(proj, offs, w_stack, pe_stack, k_gain, bsz, seq):
    nc = seq // CMP_STRIDE
    c0 = offs["b_kc"] // NSA_DK
    return pl.pallas_call(
        _nsa_compress_kernel,
        grid=(bsz, 2 * NSA_KV_GROUPS),
        in_specs=[
            pl.BlockSpec((seq, NSA_DK), lambda b, j: (b, c0 + j)),
            pl.BlockSpec((1, CMP_BLOCK, NSA_DK, NSA_DK), lambda b, j: (j // NSA_KV_GROUPS, 0, 0, 0)),
            pl.BlockSpec((1, CMP_BLOCK, NSA_DK), lambda b, j: (j // NSA_KV_GROUPS, 0, 0)),
            pl.BlockSpec((1, NSA_DK), lambda b, j: (0, 0)),
        ],
        out_specs=pl.BlockSpec((1, 1, nc, NSA_DK), lambda b, j: (b, j, 0, 0)),
        out_shape=jax.ShapeDtypeStruct((bsz, 2 * NSA_KV_GROUPS, nc, NSA_DK), bf16),
        scratch_shapes=[pltpu.VMEM((nc + 8, NSA_DK), f32)],
        compiler_params=pltpu.CompilerParams(dimension_semantics=("parallel", "parallel"),
                                             vmem_limit_bytes=V7X_VMEM_LIMIT_BYTES),
        name="nsa_compress",
    )(proj, w_stack, pe_stack, k_gain)


def _split3(x):
    hi = x.astype(bf16)
    r1 = x - hi.astype(f32)
    mid = r1.astype(bf16)
    lo = (r1 - mid.astype(f32)).astype(bf16)
    return hi, mid, lo


def _nsa_attn_kernel(tbl_ref, q_ref, kc_ref, vc_ref, ks_ref, vs_ref, kw_ref, vw_ref, et_ref, mt_ref, bs_ref,
                     bc_ref, gl_ref, o_ref, strip_ref, sc_ref, score_ref, stage_ref):
    g = pl.program_id(1)
    qi = pl.program_id(2)
    nc = kc_ref.shape[2]
    qb = Q_BLOCK

    @pl.when(qi == 0)
    def _():
        def lookup(bucket):
            v0 = jnp.zeros(bucket.shape, f32)
            v1 = jnp.zeros(bucket.shape, f32)
            for k in range(REL_BUCKETS):
                eq = bucket == k
                v0 = jnp.where(eq, tbl_ref[NSA_HG * g, k], v0)
                v1 = jnp.where(eq, tbl_ref[NSA_HG * g + 1, k], v1)
            return v0, v1

        for c in range(NEAR_TILES):
            sl = slice(c * qb, (c + 1) * qb)
            v0, v1 = lookup(bs_ref[:, sl])
            strip_ref[0, :, sl] = v0
            strip_ref[1, :, sl] = v1
        v0, v1 = lookup(bc_ref[...])
        for hh, v in enumerate((v0, v1)):
            d = v - tbl_ref[NSA_HG * g + hh, REL_BUCKETS - 1]
            hi = d.astype(bf16)
            sc_ref[hh, 0] = hi
            sc_ref[hh, 1] = (d - hi.astype(f32)).astype(bf16)

    far_bias = [tbl_ref[NSA_HG * g + hh, REL_BUCKETS - 1] for hh in range(NSA_HG)]
    q = q_ref[...]
    q2 = jnp.concatenate([q[:, :NSA_DK], q[:, NSA_DK:]], axis=0)

    r_i = lax.broadcasted_iota(jnp.int32, (qb, nc), 0)
    n_i = lax.broadcasted_iota(jnp.int32, (qb, nc), 1)
    ok_c = CMP_STRIDE * n_i <= qb * qi + r_i - (CMP_BLOCK - 1)
    shift = jnp.where(r_i + n_i == CMP_PER_Q * qi + CMP_PER_Q - 1, 1.0, 0.0).astype(bf16)
    kc = kc_ref[0, 0]
    vc = vc_ref[0, 0]
    psum = jnp.zeros((qb, nc), f32)
    o_cmp = []
    for hh in range(NSA_HG):
        s = _dot_nt(q2[hh * qb:(hh + 1) * qb], kc)
        bias = (far_bias[hh] + jnp.dot(sc_ref[hh, 0], shift, preferred_element_type=f32)
                + jnp.dot(sc_ref[hh, 1], shift, preferred_element_type=f32))
        s = jnp.where(ok_c, s + bias, NEG)
        e = jnp.exp(s - jnp.max(s, axis=-1, keepdims=True))
        p = jnp.where(ok_c, e / jnp.sum(e, axis=-1, keepdims=True), 0.0)
        o_cmp.append(jnp.dot(p.astype(bf16), vc, preferred_element_type=f32))
        psum += p

    mt = mt_ref[...]
    imp_t = sum(_dot_nt(mt, part) for part in _split3(psum))
    nblk = imp_t.shape[0]
    j_i = lax.broadcasted_iota(jnp.int32, (nblk, qb), 0)
    l_i = lax.broadcasted_iota(jnp.int32, (nblk, qb), 1)
    cur = SEL_PER_Q * qi + l_i // SEL_BLOCK
    ok_s = j_i <= cur
    forced = ok_s & ((j_i == 0) | (j_i >= cur - 1))
    score = jnp.where(forced, FORCE_SCORE, jnp.where(ok_s, imp_t, -FORCE_SCORE))
    score_ref[...] = score

    def rank_body(i, cnt):
        for u in range(SEL_PER_Q):
            jp = SEL_PER_Q * i + u
            row = score_ref[pl.ds(jp, 1), :]
            beats = (row > score) | ((row == score) & (j_i > jp))
            cnt = cnt + jnp.where(beats, 1.0, 0.0)
        return cnt

    cnt = lax.fori_loop(0, qi + 1, rank_body, jnp.zeros((nblk, qb), f32))
    pen = jnp.where(cnt < N_SELECT, 0.0, MASK_PEN).T.astype(bf16)
    pen2 = jnp.concatenate([pen, pen], axis=0)

    row2 = lax.broadcasted_iota(jnp.int32, (NSA_HG * qb, 1), 0)
    far_col = jnp.where(row2 < qb, far_bias[0], far_bias[1])
    n_far_keys = qb * jnp.maximum(qi - (NEAR_TILES - 1), 0)
    far_step = SEL_FAR_SUB * SEL_FAR_TILE
    c_far = lax.broadcasted_iota(jnp.int32, (NSA_HG * qb, SEL_FAR_TILE), 1)

    def far_body(kt, carry):
        m_old, l_old, acc_old = carry
        ss, vv = [], []
        for sub in range(SEL_FAR_SUB):
            k0 = kt * far_step + sub * SEL_FAR_TILE
            rows = pl.ds(pl.multiple_of(k0, SEL_FAR_TILE), SEL_FAR_TILE)
            s = _dot_nt(q2, ks_ref[rows, :]) + _dot_nt(pen2, et_ref[rows, :]) + far_col
            ss.append(jnp.where(c_far < n_far_keys - k0, s, MASK_PEN))
            vv.append(vs_ref[rows, :])
        m_new = functools.reduce(jnp.maximum, [m_old] + [jnp.max(s, axis=-1, keepdims=True) for s in ss])
        ps = [jnp.exp(s - m_new) for s in ss]
        alpha = jnp.exp(m_old - m_new)
        l_new = alpha * l_old + sum(jnp.sum(p, axis=-1, keepdims=True) for p in ps)
        acc_new = alpha * acc_old + sum(jnp.dot(p.astype(bf16), v, preferred_element_type=f32)
                                        for p, v in zip(ps, vv))
        return m_new, l_new, acc_new

    m_far, l_far, acc_far = lax.fori_loop(
        0, (n_far_keys + far_step - 1) // far_step, far_body,
        (jnp.full((NSA_HG * qb, 1), MASK_PEN, f32), jnp.zeros((NSA_HG * qb, 1), f32),
         jnp.zeros((NSA_HG * qb, NSA_DV), f32)))

    r2 = lax.broadcasted_iota(jnp.int32, (NSA_HG * qb, qb), 0) % qb
    c2 = lax.broadcasted_iota(jnp.int32, (NSA_HG * qb, qb), 1)

    def strip_bias(w):
        sl = slice(w * qb, (w + 1) * qb)
        return jnp.concatenate([strip_ref[0, :, sl], strip_ref[1, :, sl]], axis=0)

    def key_rows(w):
        kt = qi - (NEAR_TILES - 1) + w
        return kt, pl.ds(pl.multiple_of(jnp.maximum(kt, 0) * qb, qb), qb)

    m_sel = m_far
    for w in range(NEAR_TILES):
        kt, rows = key_rows(w)
        s = _dot_nt(q2, ks_ref[rows, :]) + _dot_nt(pen2, et_ref[rows, :]) + strip_bias(w)
        if w == NEAR_TILES - 1:
            s = jnp.where(c2 <= r2, s, NEG)
        else:
            s = jnp.where(kt >= 0, s, NEG)
        stage_ref[w] = s
        m_sel = jnp.maximum(m_sel, jnp.max(s, axis=-1, keepdims=True))
    alpha = jnp.exp(m_far - m_sel)
    l_sel = alpha * l_far
    acc_sel = alpha * acc_far
    for w in range(NEAR_TILES):
        _, rows = key_rows(w)
        e = jnp.exp(stage_ref[w] - m_sel)
        l_sel = l_sel + jnp.sum(e, axis=-1, keepdims=True)
        acc_sel = acc_sel + jnp.dot(e.astype(bf16), vs_ref[rows, :], preferred_element_type=f32)
    o_sel = acc_sel / l_sel

    m_w = jnp.full((NSA_HG * qb, 1), NEG, f32)
    for w in range(NEAR_TILES - WIN_TILES, NEAR_TILES):
        kt, rows = key_rows(w)
        s = _dot_nt(q2, kw_ref[rows, :]) + strip_bias(w)
        if w == NEAR_TILES - WIN_TILES:
            valid = (c2 > r2) & (kt >= 0)
        elif w == NEAR_TILES - 1:
            valid = c2 <= r2
        else:
            valid = jnp.broadcast_to(kt >= 0, c2.shape)
        s = jnp.where(valid, s, NEG)
        stage_ref[w] = s
        m_w = jnp.maximum(m_w, jnp.max(s, axis=-1, keepdims=True))
    den = jnp.zeros((NSA_HG * qb, 1), f32)
    acc_w = jnp.zeros((NSA_HG * qb, NSA_DV), f32)
    for w in range(NEAR_TILES - WIN_TILES, NEAR_TILES):
        _, rows = key_rows(w)
        e = jnp.exp(stage_ref[w] - m_w)
        den = den + jnp.sum(e, axis=-1, keepdims=True)
        acc_w = acc_w + jnp.dot(e.astype(bf16), vw_ref[rows, :], preferred_element_type=f32)
    o_win = acc_w / den

    gate = jax.nn.sigmoid(gl_ref[...])
    for hh in range(NSA_HG):
        rs = slice(hh * qb, (hh + 1) * qb)
        o_ref[:, hh * NSA_DV:(hh + 1) * NSA_DV] = (gate[:, 3 * hh:3 * hh + 1] * o_cmp[hh]
                                                   + gate[:, 3 * hh + 1:3 * hh + 2] * o_sel[rs]
                                                   + gate[:, 3 * hh + 2:3 * hh + 3] * o_win[rs])


def _rel_bucket(dist):
    n = jnp.maximum(dist, 0)
    nf = jnp.maximum(n, 1).astype(f32)
    large = REL_MAX_EXACT + (jnp.log(nf / REL_MAX_EXACT) / math.log(REL_MAX_DIST / REL_MAX_EXACT)
                             * (REL_BUCKETS - REL_MAX_EXACT)).astype(jnp.int32)
    large = jnp.minimum(large, REL_BUCKETS - 1)
    return jnp.where(n < REL_MAX_EXACT, n, large)


def _nsa_constants(seq):
    nc = seq // CMP_STRIDE
    nblk = max(seq // SEL_BLOCK, LANE)
    r = jnp.arange(Q_BLOCK)[:, None]
    bs = _rel_bucket(STRIP_D0 + r - jnp.arange(STRIP_W)[None, :])
    bc = _rel_bucket(r + CMP_STRIDE * jnp.arange(LANE)[None, :] - (CMP_STRIDE * (CMP_PER_Q - 1) + CMP_BLOCK - 1))
    et = (np.arange(seq)[:, None] // SEL_BLOCK == np.arange(nblk)[None, :]).astype(np.float32)
    ratio = SEL_BLOCK // CMP_STRIDE
    mt = np.zeros((nblk, nc), np.float32)
    for j in range(nblk):
        for i, wt in enumerate(SEL_OVERLAP_WEIGHTS):
            mcol = ratio * j - 1 + i
            if 0 <= mcol < nc:
                mt[j, mcol] = wt
    return bs.astype(jnp.int32), bc.astype(jnp.int32), jnp.asarray(et, bf16), jnp.asarray(mt, bf16)


def _nsa_attention(qn, cmp_kv, ksvs, kwvw, proj, offs, tbl, consts, bsz, seq):
    bs, bc, et, mt = consts
    m = bsz * seq
    n_qb = seq // Q_BLOCK
    nc = seq // CMP_STRIDE
    nblk = et.shape[1]
    g_ = NSA_KV_GROUPS
    kv = lambda col0: pl.BlockSpec((seq, NSA_DK), lambda b, g, i: (b, col0 + g))
    cmp_spec = lambda j0: pl.BlockSpec((1, 1, nc, NSA_DK), lambda b, g, i: (b, j0 + g, 0, 0))
    const2 = lambda shape: pl.BlockSpec(shape, lambda b, g, i: (0, 0))
    gcol = offs["b_g"] // LANE
    return pl.pallas_call(
        _nsa_attn_kernel,
        grid=(bsz, g_, n_qb),
        in_specs=[
            pl.BlockSpec(memory_space=pltpu.SMEM),
            pl.BlockSpec((Q_BLOCK, NSA_HG * NSA_DK), lambda b, g, i: (b * n_qb + i, g)),
            cmp_spec(0), cmp_spec(g_),
            kv(0), kv(g_), kv(0), kv(g_),
            const2((seq, nblk)), const2((nblk, nc)), const2((Q_BLOCK, STRIP_W)), const2((Q_BLOCK, LANE)),
            pl.BlockSpec((Q_BLOCK, LANE), lambda b, g, i: (b * n_qb + i, gcol + g)),
        ],
        out_specs=pl.BlockSpec((Q_BLOCK, NSA_HG * NSA_DV), lambda b, g, i: (b * n_qb + i, g)),
        out_shape=jax.ShapeDtypeStruct((m, NSA_HEADS * NSA_DV), f32),
        scratch_shapes=[
            pltpu.VMEM((NSA_HG, Q_BLOCK, STRIP_W), f32),
            pltpu.VMEM((NSA_HG, 2, Q_BLOCK, LANE), bf16),
            pltpu.VMEM((nblk, Q_BLOCK), f32),
            pltpu.VMEM((NEAR_TILES, NSA_HG * Q_BLOCK, Q_BLOCK), f32),
        ],
        compiler_params=pltpu.CompilerParams(dimension_semantics=("arbitrary", "arbitrary", "arbitrary"),
                                             vmem_limit_bytes=V7X_VMEM_LIMIT_BYTES),
        name="nsa_attention",
    )(tbl, qn, cmp_kv, cmp_kv, ksvs, ksvs, kwvw, kwvw, et, mt, bs, bc, proj)


def _nsa(proj, offs, q_norm, k_norm, pe_k, pe_v, w_ck, w_cv, rel_table, consts, bsz, seq):
    q_gain = q_norm.reshape(1, NSA_DK)
    k_gain = k_norm.reshape(1, NSA_DK)
    qn, ksvs, kwvw = _nsa_prep(proj, offs, q_gain, k_gain)
    cmp_kv = _nsa_compress(proj, offs, jnp.stack([w_ck, w_cv]).astype(bf16), jnp.stack([pe_k, pe_v]), k_gain,
                           bsz, seq)
    return _nsa_attention(qn, cmp_kv, ksvs, kwvw, proj, offs, rel_table.T, consts, bsz, seq)


SMALL_A_LANE = 0
SMALL_B_LANE = GDN_HEADS
SMALL_DT_LANE = 2 * GDN_HEADS
CONV_HALO = 8


def _softplus(x):
    return jnp.maximum(x, 0.0) + jnp.log1p(jnp.exp(-jnp.abs(x)))


def _exact_dot(a_f32, b_01):
    return sum(jnp.dot(part, b_01, preferred_element_type=f32) for part in _split3(a_f32))


def _causal_conv_silu(x_ref, tail_ref, buf_ref, w_ref, bias):
    rows = x_ref.shape[0]
    buf_ref[pl.ds(0, CONV_HALO), :] = tail_ref[...]
    buf_ref[pl.ds(CONV_HALO, rows), :] = x_ref[...]
    tail_ref[...] = x_ref[pl.ds(rows - CONV_HALO, CONV_HALO), :]
    acc = bias
    for k in range(CONV_K):
        acc = acc + w_ref[k:k + 1, :] * buf_ref[pl.ds(CONV_HALO - (CONV_K - 1) + k, rows), :]
    return acc * jax.nn.sigmoid(acc)


SSM_PAIR = LANE // SSM_HEAD_DIM
SSM_HEADS_PER_GROUP = SSM_HEADS // SSM_GROUPS


def _ssd_kernel(xbc_ref, z_ref, sm_ref, cw_ref, cb_ref, dtb_ref, alog_ref, dvec_ref, ng_ref, tri_ref, exp_ref,
                o_ref, tail_ref, buf_ref, state_ref):
    L = SSM_CHUNK
    N = SSM_STATE

    @pl.when(pl.program_id(1) == 0)
    def _():
        tail_ref[...] = jnp.zeros(tail_ref.shape, f32)
        state_ref[...] = jnp.zeros(state_ref.shape, f32)

    xbc = _causal_conv_silu(xbc_ref, tail_ref, buf_ref, cw_ref, cb_ref[...])
    xs = xbc[:, :SSM_INNER]
    dt = _softplus(sm_ref[...] + dtb_ref[...])
    a = dt * (-jnp.exp(alog_ref[...]))
    tri = tri_ref[...]
    ac = _exact_dot_left(tri, a)
    ac_t = ac.T
    eac = jnp.exp(ac)
    w_t = jnp.exp(ac_t[:, L - 1:L] - ac_t)
    chunk_decay = jnp.exp(ac[L - 1:L, :])
    xdt = xs * _exact_dot(dt, exp_ref[...])

    row = lax.broadcasted_iota(jnp.int32, (L, L), 0)
    col = lax.broadcasted_iota(jnp.int32, (L, L), 1)
    causal = row >= col
    lane = lax.broadcasted_iota(jnp.int32, (L, LANE), 1)
    first_half = lane < SSM_HEAD_DIM

    ys = []
    for g in range(SSM_GROUPS):
        bg = xbc[:, SSM_INNER + g * N:SSM_INNER + (g + 1) * N]
        cg = xbc[:, SSM_INNER + SSM_GROUPS * N + g * N:SSM_INNER + SSM_GROUPS * N + (g + 1) * N]
        cb = _dot_nt(cg.astype(bf16), bg.astype(bf16))
        bg_t = bg.T
        for i in range(g * SSM_HEADS_PER_GROUP // SSM_PAIR, (g + 1) * SSM_HEADS_PER_GROUP // SSM_PAIR):
            xdt_pair = xdt[:, i * LANE:(i + 1) * LANE].astype(bf16)
            outs = []
            for hh in range(SSM_PAIR):
                h = SSM_PAIR * i + hh
                c = SMALL_DT_LANE + h
                seg = ac[:, c:c + 1] - ac_t[c:c + 1, :]
                sc = (cb * jnp.exp(jnp.where(causal, seg, NEG))).astype(bf16)
                y = jnp.dot(sc, xdt_pair, preferred_element_type=f32)
                s_in = state_ref[h]
                y += jnp.dot((cg * eac[:, c:c + 1]).astype(bf16), s_in.astype(bf16), preferred_element_type=f32)
                st = jnp.dot((bg_t * w_t[c:c + 1, :]).astype(bf16), xdt_pair, preferred_element_type=f32)
                state_ref[h] = s_in * chunk_decay[:, c:c + 1] + st
                outs.append(y)
            y_pair = jnp.where(first_half, outs[0], outs[1])
            sl = slice(i * LANE, (i + 1) * LANE)
            y_pair = y_pair + xs[:, sl] * dvec_ref[:, sl]
            z = z_ref[:, sl]
            ys.append(y_pair * (z * jax.nn.sigmoid(z)))

    per_group = len(ys) // SSM_GROUPS
    gw = SSM_INNER // SSM_GROUPS
    for g in range(SSM_GROUPS):
        tiles = ys[g * per_group:(g + 1) * per_group]
        ms = sum(jnp.sum(t * t, axis=-1, keepdims=True) for t in tiles) / gw
        scale = lax.rsqrt(ms + EPS)
        for k, t in enumerate(tiles):
            sl = slice(g * gw + k * LANE, g * gw + (k + 1) * LANE)
            o_ref[:, sl] = (t * scale * ng_ref[:, sl]).astype(o_ref.dtype)


def _exact_dot_left(b_01, a_f32):
    return sum(jnp.dot(b_01, part, preferred_element_type=f32) for part in _split3(a_f32))


def _lane_vec(values, lane0):
    return jnp.zeros((1, LANE), f32).at[0, lane0:lane0 + values.shape[0]].set(values.astype(f32))


def _ssd(proj, offs, conv_w, conv_b, dt_bias, a_log, d_skip, norm_g, bsz, seq):
    m = bsz * seq
    L = SSM_CHUNK
    n_c = seq // L
    tri = jnp.asarray(np.tril(np.ones((L, L), np.float32)), bf16)
    expand = np.zeros((LANE, SSM_INNER), np.float32)
    for h in range(SSM_HEADS):
        expand[SMALL_DT_LANE + h, h * SSM_HEAD_DIM:(h + 1) * SSM_HEAD_DIM] = 1.0
    row = lambda w: pl.BlockSpec((1, w), lambda b, c: (0, 0))
    return pl.pallas_call(
        _ssd_kernel,
        grid=(bsz, n_c),
        in_specs=[
            pl.BlockSpec((L, SSM_CONV_CH), lambda b, c: (b * n_c + c, offs["c_xbc"] // SSM_CONV_CH)),
            pl.BlockSpec((L, SSM_INNER), lambda b, c: (b * n_c + c, offs["c_z"] // SSM_INNER)),
            pl.BlockSpec((L, LANE), lambda b, c: (b * n_c + c, offs["small"] // LANE)),
            pl.BlockSpec((CONV_K, SSM_CONV_CH), lambda b, c: (0, 0)),
            row(SSM_CONV_CH), row(LANE), row(LANE), row(SSM_INNER), row(SSM_INNER),
            pl.BlockSpec((L, L), lambda b, c: (0, 0)),
            pl.BlockSpec((LANE, SSM_INNER), lambda b, c: (0, 0)),
        ],
        out_specs=pl.BlockSpec((L, SSM_INNER), lambda b, c: (b * n_c + c, 0)),
        out_shape=jax.ShapeDtypeStruct((m, SSM_INNER), bf16),
        scratch_shapes=[
            pltpu.VMEM((CONV_HALO, SSM_CONV_CH), f32),
            pltpu.VMEM((CONV_HALO + L, SSM_CONV_CH), f32),
            pltpu.VMEM((SSM_HEADS, SSM_STATE, LANE), f32),
        ],
        compiler_params=pltpu.CompilerParams(dimension_semantics=("arbitrary", "arbitrary"),
                                             vmem_limit_bytes=V7X_VMEM_LIMIT_BYTES),
        name="ssd_mixer",
    )(proj, proj, proj, conv_w, conv_b.reshape(1, -1), _lane_vec(dt_bias, SMALL_DT_LANE),
      _lane_vec(a_log, SMALL_DT_LANE), jnp.repeat(d_skip, SSM_HEAD_DIM).reshape(1, -1), norm_g.reshape(1, -1),
      tri, jnp.asarray(expand, bf16))


GDN_STEP_ROWS = 128
GDN_QK_W = GDN_HEADS * GDN_DK
GDN_CONV_CH = GDN_HEADS * (2 * GDN_DK + GDN_DV)


def _split2(x):
    hi = x.astype(bf16)
    return hi, (x - hi.astype(f32)).astype(bf16)


def _dot_split(a_parts, b_parts):
    (ah, al), (bh, bl) = a_parts, b_parts
    return (jnp.dot(ah, bh, preferred_element_type=f32) + jnp.dot(ah, bl, preferred_element_type=f32)
            + jnp.dot(al, bh, preferred_element_type=f32))


def _unit_lower_solve_many(a_list, rhs_list):
    n = a_list[0].shape[0]
    ps = [_split2(-a) for a in a_list]
    xs = [rhs + _dot_split(p, _split2(rhs)) for p, rhs in zip(ps, rhs_list)]
    k = 2
    while k < n:
        ps = [_split2(_dot_split(p, p)) for p in ps]
        xs = [x + _dot_split(p, _split2(x)) for p, x in zip(ps, xs)]
        k *= 2
    return xs


def _gdn_kernel(qkv_ref, z_ref, sm_ref, cw_ref, dtb_ref, alog_ref, ng_ref, tri_ref, o_ref, tail_ref, buf_ref,
                state_ref):
    C = GDN_CHUNK

    @pl.when(pl.program_id(1) == 0)
    def _():
        tail_ref[...] = jnp.zeros(tail_ref.shape, f32)
        state_ref[...] = jnp.zeros(state_ref.shape, f32)

    qkv = _causal_conv_silu(qkv_ref, tail_ref, buf_ref, cw_ref, 0.0)
    sm = sm_ref[...]
    beta = jax.nn.sigmoid(sm)
    g_log = -jnp.exp(alog_ref[...]) * _softplus(sm + dtb_ref[...])
    gc = _exact_dot_left(tri_ref[...], g_log)
    gc_t = gc.T
    egc = jnp.exp(gc)

    row = lax.broadcasted_iota(jnp.int32, (C, C), 0)
    col = lax.broadcasted_iota(jnp.int32, (C, C), 1)
    causal = row >= col
    strict = row > col

    units = [(ch, h) for ch in range(GDN_STEP_ROWS // C) for h in range(GDN_HEADS)]
    a_list, rhs_list, pre = [], [], []
    for ch, h in units:
        rs = slice(ch * C, (ch + 1) * C)
        last = slice((ch + 1) * C - 1, (ch + 1) * C)
        q = qkv[rs, h * GDN_DK:(h + 1) * GDN_DK]
        k = qkv[rs, GDN_QK_W + h * GDN_DK:GDN_QK_W + (h + 1) * GDN_DK]
        v = qkv[rs, 2 * GDN_QK_W + h * GDN_DV:2 * GDN_QK_W + (h + 1) * GDN_DV]
        q = q * lax.rsqrt(jnp.sum(q * q, axis=-1, keepdims=True) + EPS) * (GDN_DK ** -0.5)
        k = k * lax.rsqrt(jnp.sum(k * k, axis=-1, keepdims=True) + EPS)
        ca = SMALL_A_LANE + h
        b = beta[rs, SMALL_B_LANE + h:SMALL_B_LANE + h + 1]
        g_col = gc[rs, ca:ca + 1]
        g_last = gc[last, ca:ca + 1]
        e_col = egc[rs, ca:ca + 1]
        decay = jnp.exp(jnp.where(causal, g_col - gc_t[ca:ca + 1, rs], NEG))
        kb = k * b
        k16 = k.astype(bf16)
        a_list.append(jnp.where(strict, _dot_nt(kb.astype(bf16), k16) * decay, 0.0))
        rhs_list.append(jnp.concatenate([v * b, kb * e_col], axis=1))
        attn = (_dot_nt(q.astype(bf16), k16) * decay).astype(bf16)
        q_dec = (q * e_col).astype(bf16)
        k_dec_t = (k * jnp.exp(g_last - g_col)).T.astype(bf16)
        pre.append((attn, q_dec, k_dec_t, jnp.exp(g_last)))

    sols = _unit_lower_solve_many(a_list, rhs_list)

    for (ch, h), sol, (attn, q_dec, k_dec_t, g_end) in zip(units, sols, pre):
        rs = slice(ch * C, (ch + 1) * C)
        u, w = sol[:, :GDN_DV], sol[:, GDN_DV:]
        s_in = state_ref[h]
        s16 = s_in.astype(bf16)
        v_new = u - jnp.dot(w.astype(bf16), s16, preferred_element_type=f32)
        v16 = v_new.astype(bf16)
        o = (jnp.dot(q_dec, s16, preferred_element_type=f32) + jnp.dot(attn, v16, preferred_element_type=f32))
        state_ref[h] = s_in * g_end + jnp.dot(k_dec_t, v16, preferred_element_type=f32)

        o = o * lax.rsqrt(jnp.mean(o * o, axis=-1, keepdims=True) + EPS) * ng_ref[...]
        z = z_ref[rs, h * GDN_DV:(h + 1) * GDN_DV]
        o_ref[rs, h * GDN_DV:(h + 1) * GDN_DV] = (o * (z * jax.nn.sigmoid(z))).astype(o_ref.dtype)


def _gdn(proj, offs, conv_w, a_log, dt_bias, norm_g, bsz, seq):
    m = bsz * seq
    R_ = GDN_STEP_ROWS
    n_s = seq // R_
    idx = np.arange(R_)
    tri = ((idx[:, None] >= idx[None, :]) & (idx[:, None] // GDN_CHUNK == idx[None, :] // GDN_CHUNK))
    row = lambda w: pl.BlockSpec((1, w), lambda b, c: (0, 0))
    return pl.pallas_call(
        _gdn_kernel,
        grid=(bsz, n_s),
        in_specs=[
            pl.BlockSpec((R_, GDN_CONV_CH), lambda b, c: (b * n_s + c, offs["a_q"] // GDN_CONV_CH)),
            pl.BlockSpec((R_, GDN_HEADS * GDN_DV), lambda b, c: (b * n_s + c, offs["a_z"] // (GDN_HEADS * GDN_DV))),
            pl.BlockSpec((R_, LANE), lambda b, c: (b * n_s + c, offs["small"] // LANE)),
            pl.BlockSpec((CONV_K, GDN_CONV_CH), lambda b, c: (0, 0)),
            row(LANE), row(LANE), row(GDN_DV),
            pl.BlockSpec((R_, R_), lambda b, c: (0, 0)),
        ],
        out_specs=pl.BlockSpec((R_, GDN_HEADS * GDN_DV), lambda b, c: (b * n_s + c, 0)),
        out_shape=jax.ShapeDtypeStruct((m, GDN_HEADS * GDN_DV), bf16),
        scratch_shapes=[
            pltpu.VMEM((CONV_HALO, GDN_CONV_CH), f32),
            pltpu.VMEM((CONV_HALO + R_, GDN_CONV_CH), f32),
            pltpu.VMEM((GDN_HEADS, GDN_DK, GDN_DV), f32),
        ],
        compiler_params=pltpu.CompilerParams(dimension_semantics=("arbitrary", "arbitrary"),
                                             vmem_limit_bytes=V7X_VMEM_LIMIT_BYTES),
        name="gdn_mixer",
    )(proj, proj, proj, conv_w, _lane_vec(dt_bias, SMALL_A_LANE), _lane_vec(a_log, SMALL_A_LANE),
      norm_g.reshape(1, -1), jnp.asarray(tri.astype(np.float32), bf16))


_IN_NAMES = ("a_q", "a_k", "a_v", "a_z", "a_a", "a_b", "b_q", "b_kc", "b_vc", "b_ks", "b_vs", "b_kw", "b_vw",
             "b_g", "c_z", "c_xbc", "c_dt", "m_gate")


def _in_sizes(d_model):
    return (GDN_HEADS * GDN_DK, GDN_HEADS * GDN_DK, GDN_HEADS * GDN_DV, GDN_HEADS * GDN_DV, GDN_HEADS, GDN_HEADS,
            NSA_HEADS * NSA_DK, NSA_KV_GROUPS * NSA_DK, NSA_KV_GROUPS * NSA_DV, NSA_KV_GROUPS * NSA_DK,
            NSA_KV_GROUPS * NSA_DV, NSA_KV_GROUPS * NSA_DK, NSA_KV_GROUPS * NSA_DV, 3 * NSA_HEADS,
            SSM_INNER, SSM_CONV_CH, SSM_HEADS, 3 * d_model)


_BIG_ORDER = ("m_gate", "a_q", "a_k", "a_v", "c_xbc", "c_z", "a_z", "b_q", "b_kc", "b_vc", "b_ks", "b_vs",
              "b_kw", "b_vw")
_SMALL_ORDER = ("a_a", "a_b", "c_dt")


def _in_layout(d_model, tn=512):
    sizes = dict(zip(_IN_NAMES, _in_sizes(d_model)))
    src_off = dict(zip(_IN_NAMES, np.cumsum((0,) + _in_sizes(d_model))[:-1].tolist()))
    cols, offs = [], {}

    def pad_to_lane():
        cols.extend([-1] * (_round_up(len(cols), LANE) - len(cols)))

    for name in _BIG_ORDER:
        offs[name] = len(cols)
        cols.extend(range(src_off[name], src_off[name] + sizes[name]))
    offs["small"] = len(cols)
    for name in _SMALL_ORDER:
        offs[name] = len(cols)
        cols.extend(range(src_off[name], src_off[name] + sizes[name]))
    pad_to_lane()
    offs["b_g"] = len(cols)
    per_group = 3 * NSA_HG
    for g in range(NSA_KV_GROUPS):
        cols.extend(range(src_off["b_g"] + g * per_group, src_off["b_g"] + (g + 1) * per_group))
        pad_to_lane()
    n_pad = _round_up(len(cols), tn)
    cols.extend([-1] * (n_pad - len(cols)))
    return np.asarray(cols, np.int32), offs, sizes, n_pad


def _col_runs(cols):
    runs, start = [], 0
    for i in range(1, len(cols) + 1):
        same_run = i < len(cols) and ((cols[i] < 0 and cols[i - 1] < 0)
                                      or (cols[i - 1] >= 0 and cols[i] == cols[i - 1] + 1))
        if not same_run:
            runs.append((start, None if cols[start] < 0 else int(cols[start]), i - start))
            start = i
    return runs


def _cast_cols_kernel(runs_per_out, w_ref, *o_refs):
    for runs, o_ref in zip(runs_per_out, o_refs):
        for dst, src, size in runs:
            if src is None:
                o_ref[:, dst:dst + size] = jnp.zeros((o_ref.shape[0], size), o_ref.dtype)
            else:
                o_ref[:, dst:dst + size] = w_ref[0, :, src:src + size].astype(o_ref.dtype)


def _cast_cols(w_stack, layer, runs_per_out, widths, *, tr=256):
    _, rows, n_src = w_stack.shape
    return pl.pallas_call(
        functools.partial(_cast_cols_kernel, runs_per_out),
        grid=(rows // tr,),
        in_specs=[pl.BlockSpec((1, tr, n_src), lambda i: (layer, i, 0))],
        out_specs=[pl.BlockSpec((tr, w), lambda i: (i, 0)) for w in widths],
        out_shape=[jax.ShapeDtypeStruct((rows, w), bf16) for w in widths],
        compiler_params=pltpu.CompilerParams(dimension_semantics=("parallel",),
                                             vmem_limit_bytes=V7X_VMEM_LIMIT_BYTES),
        name="weight_cast_cols",
    )(w_stack)


def _cast_rows_kernel(rows_src, w_ref, o_ref):
    tr = o_ref.shape[0]
    row = lax.broadcasted_iota(jnp.int32, o_ref.shape, 0) + pl.program_id(0) * tr
    o_ref[...] = jnp.where(row < rows_src, w_ref[0], 0.0).astype(o_ref.dtype)


def _cast_rows(w_stack, layer, rows_out, *, tr=512):
    _, rows_src, n = w_stack.shape
    return pl.pallas_call(
        functools.partial(_cast_rows_kernel, rows_src),
        grid=(rows_out // tr,),
        in_specs=[pl.BlockSpec((1, tr, n), lambda i: (layer, i, 0))],
        out_specs=pl.BlockSpec((tr, n), lambda i: (i, 0)),
        out_shape=jax.ShapeDtypeStruct((rows_out, n), bf16),
        compiler_params=pltpu.CompilerParams(dimension_semantics=("parallel",),
                                             vmem_limit_bytes=V7X_VMEM_LIMIT_BYTES),
        name="weight_cast_rows",
    )(w_stack)


def _prep_w_in(w_in, layer, cols):
    return _cast_cols(w_in, layer, [_col_runs(cols)], [len(cols)])[0]


def _prep_ffn(w_up, w_down, layer, tf=512):
    ff = w_down.shape[1]
    ffp = _round_up(ff, tf)
    half = lambda src: [(0, src, ff)] + ([(ff, None, ffp - ff)] if ffp > ff else [])
    wa, wb = _cast_cols(w_up, layer, [half(0), half(ff)], [ffp, ffp])
    return wa, wb, _cast_rows(w_down, layer, ffp)


def kernel(x, rel_table, g_ffn1, w_up1, w_down1, g_mix, w_in, gdn_conv, gdn_a_log, gdn_dt_bias, gdn_norm,
           nsa_q_norm, nsa_k_norm, nsa_pe_k, nsa_pe_v, nsa_w_ck, nsa_w_cv, ssm_conv_w, ssm_conv_b,
           ssm_dt_bias, ssm_a_log, ssm_d, ssm_norm, p_a, p_b, p_c, w_o, g_ffn2, w_up2, w_down2):
    bsz, seq, d = x.shape
    depth = w_in.shape[0]
    m = bsz * seq
    cols, offs, sizes, _ = _in_layout(d)
    x2 = x.reshape(m, d)
    nsa_consts = _nsa_constants(seq)

    def seg(proj, name):
        return proj[:, offs[name]:offs[name] + sizes[name]].reshape(bsz, seq, sizes[name])

    for l in range(depth):
        wa, wb, wd = _prep_ffn(w_up1, w_down1, l)
        x2 = _ffn(x2, g_ffn1[l].reshape(1, d), wa, wb, wd)

        proj = _norm_matmul(x2, g_mix[l].reshape(1, d), _prep_w_in(w_in, l, cols))
        y_a = _gdn(proj, offs, gdn_conv[l], gdn_a_log[l], gdn_dt_bias[l], gdn_norm[l], bsz, seq)
        y_b = _nsa(proj, offs, nsa_q_norm[l], nsa_k_norm[l], nsa_pe_k[l], nsa_pe_v[l], nsa_w_ck[l], nsa_w_cv[l],
                   rel_table, nsa_consts, bsz, seq)
        y_c = _ssd(proj, offs, ssm_conv_w[l], ssm_conv_b[l], ssm_dt_bias[l], ssm_a_log[l], ssm_d[l], ssm_norm[l],
                   bsz, seq)
        x2 = _merge(x2, y_a, y_b, y_c, proj, offs["m_gate"],
                    p_a[l].astype(bf16), p_b[l].astype(bf16), p_c[l].astype(bf16), w_o[l].astype(bf16))

        wa, wb, wd = _prep_ffn(w_up2, w_down2, l)
        x2 = _ffn(x2, g_ffn2[l].reshape(1, d), wa, wb, wd)
    return x2.reshape(bsz, seq, d)
```

```python
import functools
import math

import jax
import jax.numpy as jnp
import numpy as np
from jax import lax
from jax.experimental import pallas as pl
from jax.experimental.pallas import tpu as pltpu

EPS = 1e-6
CONV_K = 4

GDN_HEADS = 4
GDN_DK = 128
GDN_DV = 128
GDN_CHUNK = 64

NSA_HEADS = 4
NSA_KV_GROUPS = 2
NSA_DK = 128
NSA_DV = 128
CMP_BLOCK = 32
CMP_STRIDE = 16
SEL_BLOCK = 64
N_SELECT = 16
WINDOW = 512
Q_BLOCK = 128
SEL_OVERLAP_WEIGHTS = (1.0, 2.0, 2.0, 2.0, 1.0)
FORCE_SCORE = 1e4
NEG = -1e30

SSM_HEADS = 16
SSM_HEAD_DIM = 64
SSM_GROUPS = 2
SSM_STATE = 128
SSM_CHUNK = 128
SSM_INNER = SSM_HEADS * SSM_HEAD_DIM
SSM_CONV_CH = SSM_INNER + 2 * SSM_GROUPS * SSM_STATE

REL_BUCKETS = 32
REL_MAX_EXACT = 16
REL_MAX_DIST = 1024

V7X_VMEM_LIMIT_BYTES = 56 * 1024 * 1024
LANE = 128

bf16 = jnp.bfloat16
f32 = jnp.float32


def _round_up(n, m):
    return (n + m - 1) // m * m


def _ffn_kernel(x_ref, g_ref, wa_ref, wb_ref, wd_ref, o_ref, h_ref):
    j = pl.program_id(1)

    @pl.when(j == 0)
    def _():
        x = x_ref[...]
        h = x * lax.rsqrt(jnp.mean(x * x, axis=-1, keepdims=True) + EPS) * g_ref[...]
        h_ref[...] = h.astype(bf16)
        o_ref[...] = x

    h = h_ref[...]
    a = jnp.dot(h, wa_ref[...], preferred_element_type=f32)
    b = jnp.dot(h, wb_ref[...], preferred_element_type=f32)
    act = (0.5 * a * jax.nn.sigmoid(a) * b).astype(bf16)
    o_ref[...] += jnp.dot(act, wd_ref[...], preferred_element_type=f32)


def _ffn(x2, g, wa, wb, wd, *, tm=1024, tf=512):
    m, d = x2.shape
    ffp = wa.shape[1]
    return pl.pallas_call(
        _ffn_kernel,
        grid=(m // tm, ffp // tf),
        in_specs=[
            pl.BlockSpec((tm, d), lambda i, j: (i, 0)),
            pl.BlockSpec((1, d), lambda i, j: (0, 0)),
            pl.BlockSpec((d, tf), lambda i, j: (0, j)),
            pl.BlockSpec((d, tf), lambda i, j: (0, j)),
            pl.BlockSpec((tf, d), lambda i, j: (j, 0)),
        ],
        out_specs=pl.BlockSpec((tm, d), lambda i, j: (i, 0)),
        out_shape=jax.ShapeDtypeStruct((m, d), f32),
        scratch_shapes=[pltpu.VMEM((tm, d), bf16)],
        compiler_params=pltpu.CompilerParams(
            dimension_semantics=("parallel", "arbitrary"),
            vmem_limit_bytes=V7X_VMEM_LIMIT_BYTES),
        name="ffn_swiglu",
    )(x2, g, wa, wb, wd)


def _norm_matmul_kernel(x_ref, g_ref, w_ref, o_ref, h_ref):
    j = pl.program_id(1)

    @pl.when(j == 0)
    def _():
        x = x_ref[...]
        h = x * lax.rsqrt(jnp.mean(x * x, axis=-1, keepdims=True) + EPS) * g_ref[...]
        h_ref[...] = h.astype(bf16)

    o_ref[...] = jnp.dot(h_ref[...], w_ref[...], preferred_element_type=f32)


def _norm_matmul(x2, g, w, *, tm=1024, tn=1024):
    m, d = x2.shape
    n = w.shape[1]
    return pl.pallas_call(
        _norm_matmul_kernel,
        grid=(m // tm, n // tn),
        in_specs=[
            pl.BlockSpec((tm, d), lambda i, j: (i, 0)),
            pl.BlockSpec((1, d), lambda i, j: (0, 0)),
            pl.BlockSpec((d, tn), lambda i, j: (0, j)),
        ],
        out_specs=pl.BlockSpec((tm, tn), lambda i, j: (i, j)),
        out_shape=jax.ShapeDtypeStruct((m, n), f32),
        scratch_shapes=[pltpu.VMEM((tm, d), bf16)],
        compiler_params=pltpu.CompilerParams(
            dimension_semantics=("parallel", "arbitrary"),
            vmem_limit_bytes=V7X_VMEM_LIMIT_BYTES),
        name="norm_in_proj",
    )(x2, g, w)


def _merge_kernel(x_ref, ya_ref, yb_ref, yc_ref, ga_ref, gb_ref, gc_ref, pa_ref, pb_ref, pc_ref, wo_ref, o_ref):
    ma = jnp.dot(ya_ref[...].astype(bf16), pa_ref[...], preferred_element_type=f32)
    mb = jnp.dot(yb_ref[...].astype(bf16), pb_ref[...], preferred_element_type=f32)
    mc = jnp.dot(yc_ref[...].astype(bf16), pc_ref[...], preferred_element_type=f32)
    merged = (jax.nn.sigmoid(ga_ref[...]) * ma + jax.nn.sigmoid(gb_ref[...]) * mb
              + jax.nn.sigmoid(gc_ref[...]) * mc)
    o_ref[...] = x_ref[...] + jnp.dot(merged.astype(bf16), wo_ref[...], preferred_element_type=f32)


def _merge(x2, ya, yb, yc, gates, gate_col0, pa, pb, pc, wo, *, tm=256):
    m, d = x2.shape
    gb0 = gate_col0 // d
    const = dict(pipeline_mode=pl.Buffered(1))
    return pl.pallas_call(
        _merge_kernel,
        grid=(m // tm,),
        in_specs=[
            pl.BlockSpec((tm, d), lambda i: (i, 0)),
            pl.BlockSpec((tm, ya.shape[1]), lambda i: (i, 0)),
            pl.BlockSpec((tm, yb.shape[1]), lambda i: (i, 0)),
            pl.BlockSpec((tm, yc.shape[1]), lambda i: (i, 0)),
            pl.BlockSpec((tm, d), lambda i: (i, gb0)),
            pl.BlockSpec((tm, d), lambda i: (i, gb0 + 1)),
            pl.BlockSpec((tm, d), lambda i: (i, gb0 + 2)),
            pl.BlockSpec(pa.shape, lambda i: (0, 0), **const),
            pl.BlockSpec(pb.shape, lambda i: (0, 0), **const),
            pl.BlockSpec(pc.shape, lambda i: (0, 0), **const),
            pl.BlockSpec(wo.shape, lambda i: (0, 0), **const),
        ],
        out_specs=pl.BlockSpec((tm, d), lambda i: (i, 0)),
        out_shape=jax.ShapeDtypeStruct((m, d), f32),
        compiler_params=pltpu.CompilerParams(
            dimension_semantics=("parallel",),
            vmem_limit_bytes=V7X_VMEM_LIMIT_BYTES),
        name="merge_out_proj",
    )(x2, ya, yb, yc, gates, gates, gates, pa, pb, pc, wo)


NSA_HG = NSA_HEADS // NSA_KV_GROUPS
SEL_FAR_TILE = 512
SEL_FAR_SUB = 2
NEAR_TILES = 8
STRIP_W = NEAR_TILES * Q_BLOCK
STRIP_D0 = (NEAR_TILES - 1) * Q_BLOCK
WIN_TILES = WINDOW // Q_BLOCK + 1
CMP_PER_Q = Q_BLOCK // CMP_STRIDE
SEL_PER_Q = Q_BLOCK // SEL_BLOCK
MASK_PEN = -1e30


def _dot_nt(a, b):
    return lax.dot_general(a, b, (((1,), (1,)), ((), ())), preferred_element_type=f32)


def _lane_rms(x, gain):
    return x * lax.rsqrt(jnp.mean(x * x, axis=-1, keepdims=True) + EPS) * gain


def _nsa_prep_kernel(seq, q_ref, ks_ref, kw_ref, qg_ref, kg_ref, qo_ref, kao_ref, vso_ref, kwo_ref):
    tm = q_ref.shape[0]
    qg = qg_ref[...] * (NSA_DK ** -0.5)
    kg = kg_ref[...]
    for h in range(NSA_HEADS):
        sl = slice(h * NSA_DK, (h + 1) * NSA_DK)
        qo_ref[:, sl] = _lane_rms(q_ref[:, sl], qg).astype(bf16)
    kw_ = NSA_KV_GROUPS * NSA_DK
    tok = (lax.broadcasted_iota(jnp.int32, (tm, LANE), 0) + pl.program_id(0) * tm) % seq
    onehot = jnp.where(lax.broadcasted_iota(jnp.int32, (tm, LANE), 1) == tok // SEL_BLOCK, 1.0, 0.0).astype(bf16)
    for g in range(NSA_KV_GROUPS):
        sl = slice(g * NSA_DK, (g + 1) * NSA_DK)
        kao_ref[:, 2 * g * NSA_DK:(2 * g + 1) * NSA_DK] = _lane_rms(ks_ref[:, sl], kg).astype(bf16)
        kao_ref[:, (2 * g + 1) * NSA_DK:(2 * g + 2) * NSA_DK] = onehot
        kwo_ref[:, sl] = _lane_rms(kw_ref[:, sl], kg).astype(bf16)
    vso_ref[...] = ks_ref[:, kw_:].astype(bf16)
    kwo_ref[:, kw_:] = kw_ref[:, kw_:].astype(bf16)


def _nsa_prep(proj, offs, q_gain, k_gain, seq, *, tm=512):
    m = proj.shape[0]
    w = NSA_HEADS * NSA_DK
    assert seq // SEL_BLOCK <= LANE and NSA_DK == LANE, "selection blocks must fit one 128-lane one-hot"
    blk = lambda name: pl.BlockSpec((tm, w), lambda i, c=offs[name] // w: (i, c))
    widths = (w, 2 * NSA_KV_GROUPS * NSA_DK, NSA_KV_GROUPS * NSA_DV, w)
    return pl.pallas_call(
        functools.partial(_nsa_prep_kernel, seq),
        grid=(m // tm,),
        in_specs=[blk("b_q"), blk("b_ks"), blk("b_kw"),
                  pl.BlockSpec((1, NSA_DK), lambda i: (0, 0)), pl.BlockSpec((1, NSA_DK), lambda i: (0, 0))],
        out_specs=[pl.BlockSpec((tm, wd), lambda i: (i, 0)) for wd in widths],
        out_shape=[jax.ShapeDtypeStruct((m, wd), bf16) for wd in widths],
        compiler_params=pltpu.CompilerParams(dimension_semantics=("parallel",),
                                             vmem_limit_bytes=V7X_VMEM_LIMIT_BYTES),
        name="nsa_prep",
    )(proj, proj, proj, q_gain, k_gain)


def _nsa_compress_kernel(x_ref, w_ref, pe_ref, kg_ref, o_ref, hi_ref):
    j = pl.program_id(1)
    nc = o_ref.shape[2]
    lo = jnp.zeros((nc, NSA_DK), f32)
    hi = jnp.zeros((nc, NSA_DK), f32)
    for l in range(CMP_STRIDE):
        rows = x_ref[pl.ds(l, nc, stride=CMP_STRIDE), :]
        lo += jnp.dot((rows + pe_ref[0, l:l + 1, :]).astype(bf16), w_ref[0, l], preferred_element_type=f32)
        hi += jnp.dot((rows + pe_ref[0, CMP_STRIDE + l:CMP_STRIDE + l + 1, :]).astype(bf16),
                      w_ref[0, CMP_STRIDE + l], preferred_element_type=f32)
    hi_ref[pl.ds(0, nc), :] = hi
    hi_ref[pl.ds(nc, 8), :] = jnp.zeros((8, NSA_DK), f32)
    c = lo + hi_ref[pl.ds(1, nc), :]
    normed = _lane_rms(c, kg_ref[...])
    o_ref[0, 0] = jnp.where(j < NSA_KV_GROUPS, normed, c).astype(bf16)


def _nsa_compress(proj, offs, w_stack, pe_stack, k_gain, bsz, seq):
    nc = seq // CMP_STRIDE
    c0 = offs["b_kc"] // NSA_DK
    return pl.pallas_call(
        _nsa_compress_kernel,
        grid=(bsz, 2 * NSA_KV_GROUPS),
        in_specs=[
            pl.BlockSpec((seq, NSA_DK), lambda b, j: (b, c0 + j)),
            pl.BlockSpec((1, CMP_BLOCK, NSA_DK, NSA_DK), lambda b, j: (j // NSA_KV_GROUPS, 0, 0, 0)),
            pl.BlockSpec((1, CMP_BLOCK, NSA_DK), lambda b, j: (j // NSA_KV_GROUPS, 0, 0)),
            pl.BlockSpec((1, NSA_DK), lambda b, j: (0, 0)),
        ],
        out_specs=pl.BlockSpec((1, 1, nc, NSA_DK), lambda b, j: (b, j, 0, 0)),
        out_shape=jax.ShapeDtypeStruct((bsz, 2 * NSA_KV_GROUPS, nc, NSA_DK), bf16),
        scratch_shapes=[pltpu.VMEM((nc + 8, NSA_DK), f32)],
        compiler_params=pltpu.CompilerParams(dimension_semantics=("parallel", "parallel"),
                                             vmem_limit_bytes=V7X_VMEM_LIMIT_BYTES),
        name="nsa_compress",
    )(proj, w_stack, pe_stack, k_gain)


def _split3(x):
    hi = x.astype(bf16)
    r1 = x - hi.astype(f32)
    mid = r1.astype(bf16)
    lo = (r1 - mid.astype(f32)).astype(bf16)
    return hi, mid, lo


def _nsa_attn_kernel(tbl_ref, q_ref, kc_ref, vc_ref, ka_ref, vs_ref, kw_ref, vw_ref, mt_ref, bs_ref,
                     bc_ref, gl_ref, o_ref, strip_ref, sc_ref, score_ref, stage_ref, far_ref):
    g = pl.program_id(1)
    qi = pl.program_id(2)
    nc = kc_ref.shape[2]
    qb = Q_BLOCK

    @pl.when(qi == 0)
    def _():
        def lookup(bucket):
            v0 = jnp.zeros(bucket.shape, f32)
            v1 = jnp.zeros(bucket.shape, f32)
            for k in range(REL_BUCKETS):
                eq = bucket == k
                v0 = jnp.where(eq, tbl_ref[NSA_HG * g, k], v0)
                v1 = jnp.where(eq, tbl_ref[NSA_HG * g + 1, k], v1)
            return v0, v1

        for c in range(NEAR_TILES):
            sl = slice(c * qb, (c + 1) * qb)
            v0, v1 = lookup(bs_ref[:, sl])
            strip_ref[0, :, sl] = v0
            strip_ref[1, :, sl] = v1
        v0, v1 = lookup(bc_ref[...])
        for hh, v in enumerate((v0, v1)):
            d = v - tbl_ref[NSA_HG * g + hh, REL_BUCKETS - 1]
            hi = d.astype(bf16)
            sc_ref[hh, 0] = hi
            sc_ref[hh, 1] = (d - hi.astype(f32)).astype(bf16)

    far_bias = [tbl_ref[NSA_HG * g + hh, REL_BUCKETS - 1] for hh in range(NSA_HG)]
    q = q_ref[...]
    q2 = jnp.concatenate([q[:, :NSA_DK], q[:, NSA_DK:]], axis=0)

    r_i = lax.broadcasted_iota(jnp.int32, (qb, nc), 0)
    n_i = lax.broadcasted_iota(jnp.int32, (qb, nc), 1)
    ok_c = CMP_STRIDE * n_i <= qb * qi + r_i - (CMP_BLOCK - 1)
    shift = jnp.where(r_i + n_i == CMP_PER_Q * qi + CMP_PER_Q - 1, 1.0, 0.0).astype(bf16)
    kc = kc_ref[0, 0]
    vc = vc_ref[0, 0]
    psum = jnp.zeros((qb, nc), f32)
    o_cmp = []
    for hh in range(NSA_HG):
        s = _dot_nt(q2[hh * qb:(hh + 1) * qb], kc)
        bias = (far_bias[hh] + jnp.dot(sc_ref[hh, 0], shift, preferred_element_type=f32)
                + jnp.dot(sc_ref[hh, 1], shift, preferred_element_type=f32))
        s = jnp.where(ok_c, s + bias, NEG)
        e = jnp.exp(s - jnp.max(s, axis=-1, keepdims=True))
        p = jnp.where(ok_c, e / jnp.sum(e, axis=-1, keepdims=True), 0.0)
        o_cmp.append(jnp.dot(p.astype(bf16), vc, preferred_element_type=f32))
        psum += p

    mt = mt_ref[...]
    imp_t = sum(_dot_nt(mt, part) for part in _split3(psum))
    nblk = imp_t.shape[0]
    j_i = lax.broadcasted_iota(jnp.int32, (nblk, qb), 0)
    l_i = lax.broadcasted_iota(jnp.int32, (nblk, qb), 1)
    cur = SEL_PER_Q * qi + l_i // SEL_BLOCK
    ok_s = j_i <= cur
    forced = ok_s & ((j_i == 0) | (j_i >= cur - 1))
    score = jnp.where(forced, FORCE_SCORE, jnp.where(ok_s, imp_t, -FORCE_SCORE))
    key = pltpu.bitcast(score, jnp.int32)
    key_m1 = key - 1
    score_ref[...] = key

    def rank_body(i, cnt):
        for u in range(SEL_PER_Q):
            jp = SEL_PER_Q * i + u
            row = score_ref[pl.ds(jp, 1), :]
            cnt = cnt + jnp.where(row > jnp.where(j_i > jp, key_m1, key), 1, 0)
        return cnt

    cnt = lax.fori_loop(0, qi + 1, rank_body, jnp.zeros((nblk, qb), jnp.int32))
    pen = jnp.where(cnt < N_SELECT, 0.0, MASK_PEN).T.astype(bf16)
    q_aug = jnp.concatenate([q2, jnp.concatenate([pen, pen], axis=0)], axis=1)

    row2 = lax.broadcasted_iota(jnp.int32, (NSA_HG * qb, 1), 0)
    far_col = jnp.where(row2 < qb, far_bias[0], far_bias[1])
    n_far_keys = qb * jnp.maximum(qi - (NEAR_TILES - 1), 0)
    far_step = SEL_FAR_SUB * SEL_FAR_TILE
    c_far = lax.broadcasted_iota(jnp.int32, (NSA_HG * qb, SEL_FAR_TILE), 1)

    n_far_steps = (n_far_keys + far_step - 1) // far_step

    def far_rows(kt, sub):
        k0 = kt * far_step + sub * SEL_FAR_TILE
        return k0, pl.ds(pl.multiple_of(k0, SEL_FAR_TILE), SEL_FAR_TILE)

    def far_scores(kt):
        for sub in range(SEL_FAR_SUB):
            k0, rows = far_rows(kt, sub)
            s = _dot_nt(q_aug, ka_ref[rows, :]) + far_col
            far_ref[kt % 2, :, sub * SEL_FAR_TILE:(sub + 1) * SEL_FAR_TILE] = jnp.where(
                c_far < n_far_keys - k0, s, MASK_PEN)

    @pl.when(n_far_steps > 0)
    def _():
        far_scores(0)

    def far_body(kt, carry):
        m_old, l_old, acc_old = carry
        ss = [far_ref[kt % 2, :, sub * SEL_FAR_TILE:(sub + 1) * SEL_FAR_TILE] for sub in range(SEL_FAR_SUB)]
        m_new = functools.reduce(jnp.maximum, [m_old] + [jnp.max(s, axis=-1, keepdims=True) for s in ss])
        ps = [jnp.exp(s - m_new) for s in ss]
        alpha = jnp.exp(m_old - m_new)
        l_new = alpha * l_old + sum(jnp.sum(p, axis=-1, keepdims=True) for p in ps)
        acc_new = alpha * acc_old + sum(
            jnp.dot(p.astype(bf16), vs_ref[far_rows(kt, sub)[1], :], preferred_element_type=f32)
            for sub, p in enumerate(ps))
        far_scores(jnp.minimum(kt + 1, n_far_steps - 1))
        return m_new, l_new, acc_new

    m_far, l_far, acc_far = lax.fori_loop(
        0, n_far_steps, far_body,
        (jnp.full((NSA_HG * qb, 1), MASK_PEN, f32), jnp.zeros((NSA_HG * qb, 1), f32),
         jnp.zeros((NSA_HG * qb, NSA_DV), f32)))

    r2 = lax.broadcasted_iota(jnp.int32, (NSA_HG * qb, qb), 0) % qb
    c2 = lax.broadcasted_iota(jnp.int32, (NSA_HG * qb, qb), 1)

    def strip_bias(w):
        sl = slice(w * qb, (w + 1) * qb)
        return jnp.concatenate([strip_ref[0, :, sl], strip_ref[1, :, sl]], axis=0)

    def key_rows(w):
        kt = qi - (NEAR_TILES - 1) + w
        return kt, pl.ds(pl.multiple_of(jnp.maximum(kt, 0) * qb, qb), qb)

    m_sel = m_far
    for w in range(NEAR_TILES):
        kt, rows = key_rows(w)
        s = _dot_nt(q_aug, ka_ref[rows, :]) + strip_bias(w)
        if w == NEAR_TILES - 1:
            s = jnp.where(c2 <= r2, s, NEG)
        else:
            s = jnp.where(kt >= 0, s, NEG)
        stage_ref[w] = s
        m_sel = jnp.maximum(m_sel, jnp.max(s, axis=-1, keepdims=True))
    alpha = jnp.exp(m_far - m_sel)
    l_sel = alpha * l_far
    acc_sel = alpha * acc_far
    for w in range(NEAR_TILES):
        _, rows = key_rows(w)
        e = jnp.exp(stage_ref[w] - m_sel)
        l_sel = l_sel + jnp.sum(e, axis=-1, keepdims=True)
        acc_sel = acc_sel + jnp.dot(e.astype(bf16), vs_ref[rows, :], preferred_element_type=f32)
    o_sel = acc_sel / l_sel

    m_w = jnp.full((NSA_HG * qb, 1), NEG, f32)
    for w in range(NEAR_TILES - WIN_TILES, NEAR_TILES):
        kt, rows = key_rows(w)
        s = _dot_nt(q2, kw_ref[rows, :]) + strip_bias(w)
        if w == NEAR_TILES - WIN_TILES:
            valid = (c2 > r2) & (kt >= 0)
        elif w == NEAR_TILES - 1:
            valid = c2 <= r2
        else:
            valid = jnp.broadcast_to(kt >= 0, c2.shape)
        s = jnp.where(valid, s, NEG)
        stage_ref[w] = s
        m_w = jnp.maximum(m_w, jnp.max(s, axis=-1, keepdims=True))
    den = jnp.zeros((NSA_HG * qb, 1), f32)
    acc_w = jnp.zeros((NSA_HG * qb, NSA_DV), f32)
    for w in range(NEAR_TILES - WIN_TILES, NEAR_TILES):
        _, rows = key_rows(w)
        e = jnp.exp(stage_ref[w] - m_w)
        den = den + jnp.sum(e, axis=-1, keepdims=True)
        acc_w = acc_w + jnp.dot(e.astype(bf16), vw_ref[rows, :], preferred_element_type=f32)
    o_win = acc_w / den

    gate = jax.nn.sigmoid(gl_ref[...])
    for hh in range(NSA_HG):
        rs = slice(hh * qb, (hh + 1) * qb)
        o_ref[:, hh * NSA_DV:(hh + 1) * NSA_DV] = (gate[:, 3 * hh:3 * hh + 1] * o_cmp[hh]
                                                   + gate[:, 3 * hh + 1:3 * hh + 2] * o_sel[rs]
                                                   + gate[:, 3 * hh + 2:3 * hh + 3] * o_win[rs])


def _rel_bucket(dist):
    n = jnp.maximum(dist, 0)
    nf = jnp.maximum(n, 1).astype(f32)
    large = REL_MAX_EXACT + (jnp.log(nf / REL_MAX_EXACT) / math.log(REL_MAX_DIST / REL_MAX_EXACT)
                             * (REL_BUCKETS - REL_MAX_EXACT)).astype(jnp.int32)
    large = jnp.minimum(large, REL_BUCKETS - 1)
    return jnp.where(n < REL_MAX_EXACT, n, large)


def _nsa_constants(seq):
    nc = seq // CMP_STRIDE
    r = jnp.arange(Q_BLOCK)[:, None]
    bs = _rel_bucket(STRIP_D0 + r - jnp.arange(STRIP_W)[None, :])
    bc = _rel_bucket(r + CMP_STRIDE * jnp.arange(LANE)[None, :] - (CMP_STRIDE * (CMP_PER_Q - 1) + CMP_BLOCK - 1))
    ratio = SEL_BLOCK // CMP_STRIDE
    mt = np.zeros((LANE, nc), np.float32)
    for j in range(LANE):
        for i, wt in enumerate(SEL_OVERLAP_WEIGHTS):
            mcol = ratio * j - 1 + i
            if 0 <= mcol < nc:
                mt[j, mcol] = wt
    return bs.astype(jnp.int32), bc.astype(jnp.int32), jnp.asarray(mt, bf16)


def _nsa_attention(qn, cmp_kv, kaug, vsel, kwvw, proj, offs, tbl, consts, bsz, seq):
    bs, bc, mt = consts
    m = bsz * seq
    n_qb = seq // Q_BLOCK
    nc = seq // CMP_STRIDE
    g_ = NSA_KV_GROUPS
    kv = lambda width, col0: pl.BlockSpec((seq, width), lambda b, g, i: (b, col0 + g))
    cmp_spec = lambda j0: pl.BlockSpec((1, 1, nc, NSA_DK), lambda b, g, i: (b, j0 + g, 0, 0))
    const2 = lambda shape: pl.BlockSpec(shape, lambda b, g, i: (0, 0))
    gcol = offs["b_g"] // LANE
    return pl.pallas_call(
        _nsa_attn_kernel,
        grid=(bsz, g_, n_qb),
        in_specs=[
            pl.BlockSpec(memory_space=pltpu.SMEM),
            pl.BlockSpec((Q_BLOCK, NSA_HG * NSA_DK), lambda b, g, i: (b * n_qb + i, g)),
            cmp_spec(0), cmp_spec(g_),
            kv(NSA_DK + LANE, 0), kv(NSA_DV, 0), kv(NSA_DK, 0), kv(NSA_DV, g_),
            const2((LANE, nc)), const2((Q_BLOCK, STRIP_W)), const2((Q_BLOCK, LANE)),
            pl.BlockSpec((Q_BLOCK, LANE), lambda b, g, i: (b * n_qb + i, gcol + g)),
        ],
        out_specs=pl.BlockSpec((Q_BLOCK, NSA_HG * NSA_DV), lambda b, g, i: (b * n_qb + i, g)),
        out_shape=jax.ShapeDtypeStruct((m, NSA_HEADS * NSA_DV), f32),
        scratch_shapes=[
            pltpu.VMEM((NSA_HG, Q_BLOCK, STRIP_W), f32),
            pltpu.VMEM((NSA_HG, 2, Q_BLOCK, LANE), bf16),
            pltpu.VMEM((LANE, Q_BLOCK), jnp.int32),
            pltpu.VMEM((NEAR_TILES, NSA_HG * Q_BLOCK, Q_BLOCK), f32),
            pltpu.VMEM((2, NSA_HG * Q_BLOCK, SEL_FAR_SUB * SEL_FAR_TILE), f32),
        ],
        compiler_params=pltpu.CompilerParams(dimension_semantics=("arbitrary", "arbitrary", "arbitrary"),
                                             vmem_limit_bytes=V7X_VMEM_LIMIT_BYTES),
        name="nsa_attention",
    )(tbl, qn, cmp_kv, cmp_kv, kaug, vsel, kwvw, kwvw, mt, bs, bc, proj)


def _nsa(proj, offs, q_norm, k_norm, pe_k, pe_v, w_ck, w_cv, rel_table, consts, bsz, seq):
    q_gain = q_norm.reshape(1, NSA_DK)
    k_gain = k_norm.reshape(1, NSA_DK)
    qn, kaug, vsel, kwvw = _nsa_prep(proj, offs, q_gain, k_gain, seq)
    cmp_kv = _nsa_compress(proj, offs, jnp.stack([w_ck, w_cv]).astype(bf16), jnp.stack([pe_k, pe_v]), k_gain,
                           bsz, seq)
    return _nsa_attention(qn, cmp_kv, kaug, vsel, kwvw, proj, offs, rel_table.T, consts, bsz, seq)


SMALL_A_LANE = 0
SMALL_B_LANE = GDN_HEADS
SMALL_DT_LANE = 2 * GDN_HEADS
CONV_HALO = 8


def _softplus(x):
    return jnp.maximum(x, 0.0) + jnp.log1p(jnp.exp(-jnp.abs(x)))


def _exact_dot(a_f32, b_01):
    return sum(jnp.dot(part, b_01, preferred_element_type=f32) for part in _split3(a_f32))


def _exact_dot_left(b_01, a_f32):
    return sum(jnp.dot(b_01, part, preferred_element_type=f32) for part in _split3(a_f32))


def _lane_vec(values, lane0):
    return jnp.zeros((1, LANE), f32).at[0, lane0:lane0 + values.shape[0]].set(values.astype(f32))


def _causal_conv_silu(x_ref, tail_ref, buf_ref, w_ref, bias):
    rows = x_ref.shape[0]
    buf_ref[pl.ds(0, CONV_HALO), :] = tail_ref[...]
    buf_ref[pl.ds(CONV_HALO, rows), :] = x_ref[...]
    tail_ref[...] = x_ref[pl.ds(rows - CONV_HALO, CONV_HALO), :]
    acc = bias
    for k in range(CONV_K):
        acc = acc + w_ref[k:k + 1, :] * buf_ref[pl.ds(CONV_HALO - (CONV_K - 1) + k, rows), :]
    return acc * jax.nn.sigmoid(acc)


SSM_PAIR = LANE // SSM_HEAD_DIM
SSM_HEADS_PER_GROUP = SSM_HEADS // SSM_GROUPS


def _ssd_kernel(xbc_ref, z_ref, sm_ref, cw_ref, cb_ref, dtb_ref, alog_ref, dvec_ref, ng_ref, tri_ref, exp_ref,
                o_ref, tail_ref, buf_ref, state_ref):
    L = SSM_CHUNK
    N = SSM_STATE

    @pl.when(pl.program_id(1) == 0)
    def _():
        tail_ref[...] = jnp.zeros(tail_ref.shape, f32)
        state_ref[...] = jnp.zeros(state_ref.shape, f32)

    xbc = _causal_conv_silu(xbc_ref, tail_ref, buf_ref, cw_ref, cb_ref[...])
    xs = xbc[:, :SSM_INNER]
    dt = _softplus(sm_ref[...] + dtb_ref[...])
    a = dt * (-jnp.exp(alog_ref[...]))
    tri = tri_ref[...]
    ac = _exact_dot_left(tri, a)
    ac_t = ac.T
    eac = jnp.exp(ac)
    w_t = jnp.exp(ac_t[:, L - 1:L] - ac_t)
    chunk_decay = jnp.exp(ac[L - 1:L, :])
    xdt = xs * _exact_dot(dt, exp_ref[...])

    row = lax.broadcasted_iota(jnp.int32, (L, L), 0)
    col = lax.broadcasted_iota(jnp.int32, (L, L), 1)
    causal = row >= col
    lane = lax.broadcasted_iota(jnp.int32, (L, LANE), 1)
    first_half = lane < SSM_HEAD_DIM

    ys = []
    for g in range(SSM_GROUPS):
        bg = xbc[:, SSM_INNER + g * N:SSM_INNER + (g + 1) * N]
        cg = xbc[:, SSM_INNER + SSM_GROUPS * N + g * N:SSM_INNER + SSM_GROUPS * N + (g + 1) * N]
        cb = _dot_nt(cg.astype(bf16), bg.astype(bf16))
        bg_t = bg.T
        for i in range(g * SSM_HEADS_PER_GROUP // SSM_PAIR, (g + 1) * SSM_HEADS_PER_GROUP // SSM_PAIR):
            xdt_pair = xdt[:, i * LANE:(i + 1) * LANE].astype(bf16)
            outs = []
            for hh in range(SSM_PAIR):
                h = SSM_PAIR * i + hh
                c = SMALL_DT_LANE + h
                seg = ac[:, c:c + 1] - ac_t[c:c + 1, :]
                sc = (cb * jnp.exp(jnp.where(causal, seg, NEG))).astype(bf16)
                y = jnp.dot(sc, xdt_pair, preferred_element_type=f32)
                s_in = state_ref[h]
                y += jnp.dot((cg * eac[:, c:c + 1]).astype(bf16), s_in.astype(bf16), preferred_element_type=f32)
                st = jnp.dot((bg_t * w_t[c:c + 1, :]).astype(bf16), xdt_pair, preferred_element_type=f32)
                state_ref[h] = s_in * chunk_decay[:, c:c + 1] + st
                outs.append(y)
            y_pair = jnp.where(first_half, outs[0], outs[1])
            sl = slice(i * LANE, (i + 1) * LANE)
            y_pair = y_pair + xs[:, sl] * dvec_ref[:, sl]
            z = z_ref[:, sl]
            ys.append(y_pair * (z * jax.nn.sigmoid(z)))

    per_group = len(ys) // SSM_GROUPS
    gw = SSM_INNER // SSM_GROUPS
    for g in range(SSM_GROUPS):
        tiles = ys[g * per_group:(g + 1) * per_group]
        ms = sum(jnp.sum(t * t, axis=-1, keepdims=True) for t in tiles) / gw
        scale = lax.rsqrt(ms + EPS)
        for k, t in enumerate(tiles):
            sl = slice(g * gw + k * LANE, g * gw + (k + 1) * LANE)
            o_ref[:, sl] = (t * scale * ng_ref[:, sl]).astype(o_ref.dtype)


def _ssd(proj, offs, conv_w, conv_b, dt_bias, a_log, d_skip, norm_g, bsz, seq):
    m = bsz * seq
    L = SSM_CHUNK
    n_c = seq // L
    tri = jnp.asarray(np.tril(np.ones((L, L), np.float32)), bf16)
    expand = np.zeros((LANE, SSM_INNER), np.float32)
    for h in range(SSM_HEADS):
        expand[SMALL_DT_LANE + h, h * SSM_HEAD_DIM:(h + 1) * SSM_HEAD_DIM] = 1.0
    row = lambda w: pl.BlockSpec((1, w), lambda b, c: (0, 0))
    return pl.pallas_call(
        _ssd_kernel,
        grid=(bsz, n_c),
        in_specs=[
            pl.BlockSpec((L, SSM_CONV_CH), lambda b, c: (b * n_c + c, offs["c_xbc"] // SSM_CONV_CH)),
            pl.BlockSpec((L, SSM_INNER), lambda b, c: (b * n_c + c, offs["c_z"] // SSM_INNER)),
            pl.BlockSpec((L, LANE), lambda b, c: (b * n_c + c, offs["small"] // LANE)),
            pl.BlockSpec((CONV_K, SSM_CONV_CH), lambda b, c: (0, 0)),
            row(SSM_CONV_CH), row(LANE), row(LANE), row(SSM_INNER), row(SSM_INNER),
            pl.BlockSpec((L, L), lambda b, c: (0, 0)),
            pl.BlockSpec((LANE, SSM_INNER), lambda b, c: (0, 0)),
        ],
        out_specs=pl.BlockSpec((L, SSM_INNER), lambda b, c: (b * n_c + c, 0)),
        out_shape=jax.ShapeDtypeStruct((m, SSM_INNER), bf16),
        scratch_shapes=[
            pltpu.VMEM((CONV_HALO, SSM_CONV_CH), f32),
            pltpu.VMEM((CONV_HALO + L, SSM_CONV_CH), f32),
            pltpu.VMEM((SSM_HEADS, SSM_STATE, LANE), f32),
        ],
        compiler_params=pltpu.CompilerParams(dimension_semantics=("arbitrary", "arbitrary"),
                                             vmem_limit_bytes=V7X_VMEM_LIMIT_BYTES),
        name="ssd_mixer",
    )(proj, proj, proj, conv_w, conv_b.reshape(1, -1), _lane_vec(dt_bias, SMALL_DT_LANE),
      _lane_vec(a_log, SMALL_DT_LANE), jnp.repeat(d_skip, SSM_HEAD_DIM).reshape(1, -1), norm_g.reshape(1, -1),
      tri, jnp.asarray(expand, bf16))


GDN_STEP_ROWS = 128
GDN_QK_W = GDN_HEADS * GDN_DK
GDN_CONV_CH = GDN_HEADS * (2 * GDN_DK + GDN_DV)


def _split2(x):
    hi = x.astype(bf16)
    return hi, (x - hi.astype(f32)).astype(bf16)


def _dot_split(a_parts, b_parts):
    (ah, al), (bh, bl) = a_parts, b_parts
    return (jnp.dot(ah, bh, preferred_element_type=f32) + jnp.dot(ah, bl, preferred_element_type=f32)
            + jnp.dot(al, bh, preferred_element_type=f32))


def _unit_lower_solve_many(a_list, rhs_list):
    n = a_list[0].shape[0]
    ps = [_split2(-a) for a in a_list]
    xs = [rhs + _dot_split(p, _split2(rhs)) for p, rhs in zip(ps, rhs_list)]
    k = 2
    while k < n:
        ps = [_split2(_dot_split(p, p)) for p in ps]
        xs = [x + _dot_split(p, _split2(x)) for p, x in zip(ps, xs)]
        k *= 2
    return xs


def _gdn_kernel(qkv_ref, z_ref, sm_ref, cw_ref, dtb_ref, alog_ref, ng_ref, tri_ref, o_ref, tail_ref, buf_ref,
                state_ref):
    C = GDN_CHUNK

    @pl.when(pl.program_id(1) == 0)
    def _():
        tail_ref[...] = jnp.zeros(tail_ref.shape, f32)
        state_ref[...] = jnp.zeros(state_ref.shape, f32)

    qkv = _causal_conv_silu(qkv_ref, tail_ref, buf_ref, cw_ref, 0.0)
    sm = sm_ref[...]
    beta = jax.nn.sigmoid(sm)
    g_log = -jnp.exp(alog_ref[...]) * _softplus(sm + dtb_ref[...])
    gc = _exact_dot_left(tri_ref[...], g_log)
    gc_t = gc.T
    egc = jnp.exp(gc)

    row = lax.broadcasted_iota(jnp.int32, (C, C), 0)
    col = lax.broadcasted_iota(jnp.int32, (C, C), 1)
    causal = row >= col
    strict = row > col

    units = [(ch, h) for ch in range(GDN_STEP_ROWS // C) for h in range(GDN_HEADS)]
    a_list, rhs_list, pre = [], [], []
    for ch, h in units:
        rs = slice(ch * C, (ch + 1) * C)
        last = slice((ch + 1) * C - 1, (ch + 1) * C)
        q = qkv[rs, h * GDN_DK:(h + 1) * GDN_DK]
        k = qkv[rs, GDN_QK_W + h * GDN_DK:GDN_QK_W + (h + 1) * GDN_DK]
        v = qkv[rs, 2 * GDN_QK_W + h * GDN_DV:2 * GDN_QK_W + (h + 1) * GDN_DV]
        q = q * lax.rsqrt(jnp.sum(q * q, axis=-1, keepdims=True) + EPS) * (GDN_DK ** -0.5)
        k = k * lax.rsqrt(jnp.sum(k * k, axis=-1, keepdims=True) + EPS)
        ca = SMALL_A_LANE + h
        b = beta[rs, SMALL_B_LANE + h:SMALL_B_LANE + h + 1]
        g_col = gc[rs, ca:ca + 1]
        g_last = gc[last, ca:ca + 1]
        e_col = egc[rs, ca:ca + 1]
        decay = jnp.exp(jnp.where(causal, g_col - gc_t[ca:ca + 1, rs], NEG))
        kb = k * b
        k16 = k.astype(bf16)
        a_list.append(jnp.where(strict, _dot_nt(kb.astype(bf16), k16) * decay, 0.0))
        rhs_list.append(jnp.concatenate([v * b, kb * e_col], axis=1))
        attn = (_dot_nt(q.astype(bf16), k16) * decay).astype(bf16)
        q_dec = (q * e_col).astype(bf16)
        k_dec_t = (k * jnp.exp(g_last - g_col)).T.astype(bf16)
        pre.append((attn, q_dec, k_dec_t, jnp.exp(g_last)))

    sols = _unit_lower_solve_many(a_list, rhs_list)

    for (ch, h), sol, (attn, q_dec, k_dec_t, g_end) in zip(units, sols, pre):
        rs = slice(ch * C, (ch + 1) * C)
        u, w = sol[:, :GDN_DV], sol[:, GDN_DV:]
        s_in = state_ref[h]
        s16 = s_in.astype(bf16)
        v_new = u - jnp.dot(w.astype(bf16), s16, preferred_element_type=f32)
        v16 = v_new.astype(bf16)
        o = (jnp.dot(q_dec, s16, preferred_element_type=f32) + jnp.dot(attn, v16, preferred_element_type=f32))
        state_ref[h] = s_in * g_end + jnp.dot(k_dec_t, v16, preferred_element_type=f32)

        o = o * lax.rsqrt(jnp.mean(o * o, axis=-1, keepdims=True) + EPS) * ng_ref[...]
        z = z_ref[rs, h * GDN_DV:(h + 1) * GDN_DV]
        o_ref[rs, h * GDN_DV:(h + 1) * GDN_DV] = (o * (z * jax.nn.sigmoid(z))).astype(o_ref.dtype)


def _gdn(proj, offs, conv_w, a_log, dt_bias, norm_g, bsz, seq):
    m = bsz * seq
    R_ = GDN_STEP_ROWS
    n_s = seq // R_
    idx = np.arange(R_)
    tri = ((idx[:, None] >= idx[None, :]) & (idx[:, None] // GDN_CHUNK == idx[None, :] // GDN_CHUNK))
    row = lambda w: pl.BlockSpec((1, w), lambda b, c: (0, 0))
    return pl.pallas_call(
        _gdn_kernel,
        grid=(bsz, n_s),
        in_specs=[
            pl.BlockSpec((R_, GDN_CONV_CH), lambda b, c: (b * n_s + c, offs["a_q"] // GDN_CONV_CH)),
            pl.BlockSpec((R_, GDN_HEADS * GDN_DV), lambda b, c: (b * n_s + c, offs["a_z"] // (GDN_HEADS * GDN_DV))),
            pl.BlockSpec((R_, LANE), lambda b, c: (b * n_s + c, offs["small"] // LANE)),
            pl.BlockSpec((CONV_K, GDN_CONV_CH), lambda b, c: (0, 0)),
            row(LANE), row(LANE), row(GDN_DV),
            pl.BlockSpec((R_, R_), lambda b, c: (0, 0)),
        ],
        out_specs=pl.BlockSpec((R_, GDN_HEADS * GDN_DV), lambda b, c: (b * n_s + c, 0)),
        out_shape=jax.ShapeDtypeStruct((m, GDN_HEADS * GDN_DV), bf16),
        scratch_shapes=[
            pltpu.VMEM((CONV_HALO, GDN_CONV_CH), f32),
            pltpu.VMEM((CONV_HALO + R_, GDN_CONV_CH), f32),
            pltpu.VMEM((GDN_HEADS, GDN_DK, GDN_DV), f32),
        ],
        compiler_params=pltpu.CompilerParams(dimension_semantics=("arbitrary", "arbitrary"),
                                             vmem_limit_bytes=V7X_VMEM_LIMIT_BYTES),
        name="gdn_mixer",
    )(proj, proj, proj, conv_w, _lane_vec(dt_bias, SMALL_A_LANE), _lane_vec(a_log, SMALL_A_LANE),
      norm_g.reshape(1, -1), jnp.asarray(tri.astype(np.float32), bf16))


_IN_NAMES = ("a_q", "a_k", "a_v", "a_z", "a_a", "a_b", "b_q", "b_kc", "b_vc", "b_ks", "b_vs", "b_kw", "b_vw",
             "b_g", "c_z", "c_xbc", "c_dt", "m_gate")


def _in_sizes(d_model):
    return (GDN_HEADS * GDN_DK, GDN_HEADS * GDN_DK, GDN_HEADS * GDN_DV, GDN_HEADS * GDN_DV, GDN_HEADS, GDN_HEADS,
            NSA_HEADS * NSA_DK, NSA_KV_GROUPS * NSA_DK, NSA_KV_GROUPS * NSA_DV, NSA_KV_GROUPS * NSA_DK,
            NSA_KV_GROUPS * NSA_DV, NSA_KV_GROUPS * NSA_DK, NSA_KV_GROUPS * NSA_DV, 3 * NSA_HEADS,
            SSM_INNER, SSM_CONV_CH, SSM_HEADS, 3 * d_model)


_BIG_ORDER = ("m_gate", "a_q", "a_k", "a_v", "c_xbc", "c_z", "a_z", "b_q", "b_kc", "b_vc", "b_ks", "b_vs",
              "b_kw", "b_vw")
_SMALL_ORDER = ("a_a", "a_b", "c_dt")


def _in_layout(d_model, tn=1024):
    sizes = dict(zip(_IN_NAMES, _in_sizes(d_model)))
    src_off = dict(zip(_IN_NAMES, np.cumsum((0,) + _in_sizes(d_model))[:-1].tolist()))
    cols, offs = [], {}

    def pad_to_lane():
        cols.extend([-1] * (_round_up(len(cols), LANE) - len(cols)))

    for name in _BIG_ORDER:
        offs[name] = len(cols)
        cols.extend(range(src_off[name], src_off[name] + sizes[name]))
    offs["small"] = len(cols)
    for name in _SMALL_ORDER:
        offs[name] = len(cols)
        cols.extend(range(src_off[name], src_off[name] + sizes[name]))
    pad_to_lane()
    offs["b_g"] = len(cols)
    per_group = 3 * NSA_HG
    for g in range(NSA_KV_GROUPS):
        cols.extend(range(src_off["b_g"] + g * per_group, src_off["b_g"] + (g + 1) * per_group))
        pad_to_lane()
    n_pad = _round_up(len(cols), tn)
    cols.extend([-1] * (n_pad - len(cols)))
    return np.asarray(cols, np.int32), offs, sizes, n_pad


def _col_runs(cols):
    runs, start = [], 0
    for i in range(1, len(cols) + 1):
        same_run = i < len(cols) and ((cols[i] < 0 and cols[i - 1] < 0)
                                      or (cols[i - 1] >= 0 and cols[i] == cols[i - 1] + 1))
        if not same_run:
            runs.append((start, None if cols[start] < 0 else int(cols[start]), i - start))
            start = i
    return runs


def _cast_cols_kernel(runs_per_out, w_ref, *o_refs):
    for runs, o_ref in zip(runs_per_out, o_refs):
        for dst, src, size in runs:
            if src is None:
                o_ref[:, dst:dst + size] = jnp.zeros((o_ref.shape[0], size), o_ref.dtype)
            else:
                o_ref[:, dst:dst + size] = w_ref[0, :, src:src + size].astype(o_ref.dtype)


def _cast_cols(w_stack, layer, runs_per_out, widths, *, tr=256):
    _, rows, n_src = w_stack.shape
    return pl.pallas_call(
        functools.partial(_cast_cols_kernel, runs_per_out),
        grid=(rows // tr,),
        in_specs=[pl.BlockSpec((1, tr, n_src), lambda i: (layer, i, 0))],
        out_specs=[pl.BlockSpec((tr, w), lambda i: (i, 0)) for w in widths],
        out_shape=[jax.ShapeDtypeStruct((rows, w), bf16) for w in widths],
        compiler_params=pltpu.CompilerParams(dimension_semantics=("parallel",),
                                             vmem_limit_bytes=V7X_VMEM_LIMIT_BYTES),
        name="weight_cast_cols",
    )(w_stack)


def _cast_rows_kernel(rows_src, w_ref, o_ref):
    tr = o_ref.shape[0]
    row = lax.broadcasted_iota(jnp.int32, o_ref.shape, 0) + pl.program_id(0) * tr
    o_ref[...] = jnp.where(row < rows_src, w_ref[0], 0.0).astype(o_ref.dtype)


def _cast_rows(w_stack, layer, rows_out, *, tr=512):
    _, rows_src, n = w_stack.shape
    return pl.pallas_call(
        functools.partial(_cast_rows_kernel, rows_src),
        grid=(rows_out // tr,),
        in_specs=[pl.BlockSpec((1, tr, n), lambda i: (layer, i, 0))],
        out_specs=pl.BlockSpec((tr, n), lambda i: (i, 0)),
        out_shape=jax.ShapeDtypeStruct((rows_out, n), bf16),
        compiler_params=pltpu.CompilerParams(dimension_semantics=("parallel",),
                                             vmem_limit_bytes=V7X_VMEM_LIMIT_BYTES),
        name="weight_cast_rows",
    )(w_stack)


def _prep_w_in(w_in, layer, cols):
    return _cast_cols(w_in, layer, [_col_runs(cols)], [len(cols)])[0]


def _prep_ffn(w_up, w_down, layer, tf=512):
    ff = w_down.shape[1]
    ffp = _round_up(ff, tf)
    half = lambda src: [(0, src, ff)] + ([(ff, None, ffp - ff)] if ffp > ff else [])
    wa, wb = _cast_cols(w_up, layer, [half(0), half(ff)], [ffp, ffp])
    return wa, wb, _cast_rows(w_down, layer, ffp)


def kernel(x, rel_table, g_ffn1, w_up1, w_down1, g_mix, w_in, gdn_conv, gdn_a_log, gdn_dt_bias, gdn_norm,
           nsa_q_norm, nsa_k_norm, nsa_pe_k, nsa_pe_v, nsa_w_ck, nsa_w_cv, ssm_conv_w, ssm_conv_b,
           ssm_dt_bias, ssm_a_log, ssm_d, ssm_norm, p_a, p_b, p_c, w_o, g_ffn2, w_up2, w_down2):
    bsz, seq, d = x.shape
    depth = w_in.shape[0]
    m = bsz * seq
    cols, offs, _, _ = _in_layout(d)
    x2 = x.reshape(m, d)
    nsa_consts = _nsa_constants(seq)

    for l in range(depth):
        wa, wb, wd = _prep_ffn(w_up1, w_down1, l)
        x2 = _ffn(x2, g_ffn1[l].reshape(1, d), wa, wb, wd)

        proj = _norm_matmul(x2, g_mix[l].reshape(1, d), _prep_w_in(w_in, l, cols))
        y_a = _gdn(proj, offs, gdn_conv[l], gdn_a_log[l], gdn_dt_bias[l], gdn_norm[l], bsz, seq)
        y_b = _nsa(proj, offs, nsa_q_norm[l], nsa_k_norm[l], nsa_pe_k[l], nsa_pe_v[l], nsa_w_ck[l], nsa_w_cv[l],
                   rel_table, nsa_consts, bsz, seq)
        y_c = _ssd(proj, offs, ssm_conv_w[l], ssm_conv_b[l], ssm_dt_bias[l], ssm_a_log[l], ssm_d[l], ssm_norm[l],
                   bsz, seq)
        x2 = _merge(x2, y_a, y_b, y_c, proj, offs["m_gate"],
                    p_a[l].astype(bf16), p_b[l].astype(bf16), p_c[l].astype(bf16), w_o[l].astype(bf16))

        wa, wb, wd = _prep_ffn(w_up2, w_down2, l)
        x2 = _ffn(x2, g_ffn2[l].reshape(1, d), wa, wb, wd)
    return x2.reshape(bsz, seq, d)
```

```python
import functools
import math

import jax
import jax.numpy as jnp
import numpy as np
from jax import lax
from jax.experimental import pallas as pl
from jax.experimental.pallas import tpu as pltpu

EPS = 1e-6
CONV_K = 4

GDN_HEADS = 4
GDN_DK = 128
GDN_DV = 128
GDN_CHUNK = 64

NSA_HEADS = 4
NSA_KV_GROUPS = 2
NSA_DK = 128
NSA_DV = 128
CMP_BLOCK = 32
CMP_STRIDE = 16
SEL_BLOCK = 64
N_SELECT = 16
WINDOW = 512
Q_BLOCK = 128
SEL_OVERLAP_WEIGHTS = (1.0, 2.0, 2.0, 2.0, 1.0)
FORCE_SCORE = 1e4
NEG = -1e30

SSM_HEADS = 16
SSM_HEAD_DIM = 64
SSM_GROUPS = 2
SSM_STATE = 128
SSM_CHUNK = 128
SSM_INNER = SSM_HEADS * SSM_HEAD_DIM
SSM_CONV_CH = SSM_INNER + 2 * SSM_GROUPS * SSM_STATE

REL_BUCKETS = 32
REL_MAX_EXACT = 16
REL_MAX_DIST = 1024

V7X_VMEM_LIMIT_BYTES = 56 * 1024 * 1024
LANE = 128

bf16 = jnp.bfloat16
f32 = jnp.float32


def _round_up(n, m):
    return (n + m - 1) // m * m


def _ffn_kernel(x_ref, g_ref, wa_ref, wb_ref, wd_ref, o_ref, h_ref):
    j = pl.program_id(1)

    @pl.when(j == 0)
    def _():
        x = x_ref[...]
        h = x * lax.rsqrt(jnp.mean(x * x, axis=-1, keepdims=True) + EPS) * g_ref[...]
        h_ref[...] = h.astype(bf16)
        o_ref[...] = x

    h = h_ref[...]
    a = jnp.dot(h, wa_ref[...], preferred_element_type=f32)
    b = jnp.dot(h, wb_ref[...], preferred_element_type=f32)
    act = (0.5 * a * jax.nn.sigmoid(a) * b).astype(bf16)
    o_ref[...] += jnp.dot(act, wd_ref[...], preferred_element_type=f32)


def _ffn(x2, g, wa, wb, wd, *, tm=1024, tf=512):
    m, d = x2.shape
    ffp = wa.shape[1]
    return pl.pallas_call(
        _ffn_kernel,
        grid=(m // tm, ffp // tf),
        in_specs=[
            pl.BlockSpec((tm, d), lambda i, j: (i, 0)),
            pl.BlockSpec((1, d), lambda i, j: (0, 0)),
            pl.BlockSpec((d, tf), lambda i, j: (0, j)),
            pl.BlockSpec((d, tf), lambda i, j: (0, j)),
            pl.BlockSpec((tf, d), lambda i, j: (j, 0)),
        ],
        out_specs=pl.BlockSpec((tm, d), lambda i, j: (i, 0)),
        out_shape=jax.ShapeDtypeStruct((m, d), f32),
        scratch_shapes=[pltpu.VMEM((tm, d), bf16)],
        compiler_params=pltpu.CompilerParams(
            dimension_semantics=("parallel", "arbitrary"),
            vmem_limit_bytes=V7X_VMEM_LIMIT_BYTES),
        name="ffn_swiglu",
    )(x2, g, wa, wb, wd)


def _norm_matmul_kernel(x_ref, g_ref, w_ref, o_ref, h_ref):
    j = pl.program_id(1)

    @pl.when(j == 0)
    def _():
        x = x_ref[...]
        h = x * lax.rsqrt(jnp.mean(x * x, axis=-1, keepdims=True) + EPS) * g_ref[...]
        h_ref[...] = h.astype(bf16)

    o_ref[...] = jnp.dot(h_ref[...], w_ref[...], preferred_element_type=f32)


def _norm_matmul(x2, g, w, *, tm=1024, tn=1024):
    m, d = x2.shape
    n = w.shape[1]
    return pl.pallas_call(
        _norm_matmul_kernel,
        grid=(m // tm, n // tn),
        in_specs=[
            pl.BlockSpec((tm, d), lambda i, j: (i, 0)),
            pl.BlockSpec((1, d), lambda i, j: (0, 0)),
            pl.BlockSpec((d, tn), lambda i, j: (0, j)),
        ],
        out_specs=pl.BlockSpec((tm, tn), lambda i, j: (i, j)),
        out_shape=jax.ShapeDtypeStruct((m, n), f32),
        scratch_shapes=[pltpu.VMEM((tm, d), bf16)],
        compiler_params=pltpu.CompilerParams(
            dimension_semantics=("parallel", "arbitrary"),
            vmem_limit_bytes=V7X_VMEM_LIMIT_BYTES),
        name="norm_in_proj",
    )(x2, g, w)


def _merge_kernel(x_ref, ya_ref, yb_ref, yc_ref, ga_ref, gb_ref, gc_ref, pa_ref, pb_ref, pc_ref, wo_ref, o_ref):
    ma = jnp.dot(ya_ref[...].astype(bf16), pa_ref[...], preferred_element_type=f32)
    mb = jnp.dot(yb_ref[...].astype(bf16), pb_ref[...], preferred_element_type=f32)
    mc = jnp.dot(yc_ref[...].astype(bf16), pc_ref[...], preferred_element_type=f32)
    merged = (jax.nn.sigmoid(ga_ref[...]) * ma + jax.nn.sigmoid(gb_ref[...]) * mb
              + jax.nn.sigmoid(gc_ref[...]) * mc)
    o_ref[...] = x_ref[...] + jnp.dot(merged.astype(bf16), wo_ref[...], preferred_element_type=f32)


def _merge(x2, ya, yb, yc, gates, gate_col0, pa, pb, pc, wo, *, tm=256):
    m, d = x2.shape
    gb0 = gate_col0 // d
    const = dict(pipeline_mode=pl.Buffered(1))
    return pl.pallas_call(
        _merge_kernel,
        grid=(m // tm,),
        in_specs=[
            pl.BlockSpec((tm, d), lambda i: (i, 0)),
            pl.BlockSpec((tm, ya.shape[1]), lambda i: (i, 0)),
            pl.BlockSpec((tm, yb.shape[1]), lambda i: (i, 0)),
            pl.BlockSpec((tm, yc.shape[1]), lambda i: (i, 0)),
            pl.BlockSpec((tm, d), lambda i: (i, gb0)),
            pl.BlockSpec((tm, d), lambda i: (i, gb0 + 1)),
            pl.BlockSpec((tm, d), lambda i: (i, gb0 + 2)),
            pl.BlockSpec(pa.shape, lambda i: (0, 0), **const),
            pl.BlockSpec(pb.shape, lambda i: (0, 0), **const),
            pl.BlockSpec(pc.shape, lambda i: (0, 0), **const),
            pl.BlockSpec(wo.shape, lambda i: (0, 0), **const),
        ],
        out_specs=pl.BlockSpec((tm, d), lambda i: (i, 0)),
        out_shape=jax.ShapeDtypeStruct((m, d), f32),
        compiler_params=pltpu.CompilerParams(
            dimension_semantics=("parallel",),
            vmem_limit_bytes=V7X_VMEM_LIMIT_BYTES),
        name="merge_out_proj",
    )(x2, ya, yb, yc, gates, gates, gates, pa, pb, pc, wo)


NSA_HG = NSA_HEADS // NSA_KV_GROUPS
SEL_FAR_TILE = 512
SEL_FAR_SUB = 2
NEAR_TILES = 8
STRIP_W = NEAR_TILES * Q_BLOCK
STRIP_D0 = (NEAR_TILES - 1) * Q_BLOCK
WIN_TILES = WINDOW // Q_BLOCK + 1
CMP_PER_Q = Q_BLOCK // CMP_STRIDE
SEL_PER_Q = Q_BLOCK // SEL_BLOCK
MASK_PEN = -1e30


def _dot_nt(a, b):
    return lax.dot_general(a, b, (((1,), (1,)), ((), ())), preferred_element_type=f32)


def _lane_rms(x, gain):
    return x * lax.rsqrt(jnp.mean(x * x, axis=-1, keepdims=True) + EPS) * gain


def _nsa_prep_kernel(seq, q_ref, ks_ref, kw_ref, qg_ref, kg_ref, qo_ref, kao_ref, vso_ref, kwo_ref):
    tm = q_ref.shape[0]
    qg = qg_ref[...] * (NSA_DK ** -0.5)
    kg = kg_ref[...]
    for h in range(NSA_HEADS):
        sl = slice(h * NSA_DK, (h + 1) * NSA_DK)
        qo_ref[:, sl] = _lane_rms(q_ref[:, sl], qg).astype(bf16)
    kw_ = NSA_KV_GROUPS * NSA_DK
    tok = (lax.broadcasted_iota(jnp.int32, (tm, LANE), 0) + pl.program_id(0) * tm) % seq
    onehot = jnp.where(lax.broadcasted_iota(jnp.int32, (tm, LANE), 1) == tok // SEL_BLOCK, 1.0, 0.0).astype(bf16)
    for g in range(NSA_KV_GROUPS):
        sl = slice(g * NSA_DK, (g + 1) * NSA_DK)
        kao_ref[:, 2 * g * NSA_DK:(2 * g + 1) * NSA_DK] = _lane_rms(ks_ref[:, sl], kg).astype(bf16)
        kao_ref[:, (2 * g + 1) * NSA_DK:(2 * g + 2) * NSA_DK] = onehot
        kwo_ref[:, sl] = _lane_rms(kw_ref[:, sl], kg).astype(bf16)
    vso_ref[...] = ks_ref[:, kw_:].astype(bf16)
    kwo_ref[:, kw_:] = kw_ref[:, kw_:].astype(bf16)


def _nsa_prep(proj, offs, q_gain, k_gain, seq, *, tm=512):
    m = proj.shape[0]
    w = NSA_HEADS * NSA_DK
    assert seq // SEL_BLOCK <= LANE and NSA_DK == LANE, "selection blocks must fit one 128-lane one-hot"
    blk = lambda name: pl.BlockSpec((tm, w), lambda i, c=offs[name] // w: (i, c))
    widths = (w, 2 * NSA_KV_GROUPS * NSA_DK, NSA_KV_GROUPS * NSA_DV, w)
    return pl.pallas_call(
        functools.partial(_nsa_prep_kernel, seq),
        grid=(m // tm,),
        in_specs=[blk("b_q"), blk("b_ks"), blk("b_kw"),
                  pl.BlockSpec((1, NSA_DK), lambda i: (0, 0)), pl.BlockSpec((1, NSA_DK), lambda i: (0, 0))],
        out_specs=[pl.BlockSpec((tm, wd), lambda i: (i, 0)) for wd in widths],
        out_shape=[jax.ShapeDtypeStruct((m, wd), bf16) for wd in widths],
        compiler_params=pltpu.CompilerParams(dimension_semantics=("parallel",),
                                             vmem_limit_bytes=V7X_VMEM_LIMIT_BYTES),
        name="nsa_prep",
    )(proj, proj, proj, q_gain, k_gain)


def _nsa_compress_kernel(x_ref, w_ref, pe_ref, kg_ref, o_ref, hi_ref):
    j = pl.program_id(1)
    nc = o_ref.shape[2]
    lo = jnp.zeros((nc, NSA_DK), f32)
    hi = jnp.zeros((nc, NSA_DK), f32)
    for l in range(CMP_STRIDE):
        rows = x_ref[pl.ds(l, nc, stride=CMP_STRIDE), :]
        lo += jnp.dot((rows + pe_ref[0, l:l + 1, :]).astype(bf16), w_ref[0, l], preferred_element_type=f32)
        hi += jnp.dot((rows + pe_ref[0, CMP_STRIDE + l:CMP_STRIDE + l + 1, :]).astype(bf16),
                      w_ref[0, CMP_STRIDE + l], preferred_element_type=f32)
    hi_ref[pl.ds(0, nc), :] = hi
    hi_ref[pl.ds(nc, 8), :] = jnp.zeros((8, NSA_DK), f32)
    c = lo + hi_ref[pl.ds(1, nc), :]
    normed = _lane_rms(c, kg_ref[...])
    o_ref[0, 0] = jnp.where(j < NSA_KV_GROUPS, normed, c).astype(bf16)


def _nsa_compress(proj, offs, w_stack, pe_stack, k_gain, bsz, seq):
    nc = seq // CMP_STRIDE
    c0 = offs["b_kc"] // NSA_DK
    return pl.pallas_call(
        _nsa_compress_kernel,
        grid=(bsz, 2 * NSA_KV_GROUPS),
        in_specs=[
            pl.BlockSpec((seq, NSA_DK), lambda b, j: (b, c0 + j)),
            pl.BlockSpec((1, CMP_BLOCK, NSA_DK, NSA_DK), lambda b, j: (j // NSA_KV_GROUPS, 0, 0, 0)),
            pl.BlockSpec((1, CMP_BLOCK, NSA_DK), lambda b, j: (j // NSA_KV_GROUPS, 0, 0)),
            pl.BlockSpec((1, NSA_DK), lambda b, j: (0, 0)),
        ],
        out_specs=pl.BlockSpec((1, 1, nc, NSA_DK), lambda b, j: (b, j, 0, 0)),
        out_shape=jax.ShapeDtypeStruct((bsz, 2 * NSA_KV_GROUPS, nc, NSA_DK), bf16),
        scratch_shapes=[pltpu.VMEM((nc + 8, NSA_DK), f32)],
        compiler_params=pltpu.CompilerParams(dimension_semantics=("parallel", "parallel"),
                                             vmem_limit_bytes=V7X_VMEM_LIMIT_BYTES),
        name="nsa_compress",
    )(proj, w_stack, pe_stack, k_gain)


def _split3(x):
    hi = x.astype(bf16)
    r1 = x - hi.astype(f32)
    mid = r1.astype(bf16)
    lo = (r1 - mid.astype(f32)).astype(bf16)
    return hi, mid, lo


def _nsa_attn_kernel(tbl_ref, q_ref, kc_ref, vc_ref, ka_ref, vs_ref, kw_ref, vw_ref, mt_ref, bs_ref,
                     bc_ref, gl_ref, o_ref, strip_ref, sc_ref, score_ref, stage_ref, far_ref):
    g = pl.program_id(1)
    qi = pl.program_id(2)
    nc = kc_ref.shape[2]
    qb = Q_BLOCK

    @pl.when(qi == 0)
    def _():
        def lookup(bucket):
            v0 = jnp.zeros(bucket.shape, f32)
            v1 = jnp.zeros(bucket.shape, f32)
            for k in range(REL_BUCKETS):
                eq = bucket == k
                v0 = jnp.where(eq, tbl_ref[NSA_HG * g, k], v0)
                v1 = jnp.where(eq, tbl_ref[NSA_HG * g + 1, k], v1)
            return v0, v1

        for c in range(NEAR_TILES):
            sl = slice(c * qb, (c + 1) * qb)
            v0, v1 = lookup(bs_ref[:, sl])
            strip_ref[0, :, sl] = v0
            strip_ref[1, :, sl] = v1
        v0, v1 = lookup(bc_ref[...])
        for hh, v in enumerate((v0, v1)):
            d = v - tbl_ref[NSA_HG * g + hh, REL_BUCKETS - 1]
            hi = d.astype(bf16)
            sc_ref[hh, 0] = hi
            sc_ref[hh, 1] = (d - hi.astype(f32)).astype(bf16)

    far_bias = [tbl_ref[NSA_HG * g + hh, REL_BUCKETS - 1] for hh in range(NSA_HG)]
    q = q_ref[...]
    q2 = jnp.concatenate([q[:, :NSA_DK], q[:, NSA_DK:]], axis=0)

    r2 = lax.broadcasted_iota(jnp.int32, (NSA_HG * qb, qb), 0) % qb
    c2 = lax.broadcasted_iota(jnp.int32, (NSA_HG * qb, qb), 1)

    def strip_bias(w):
        sl = slice(w * qb, (w + 1) * qb)
        return jnp.concatenate([strip_ref[0, :, sl], strip_ref[1, :, sl]], axis=0)

    def key_rows(w):
        kt = qi - (NEAR_TILES - 1) + w
        return kt, pl.ds(pl.multiple_of(jnp.maximum(kt, 0) * qb, qb), qb)

    m_w = jnp.full((NSA_HG * qb, 1), NEG, f32)
    for w in range(NEAR_TILES - WIN_TILES, NEAR_TILES):
        kt, rows = key_rows(w)
        s = _dot_nt(q2, kw_ref[rows, :]) + strip_bias(w)
        if w == NEAR_TILES - WIN_TILES:
            valid = (c2 > r2) & (kt >= 0)
        elif w == NEAR_TILES - 1:
            valid = c2 <= r2
        else:
            valid = jnp.broadcast_to(kt >= 0, c2.shape)
        s = jnp.where(valid, s, NEG)
        stage_ref[w] = s
        m_w = jnp.maximum(m_w, jnp.max(s, axis=-1, keepdims=True))
    den = jnp.zeros((NSA_HG * qb, 1), f32)
    acc_w = jnp.zeros((NSA_HG * qb, NSA_DV), f32)
    for w in range(NEAR_TILES - WIN_TILES, NEAR_TILES):
        _, rows = key_rows(w)
        e = jnp.exp(stage_ref[w] - m_w)
        den = den + jnp.sum(e, axis=-1, keepdims=True)
        acc_w = acc_w + jnp.dot(e.astype(bf16), vw_ref[rows, :], preferred_element_type=f32)
    o_win = acc_w / den

    r_i = lax.broadcasted_iota(jnp.int32, (qb, nc), 0)
    n_i = lax.broadcasted_iota(jnp.int32, (qb, nc), 1)
    ok_c = CMP_STRIDE * n_i <= qb * qi + r_i - (CMP_BLOCK - 1)
    shift = jnp.where(r_i + n_i == CMP_PER_Q * qi + CMP_PER_Q - 1, 1.0, 0.0).astype(bf16)
    kc = kc_ref[0, 0]
    vc = vc_ref[0, 0]
    psum = jnp.zeros((qb, nc), f32)
    o_cmp = []
    for hh in range(NSA_HG):
        s = _dot_nt(q2[hh * qb:(hh + 1) * qb], kc)
        bias = (far_bias[hh] + jnp.dot(sc_ref[hh, 0], shift, preferred_element_type=f32)
                + jnp.dot(sc_ref[hh, 1], shift, preferred_element_type=f32))
        s = jnp.where(ok_c, s + bias, NEG)
        e = jnp.exp(s - jnp.max(s, axis=-1, keepdims=True))
        p = jnp.where(ok_c, e / jnp.sum(e, axis=-1, keepdims=True), 0.0)
        o_cmp.append(jnp.dot(p.astype(bf16), vc, preferred_element_type=f32))
        psum += p

    mt = mt_ref[...]
    imp_t = sum(_dot_nt(mt, part) for part in _split3(psum))
    nblk = imp_t.shape[0]
    j_i = lax.broadcasted_iota(jnp.int32, (nblk, qb), 0)
    l_i = lax.broadcasted_iota(jnp.int32, (nblk, qb), 1)
    cur = SEL_PER_Q * qi + l_i // SEL_BLOCK
    ok_s = j_i <= cur
    forced = ok_s & ((j_i == 0) | (j_i >= cur - 1))
    score = jnp.where(forced, FORCE_SCORE, jnp.where(ok_s, imp_t, -FORCE_SCORE))
    key = pltpu.bitcast(score, jnp.int32)
    score_ref[...] = key
    half = nblk // 2

    def rank_half(lo):
        key_h = key[lo:lo + half]
        key_h_m1 = key_h - 1
        j_h = j_i[lo:lo + half]

        def rank_body(i, cnt):
            for u in range(SEL_PER_Q):
                jp = SEL_PER_Q * i + u
                row = score_ref[pl.ds(jp, 1), :]
                cnt = cnt + jnp.where(row > jnp.where(j_h > jp, key_h_m1, key_h), 1, 0)
            return cnt

        return lax.fori_loop(0, qi + 1, rank_body, jnp.zeros((half, qb), jnp.int32))

    cnt_hi = lax.cond(SEL_PER_Q * qi + SEL_PER_Q > half, lambda: rank_half(half),
                      lambda: jnp.full((half, qb), N_SELECT, jnp.int32))
    cnt = jnp.concatenate([rank_half(0), cnt_hi], axis=0)
    pen = jnp.where(cnt < N_SELECT, 0.0, MASK_PEN).T.astype(bf16)
    q_aug = jnp.concatenate([q2, jnp.concatenate([pen, pen], axis=0)], axis=1)

    row2 = lax.broadcasted_iota(jnp.int32, (NSA_HG * qb, 1), 0)
    far_col = jnp.where(row2 < qb, far_bias[0], far_bias[1])
    n_far_keys = qb * jnp.maximum(qi - (NEAR_TILES - 1), 0)
    far_step = SEL_FAR_SUB * SEL_FAR_TILE
    c_far = lax.broadcasted_iota(jnp.int32, (NSA_HG * qb, SEL_FAR_TILE), 1)

    n_far_steps = (n_far_keys + far_step - 1) // far_step

    def far_rows(kt, sub):
        k0 = kt * far_step + sub * SEL_FAR_TILE
        return k0, pl.ds(pl.multiple_of(k0, SEL_FAR_TILE), SEL_FAR_TILE)

    def far_scores(kt):
        for sub in range(SEL_FAR_SUB):
            k0, rows = far_rows(kt, sub)
            s = _dot_nt(q_aug, ka_ref[rows, :]) + far_col
            far_ref[kt % 2, :, sub * SEL_FAR_TILE:(sub + 1) * SEL_FAR_TILE] = jnp.where(
                c_far < n_far_keys - k0, s, MASK_PEN)

    far_scores(0)

    m_near = jnp.full((NSA_HG * qb, 1), NEG, f32)
    for w in range(NEAR_TILES):
        kt, rows = key_rows(w)
        s = _dot_nt(q_aug, ka_ref[rows, :]) + strip_bias(w)
        if w == NEAR_TILES - 1:
            s = jnp.where(c2 <= r2, s, NEG)
        else:
            s = jnp.where(kt >= 0, s, NEG)
        stage_ref[w] = s
        m_near = jnp.maximum(m_near, jnp.max(s, axis=-1, keepdims=True))

    def far_body(kt, carry):
        m_old, l_old, acc_old = carry
        ss = [far_ref[kt % 2, :, sub * SEL_FAR_TILE:(sub + 1) * SEL_FAR_TILE] for sub in range(SEL_FAR_SUB)]
        m_new = functools.reduce(jnp.maximum, [m_old] + [jnp.max(s, axis=-1, keepdims=True) for s in ss])
        ps = [jnp.exp(s - m_new) for s in ss]
        alpha = jnp.exp(m_old - m_new)
        l_new = alpha * l_old + sum(jnp.sum(p, axis=-1, keepdims=True) for p in ps)
        acc_new = alpha * acc_old + sum(
            jnp.dot(p.astype(bf16), vs_ref[far_rows(kt, sub)[1], :], preferred_element_type=f32)
            for sub, p in enumerate(ps))
        far_scores(jnp.minimum(kt + 1, n_far_steps - 1))
        return m_new, l_new, acc_new

    m_far, l_far, acc_far = lax.fori_loop(
        0, n_far_steps, far_body,
        (jnp.full((NSA_HG * qb, 1), MASK_PEN, f32), jnp.zeros((NSA_HG * qb, 1), f32),
         jnp.zeros((NSA_HG * qb, NSA_DV), f32)))

    m_sel = jnp.maximum(m_far, m_near)
    alpha = jnp.exp(m_far - m_sel)
    l_sel = alpha * l_far
    acc_sel = alpha * acc_far
    for w in range(NEAR_TILES):
        _, rows = key_rows(w)
        e = jnp.exp(stage_ref[w] - m_sel)
        l_sel = l_sel + jnp.sum(e, axis=-1, keepdims=True)
        acc_sel = acc_sel + jnp.dot(e.astype(bf16), vs_ref[rows, :], preferred_element_type=f32)
    o_sel = acc_sel / l_sel

    gate = jax.nn.sigmoid(gl_ref[...])
    for hh in range(NSA_HG):
        rs = slice(hh * qb, (hh + 1) * qb)
        o_ref[:, hh * NSA_DV:(hh + 1) * NSA_DV] = (gate[:, 3 * hh:3 * hh + 1] * o_cmp[hh]
                                                   + gate[:, 3 * hh + 1:3 * hh + 2] * o_sel[rs]
                                                   + gate[:, 3 * hh + 2:3 * hh + 3] * o_win[rs])


def _rel_bucket(dist):
    n = jnp.maximum(dist, 0)
    nf = jnp.maximum(n, 1).astype(f32)
    large = REL_MAX_EXACT + (jnp.log(nf / REL_MAX_EXACT) / math.log(REL_MAX_DIST / REL_MAX_EXACT)
                             * (REL_BUCKETS - REL_MAX_EXACT)).astype(jnp.int32)
    large = jnp.minimum(large, REL_BUCKETS - 1)
    return jnp.where(n < REL_MAX_EXACT, n, large)


def _nsa_constants(seq):
    nc = seq // CMP_STRIDE
    r = jnp.arange(Q_BLOCK)[:, None]
    bs = _rel_bucket(STRIP_D0 + r - jnp.arange(STRIP_W)[None, :])
    bc = _rel_bucket(r + CMP_STRIDE * jnp.arange(LANE)[None, :] - (CMP_STRIDE * (CMP_PER_Q - 1) + CMP_BLOCK - 1))
    ratio = SEL_BLOCK // CMP_STRIDE
    mt = np.zeros((LANE, nc), np.float32)
    for j in range(LANE):
        for i, wt in enumerate(SEL_OVERLAP_WEIGHTS):
            mcol = ratio * j - 1 + i
            if 0 <= mcol < nc:
                mt[j, mcol] = wt
    return bs.astype(jnp.int32), bc.astype(jnp.int32), jnp.asarray(mt, bf16)


def _nsa_attention(qn, cmp_kv, kaug, vsel, kwvw, proj, offs, tbl, consts, bsz, seq):
    bs, bc, mt = consts
    assert seq % (SEL_FAR_SUB * SEL_FAR_TILE) == 0, "far steps must tile the sequence"
    m = bsz * seq
    n_qb = seq // Q_BLOCK
    nc = seq // CMP_STRIDE
    g_ = NSA_KV_GROUPS
    kv = lambda width, col0: pl.BlockSpec((seq, width), lambda b, g, i: (b, col0 + g))
    cmp_spec = lambda j0: pl.BlockSpec((1, 1, nc, NSA_DK), lambda b, g, i: (b, j0 + g, 0, 0))
    const2 = lambda shape: pl.BlockSpec(shape, lambda b, g, i: (0, 0))
    gcol = offs["b_g"] // LANE
    return pl.pallas_call(
        _nsa_attn_kernel,
        grid=(bsz, g_, n_qb),
        in_specs=[
            pl.BlockSpec(memory_space=pltpu.SMEM),
            pl.BlockSpec((Q_BLOCK, NSA_HG * NSA_DK), lambda b, g, i: (b * n_qb + i, g)),
            cmp_spec(0), cmp_spec(g_),
            kv(NSA_DK + LANE, 0), kv(NSA_DV, 0), kv(NSA_DK, 0), kv(NSA_DV, g_),
            const2((LANE, nc)), const2((Q_BLOCK, STRIP_W)), const2((Q_BLOCK, LANE)),
            pl.BlockSpec((Q_BLOCK, LANE), lambda b, g, i: (b * n_qb + i, gcol + g)),
        ],
        out_specs=pl.BlockSpec((Q_BLOCK, NSA_HG * NSA_DV), lambda b, g, i: (b * n_qb + i, g)),
        out_shape=jax.ShapeDtypeStruct((m, NSA_HEADS * NSA_DV), f32),
        scratch_shapes=[
            pltpu.VMEM((NSA_HG, Q_BLOCK, STRIP_W), f32),
            pltpu.VMEM((NSA_HG, 2, Q_BLOCK, LANE), bf16),
            pltpu.VMEM((LANE, Q_BLOCK), jnp.int32),
            pltpu.VMEM((NEAR_TILES, NSA_HG * Q_BLOCK, Q_BLOCK), f32),
            pltpu.VMEM((2, NSA_HG * Q_BLOCK, SEL_FAR_SUB * SEL_FAR_TILE), f32),
        ],
        compiler_params=pltpu.CompilerParams(dimension_semantics=("arbitrary", "arbitrary", "arbitrary"),
                                             vmem_limit_bytes=V7X_VMEM_LIMIT_BYTES),
        name="nsa_attention",
    )(tbl, qn, cmp_kv, cmp_kv, kaug, vsel, kwvw, kwvw, mt, bs, bc, proj)


def _nsa(proj, offs, q_norm, k_norm, pe_k, pe_v, w_ck, w_cv, rel_table, consts, bsz, seq):
    q_gain = q_norm.reshape(1, NSA_DK)
    k_gain = k_norm.reshape(1, NSA_DK)
    qn, kaug, vsel, kwvw = _nsa_prep(proj, offs, q_gain, k_gain, seq)
    cmp_kv = _nsa_compress(proj, offs, jnp.stack([w_ck, w_cv]).astype(bf16), jnp.stack([pe_k, pe_v]), k_gain,
                           bsz, seq)
    return _nsa_attention(qn, cmp_kv, kaug, vsel, kwvw, proj, offs, rel_table.T, consts, bsz, seq)


SMALL_A_LANE = 0
SMALL_B_LANE = GDN_HEADS
SMALL_DT_LANE = 2 * GDN_HEADS
CONV_HALO = 8


def _softplus(x):
    return jnp.maximum(x, 0.0) + jnp.log1p(jnp.exp(-jnp.abs(x)))


def _exact_dot(a_f32, b_01):
    return sum(jnp.dot(part, b_01, preferred_element_type=f32) for part in _split3(a_f32))


def _exact_dot_left(b_01, a_f32):
    return sum(jnp.dot(b_01, part, preferred_element_type=f32) for part in _split3(a_f32))


def _lane_vec(values, lane0):
    return jnp.zeros((1, LANE), f32).at[0, lane0:lane0 + values.shape[0]].set(values.astype(f32))


def _causal_conv_silu(x_ref, tail_ref, buf_ref, w_ref, bias):
    rows = x_ref.shape[0]
    buf_ref[pl.ds(0, CONV_HALO), :] = tail_ref[...]
    buf_ref[pl.ds(CONV_HALO, rows), :] = x_ref[...]
    tail_ref[...] = x_ref[pl.ds(rows - CONV_HALO, CONV_HALO), :]
    acc = bias
    for k in range(CONV_K):
        acc = acc + w_ref[k:k + 1, :] * buf_ref[pl.ds(CONV_HALO - (CONV_K - 1) + k, rows), :]
    return acc * jax.nn.sigmoid(acc)


SSM_PAIR = LANE // SSM_HEAD_DIM
SSM_HEADS_PER_GROUP = SSM_HEADS // SSM_GROUPS


def _ssd_kernel(xbc_ref, z_ref, sm_ref, cw_ref, cb_ref, dtb_ref, alog_ref, dvec_ref, ng_ref, tri_ref, exp_ref,
                o_ref, tail_ref, buf_ref, state_ref):
    L = SSM_CHUNK
    N = SSM_STATE

    @pl.when(pl.program_id(1) == 0)
    def _():
        tail_ref[...] = jnp.zeros(tail_ref.shape, f32)
        state_ref[...] = jnp.zeros(state_ref.shape, f32)

    xbc = _causal_conv_silu(xbc_ref, tail_ref, buf_ref, cw_ref, cb_ref[...])
    xs = xbc[:, :SSM_INNER]
    dt = _softplus(sm_ref[...] + dtb_ref[...])
    a = dt * (-jnp.exp(alog_ref[...]))
    tri = tri_ref[...]
    ac = _exact_dot_left(tri, a)
    ac_t = ac.T
    eac = jnp.exp(ac)
    w_t = jnp.exp(ac_t[:, L - 1:L] - ac_t)
    chunk_decay = jnp.exp(ac[L - 1:L, :])
    xdt = xs * _exact_dot(dt, exp_ref[...])

    row = lax.broadcasted_iota(jnp.int32, (L, L), 0)
    col = lax.broadcasted_iota(jnp.int32, (L, L), 1)
    causal = row >= col
    lane = lax.broadcasted_iota(jnp.int32, (L, LANE), 1)
    first_half = lane < SSM_HEAD_DIM

    ys = []
    for g in range(SSM_GROUPS):
        bg = xbc[:, SSM_INNER + g * N:SSM_INNER + (g + 1) * N]
        cg = xbc[:, SSM_INNER + SSM_GROUPS * N + g * N:SSM_INNER + SSM_GROUPS * N + (g + 1) * N]
        cb = _dot_nt(cg.astype(bf16), bg.astype(bf16))
        bg_t = bg.T
        for i in range(g * SSM_HEADS_PER_GROUP // SSM_PAIR, (g + 1) * SSM_HEADS_PER_GROUP // SSM_PAIR):
            xdt_pair = xdt[:, i * LANE:(i + 1) * LANE].astype(bf16)
            outs = []
            for hh in range(SSM_PAIR):
                h = SSM_PAIR * i + hh
                c = SMALL_DT_LANE + h
                seg = ac[:, c:c + 1] - ac_t[c:c + 1, :]
                sc = (cb * jnp.exp(jnp.where(causal, seg, NEG))).astype(bf16)
                y = jnp.dot(sc, xdt_pair, preferred_element_type=f32)
                s_in = state_ref[h]
                y += jnp.dot((cg * eac[:, c:c + 1]).astype(bf16), s_in.astype(bf16), preferred_element_type=f32)
                st = jnp.dot((bg_t * w_t[c:c + 1, :]).astype(bf16), xdt_pair, preferred_element_type=f32)
                state_ref[h] = s_in * chunk_decay[:, c:c + 1] + st
                outs.append(y)
            y_pair = jnp.where(first_half, outs[0], outs[1])
            sl = slice(i * LANE, (i + 1) * LANE)
            y_pair = y_pair + xs[:, sl] * dvec_ref[:, sl]
            z = z_ref[:, sl]
            ys.append(y_pair * (z * jax.nn.sigmoid(z)))

    per_group = len(ys) // SSM_GROUPS
    gw = SSM_INNER // SSM_GROUPS
    for g in range(SSM_GROUPS):
        tiles = ys[g * per_group:(g + 1) * per_group]
        ms = sum(jnp.sum(t * t, axis=-1, keepdims=True) for t in tiles) / gw
        scale = lax.rsqrt(ms + EPS)
        for k, t in enumerate(tiles):
            sl = slice(g * gw + k * LANE, g * gw + (k + 1) * LANE)
            o_ref[:, sl] = (t * scale * ng_ref[:, sl]).astype(o_ref.dtype)


def _ssd(proj, offs, conv_w, conv_b, dt_bias, a_log, d_skip, norm_g, bsz, seq):
    m = bsz * seq
    L = SSM_CHUNK
    n_c = seq // L
    tri = jnp.asarray(np.tril(np.ones((L, L), np.float32)), bf16)
    expand = np.zeros((LANE, SSM_INNER), np.float32)
    for h in range(SSM_HEADS):
        expand[SMALL_DT_LANE + h, h * SSM_HEAD_DIM:(h + 1) * SSM_HEAD_DIM] = 1.0
    row = lambda w: pl.BlockSpec((1, w), lambda b, c: (0, 0))
    return pl.pallas_call(
        _ssd_kernel,
        grid=(bsz, n_c),
        in_specs=[
            pl.BlockSpec((L, SSM_CONV_CH), lambda b, c: (b * n_c + c, offs["c_xbc"] // SSM_CONV_CH)),
            pl.BlockSpec((L, SSM_INNER), lambda b, c: (b * n_c + c, offs["c_z"] // SSM_INNER)),
            pl.BlockSpec((L, LANE), lambda b, c: (b * n_c + c, offs["small"] // LANE)),
            pl.BlockSpec((CONV_K, SSM_CONV_CH), lambda b, c: (0, 0)),
            row(SSM_CONV_CH), row(LANE), row(LANE), row(SSM_INNER), row(SSM_INNER),
            pl.BlockSpec((L, L), lambda b, c: (0, 0)),
            pl.BlockSpec((LANE, SSM_INNER), lambda b, c: (0, 0)),
        ],
        out_specs=pl.BlockSpec((L, SSM_INNER), lambda b, c: (b * n_c + c, 0)),
        out_shape=jax.ShapeDtypeStruct((m, SSM_INNER), bf16),
        scratch_shapes=[
            pltpu.VMEM((CONV_HALO, SSM_CONV_CH), f32),
            pltpu.VMEM((CONV_HALO + L, SSM_CONV_CH), f32),
            pltpu.VMEM((SSM_HEADS, SSM_STATE, LANE), f32),
        ],
        compiler_params=pltpu.CompilerParams(dimension_semantics=("arbitrary", "arbitrary"),
                                             vmem_limit_bytes=V7X_VMEM_LIMIT_BYTES),
        name="ssd_mixer",
    )(proj, proj, proj, conv_w, conv_b.reshape(1, -1), _lane_vec(dt_bias, SMALL_DT_LANE),
      _lane_vec(a_log, SMALL_DT_LANE), jnp.repeat(d_skip, SSM_HEAD_DIM).reshape(1, -1), norm_g.reshape(1, -1),
      tri, jnp.asarray(expand, bf16))


GDN_STEP_ROWS = 128
GDN_QK_W = GDN_HEADS * GDN_DK
GDN_CONV_CH = GDN_HEADS * (2 * GDN_DK + GDN_DV)


def _split2(x):
    hi = x.astype(bf16)
    return hi, (x - hi.astype(f32)).astype(bf16)


def _dot_split(a_parts, b_parts):
    (ah, al), (bh, bl) = a_parts, b_parts
    return (jnp.dot(ah, bh, preferred_element_type=f32) + jnp.dot(ah, bl, preferred_element_type=f32)
            + jnp.dot(al, bh, preferred_element_type=f32))


def _unit_lower_solve_many(a_list, rhs_list):
    n = a_list[0].shape[0]
    ps = [_split2(-a) for a in a_list]
    xs = [rhs + _dot_split(p, _split2(rhs)) for p, rhs in zip(ps, rhs_list)]
    k = 2
    while k < n:
        ps = [_split2(_dot_split(p, p)) for p in ps]
        xs = [x + _dot_split(p, _split2(x)) for p, x in zip(ps, xs)]
        k *= 2
    return xs


def _gdn_kernel(qkv_ref, z_ref, sm_ref, cw_ref, dtb_ref, alog_ref, ng_ref, tri_ref, o_ref, tail_ref, buf_ref,
                state_ref):
    C = GDN_CHUNK

    @pl.when(pl.program_id(1) == 0)
    def _():
        tail_ref[...] = jnp.zeros(tail_ref.shape, f32)
        state_ref[...] = jnp.zeros(state_ref.shape, f32)

    qkv = _causal_conv_silu(qkv_ref, tail_ref, buf_ref, cw_ref, 0.0)
    sm = sm_ref[...]
    beta = jax.nn.sigmoid(sm)
    g_log = -jnp.exp(alog_ref[...]) * _softplus(sm + dtb_ref[...])
    gc = _exact_dot_left(tri_ref[...], g_log)
    gc_t = gc.T
    egc = jnp.exp(gc)

    row = lax.broadcasted_iota(jnp.int32, (C, C), 0)
    col = lax.broadcasted_iota(jnp.int32, (C, C), 1)
    causal = row >= col
    strict = row > col

    units = [(ch, h) for ch in range(GDN_STEP_ROWS // C) for h in range(GDN_HEADS)]
    a_list, rhs_list, pre = [], [], []
    for ch, h in units:
        rs = slice(ch * C, (ch + 1) * C)
        last = slice((ch + 1) * C - 1, (ch + 1) * C)
        q = qkv[rs, h * GDN_DK:(h + 1) * GDN_DK]
        k = qkv[rs, GDN_QK_W + h * GDN_DK:GDN_QK_W + (h + 1) * GDN_DK]
        v = qkv[rs, 2 * GDN_QK_W + h * GDN_DV:2 * GDN_QK_W + (h + 1) * GDN_DV]
        q = q * lax.rsqrt(jnp.sum(q * q, axis=-1, keepdims=True) + EPS) * (GDN_DK ** -0.5)
        k = k * lax.rsqrt(jnp.sum(k * k, axis=-1, keepdims=True) + EPS)
        ca = SMALL_A_LANE + h
        b = beta[rs, SMALL_B_LANE + h:SMALL_B_LANE + h + 1]
        g_col = gc[rs, ca:ca + 1]
        g_last = gc[last, ca:ca + 1]
        e_col = egc[rs, ca:ca + 1]
        decay = jnp.exp(jnp.where(causal, g_col - gc_t[ca:ca + 1, rs], NEG))
        kb = k * b
        k16 = k.astype(bf16)
        a_list.append(jnp.where(strict, _dot_nt(kb.astype(bf16), k16) * decay, 0.0))
        rhs_list.append(jnp.concatenate([v * b, kb * e_col], axis=1))
        attn = (_dot_nt(q.astype(bf16), k16) * decay).astype(bf16)
        q_dec = (q * e_col).astype(bf16)
        k_dec_t = (k * jnp.exp(g_last - g_col)).T.astype(bf16)
        pre.append((attn, q_dec, k_dec_t, jnp.exp(g_last)))

    sols = _unit_lower_solve_many(a_list, rhs_list)

    for (ch, h), sol, (attn, q_dec, k_dec_t, g_end) in zip(units, sols, pre):
        rs = slice(ch * C, (ch + 1) * C)
        u, w = sol[:, :GDN_DV], sol[:, GDN_DV:]
        s_in = state_ref[h]
        s16 = s_in.astype(bf16)
        v_new = u - jnp.dot(w.astype(bf16), s16, preferred_element_type=f32)
        v16 = v_new.astype(bf16)
        o = (jnp.dot(q_dec, s16, preferred_element_type=f32) + jnp.dot(attn, v16, preferred_element_type=f32))
        state_ref[h] = s_in * g_end + jnp.dot(k_dec_t, v16, preferred_element_type=f32)

        o = o * lax.rsqrt(jnp.mean(o * o, axis=-1, keepdims=True) + EPS) * ng_ref[...]
        z = z_ref[rs, h * GDN_DV:(h + 1) * GDN_DV]
        o_ref[rs, h * GDN_DV:(h + 1) * GDN_DV] = (o * (z * jax.nn.sigmoid(z))).astype(o_ref.dtype)


def _gdn(proj, offs, conv_w, a_log, dt_bias, norm_g, bsz, seq):
    m = bsz * seq
    R_ = GDN_STEP_ROWS
    n_s = seq // R_
    idx = np.arange(R_)
    tri = ((idx[:, None] >= idx[None, :]) & (idx[:, None] // GDN_CHUNK == idx[None, :] // GDN_CHUNK))
    row = lambda w: pl.BlockSpec((1, w), lambda b, c: (0, 0))
    return pl.pallas_call(
        _gdn_kernel,
        grid=(bsz, n_s),
        in_specs=[
            pl.BlockSpec((R_, GDN_CONV_CH), lambda b, c: (b * n_s + c, offs["a_q"] // GDN_CONV_CH)),
            pl.BlockSpec((R_, GDN_HEADS * GDN_DV), lambda b, c: (b * n_s + c, offs["a_z"] // (GDN_HEADS * GDN_DV))),
            pl.BlockSpec((R_, LANE), lambda b, c: (b * n_s + c, offs["small"] // LANE)),
            pl.BlockSpec((CONV_K, GDN_CONV_CH), lambda b, c: (0, 0)),
            row(LANE), row(LANE), row(GDN_DV),
            pl.BlockSpec((R_, R_), lambda b, c: (0, 0)),
        ],
        out_specs=pl.BlockSpec((R_, GDN_HEADS * GDN_DV), lambda b, c: (b * n_s + c, 0)),
        out_shape=jax.ShapeDtypeStruct((m, GDN_HEADS * GDN_DV), bf16),
        scratch_shapes=[
            pltpu.VMEM((CONV_HALO, GDN_CONV_CH), f32),
            pltpu.VMEM((CONV_HALO + R_, GDN_CONV_CH), f32),
            pltpu.VMEM((GDN_HEADS, GDN_DK, GDN_DV), f32),
        ],
        compiler_params=pltpu.CompilerParams(dimension_semantics=("arbitrary", "arbitrary"),
                                             vmem_limit_bytes=V7X_VMEM_LIMIT_BYTES),
        name="gdn_mixer",
    )(proj, proj, proj, conv_w, _lane_vec(dt_bias, SMALL_A_LANE), _lane_vec(a_log, SMALL_A_LANE),
      norm_g.reshape(1, -1), jnp.asarray(tri.astype(np.float32), bf16))


_IN_NAMES = ("a_q", "a_k", "a_v", "a_z", "a_a", "a_b", "b_q", "b_kc", "b_vc", "b_ks", "b_vs", "b_kw", "b_vw",
             "b_g", "c_z", "c_xbc", "c_dt", "m_gate")


def _in_sizes(d_model):
    return (GDN_HEADS * GDN_DK, GDN_HEADS * GDN_DK, GDN_HEADS * GDN_DV, GDN_HEADS * GDN_DV, GDN_HEADS, GDN_HEADS,
            NSA_HEADS * NSA_DK, NSA_KV_GROUPS * NSA_DK, NSA_KV_GROUPS * NSA_DV, NSA_KV_GROUPS * NSA_DK,
            NSA_KV_GROUPS * NSA_DV, NSA_KV_GROUPS * NSA_DK, NSA_KV_GROUPS * NSA_DV, 3 * NSA_HEADS,
            SSM_INNER, SSM_CONV_CH, SSM_HEADS, 3 * d_model)


_BIG_ORDER = ("m_gate", "a_q", "a_k", "a_v", "c_xbc", "c_z", "a_z", "b_q", "b_kc", "b_vc", "b_ks", "b_vs",
              "b_kw", "b_vw")
_SMALL_ORDER = ("a_a", "a_b", "c_dt")


def _in_layout(d_model, tn=1024):
    sizes = dict(zip(_IN_NAMES, _in_sizes(d_model)))
    src_off = dict(zip(_IN_NAMES, np.cumsum((0,) + _in_sizes(d_model))[:-1].tolist()))
    cols, offs = [], {}

    def pad_to_lane():
        cols.extend([-1] * (_round_up(len(cols), LANE) - len(cols)))

    for name in _BIG_ORDER:
        offs[name] = len(cols)
        cols.extend(range(src_off[name], src_off[name] + sizes[name]))
    offs["small"] = len(cols)
    for name in _SMALL_ORDER:
        offs[name] = len(cols)
        cols.extend(range(src_off[name], src_off[name] + sizes[name]))
    pad_to_lane()
    offs["b_g"] = len(cols)
    per_group = 3 * NSA_HG
    for g in range(NSA_KV_GROUPS):
        cols.extend(range(src_off["b_g"] + g * per_group, src_off["b_g"] + (g + 1) * per_group))
        pad_to_lane()
    n_pad = _round_up(len(cols), tn)
    cols.extend([-1] * (n_pad - len(cols)))
    return np.asarray(cols, np.int32), offs, sizes, n_pad


def _col_runs(cols):
    runs, start = [], 0
    for i in range(1, len(cols) + 1):
        same_run = i < len(cols) and ((cols[i] < 0 and cols[i - 1] < 0)
                                      or (cols[i - 1] >= 0 and cols[i] == cols[i - 1] + 1))
        if not same_run:
            runs.append((start, None if cols[start] < 0 else int(cols[start]), i - start))
            start = i
    return runs


def _cast_cols_kernel(runs_per_out, w_ref, *o_refs):
    for runs, o_ref in zip(runs_per_out, o_refs):
        for dst, src, size in runs:
            if src is None:
                o_ref[:, dst:dst + size] = jnp.zeros((o_ref.shape[0], size), o_ref.dtype)
            else:
                o_ref[:, dst:dst + size] = w_ref[0, :, src:src + size].astype(o_ref.dtype)


def _cast_cols(w_stack, layer, runs_per_out, widths, *, tr=256):
    _, rows, n_src = w_stack.shape
    return pl.pallas_call(
        functools.partial(_cast_cols_kernel, runs_per_out),
        grid=(rows // tr,),
        in_specs=[pl.BlockSpec((1, tr, n_src), lambda i: (layer, i, 0))],
        out_specs=[pl.BlockSpec((tr, w), lambda i: (i, 0)) for w in widths],
        out_shape=[jax.ShapeDtypeStruct((rows, w), bf16) for w in widths],
        compiler_params=pltpu.CompilerParams(dimension_semantics=("parallel",),
                                             vmem_limit_bytes=V7X_VMEM_LIMIT_BYTES),
        name="weight_cast_cols",
    )(w_stack)


def _cast_rows_kernel(rows_src, w_ref, o_ref):
    tr = o_ref.shape[0]
    row = lax.broadcasted_iota(jnp.int32, o_ref.shape, 0) + pl.program_id(0) * tr
    o_ref[...] = jnp.where(row < rows_src, w_ref[0], 0.0).astype(o_ref.dtype)


def _cast_rows(w_stack, layer, rows_out, *, tr=512):
    _, rows_src, n = w_stack.shape
    return pl.pallas_call(
        functools.partial(_cast_rows_kernel, rows_src),
        grid=(rows_out // tr,),
        in_specs=[pl.BlockSpec((1, tr, n), lambda i: (layer, i, 0))],
        out_specs=pl.BlockSpec((tr, n), lambda i: (i, 0)),
        out_shape=jax.ShapeDtypeStruct((rows_out, n), bf16),
        compiler_params=pltpu.CompilerParams(dimension_semantics=("parallel",),
                                             vmem_limit_bytes=V7X_VMEM_LIMIT_BYTES),
        name="weight_cast_rows",
    )(w_stack)


def _prep_w_in(w_in, layer, cols):
    return _cast_cols(w_in, layer, [_col_runs(cols)], [len(cols)])[0]


def _prep_ffn(w_up, w_down, layer, tf=512):
    ff = w_down.shape[1]
    ffp = _round_up(ff, tf)
    half = lambda src: [(0, src, ff)] + ([(ff, None, ffp - ff)] if ffp > ff else [])
    wa, wb = _cast_cols(w_up, layer, [half(0), half(ff)], [ffp, ffp])
    return wa, wb, _cast_rows(w_down, layer, ffp)


def kernel(x, rel_table, g_ffn1, w_up1, w_down1, g_mix, w_in, gdn_conv, gdn_a_log, gdn_dt_bias, gdn_norm,
           nsa_q_norm, nsa_k_norm, nsa_pe_k, nsa_pe_v, nsa_w_ck, nsa_w_cv, ssm_conv_w, ssm_conv_b,
           ssm_dt_bias, ssm_a_log, ssm_d, ssm_norm, p_a, p_b, p_c, w_o, g_ffn2, w_up2, w_down2):
    bsz, seq, d = x.shape
    depth = w_in.shape[0]
    m = bsz * seq
    cols, offs, _, _ = _in_layout(d)
    x2 = x.reshape(m, d)
    nsa_consts = _nsa_constants(seq)

    for l in range(depth):
        wa, wb, wd = _prep_ffn(w_up1, w_down1, l)
        x2 = _ffn(x2, g_ffn1[l].reshape(1, d), wa, wb, wd)

        proj = _norm_matmul(x2, g_mix[l].reshape(1, d), _prep_w_in(w_in, l, cols))
        y_a = _gdn(proj, offs, gdn_conv[l], gdn_a_log[l], gdn_dt_bias[l], gdn_norm[l], bsz, seq)
        y_b = _nsa(proj, offs, nsa_q_norm[l], nsa_k_norm[l], nsa_pe_k[l], nsa_pe_v[l], nsa_w_ck[l], nsa_w_cv[l],
                   rel_table, nsa_consts, bsz, seq)
        y_c = _ssd(proj, offs, ssm_conv_w[l], ssm_conv_b[l], ssm_dt_bias[l], ssm_a_log[l], ssm_d[l], ssm_norm[l],
                   bsz, seq)
        x2 = _merge(x2, y_a, y_b, y_c, proj, offs["m_gate"],
                    p_a[l].astype(bf16), p_b[l].astype(bf16), p_c[l].astype(bf16), w_o[l].astype(bf16))

        wa, wb, wd = _prep_ffn(w_up2, w_down2, l)
        x2 = _ffn(x2, g_ffn2[l].reshape(1, d), wa, wb, wd)
    return x2.reshape(bsz, seq, d)
```

```python
import functools
import math

import jax
import jax.numpy as jnp
import numpy as np
from jax import lax
from jax.experimental import pallas as pl
from jax.experimental.pallas import tpu as pltpu

EPS = 1e-6
CONV_K = 4

GDN_HEADS = 4
GDN_DK = 128
GDN_DV = 128
GDN_CHUNK = 64

NSA_HEADS = 4
NSA_KV_GROUPS = 2
NSA_DK = 128
NSA_DV = 128
CMP_BLOCK = 32
CMP_STRIDE = 16
SEL_BLOCK = 64
N_SELECT = 16
WINDOW = 512
Q_BLOCK = 128
SEL_OVERLAP_WEIGHTS = (1.0, 2.0, 2.0, 2.0, 1.0)
FORCE_SCORE = 1e4
NEG = -1e30

SSM_HEADS = 16
SSM_HEAD_DIM = 64
SSM_GROUPS = 2
SSM_STATE = 128
SSM_CHUNK = 128
SSM_INNER = SSM_HEADS * SSM_HEAD_DIM
SSM_CONV_CH = SSM_INNER + 2 * SSM_GROUPS * SSM_STATE

REL_BUCKETS = 32
REL_MAX_EXACT = 16
REL_MAX_DIST = 1024

V7X_VMEM_LIMIT_BYTES = 56 * 1024 * 1024
LANE = 128

bf16 = jnp.bfloat16
f32 = jnp.float32


def _round_up(n, m):
    return (n + m - 1) // m * m


def _ffn_kernel(x_ref, g_ref, wa_ref, wb_ref, wd_ref, o_ref, h_ref):
    j = pl.program_id(1)

    @pl.when(j == 0)
    def _():
        x = x_ref[...]
        h = x * lax.rsqrt(jnp.mean(x * x, axis=-1, keepdims=True) + EPS) * g_ref[...]
        h_ref[...] = h.astype(bf16)
        o_ref[...] = x

    h = h_ref[...]
    a = jnp.dot(h, wa_ref[...], preferred_element_type=f32)
    b = jnp.dot(h, wb_ref[...], preferred_element_type=f32)
    act = (0.5 * a * jax.nn.sigmoid(a) * b).astype(bf16)
    o_ref[...] += jnp.dot(act, wd_ref[...], preferred_element_type=f32)


def _ffn(x2, g, wa, wb, wd, *, tm=1024, tf=512):
    m, d = x2.shape
    ffp = wa.shape[1]
    return pl.pallas_call(
        _ffn_kernel,
        grid=(m // tm, ffp // tf),
        in_specs=[
            pl.BlockSpec((tm, d), lambda i, j: (i, 0)),
            pl.BlockSpec((1, d), lambda i, j: (0, 0)),
            pl.BlockSpec((d, tf), lambda i, j: (0, j)),
            pl.BlockSpec((d, tf), lambda i, j: (0, j)),
            pl.BlockSpec((tf, d), lambda i, j: (j, 0)),
        ],
        out_specs=pl.BlockSpec((tm, d), lambda i, j: (i, 0)),
        out_shape=jax.ShapeDtypeStruct((m, d), f32),
        scratch_shapes=[pltpu.VMEM((tm, d), bf16)],
        compiler_params=pltpu.CompilerParams(
            dimension_semantics=("parallel", "arbitrary"),
            vmem_limit_bytes=V7X_VMEM_LIMIT_BYTES),
        name="ffn_swiglu",
    )(x2, g, wa, wb, wd)


def _norm_matmul_kernel(x_ref, g_ref, w_ref, o_ref, h_ref):
    j = pl.program_id(1)

    @pl.when(j == 0)
    def _():
        x = x_ref[...]
        h = x * lax.rsqrt(jnp.mean(x * x, axis=-1, keepdims=True) + EPS) * g_ref[...]
        h_ref[...] = h.astype(bf16)

    o_ref[...] = jnp.dot(h_ref[...], w_ref[...], preferred_element_type=f32)


def _norm_matmul(x2, g, w, *, tm=1024, tn=1024):
    m, d = x2.shape
    n = w.shape[1]
    return pl.pallas_call(
        _norm_matmul_kernel,
        grid=(m // tm, n // tn),
        in_specs=[
            pl.BlockSpec((tm, d), lambda i, j: (i, 0)),
            pl.BlockSpec((1, d), lambda i, j: (0, 0)),
            pl.BlockSpec((d, tn), lambda i, j: (0, j)),
        ],
        out_specs=pl.BlockSpec((tm, tn), lambda i, j: (i, j)),
        out_shape=jax.ShapeDtypeStruct((m, n), f32),
        scratch_shapes=[pltpu.VMEM((tm, d), bf16)],
        compiler_params=pltpu.CompilerParams(
            dimension_semantics=("parallel", "arbitrary"),
            vmem_limit_bytes=V7X_VMEM_LIMIT_BYTES),
        name="norm_in_proj",
    )(x2, g, w)


def _merge_kernel(x_ref, ya_ref, yb_ref, yc_ref, ga_ref, gb_ref, gc_ref, pa_ref, pb_ref, pc_ref, wo_ref, o_ref):
    ma = jnp.dot(ya_ref[...].astype(bf16), pa_ref[...], preferred_element_type=f32)
    mb = jnp.dot(yb_ref[...].astype(bf16), pb_ref[...], preferred_element_type=f32)
    mc = jnp.dot(yc_ref[...].astype(bf16), pc_ref[...], preferred_element_type=f32)
    merged = (jax.nn.sigmoid(ga_ref[...]) * ma + jax.nn.sigmoid(gb_ref[...]) * mb
              + jax.nn.sigmoid(gc_ref[...]) * mc)
    o_ref[...] = x_ref[...] + jnp.dot(merged.astype(bf16), wo_ref[...], preferred_element_type=f32)


def _merge(x2, ya, yb, yc, gates, gate_col0, pa, pb, pc, wo, *, tm=256):
    m, d = x2.shape
    gb0 = gate_col0 // d
    const = dict(pipeline_mode=pl.Buffered(1))
    return pl.pallas_call(
        _merge_kernel,
        grid=(m // tm,),
        in_specs=[
            pl.BlockSpec((tm, d), lambda i: (i, 0)),
            pl.BlockSpec((tm, ya.shape[1]), lambda i: (i, 0)),
            pl.BlockSpec((tm, yb.shape[1]), lambda i: (i, 0)),
            pl.BlockSpec((tm, yc.shape[1]), lambda i: (i, 0)),
            pl.BlockSpec((tm, d), lambda i: (i, gb0)),
            pl.BlockSpec((tm, d), lambda i: (i, gb0 + 1)),
            pl.BlockSpec((tm, d), lambda i: (i, gb0 + 2)),
            pl.BlockSpec(pa.shape, lambda i: (0, 0), **const),
            pl.BlockSpec(pb.shape, lambda i: (0, 0), **const),
            pl.BlockSpec(pc.shape, lambda i: (0, 0), **const),
            pl.BlockSpec(wo.shape, lambda i: (0, 0), **const),
        ],
        out_specs=pl.BlockSpec((tm, d), lambda i: (i, 0)),
        out_shape=jax.ShapeDtypeStruct((m, d), f32),
        compiler_params=pltpu.CompilerParams(
            dimension_semantics=("parallel",),
            vmem_limit_bytes=V7X_VMEM_LIMIT_BYTES),
        name="merge_out_proj",
    )(x2, ya, yb, yc, gates, gates, gates, pa, pb, pc, wo)


NSA_HG = NSA_HEADS // NSA_KV_GROUPS
SEL_FAR_TILE = 512
SEL_FAR_SUB = 2
NEAR_TILES = 8
STRIP_W = NEAR_TILES * Q_BLOCK
STRIP_D0 = (NEAR_TILES - 1) * Q_BLOCK
WIN_TILES = WINDOW // Q_BLOCK + 1
CMP_PER_Q = Q_BLOCK // CMP_STRIDE
SEL_PER_Q = Q_BLOCK // SEL_BLOCK
MASK_PEN = -1e30


def _dot_nt(a, b):
    return lax.dot_general(a, b, (((1,), (1,)), ((), ())), preferred_element_type=f32)


def _lane_rms(x, gain):
    return x * lax.rsqrt(jnp.mean(x * x, axis=-1, keepdims=True) + EPS) * gain


def _nsa_prep_kernel(seq, q_ref, ks_ref, kw_ref, qg_ref, kg_ref, qo_ref, kao_ref, vso_ref, kwo_ref):
    tm = q_ref.shape[0]
    qg = qg_ref[...] * (NSA_DK ** -0.5)
    kg = kg_ref[...]
    for h in range(NSA_HEADS):
        sl = slice(h * NSA_DK, (h + 1) * NSA_DK)
        qo_ref[:, sl] = _lane_rms(q_ref[:, sl], qg).astype(bf16)
    kw_ = NSA_KV_GROUPS * NSA_DK
    tok = (lax.broadcasted_iota(jnp.int32, (tm, LANE), 0) + pl.program_id(0) * tm) % seq
    onehot = jnp.where(lax.broadcasted_iota(jnp.int32, (tm, LANE), 1) == tok // SEL_BLOCK, 1.0, 0.0).astype(bf16)
    for g in range(NSA_KV_GROUPS):
        sl = slice(g * NSA_DK, (g + 1) * NSA_DK)
        kao_ref[:, 2 * g * NSA_DK:(2 * g + 1) * NSA_DK] = _lane_rms(ks_ref[:, sl], kg).astype(bf16)
        kao_ref[:, (2 * g + 1) * NSA_DK:(2 * g + 2) * NSA_DK] = onehot
        kwo_ref[:, sl] = _lane_rms(kw_ref[:, sl], kg).astype(bf16)
    vso_ref[...] = ks_ref[:, kw_:].astype(bf16)
    kwo_ref[:, kw_:] = kw_ref[:, kw_:].astype(bf16)


def _nsa_prep(proj, offs, q_gain, k_gain, seq, *, tm=512):
    m = proj.shape[0]
    w = NSA_HEADS * NSA_DK
    assert seq // SEL_BLOCK <= LANE and NSA_DK == LANE, "selection blocks must fit one 128-lane one-hot"
    blk = lambda name: pl.BlockSpec((tm, w), lambda i, c=offs[name] // w: (i, c))
    widths = (w, 2 * NSA_KV_GROUPS * NSA_DK, NSA_KV_GROUPS * NSA_DV, w)
    return pl.pallas_call(
        functools.partial(_nsa_prep_kernel, seq),
        grid=(m // tm,),
        in_specs=[blk("b_q"), blk("b_ks"), blk("b_kw"),
                  pl.BlockSpec((1, NSA_DK), lambda i: (0, 0)), pl.BlockSpec((1, NSA_DK), lambda i: (0, 0))],
        out_specs=[pl.BlockSpec((tm, wd), lambda i: (i, 0)) for wd in widths],
        out_shape=[jax.ShapeDtypeStruct((m, wd), bf16) for wd in widths],
        compiler_params=pltpu.CompilerParams(dimension_semantics=("parallel",),
                                             vmem_limit_bytes=V7X_VMEM_LIMIT_BYTES),
        name="nsa_prep",
    )(proj, proj, proj, q_gain, k_gain)


def _nsa_compress_kernel(x_ref, w_ref, pe_ref, kg_ref, o_ref, hi_ref):
    j = pl.program_id(1)
    nc = o_ref.shape[2]
    lo = jnp.zeros((nc, NSA_DK), f32)
    hi = jnp.zeros((nc, NSA_DK), f32)
    for l in range(CMP_STRIDE):
        rows = x_ref[pl.ds(l, nc, stride=CMP_STRIDE), :]
        lo += jnp.dot((rows + pe_ref[0, l:l + 1, :]).astype(bf16), w_ref[0, l], preferred_element_type=f32)
        hi += jnp.dot((rows + pe_ref[0, CMP_STRIDE + l:CMP_STRIDE + l + 1, :]).astype(bf16),
                      w_ref[0, CMP_STRIDE + l], preferred_element_type=f32)
    hi_ref[pl.ds(0, nc), :] = hi
    hi_ref[pl.ds(nc, 8), :] = jnp.zeros((8, NSA_DK), f32)
    c = lo + hi_ref[pl.ds(1, nc), :]
    normed = _lane_rms(c, kg_ref[...])
    o_ref[0, 0] = jnp.where(j < NSA_KV_GROUPS, normed, c).astype(bf16)


def _nsa_compress(proj, offs, w_stack, pe_stack, k_gain, bsz, seq):
    nc = seq // CMP_STRIDE
    c0 = offs["b_kc"] // NSA_DK
    return pl.pallas_call(
        _nsa_compress_kernel,
        grid=(bsz, 2 * NSA_KV_GROUPS),
        in_specs=[
            pl.BlockSpec((seq, NSA_DK), lambda b, j: (b, c0 + j)),
            pl.BlockSpec((1, CMP_BLOCK, NSA_DK, NSA_DK), lambda b, j: (j // NSA_KV_GROUPS, 0, 0, 0)),
            pl.BlockSpec((1, CMP_BLOCK, NSA_DK), lambda b, j: (j // NSA_KV_GROUPS, 0, 0)),
            pl.BlockSpec((1, NSA_DK), lambda b, j: (0, 0)),
        ],
        out_specs=pl.BlockSpec((1, 1, nc, NSA_DK), lambda b, j: (b, j, 0, 0)),
        out_shape=jax.ShapeDtypeStruct((bsz, 2 * NSA_KV_GROUPS, nc, NSA_DK), bf16),
        scratch_shapes=[pltpu.VMEM((nc + 8, NSA_DK), f32)],
        compiler_params=pltpu.CompilerParams(dimension_semantics=("parallel", "parallel"),
                                             vmem_limit_bytes=V7X_VMEM_LIMIT_BYTES),
        name="nsa_compress",
    )(proj, w_stack, pe_stack, k_gain)


def _split3(x):
    hi = x.astype(bf16)
    r1 = x - hi.astype(f32)
    mid = r1.astype(bf16)
    lo = (r1 - mid.astype(f32)).astype(bf16)
    return hi, mid, lo


def _nsa_attn_kernel(tbl_ref, q_ref, kc_ref, vc_ref, ka_ref, vs_ref, kw_ref, vw_ref, mt_ref, bs_ref,
                     bc_ref, gl_ref, o_ref, strip_ref, sc_ref, score_ref, stage_ref, far_ref):
    g = pl.program_id(1)
    qi = pl.program_id(2)
    nc = kc_ref.shape[2]
    qb = Q_BLOCK

    @pl.when(qi == 0)
    def _():
        def lookup(bucket):
            v0 = jnp.zeros(bucket.shape, f32)
            v1 = jnp.zeros(bucket.shape, f32)
            for k in range(REL_BUCKETS):
                eq = bucket == k
                v0 = jnp.where(eq, tbl_ref[NSA_HG * g, k], v0)
                v1 = jnp.where(eq, tbl_ref[NSA_HG * g + 1, k], v1)
            return v0, v1

        for c in range(NEAR_TILES):
            sl = slice(c * qb, (c + 1) * qb)
            v0, v1 = lookup(bs_ref[:, sl])
            strip_ref[0, :, sl] = v0
            strip_ref[1, :, sl] = v1
        v0, v1 = lookup(bc_ref[...])
        for hh, v in enumerate((v0, v1)):
            d = v - tbl_ref[NSA_HG * g + hh, REL_BUCKETS - 1]
            hi = d.astype(bf16)
            sc_ref[hh, 0] = hi
            sc_ref[hh, 1] = (d - hi.astype(f32)).astype(bf16)

    far_bias = [tbl_ref[NSA_HG * g + hh, REL_BUCKETS - 1] for hh in range(NSA_HG)]
    q = q_ref[...]
    q2 = jnp.concatenate([q[:, :NSA_DK], q[:, NSA_DK:]], axis=0)

    r2 = lax.broadcasted_iota(jnp.int32, (NSA_HG * qb, qb), 0) % qb
    c2 = lax.broadcasted_iota(jnp.int32, (NSA_HG * qb, qb), 1)

    def strip_bias(w):
        sl = slice(w * qb, (w + 1) * qb)
        return jnp.concatenate([strip_ref[0, :, sl], strip_ref[1, :, sl]], axis=0)

    def key_rows(w):
        kt = qi - (NEAR_TILES - 1) + w
        return kt, pl.ds(pl.multiple_of(jnp.maximum(kt, 0) * qb, qb), qb)

    m_w = jnp.full((NSA_HG * qb, 1), NEG, f32)
    for w in range(NEAR_TILES - WIN_TILES, NEAR_TILES):
        kt, rows = key_rows(w)
        s = _dot_nt(q2, kw_ref[rows, :]) + strip_bias(w)
        if w == NEAR_TILES - WIN_TILES:
            valid = (c2 > r2) & (kt >= 0)
        elif w == NEAR_TILES - 1:
            valid = c2 <= r2
        else:
            valid = jnp.broadcast_to(kt >= 0, c2.shape)
        s = jnp.where(valid, s, NEG)
        stage_ref[w] = s
        m_w = jnp.maximum(m_w, jnp.max(s, axis=-1, keepdims=True))
    den = jnp.zeros((NSA_HG * qb, 1), f32)
    acc_w = jnp.zeros((NSA_HG * qb, NSA_DV), f32)
    for w in range(NEAR_TILES - WIN_TILES, NEAR_TILES):
        _, rows = key_rows(w)
        e = jnp.exp(stage_ref[w] - m_w)
        den = den + jnp.sum(e, axis=-1, keepdims=True)
        acc_w = acc_w + jnp.dot(e.astype(bf16), vw_ref[rows, :], preferred_element_type=f32)
    o_win = acc_w / den

    r_i = lax.broadcasted_iota(jnp.int32, (qb, nc), 0)
    n_i = lax.broadcasted_iota(jnp.int32, (qb, nc), 1)
    ok_c = CMP_STRIDE * n_i <= qb * qi + r_i - (CMP_BLOCK - 1)
    shift = jnp.where(r_i + n_i == CMP_PER_Q * qi + CMP_PER_Q - 1, 1.0, 0.0).astype(bf16)
    kc = kc_ref[0, 0]
    vc = vc_ref[0, 0]
    psum = jnp.zeros((qb, nc), f32)
    o_cmp = []
    for hh in range(NSA_HG):
        s = _dot_nt(q2[hh * qb:(hh + 1) * qb], kc)
        bias = (far_bias[hh] + jnp.dot(sc_ref[hh, 0], shift, preferred_element_type=f32)
                + jnp.dot(sc_ref[hh, 1], shift, preferred_element_type=f32))
        s = jnp.where(ok_c, s + bias, NEG)
        e = jnp.exp(s - jnp.max(s, axis=-1, keepdims=True))
        p = jnp.where(ok_c, e / jnp.sum(e, axis=-1, keepdims=True), 0.0)
        o_cmp.append(jnp.dot(p.astype(bf16), vc, preferred_element_type=f32))
        psum += p

    mt = mt_ref[...]
    imp_t = sum(_dot_nt(mt, part) for part in _split3(psum))
    nblk = imp_t.shape[0]
    j_i = lax.broadcasted_iota(jnp.int32, (nblk, qb), 0)
    l_i = lax.broadcasted_iota(jnp.int32, (nblk, qb), 1)
    cur = SEL_PER_Q * qi + l_i // SEL_BLOCK
    ok_s = j_i <= cur
    forced = ok_s & ((j_i == 0) | (j_i >= cur - 1))
    score = jnp.where(forced, FORCE_SCORE, jnp.where(ok_s, imp_t, -FORCE_SCORE))
    key = pltpu.bitcast(score, jnp.int32)
    score_ref[...] = key
    half = nblk // 2

    def rank_half(lo):
        key_h = key[lo:lo + half]
        key_h_m1 = key_h - 1
        j_h = j_i[lo:lo + half]

        def rank_body(i, cnt):
            for u in range(SEL_PER_Q):
                jp = SEL_PER_Q * i + u
                row = score_ref[pl.ds(jp, 1), :]
                cnt = cnt + jnp.where(row > jnp.where(j_h > jp, key_h_m1, key_h), 1, 0)
            return cnt

        return lax.fori_loop(0, qi + 1, rank_body, jnp.zeros((half, qb), jnp.int32))

    cnt_hi = lax.cond(SEL_PER_Q * qi + SEL_PER_Q > half, lambda: rank_half(half),
                      lambda: jnp.full((half, qb), N_SELECT, jnp.int32))
    cnt = jnp.concatenate([rank_half(0), cnt_hi], axis=0)
    pen = jnp.where(cnt < N_SELECT, 0.0, MASK_PEN).T.astype(bf16)
    q_aug = jnp.concatenate([q2, jnp.concatenate([pen, pen], axis=0)], axis=1)

    row2 = lax.broadcasted_iota(jnp.int32, (NSA_HG * qb, 1), 0)
    far_col = jnp.where(row2 < qb, far_bias[0], far_bias[1])
    n_far_keys = qb * jnp.maximum(qi - (NEAR_TILES - 1), 0)
    far_step = SEL_FAR_SUB * SEL_FAR_TILE
    c_far = lax.broadcasted_iota(jnp.int32, (NSA_HG * qb, SEL_FAR_TILE), 1)

    n_far_steps = (n_far_keys + far_step - 1) // far_step

    def far_rows(kt, sub):
        k0 = kt * far_step + sub * SEL_FAR_TILE
        return k0, pl.ds(pl.multiple_of(k0, SEL_FAR_TILE), SEL_FAR_TILE)

    def far_scores(kt):
        for sub in range(SEL_FAR_SUB):
            k0, rows = far_rows(kt, sub)
            s = _dot_nt(q_aug, ka_ref[rows, :]) + far_col
            far_ref[kt % 2, :, sub * SEL_FAR_TILE:(sub + 1) * SEL_FAR_TILE] = jnp.where(
                c_far < n_far_keys - k0, s, MASK_PEN)

    far_scores(0)

    m_near = jnp.full((NSA_HG * qb, 1), NEG, f32)
    for w in range(NEAR_TILES):
        kt, rows = key_rows(w)
        s = _dot_nt(q_aug, ka_ref[rows, :]) + strip_bias(w)
        if w == NEAR_TILES - 1:
            s = jnp.where(c2 <= r2, s, NEG)
        else:
            s = jnp.where(kt >= 0, s, NEG)
        stage_ref[w] = s
        m_near = jnp.maximum(m_near, jnp.max(s, axis=-1, keepdims=True))

    def far_body(kt, carry):
        m_old, l_old, acc_old = carry
        ss = [far_ref[kt % 2, :, sub * SEL_FAR_TILE:(sub + 1) * SEL_FAR_TILE] for sub in range(SEL_FAR_SUB)]
        m_new = functools.reduce(jnp.maximum, [m_old] + [jnp.max(s, axis=-1, keepdims=True) for s in ss])
        ps = [jnp.exp(s - m_new) for s in ss]
        alpha = jnp.exp(m_old - m_new)
        l_new = alpha * l_old + sum(jnp.sum(p, axis=-1, keepdims=True) for p in ps)
        acc_new = alpha * acc_old + sum(
            jnp.dot(p.astype(bf16), vs_ref[far_rows(kt, sub)[1], :], preferred_element_type=f32)
            for sub, p in enumerate(ps))
        far_scores(jnp.minimum(kt + 1, n_far_steps - 1))
        return m_new, l_new, acc_new

    m_far, l_far, acc_far = lax.fori_loop(
        0, n_far_steps, far_body,
        (jnp.full((NSA_HG * qb, 1), MASK_PEN, f32), jnp.zeros((NSA_HG * qb, 1), f32),
         jnp.zeros((NSA_HG * qb, NSA_DV), f32)))

    m_sel = jnp.maximum(m_far, m_near)
    alpha = jnp.exp(m_far - m_sel)
    l_sel = alpha * l_far
    acc_sel = alpha * acc_far
    for w in range(NEAR_TILES):
        _, rows = key_rows(w)
        e = jnp.exp(stage_ref[w] - m_sel)
        l_sel = l_sel + jnp.sum(e, axis=-1, keepdims=True)
        acc_sel = acc_sel + jnp.dot(e.astype(bf16), vs_ref[rows, :], preferred_element_type=f32)
    o_sel = acc_sel / l_sel

    gate = jax.nn.sigmoid(gl_ref[...])
    for hh in range(NSA_HG):
        rs = slice(hh * qb, (hh + 1) * qb)
        o_ref[:, hh * NSA_DV:(hh + 1) * NSA_DV] = (gate[:, 3 * hh:3 * hh + 1] * o_cmp[hh]
                                                   + gate[:, 3 * hh + 1:3 * hh + 2] * o_sel[rs]
                                                   + gate[:, 3 * hh + 2:3 * hh + 3] * o_win[rs])


def _rel_bucket(dist):
    n = jnp.maximum(dist, 0)
    nf = jnp.maximum(n, 1).astype(f32)
    large = REL_MAX_EXACT + (jnp.log(nf / REL_MAX_EXACT) / math.log(REL_MAX_DIST / REL_MAX_EXACT)
                             * (REL_BUCKETS - REL_MAX_EXACT)).astype(jnp.int32)
    large = jnp.minimum(large, REL_BUCKETS - 1)
    return jnp.where(n < REL_MAX_EXACT, n, large)


def _nsa_constants(seq):
    nc = seq // CMP_STRIDE
    r = jnp.arange(Q_BLOCK)[:, None]
    bs = _rel_bucket(STRIP_D0 + r - jnp.arange(STRIP_W)[None, :])
    bc = _rel_bucket(r + CMP_STRIDE * jnp.arange(LANE)[None, :] - (CMP_STRIDE * (CMP_PER_Q - 1) + CMP_BLOCK - 1))
    ratio = SEL_BLOCK // CMP_STRIDE
    mt = np.zeros((LANE, nc), np.float32)
    for j in range(LANE):
        for i, wt in enumerate(SEL_OVERLAP_WEIGHTS):
            mcol = ratio * j - 1 + i
            if 0 <= mcol < nc:
                mt[j, mcol] = wt
    return bs.astype(jnp.int32), bc.astype(jnp.int32), jnp.asarray(mt, bf16)


def _nsa_attention(qn, cmp_kv, kaug, vsel, kwvw, proj, offs, tbl, consts, bsz, seq):
    bs, bc, mt = consts
    assert seq % (SEL_FAR_SUB * SEL_FAR_TILE) == 0, "far steps must tile the sequence"
    m = bsz * seq
    n_qb = seq // Q_BLOCK
    nc = seq // CMP_STRIDE
    g_ = NSA_KV_GROUPS
    kv = lambda width, col0: pl.BlockSpec((seq, width), lambda b, g, i: (b, col0 + g))
    cmp_spec = lambda j0: pl.BlockSpec((1, 1, nc, NSA_DK), lambda b, g, i: (b, j0 + g, 0, 0))
    const2 = lambda shape: pl.BlockSpec(shape, lambda b, g, i: (0, 0))
    gcol = offs["b_g"] // LANE
    return pl.pallas_call(
        _nsa_attn_kernel,
        grid=(bsz, g_, n_qb),
        in_specs=[
            pl.BlockSpec(memory_space=pltpu.SMEM),
            pl.BlockSpec((Q_BLOCK, NSA_HG * NSA_DK), lambda b, g, i: (b * n_qb + i, g)),
            cmp_spec(0), cmp_spec(g_),
            kv(NSA_DK + LANE, 0), kv(NSA_DV, 0), kv(NSA_DK, 0), kv(NSA_DV, g_),
            const2((LANE, nc)), const2((Q_BLOCK, STRIP_W)), const2((Q_BLOCK, LANE)),
            pl.BlockSpec((Q_BLOCK, LANE), lambda b, g, i: (b * n_qb + i, gcol + g)),
        ],
        out_specs=pl.BlockSpec((Q_BLOCK, NSA_HG * NSA_DV), lambda b, g, i: (b * n_qb + i, g)),
        out_shape=jax.ShapeDtypeStruct((m, NSA_HEADS * NSA_DV), f32),
        scratch_shapes=[
            pltpu.VMEM((NSA_HG, Q_BLOCK, STRIP_W), f32),
            pltpu.VMEM((NSA_HG, 2, Q_BLOCK, LANE), bf16),
            pltpu.VMEM((LANE, Q_BLOCK), jnp.int32),
            pltpu.VMEM((NEAR_TILES, NSA_HG * Q_BLOCK, Q_BLOCK), f32),
            pltpu.VMEM((2, NSA_HG * Q_BLOCK, SEL_FAR_SUB * SEL_FAR_TILE), f32),
        ],
        compiler_params=pltpu.CompilerParams(dimension_semantics=("arbitrary", "arbitrary", "arbitrary"),
                                             vmem_limit_bytes=V7X_VMEM_LIMIT_BYTES),
        name="nsa_attention",
    )(tbl, qn, cmp_kv, cmp_kv, kaug, vsel, kwvw, kwvw, mt, bs, bc, proj)


def _nsa(proj, offs, q_norm, k_norm, pe_k, pe_v, w_ck, w_cv, rel_table, consts, bsz, seq):
    q_gain = q_norm.reshape(1, NSA_DK)
    k_gain = k_norm.reshape(1, NSA_DK)
    qn, kaug, vsel, kwvw = _nsa_prep(proj, offs, q_gain, k_gain, seq)
    cmp_kv = _nsa_compress(proj, offs, jnp.stack([w_ck, w_cv]).astype(bf16), jnp.stack([pe_k, pe_v]), k_gain,
                           bsz, seq)
    return _nsa_attention(qn, cmp_kv, kaug, vsel, kwvw, proj, offs, rel_table.T, consts, bsz, seq)


SMALL_A_LANE = 0
SMALL_B_LANE = GDN_HEADS
SMALL_DT_LANE = 2 * GDN_HEADS
CONV_HALO = 8


def _softplus(x):
    return jnp.maximum(x, 0.0) + jnp.log1p(jnp.exp(-jnp.abs(x)))


def _exact_dot(a_f32, b_01):
    return sum(jnp.dot(part, b_01, preferred_element_type=f32) for part in _split3(a_f32))


def _exact_dot_left(b_01, a_f32):
    return sum(jnp.dot(b_01, part, preferred_element_type=f32) for part in _split3(a_f32))


def _lane_vec(values, lane0):
    return jnp.zeros((1, LANE), f32).at[0, lane0:lane0 + values.shape[0]].set(values.astype(f32))


def _causal_conv_silu(x_ref, tail_ref, buf_ref, w_ref, bias):
    rows = x_ref.shape[0]
    buf_ref[pl.ds(0, CONV_HALO), :] = tail_ref[...]
    buf_ref[pl.ds(CONV_HALO, rows), :] = x_ref[...]
    tail_ref[...] = x_ref[pl.ds(rows - CONV_HALO, CONV_HALO), :]
    acc = bias
    for k in range(CONV_K):
        acc = acc + w_ref[k:k + 1, :] * buf_ref[pl.ds(CONV_HALO - (CONV_K - 1) + k, rows), :]
    return acc * jax.nn.sigmoid(acc)


SSM_PAIR = LANE // SSM_HEAD_DIM
SSM_HEADS_PER_GROUP = SSM_HEADS // SSM_GROUPS


def _ssd_kernel(xbc_ref, z_ref, sm_ref, cw_ref, cb_ref, dtb_ref, alog_ref, dvec_ref, ng_ref, tri_ref, exp_ref,
                o_ref, tail_ref, buf_ref, state_ref):
    L = SSM_CHUNK
    N = SSM_STATE

    xbc = _causal_conv_silu(xbc_ref, tail_ref, buf_ref, cw_ref, cb_ref[...])
    xs = xbc[:, :SSM_INNER]
    dt = _softplus(sm_ref[...] + dtb_ref[...])
    a = dt * (-jnp.exp(alog_ref[...]))
    tri = tri_ref[...]
    ac = _exact_dot_left(tri, a)
    ac_t = ac.T
    eac = jnp.exp(ac)
    w_t = jnp.exp(ac_t[:, L - 1:L] - ac_t)
    chunk_decay = jnp.exp(ac[L - 1:L, :])
    xdt = xs * _exact_dot(dt, exp_ref[...])

    row = lax.broadcasted_iota(jnp.int32, (L, L), 0)
    col = lax.broadcasted_iota(jnp.int32, (L, L), 1)
    causal = row >= col
    lane = lax.broadcasted_iota(jnp.int32, (L, LANE), 1)
    first_half = lane < SSM_HEAD_DIM

    ys = []
    for g in range(SSM_GROUPS):
        bg = xbc[:, SSM_INNER + g * N:SSM_INNER + (g + 1) * N]
        cg = xbc[:, SSM_INNER + SSM_GROUPS * N + g * N:SSM_INNER + SSM_GROUPS * N + (g + 1) * N]
        cb = _dot_nt(cg.astype(bf16), bg.astype(bf16))
        bg_t = bg.T
        for i in range(g * SSM_HEADS_PER_GROUP // SSM_PAIR, (g + 1) * SSM_HEADS_PER_GROUP // SSM_PAIR):
            xdt_pair = xdt[:, i * LANE:(i + 1) * LANE].astype(bf16)
            outs = []
            for hh in range(SSM_PAIR):
                h = SSM_PAIR * i + hh
                c = SMALL_DT_LANE + h
                seg = ac[:, c:c + 1] - ac_t[c:c + 1, :]
                sc = (cb * jnp.exp(jnp.where(causal, seg, NEG))).astype(bf16)
                y = jnp.dot(sc, xdt_pair, preferred_element_type=f32)
                s_in = state_ref[h]
                y += jnp.dot((cg * eac[:, c:c + 1]).astype(bf16), s_in.astype(bf16), preferred_element_type=f32)
                st = jnp.dot((bg_t * w_t[c:c + 1, :]).astype(bf16), xdt_pair, preferred_element_type=f32)
                state_ref[h] = s_in * chunk_decay[:, c:c + 1] + st
                outs.append(y)
            y_pair = jnp.where(first_half, outs[0], outs[1])
            sl = slice(i * LANE, (i + 1) * LANE)
            y_pair = y_pair + xs[:, sl] * dvec_ref[:, sl]
            z = z_ref[:, sl]
            ys.append(y_pair * (z * jax.nn.sigmoid(z)))

    per_group = len(ys) // SSM_GROUPS
    gw = SSM_INNER // SSM_GROUPS
    for g in range(SSM_GROUPS):
        tiles = ys[g * per_group:(g + 1) * per_group]
        ms = sum(jnp.sum(t * t, axis=-1, keepdims=True) for t in tiles) / gw
        scale = lax.rsqrt(ms + EPS)
        for k, t in enumerate(tiles):
            sl = slice(g * gw + k * LANE, g * gw + (k + 1) * LANE)
            o_ref[:, sl] = (t * scale * ng_ref[:, sl]).astype(o_ref.dtype)


GDN_STEP_ROWS = 128
GDN_QK_W = GDN_HEADS * GDN_DK
GDN_CONV_CH = GDN_HEADS * (2 * GDN_DK + GDN_DV)


def _split2(x):
    hi = x.astype(bf16)
    return hi, (x - hi.astype(f32)).astype(bf16)


def _dot_split(a_parts, b_parts):
    (ah, al), (bh, bl) = a_parts, b_parts
    return (jnp.dot(ah, bh, preferred_element_type=f32) + jnp.dot(ah, bl, preferred_element_type=f32)
            + jnp.dot(al, bh, preferred_element_type=f32))


def _unit_lower_solve_many(a_list, rhs_list):
    n = a_list[0].shape[0]
    ps = [_split2(-a) for a in a_list]
    xs = [rhs + _dot_split(p, _split2(rhs)) for p, rhs in zip(ps, rhs_list)]
    k = 2
    while k < n:
        ps = [_split2(_dot_split(p, p)) for p in ps]
        xs = [x + _dot_split(p, _split2(x)) for p, x in zip(ps, xs)]
        k *= 2
    return xs


def _gdn_kernel(qkv_ref, z_ref, sm_ref, cw_ref, dtb_ref, alog_ref, ng_ref, tri_ref, o_ref, tail_ref, buf_ref,
                state_ref):
    C = GDN_CHUNK

    qkv = _causal_conv_silu(qkv_ref, tail_ref, buf_ref, cw_ref, 0.0)
    sm = sm_ref[...]
    beta = jax.nn.sigmoid(sm)
    g_log = -jnp.exp(alog_ref[...]) * _softplus(sm + dtb_ref[...])
    gc = _exact_dot_left(tri_ref[...], g_log)
    gc_t = gc.T
    egc = jnp.exp(gc)

    row = lax.broadcasted_iota(jnp.int32, (C, C), 0)
    col = lax.broadcasted_iota(jnp.int32, (C, C), 1)
    causal = row >= col
    strict = row > col

    units = [(ch, h) for ch in range(GDN_STEP_ROWS // C) for h in range(GDN_HEADS)]
    a_list, rhs_list, pre = [], [], []
    for ch, h in units:
        rs = slice(ch * C, (ch + 1) * C)
        last = slice((ch + 1) * C - 1, (ch + 1) * C)
        q = qkv[rs, h * GDN_DK:(h + 1) * GDN_DK]
        k = qkv[rs, GDN_QK_W + h * GDN_DK:GDN_QK_W + (h + 1) * GDN_DK]
        v = qkv[rs, 2 * GDN_QK_W + h * GDN_DV:2 * GDN_QK_W + (h + 1) * GDN_DV]
        q = q * lax.rsqrt(jnp.sum(q * q, axis=-1, keepdims=True) + EPS) * (GDN_DK ** -0.5)
        k = k * lax.rsqrt(jnp.sum(k * k, axis=-1, keepdims=True) + EPS)
        ca = SMALL_A_LANE + h
        b = beta[rs, SMALL_B_LANE + h:SMALL_B_LANE + h + 1]
        g_col = gc[rs, ca:ca + 1]
        g_last = gc[last, ca:ca + 1]
        e_col = egc[rs, ca:ca + 1]
        decay = jnp.exp(jnp.where(causal, g_col - gc_t[ca:ca + 1, rs], NEG))
        kb = k * b
        k16 = k.astype(bf16)
        a_list.append(jnp.where(strict, _dot_nt(kb.astype(bf16), k16) * decay, 0.0))
        rhs_list.append(jnp.concatenate([v * b, kb * e_col], axis=1))
        attn = (_dot_nt(q.astype(bf16), k16) * decay).astype(bf16)
        q_dec = (q * e_col).astype(bf16)
        k_dec_t = (k * jnp.exp(g_last - g_col)).T.astype(bf16)
        pre.append((attn, q_dec, k_dec_t, jnp.exp(g_last)))

    sols = _unit_lower_solve_many(a_list, rhs_list)

    for (ch, h), sol, (attn, q_dec, k_dec_t, g_end) in zip(units, sols, pre):
        rs = slice(ch * C, (ch + 1) * C)
        u, w = sol[:, :GDN_DV], sol[:, GDN_DV:]
        s_in = state_ref[h]
        s16 = s_in.astype(bf16)
        v_new = u - jnp.dot(w.astype(bf16), s16, preferred_element_type=f32)
        v16 = v_new.astype(bf16)
        o = (jnp.dot(q_dec, s16, preferred_element_type=f32) + jnp.dot(attn, v16, preferred_element_type=f32))
        state_ref[h] = s_in * g_end + jnp.dot(k_dec_t, v16, preferred_element_type=f32)

        o = o * lax.rsqrt(jnp.mean(o * o, axis=-1, keepdims=True) + EPS) * ng_ref[...]
        z = z_ref[rs, h * GDN_DV:(h + 1) * GDN_DV]
        o_ref[rs, h * GDN_DV:(h + 1) * GDN_DV] = (o * (z * jax.nn.sigmoid(z))).astype(o_ref.dtype)


REC_ROWS = GDN_STEP_ROWS
assert REC_ROWS == SSM_CHUNK


def _rec_mixers_kernel(qkv_ref, za_ref, sm_ref, gcw_ref, gdtb_ref, galog_ref, gng_ref, gtri_ref,
                       xbc_ref, zc_ref, scw_ref, scb_ref, sdtb_ref, salog_ref, dvec_ref, sng_ref, stri_ref, exp_ref,
                       oa_ref, oc_ref, gtail_ref, gbuf_ref, gstate_ref, stail_ref, sbuf_ref, sstate_ref):
    @pl.when(pl.program_id(1) == 0)
    def _():
        for ref in (gtail_ref, gstate_ref, stail_ref, sstate_ref):
            ref[...] = jnp.zeros(ref.shape, f32)

    _gdn_kernel(qkv_ref, za_ref, sm_ref, gcw_ref, gdtb_ref, galog_ref, gng_ref, gtri_ref, oa_ref,
                gtail_ref, gbuf_ref, gstate_ref)
    _ssd_kernel(xbc_ref, zc_ref, sm_ref, scw_ref, scb_ref, sdtb_ref, salog_ref, dvec_ref, sng_ref, stri_ref,
                exp_ref, oc_ref, stail_ref, sbuf_ref, sstate_ref)


def _rec_mixers(proj, offs, gdn_conv, gdn_a_log, gdn_dt_bias, gdn_norm, ssm_conv_w, ssm_conv_b, ssm_dt_bias,
                ssm_a_log, ssm_d, ssm_norm, bsz, seq):
    m = bsz * seq
    R_ = REC_ROWS
    n_s = seq // R_
    idx = np.arange(R_)
    gtri = ((idx[:, None] >= idx[None, :]) & (idx[:, None] // GDN_CHUNK == idx[None, :] // GDN_CHUNK))
    stri = np.tril(np.ones((R_, R_), np.float32))
    expand = np.zeros((LANE, SSM_INNER), np.float32)
    for h in range(SSM_HEADS):
        expand[SMALL_DT_LANE + h, h * SSM_HEAD_DIM:(h + 1) * SSM_HEAD_DIM] = 1.0
    gz = GDN_HEADS * GDN_DV
    tok = lambda w, col: pl.BlockSpec((R_, w), lambda b, c: (b * n_s + c, col // w))
    const = lambda shape: pl.BlockSpec(shape, lambda b, c: (0, 0))
    return pl.pallas_call(
        _rec_mixers_kernel,
        grid=(bsz, n_s),
        in_specs=[
            tok(GDN_CONV_CH, offs["a_q"]), tok(gz, offs["a_z"]), tok(LANE, offs["small"]),
            const((CONV_K, GDN_CONV_CH)), const((1, LANE)), const((1, LANE)), const((1, GDN_DV)), const((R_, R_)),
            tok(SSM_CONV_CH, offs["c_xbc"]), tok(SSM_INNER, offs["c_z"]),
            const((CONV_K, SSM_CONV_CH)), const((1, SSM_CONV_CH)), const((1, LANE)), const((1, LANE)),
            const((1, SSM_INNER)), const((1, SSM_INNER)), const((R_, R_)), const((LANE, SSM_INNER)),
        ],
        out_specs=[pl.BlockSpec((R_, gz), lambda b, c: (b * n_s + c, 0)),
                   pl.BlockSpec((R_, SSM_INNER), lambda b, c: (b * n_s + c, 0))],
        out_shape=[jax.ShapeDtypeStruct((m, gz), bf16), jax.ShapeDtypeStruct((m, SSM_INNER), bf16)],
        scratch_shapes=[
            pltpu.VMEM((CONV_HALO, GDN_CONV_CH), f32),
            pltpu.VMEM((CONV_HALO + R_, GDN_CONV_CH), f32),
            pltpu.VMEM((GDN_HEADS, GDN_DK, GDN_DV), f32),
            pltpu.VMEM((CONV_HALO, SSM_CONV_CH), f32),
            pltpu.VMEM((CONV_HALO + R_, SSM_CONV_CH), f32),
            pltpu.VMEM((SSM_HEADS, SSM_STATE, LANE), f32),
        ],
        compiler_params=pltpu.CompilerParams(dimension_semantics=("arbitrary", "arbitrary"),
                                             vmem_limit_bytes=V7X_VMEM_LIMIT_BYTES),
        name="recurrent_mixers",
    )(proj, proj, proj, gdn_conv, _lane_vec(gdn_dt_bias, SMALL_A_LANE), _lane_vec(gdn_a_log, SMALL_A_LANE),
      gdn_norm.reshape(1, -1), jnp.asarray(gtri.astype(np.float32), bf16),
      proj, proj, ssm_conv_w, ssm_conv_b.reshape(1, -1), _lane_vec(ssm_dt_bias, SMALL_DT_LANE),
      _lane_vec(ssm_a_log, SMALL_DT_LANE), jnp.repeat(ssm_d, SSM_HEAD_DIM).reshape(1, -1),
      ssm_norm.reshape(1, -1), jnp.asarray(stri, bf16), jnp.asarray(expand, bf16))


_IN_NAMES = ("a_q", "a_k", "a_v", "a_z", "a_a", "a_b", "b_q", "b_kc", "b_vc", "b_ks", "b_vs", "b_kw", "b_vw",
             "b_g", "c_z", "c_xbc", "c_dt", "m_gate")


def _in_sizes(d_model):
    return (GDN_HEADS * GDN_DK, GDN_HEADS * GDN_DK, GDN_HEADS * GDN_DV, GDN_HEADS * GDN_DV, GDN_HEADS, GDN_HEADS,
            NSA_HEADS * NSA_DK, NSA_KV_GROUPS * NSA_DK, NSA_KV_GROUPS * NSA_DV, NSA_KV_GROUPS * NSA_DK,
            NSA_KV_GROUPS * NSA_DV, NSA_KV_GROUPS * NSA_DK, NSA_KV_GROUPS * NSA_DV, 3 * NSA_HEADS,
            SSM_INNER, SSM_CONV_CH, SSM_HEADS, 3 * d_model)


_BIG_ORDER = ("m_gate", "a_q", "a_k", "a_v", "c_xbc", "c_z", "a_z", "b_q", "b_kc", "b_vc", "b_ks", "b_vs",
              "b_kw", "b_vw")
_SMALL_ORDER = ("a_a", "a_b", "c_dt")


def _in_layout(d_model, tn=1024):
    sizes = dict(zip(_IN_NAMES, _in_sizes(d_model)))
    src_off = dict(zip(_IN_NAMES, np.cumsum((0,) + _in_sizes(d_model))[:-1].tolist()))
    cols, offs = [], {}

    def pad_to_lane():
        cols.extend([-1] * (_round_up(len(cols), LANE) - len(cols)))

    for name in _BIG_ORDER:
        offs[name] = len(cols)
        cols.extend(range(src_off[name], src_off[name] + sizes[name]))
    offs["small"] = len(cols)
    for name in _SMALL_ORDER:
        offs[name] = len(cols)
        cols.extend(range(src_off[name], src_off[name] + sizes[name]))
    pad_to_lane()
    offs["b_g"] = len(cols)
    per_group = 3 * NSA_HG
    for g in range(NSA_KV_GROUPS):
        cols.extend(range(src_off["b_g"] + g * per_group, src_off["b_g"] + (g + 1) * per_group))
        pad_to_lane()
    n_pad = _round_up(len(cols), tn)
    cols.extend([-1] * (n_pad - len(cols)))
    return np.asarray(cols, np.int32), offs, sizes, n_pad


def _col_runs(cols):
    runs, start = [], 0
    for i in range(1, len(cols) + 1):
        same_run = i < len(cols) and ((cols[i] < 0 and cols[i - 1] < 0)
                                      or (cols[i - 1] >= 0 and cols[i] == cols[i - 1] + 1))
        if not same_run:
            runs.append((start, None if cols[start] < 0 else int(cols[start]), i - start))
            start = i
    return runs


def _cast_cols_kernel(runs_per_out, w_ref, *o_refs):
    for runs, o_ref in zip(runs_per_out, o_refs):
        for dst, src, size in runs:
            if src is None:
                o_ref[:, dst:dst + size] = jnp.zeros((o_ref.shape[0], size), o_ref.dtype)
            else:
                o_ref[:, dst:dst + size] = w_ref[0, :, src:src + size].astype(o_ref.dtype)


def _cast_cols(w_stack, layer, runs_per_out, widths, *, tr=256):
    _, rows, n_src = w_stack.shape
    return pl.pallas_call(
        functools.partial(_cast_cols_kernel, runs_per_out),
        grid=(rows // tr,),
        in_specs=[pl.BlockSpec((1, tr, n_src), lambda i: (layer, i, 0))],
        out_specs=[pl.BlockSpec((tr, w), lambda i: (i, 0)) for w in widths],
        out_shape=[jax.ShapeDtypeStruct((rows, w), bf16) for w in widths],
        compiler_params=pltpu.CompilerParams(dimension_semantics=("parallel",),
                                             vmem_limit_bytes=V7X_VMEM_LIMIT_BYTES),
        name="weight_cast_cols",
    )(w_stack)


def _cast_rows_kernel(rows_src, w_ref, o_ref):
    tr = o_ref.shape[0]
    row = lax.broadcasted_iota(jnp.int32, o_ref.shape, 0) + pl.program_id(0) * tr
    o_ref[...] = jnp.where(row < rows_src, w_ref[0], 0.0).astype(o_ref.dtype)


def _cast_rows(w_stack, layer, rows_out, *, tr=512):
    _, rows_src, n = w_stack.shape
    return pl.pallas_call(
        functools.partial(_cast_rows_kernel, rows_src),
        grid=(rows_out // tr,),
        in_specs=[pl.BlockSpec((1, tr, n), lambda i: (layer, i, 0))],
        out_specs=pl.BlockSpec((tr, n), lambda i: (i, 0)),
        out_shape=jax.ShapeDtypeStruct((rows_out, n), bf16),
        compiler_params=pltpu.CompilerParams(dimension_semantics=("parallel",),
                                             vmem_limit_bytes=V7X_VMEM_LIMIT_BYTES),
        name="weight_cast_rows",
    )(w_stack)


def _prep_w_in(w_in, layer, cols):
    return _cast_cols(w_in, layer, [_col_runs(cols)], [len(cols)])[0]


def _prep_ffn(w_up, w_down, layer, tf=512):
    ff = w_down.shape[1]
    ffp = _round_up(ff, tf)
    half = lambda src: [(0, src, ff)] + ([(ff, None, ffp - ff)] if ffp > ff else [])
    wa, wb = _cast_cols(w_up, layer, [half(0), half(ff)], [ffp, ffp])
    return wa, wb, _cast_rows(w_down, layer, ffp)


def kernel(x, rel_table, g_ffn1, w_up1, w_down1, g_mix, w_in, gdn_conv, gdn_a_log, gdn_dt_bias, gdn_norm,
           nsa_q_norm, nsa_k_norm, nsa_pe_k, nsa_pe_v, nsa_w_ck, nsa_w_cv, ssm_conv_w, ssm_conv_b,
           ssm_dt_bias, ssm_a_log, ssm_d, ssm_norm, p_a, p_b, p_c, w_o, g_ffn2, w_up2, w_down2):
    bsz, seq, d = x.shape
    depth = w_in.shape[0]
    m = bsz * seq
    cols, offs, _, _ = _in_layout(d)
    x2 = x.reshape(m, d)
    nsa_consts = _nsa_constants(seq)

    for l in range(depth):
        wa, wb, wd = _prep_ffn(w_up1, w_down1, l)
        x2 = _ffn(x2, g_ffn1[l].reshape(1, d), wa, wb, wd)

        proj = _norm_matmul(x2, g_mix[l].reshape(1, d), _prep_w_in(w_in, l, cols))
        y_a, y_c = _rec_mixers(proj, offs, gdn_conv[l], gdn_a_log[l], gdn_dt_bias[l], gdn_norm[l], ssm_conv_w[l],
                               ssm_conv_b[l], ssm_dt_bias[l], ssm_a_log[l], ssm_d[l], ssm_norm[l], bsz, seq)
        y_b = _nsa(proj, offs, nsa_q_norm[l], nsa_k_norm[l], nsa_pe_k[l], nsa_pe_v[l], nsa_w_ck[l], nsa_w_cv[l],
                   rel_table, nsa_consts, bsz, seq)
        x2 = _merge(x2, y_a, y_b, y_c, proj, offs["m_gate"],
                    p_a[l].astype(bf16), p_b[l].astype(bf16), p_c[l].astype(bf16), w_o[l].astype(bf16))

        wa, wb, wd = _prep_ffn(w_up2, w_down2, l)
        x2 = _ffn(x2, g_ffn2[l].reshape(1, d), wa, wb, wd)
    return x2.reshape(bsz, seq, d)
```

```python
import functools
import math

import jax
import jax.numpy as jnp
import numpy as np
from jax import lax
from jax.experimental import pallas as pl
from jax.experimental.pallas import tpu as pltpu

EPS = 1e-6
CONV_K = 4

GDN_HEADS = 4
GDN_DK = 128
GDN_DV = 128
GDN_CHUNK = 64

NSA_HEADS = 4
NSA_KV_GROUPS = 2
NSA_DK = 128
NSA_DV = 128
CMP_BLOCK = 32
CMP_STRIDE = 16
SEL_BLOCK = 64
N_SELECT = 16
WINDOW = 512
Q_BLOCK = 128
SEL_OVERLAP_WEIGHTS = (1.0, 2.0, 2.0, 2.0, 1.0)
FORCE_SCORE = 1e4
NEG = -1e30

SSM_HEADS = 16
SSM_HEAD_DIM = 64
SSM_GROUPS = 2
SSM_STATE = 128
SSM_CHUNK = 128
SSM_INNER = SSM_HEADS * SSM_HEAD_DIM
SSM_CONV_CH = SSM_INNER + 2 * SSM_GROUPS * SSM_STATE

REL_BUCKETS = 32
REL_MAX_EXACT = 16
REL_MAX_DIST = 1024

V7X_VMEM_LIMIT_BYTES = 56 * 1024 * 1024
LANE = 128

bf16 = jnp.bfloat16
f32 = jnp.float32


def _round_up(n, m):
    return (n + m - 1) // m * m


def _ffn_kernel(x_ref, g_ref, wa_ref, wb_ref, wd_ref, o_ref, h_ref):
    j = pl.program_id(1)

    @pl.when(j == 0)
    def _():
        x = x_ref[...]
        h = x * lax.rsqrt(jnp.mean(x * x, axis=-1, keepdims=True) + EPS) * g_ref[...]
        h_ref[...] = h.astype(bf16)
        o_ref[...] = x

    h = h_ref[...]
    a = jnp.dot(h, wa_ref[...], preferred_element_type=f32)
    b = jnp.dot(h, wb_ref[...], preferred_element_type=f32)
    act = (0.5 * a * jax.nn.sigmoid(a) * b).astype(bf16)
    o_ref[...] += jnp.dot(act, wd_ref[...], preferred_element_type=f32)


def _ffn(x2, g, wa, wb, wd, *, tm=1024, tf=512):
    m, d = x2.shape
    ffp = wa.shape[1]
    return pl.pallas_call(
        _ffn_kernel,
        grid=(m // tm, ffp // tf),
        in_specs=[
            pl.BlockSpec((tm, d), lambda i, j: (i, 0)),
            pl.BlockSpec((1, d), lambda i, j: (0, 0)),
            pl.BlockSpec((d, tf), lambda i, j: (0, j)),
            pl.BlockSpec((d, tf), lambda i, j: (0, j)),
            pl.BlockSpec((tf, d), lambda i, j: (j, 0)),
        ],
        out_specs=pl.BlockSpec((tm, d), lambda i, j: (i, 0)),
        out_shape=jax.ShapeDtypeStruct((m, d), f32),
        scratch_shapes=[pltpu.VMEM((tm, d), bf16)],
        compiler_params=pltpu.CompilerParams(
            dimension_semantics=("parallel", "arbitrary"),
            vmem_limit_bytes=V7X_VMEM_LIMIT_BYTES),
        name="ffn_swiglu",
    )(x2, g, wa, wb, wd)


def _norm_matmul_kernel(x_ref, g_ref, w_ref, o_ref, h_ref):
    j = pl.program_id(1)

    @pl.when(j == 0)
    def _():
        x = x_ref[...]
        h = x * lax.rsqrt(jnp.mean(x * x, axis=-1, keepdims=True) + EPS) * g_ref[...]
        h_ref[...] = h.astype(bf16)

    o_ref[...] = jnp.dot(h_ref[...], w_ref[...], preferred_element_type=f32)


def _norm_matmul(x2, g, w, *, tm=1024, tn=1024):
    m, d = x2.shape
    n = w.shape[1]
    return pl.pallas_call(
        _norm_matmul_kernel,
        grid=(m // tm, n // tn),
        in_specs=[
            pl.BlockSpec((tm, d), lambda i, j: (i, 0)),
            pl.BlockSpec((1, d), lambda i, j: (0, 0)),
            pl.BlockSpec((d, tn), lambda i, j: (0, j)),
        ],
        out_specs=pl.BlockSpec((tm, tn), lambda i, j: (i, j)),
        out_shape=jax.ShapeDtypeStruct((m, n), f32),
        scratch_shapes=[pltpu.VMEM((tm, d), bf16)],
        compiler_params=pltpu.CompilerParams(
            dimension_semantics=("parallel", "arbitrary"),
            vmem_limit_bytes=V7X_VMEM_LIMIT_BYTES),
        name="norm_in_proj",
    )(x2, g, w)


def _merge_kernel(x_ref, ya_ref, yb_ref, yc_ref, ga_ref, gb_ref, gc_ref, pa_ref, pb_ref, pc_ref, wo_ref, o_ref):
    ma = jnp.dot(ya_ref[...].astype(bf16), pa_ref[...], preferred_element_type=f32)
    mb = jnp.dot(yb_ref[...].astype(bf16), pb_ref[...], preferred_element_type=f32)
    mc = jnp.dot(yc_ref[...].astype(bf16), pc_ref[...], preferred_element_type=f32)
    merged = (jax.nn.sigmoid(ga_ref[...]) * ma + jax.nn.sigmoid(gb_ref[...]) * mb
              + jax.nn.sigmoid(gc_ref[...]) * mc)
    o_ref[...] = x_ref[...] + jnp.dot(merged.astype(bf16), wo_ref[...], preferred_element_type=f32)


def _merge(x2, ya, yb, yc, gates, gate_col0, pa, pb, pc, wo, *, tm=256):
    m, d = x2.shape
    gb0 = gate_col0 // d
    const = dict(pipeline_mode=pl.Buffered(1))
    return pl.pallas_call(
        _merge_kernel,
        grid=(m // tm,),
        in_specs=[
            pl.BlockSpec((tm, d), lambda i: (i, 0)),
            pl.BlockSpec((tm, ya.shape[1]), lambda i: (i, 0)),
            pl.BlockSpec((tm, yb.shape[1]), lambda i: (i, 0)),
            pl.BlockSpec((tm, yc.shape[1]), lambda i: (i, 0)),
            pl.BlockSpec((tm, d), lambda i: (i, gb0)),
            pl.BlockSpec((tm, d), lambda i: (i, gb0 + 1)),
            pl.BlockSpec((tm, d), lambda i: (i, gb0 + 2)),
            pl.BlockSpec(pa.shape, lambda i: (0, 0), **const),
            pl.BlockSpec(pb.shape, lambda i: (0, 0), **const),
            pl.BlockSpec(pc.shape, lambda i: (0, 0), **const),
            pl.BlockSpec(wo.shape, lambda i: (0, 0), **const),
        ],
        out_specs=pl.BlockSpec((tm, d), lambda i: (i, 0)),
        out_shape=jax.ShapeDtypeStruct((m, d), f32),
        compiler_params=pltpu.CompilerParams(
            dimension_semantics=("parallel",),
            vmem_limit_bytes=V7X_VMEM_LIMIT_BYTES),
        name="merge_out_proj",
    )(x2, ya, yb, yc, gates, gates, gates, pa, pb, pc, wo)


NSA_HG = NSA_HEADS // NSA_KV_GROUPS
SEL_FAR_TILE = 512
SEL_FAR_SUB = 2
NEAR_TILES = 8
STRIP_W = NEAR_TILES * Q_BLOCK
STRIP_D0 = (NEAR_TILES - 1) * Q_BLOCK
WIN_TILES = WINDOW // Q_BLOCK + 1
CMP_PER_Q = Q_BLOCK // CMP_STRIDE
SEL_PER_Q = Q_BLOCK // SEL_BLOCK
MASK_PEN = -1e30


def _dot_nt(a, b):
    return lax.dot_general(a, b, (((1,), (1,)), ((), ())), preferred_element_type=f32)


def _lane_rms(x, gain):
    return x * lax.rsqrt(jnp.mean(x * x, axis=-1, keepdims=True) + EPS) * gain


def _nsa_prep_kernel(seq, q_ref, ks_ref, kw_ref, qg_ref, kg_ref, qo_ref, kao_ref, vso_ref, kwo_ref):
    tm = q_ref.shape[0]
    qg = qg_ref[...] * (NSA_DK ** -0.5)
    kg = kg_ref[...]
    for h in range(NSA_HEADS):
        sl = slice(h * NSA_DK, (h + 1) * NSA_DK)
        qo_ref[:, sl] = _lane_rms(q_ref[:, sl], qg).astype(bf16)
    kw_ = NSA_KV_GROUPS * NSA_DK
    tok = (lax.broadcasted_iota(jnp.int32, (tm, LANE), 0) + pl.program_id(0) * tm) % seq
    onehot = jnp.where(lax.broadcasted_iota(jnp.int32, (tm, LANE), 1) == tok // SEL_BLOCK, 1.0, 0.0).astype(bf16)
    for g in range(NSA_KV_GROUPS):
        sl = slice(g * NSA_DK, (g + 1) * NSA_DK)
        kao_ref[:, 2 * g * NSA_DK:(2 * g + 1) * NSA_DK] = _lane_rms(ks_ref[:, sl], kg).astype(bf16)
        kao_ref[:, (2 * g + 1) * NSA_DK:(2 * g + 2) * NSA_DK] = onehot
        kwo_ref[:, sl] = _lane_rms(kw_ref[:, sl], kg).astype(bf16)
    vso_ref[...] = ks_ref[:, kw_:].astype(bf16)
    kwo_ref[:, kw_:] = kw_ref[:, kw_:].astype(bf16)


def _nsa_prep(proj, offs, q_gain, k_gain, seq, *, tm=512):
    m = proj.shape[0]
    w = NSA_HEADS * NSA_DK
    assert seq // SEL_BLOCK <= LANE and NSA_DK == LANE, "selection blocks must fit one 128-lane one-hot"
    blk = lambda name: pl.BlockSpec((tm, w), lambda i, c=offs[name] // w: (i, c))
    widths = (w, 2 * NSA_KV_GROUPS * NSA_DK, NSA_KV_GROUPS * NSA_DV, w)
    return pl.pallas_call(
        functools.partial(_nsa_prep_kernel, seq),
        grid=(m // tm,),
        in_specs=[blk("b_q"), blk("b_ks"), blk("b_kw"),
                  pl.BlockSpec((1, NSA_DK), lambda i: (0, 0)), pl.BlockSpec((1, NSA_DK), lambda i: (0, 0))],
        out_specs=[pl.BlockSpec((tm, wd), lambda i: (i, 0)) for wd in widths],
        out_shape=[jax.ShapeDtypeStruct((m, wd), bf16) for wd in widths],
        compiler_params=pltpu.CompilerParams(dimension_semantics=("parallel",),
                                             vmem_limit_bytes=V7X_VMEM_LIMIT_BYTES),
        name="nsa_prep",
    )(proj, proj, proj, q_gain, k_gain)


def _nsa_compress_kernel(x_ref, w_ref, pe_ref, kg_ref, o_ref, hi_ref):
    j = pl.program_id(1)
    nc = o_ref.shape[2]
    lo = jnp.zeros((nc, NSA_DK), f32)
    hi = jnp.zeros((nc, NSA_DK), f32)
    for l in range(CMP_STRIDE):
        rows = x_ref[pl.ds(l, nc, stride=CMP_STRIDE), :]
        lo += jnp.dot((rows + pe_ref[0, l:l + 1, :]).astype(bf16), w_ref[0, l], preferred_element_type=f32)
        hi += jnp.dot((rows + pe_ref[0, CMP_STRIDE + l:CMP_STRIDE + l + 1, :]).astype(bf16),
                      w_ref[0, CMP_STRIDE + l], preferred_element_type=f32)
    hi_ref[pl.ds(0, nc), :] = hi
    hi_ref[pl.ds(nc, 8), :] = jnp.zeros((8, NSA_DK), f32)
    c = lo + hi_ref[pl.ds(1, nc), :]
    normed = _lane_rms(c, kg_ref[...])
    o_ref[0, 0] = jnp.where(j < NSA_KV_GROUPS, normed, c).astype(bf16)


def _nsa_compress(proj, offs, w_stack, pe_stack, k_gain, bsz, seq):
    nc = seq // CMP_STRIDE
    c0 = offs["b_kc"] // NSA_DK
    return pl.pallas_call(
        _nsa_compress_kernel,
        grid=(bsz, 2 * NSA_KV_GROUPS),
        in_specs=[
            pl.BlockSpec((seq, NSA_DK), lambda b, j: (b, c0 + j)),
            pl.BlockSpec((1, CMP_BLOCK, NSA_DK, NSA_DK), lambda b, j: (j // NSA_KV_GROUPS, 0, 0, 0)),
            pl.BlockSpec((1, CMP_BLOCK, NSA_DK), lambda b, j: (j // NSA_KV_GROUPS, 0, 0)),
            pl.BlockSpec((1, NSA_DK), lambda b, j: (0, 0)),
        ],
        out_specs=pl.BlockSpec((1, 1, nc, NSA_DK), lambda b, j: (b, j, 0, 0)),
        out_shape=jax.ShapeDtypeStruct((bsz, 2 * NSA_KV_GROUPS, nc, NSA_DK), bf16),
        scratch_shapes=[pltpu.VMEM((nc + 8, NSA_DK), f32)],
        compiler_params=pltpu.CompilerParams(dimension_semantics=("parallel", "parallel"),
                                             vmem_limit_bytes=V7X_VMEM_LIMIT_BYTES),
        name="nsa_compress",
    )(proj, w_stack, pe_stack, k_gain)


def _split3(x):
    hi = x.astype(bf16)
    r1 = x - hi.astype(f32)
    mid = r1.astype(bf16)
    lo = (r1 - mid.astype(f32)).astype(bf16)
    return hi, mid, lo


def _nsa_attn_kernel(tbl_ref, q_ref, kc_ref, vc_ref, ka_ref, vs_ref, kw_ref, vw_ref, mt_ref, bs_ref,
                     bc_ref, gl_ref, o_ref, strip_ref, sc_ref, score_ref, stage_ref, far_ref):
    g = pl.program_id(1)
    qi = pl.program_id(2)
    nc = kc_ref.shape[2]
    qb = Q_BLOCK

    @pl.when(qi == 0)
    def _():
        def lookup(bucket):
            v0 = jnp.zeros(bucket.shape, f32)
            v1 = jnp.zeros(bucket.shape, f32)
            for k in range(REL_BUCKETS):
                eq = bucket == k
                v0 = jnp.where(eq, tbl_ref[NSA_HG * g, k], v0)
                v1 = jnp.where(eq, tbl_ref[NSA_HG * g + 1, k], v1)
            return v0, v1

        for c in range(NEAR_TILES):
            sl = slice(c * qb, (c + 1) * qb)
            v0, v1 = lookup(bs_ref[:, sl])
            strip_ref[0, :, sl] = v0
            strip_ref[1, :, sl] = v1
        v0, v1 = lookup(bc_ref[...])
        for hh, v in enumerate((v0, v1)):
            d = v - tbl_ref[NSA_HG * g + hh, REL_BUCKETS - 1]
            hi = d.astype(bf16)
            sc_ref[hh, 0] = hi
            sc_ref[hh, 1] = (d - hi.astype(f32)).astype(bf16)

    far_bias = [tbl_ref[NSA_HG * g + hh, REL_BUCKETS - 1] for hh in range(NSA_HG)]
    q = q_ref[...]
    q2 = jnp.concatenate([q[:, :NSA_DK], q[:, NSA_DK:]], axis=0)

    r2 = lax.broadcasted_iota(jnp.int32, (NSA_HG * qb, qb), 0) % qb
    c2 = lax.broadcasted_iota(jnp.int32, (NSA_HG * qb, qb), 1)

    def strip_bias(w):
        sl = slice(w * qb, (w + 1) * qb)
        return jnp.concatenate([strip_ref[0, :, sl], strip_ref[1, :, sl]], axis=0)

    def key_rows(w):
        kt = qi - (NEAR_TILES - 1) + w
        return kt, pl.ds(pl.multiple_of(jnp.maximum(kt, 0) * qb, qb), qb)

    m_w = jnp.full((NSA_HG * qb, 1), NEG, f32)
    for w in range(NEAR_TILES - WIN_TILES, NEAR_TILES):
        kt, rows = key_rows(w)
        s = _dot_nt(q2, kw_ref[rows, :]) + strip_bias(w)
        if w == NEAR_TILES - WIN_TILES:
            valid = (c2 > r2) & (kt >= 0)
        elif w == NEAR_TILES - 1:
            valid = c2 <= r2
        else:
            valid = jnp.broadcast_to(kt >= 0, c2.shape)
        s = jnp.where(valid, s, NEG)
        stage_ref[w] = s
        m_w = jnp.maximum(m_w, jnp.max(s, axis=-1, keepdims=True))
    den = jnp.zeros((NSA_HG * qb, 1), f32)
    acc_w = jnp.zeros((NSA_HG * qb, NSA_DV), f32)
    for w in range(NEAR_TILES - WIN_TILES, NEAR_TILES):
        _, rows = key_rows(w)
        e = jnp.exp(stage_ref[w] - m_w)
        den = den + jnp.sum(e, axis=-1, keepdims=True)
        acc_w = acc_w + jnp.dot(e.astype(bf16), vw_ref[rows, :], preferred_element_type=f32)
    o_win = acc_w / den

    r_i = lax.broadcasted_iota(jnp.int32, (qb, nc), 0)
    n_i = lax.broadcasted_iota(jnp.int32, (qb, nc), 1)
    ok_c = CMP_STRIDE * n_i <= qb * qi + r_i - (CMP_BLOCK - 1)
    shift = jnp.where(r_i + n_i == CMP_PER_Q * qi + CMP_PER_Q - 1, 1.0, 0.0).astype(bf16)
    kc = kc_ref[0, 0]
    vc = vc_ref[0, 0]
    psum = jnp.zeros((qb, nc), f32)
    o_cmp = []
    for hh in range(NSA_HG):
        s = _dot_nt(q2[hh * qb:(hh + 1) * qb], kc)
        bias = (far_bias[hh] + jnp.dot(sc_ref[hh, 0], shift, preferred_element_type=f32)
                + jnp.dot(sc_ref[hh, 1], shift, preferred_element_type=f32))
        s = jnp.where(ok_c, s + bias, NEG)
        e = jnp.exp(s - jnp.max(s, axis=-1, keepdims=True))
        p = jnp.where(ok_c, e / jnp.sum(e, axis=-1, keepdims=True), 0.0)
        o_cmp.append(jnp.dot(p.astype(bf16), vc, preferred_element_type=f32))
        psum += p

    mt = mt_ref[...]
    imp_t = sum(_dot_nt(mt, part) for part in _split3(psum))
    nblk = imp_t.shape[0]
    j_i = lax.broadcasted_iota(jnp.int32, (nblk, qb), 0)
    l_i = lax.broadcasted_iota(jnp.int32, (nblk, qb), 1)
    cur = SEL_PER_Q * qi + l_i // SEL_BLOCK
    ok_s = j_i <= cur
    forced = ok_s & ((j_i == 0) | (j_i >= cur - 1))
    score = jnp.where(forced, FORCE_SCORE, jnp.where(ok_s, imp_t, -FORCE_SCORE))
    key = pltpu.bitcast(score, jnp.int32)
    score_ref[...] = key
    half = nblk // 2

    def rank_half(lo):
        key_h = key[lo:lo + half]
        key_h_m1 = key_h - 1
        j_h = j_i[lo:lo + half]

        def rank_body(i, cnt):
            for u in range(SEL_PER_Q):
                jp = SEL_PER_Q * i + u
                row = score_ref[pl.ds(jp, 1), :]
                cnt = cnt + jnp.where(row > jnp.where(j_h > jp, key_h_m1, key_h), 1, 0)
            return cnt

        return lax.fori_loop(0, qi + 1, rank_body, jnp.zeros((half, qb), jnp.int32))

    cnt_hi = lax.cond(SEL_PER_Q * qi + SEL_PER_Q > half, lambda: rank_half(half),
                      lambda: jnp.full((half, qb), N_SELECT, jnp.int32))
    cnt = jnp.concatenate([rank_half(0), cnt_hi], axis=0)
    pen = jnp.where(cnt < N_SELECT, 0.0, MASK_PEN).T.astype(bf16)
    q_aug = jnp.concatenate([q2, jnp.concatenate([pen, pen], axis=0)], axis=1)

    row2 = lax.broadcasted_iota(jnp.int32, (NSA_HG * qb, 1), 0)
    far_col = jnp.where(row2 < qb, far_bias[0], far_bias[1])
    n_far_keys = qb * jnp.maximum(qi - (NEAR_TILES - 1), 0)
    far_step = SEL_FAR_SUB * SEL_FAR_TILE
    c_far = lax.broadcasted_iota(jnp.int32, (NSA_HG * qb, SEL_FAR_TILE), 1)

    n_far_steps = (n_far_keys + far_step - 1) // far_step

    def far_rows(kt, sub):
        k0 = kt * far_step + sub * SEL_FAR_TILE
        return k0, pl.ds(pl.multiple_of(k0, SEL_FAR_TILE), SEL_FAR_TILE)

    def far_scores(kt):
        m_raw = jnp.full((NSA_HG * qb, 1), MASK_PEN, f32)
        for sub in range(SEL_FAR_SUB):
            k0, rows = far_rows(kt, sub)
            s = _dot_nt(q_aug, ka_ref[rows, :]) + far_col
            s = jnp.where(c_far < n_far_keys - k0, s, MASK_PEN)
            far_ref[kt % 2, :, sub * SEL_FAR_TILE:(sub + 1) * SEL_FAR_TILE] = s
            m_raw = jnp.maximum(m_raw, jnp.max(s, axis=-1, keepdims=True))
        return m_raw

    m_raw0 = far_scores(0)

    m_near = jnp.full((NSA_HG * qb, 1), NEG, f32)
    for w in range(NEAR_TILES):
        kt, rows = key_rows(w)
        s = _dot_nt(q_aug, ka_ref[rows, :]) + strip_bias(w)
        if w == NEAR_TILES - 1:
            s = jnp.where(c2 <= r2, s, NEG)
        else:
            s = jnp.where(kt >= 0, s, NEG)
        stage_ref[w] = s
        m_near = jnp.maximum(m_near, jnp.max(s, axis=-1, keepdims=True))

    def far_body(kt, carry):
        m_old, l_old, acc_old, m_raw = carry
        ss = [far_ref[kt % 2, :, sub * SEL_FAR_TILE:(sub + 1) * SEL_FAR_TILE] for sub in range(SEL_FAR_SUB)]
        m_new = jnp.maximum(m_old, m_raw)
        ps = [jnp.exp(s - m_new) for s in ss]
        alpha = jnp.exp(m_old - m_new)
        l_new = alpha * l_old + sum(jnp.sum(p, axis=-1, keepdims=True) for p in ps)
        acc_new = alpha * acc_old + sum(
            jnp.dot(p.astype(bf16), vs_ref[far_rows(kt, sub)[1], :], preferred_element_type=f32)
            for sub, p in enumerate(ps))
        m_raw_next = far_scores(jnp.minimum(kt + 1, n_far_steps - 1))
        return m_new, l_new, acc_new, m_raw_next

    m_far, l_far, acc_far, _ = lax.fori_loop(
        0, n_far_steps, far_body,
        (jnp.full((NSA_HG * qb, 1), MASK_PEN, f32), jnp.zeros((NSA_HG * qb, 1), f32),
         jnp.zeros((NSA_HG * qb, NSA_DV), f32), m_raw0))

    m_sel = jnp.maximum(m_far, m_near)
    alpha = jnp.exp(m_far - m_sel)
    l_sel = alpha * l_far
    acc_sel = alpha * acc_far
    for w in range(NEAR_TILES):
        _, rows = key_rows(w)
        e = jnp.exp(stage_ref[w] - m_sel)
        l_sel = l_sel + jnp.sum(e, axis=-1, keepdims=True)
        acc_sel = acc_sel + jnp.dot(e.astype(bf16), vs_ref[rows, :], preferred_element_type=f32)
    o_sel = acc_sel / l_sel

    gate = jax.nn.sigmoid(gl_ref[...])
    for hh in range(NSA_HG):
        rs = slice(hh * qb, (hh + 1) * qb)
        o_ref[:, hh * NSA_DV:(hh + 1) * NSA_DV] = (gate[:, 3 * hh:3 * hh + 1] * o_cmp[hh]
                                                   + gate[:, 3 * hh + 1:3 * hh + 2] * o_sel[rs]
                                                   + gate[:, 3 * hh + 2:3 * hh + 3] * o_win[rs])


def _rel_bucket(dist):
    n = jnp.maximum(dist, 0)
    nf = jnp.maximum(n, 1).astype(f32)
    large = REL_MAX_EXACT + (jnp.log(nf / REL_MAX_EXACT) / math.log(REL_MAX_DIST / REL_MAX_EXACT)
                             * (REL_BUCKETS - REL_MAX_EXACT)).astype(jnp.int32)
    large = jnp.minimum(large, REL_BUCKETS - 1)
    return jnp.where(n < REL_MAX_EXACT, n, large)


def _nsa_constants(seq):
    nc = seq // CMP_STRIDE
    r = jnp.arange(Q_BLOCK)[:, None]
    bs = _rel_bucket(STRIP_D0 + r - jnp.arange(STRIP_W)[None, :])
    bc = _rel_bucket(r + CMP_STRIDE * jnp.arange(LANE)[None, :] - (CMP_STRIDE * (CMP_PER_Q - 1) + CMP_BLOCK - 1))
    ratio = SEL_BLOCK // CMP_STRIDE
    mt = np.zeros((LANE, nc), np.float32)
    for j in range(LANE):
        for i, wt in enumerate(SEL_OVERLAP_WEIGHTS):
            mcol = ratio * j - 1 + i
            if 0 <= mcol < nc:
                mt[j, mcol] = wt
    return bs.astype(jnp.int32), bc.astype(jnp.int32), jnp.asarray(mt, bf16)


def _nsa_attention(qn, cmp_kv, kaug, vsel, kwvw, proj, offs, tbl, consts, bsz, seq):
    bs, bc, mt = consts
    assert seq % (SEL_FAR_SUB * SEL_FAR_TILE) == 0, "far steps must tile the sequence"
    m = bsz * seq
    n_qb = seq // Q_BLOCK
    nc = seq // CMP_STRIDE
    g_ = NSA_KV_GROUPS
    kv = lambda width, col0: pl.BlockSpec((seq, width), lambda b, g, i: (b, col0 + g))
    cmp_spec = lambda j0: pl.BlockSpec((1, 1, nc, NSA_DK), lambda b, g, i: (b, j0 + g, 0, 0))
    const2 = lambda shape: pl.BlockSpec(shape, lambda b, g, i: (0, 0))
    gcol = offs["b_g"] // LANE
    return pl.pallas_call(
        _nsa_attn_kernel,
        grid=(bsz, g_, n_qb),
        in_specs=[
            pl.BlockSpec(memory_space=pltpu.SMEM),
            pl.BlockSpec((Q_BLOCK, NSA_HG * NSA_DK), lambda b, g, i: (b * n_qb + i, g)),
            cmp_spec(0), cmp_spec(g_),
            kv(NSA_DK + LANE, 0), kv(NSA_DV, 0), kv(NSA_DK, 0), kv(NSA_DV, g_),
            const2((LANE, nc)), const2((Q_BLOCK, STRIP_W)), const2((Q_BLOCK, LANE)),
            pl.BlockSpec((Q_BLOCK, LANE), lambda b, g, i: (b * n_qb + i, gcol + g)),
        ],
        out_specs=pl.BlockSpec((Q_BLOCK, NSA_HG * NSA_DV), lambda b, g, i: (b * n_qb + i, g)),
        out_shape=jax.ShapeDtypeStruct((m, NSA_HEADS * NSA_DV), f32),
        scratch_shapes=[
            pltpu.VMEM((NSA_HG, Q_BLOCK, STRIP_W), f32),
            pltpu.VMEM((NSA_HG, 2, Q_BLOCK, LANE), bf16),
            pltpu.VMEM((LANE, Q_BLOCK), jnp.int32),
            pltpu.VMEM((NEAR_TILES, NSA_HG * Q_BLOCK, Q_BLOCK), f32),
            pltpu.VMEM((2, NSA_HG * Q_BLOCK, SEL_FAR_SUB * SEL_FAR_TILE), f32),
        ],
        compiler_params=pltpu.CompilerParams(dimension_semantics=("arbitrary", "arbitrary", "arbitrary"),
                                             vmem_limit_bytes=V7X_VMEM_LIMIT_BYTES),
        name="nsa_attention",
    )(tbl, qn, cmp_kv, cmp_kv, kaug, vsel, kwvw, kwvw, mt, bs, bc, proj)


def _nsa(proj, offs, q_norm, k_norm, pe_k, pe_v, w_ck, w_cv, rel_table, consts, bsz, seq):
    q_gain = q_norm.reshape(1, NSA_DK)
    k_gain = k_norm.reshape(1, NSA_DK)
    qn, kaug, vsel, kwvw = _nsa_prep(proj, offs, q_gain, k_gain, seq)
    cmp_kv = _nsa_compress(proj, offs, jnp.stack([w_ck, w_cv]).astype(bf16), jnp.stack([pe_k, pe_v]), k_gain,
                           bsz, seq)
    return _nsa_attention(qn, cmp_kv, kaug, vsel, kwvw, proj, offs, rel_table.T, consts, bsz, seq)


SMALL_A_LANE = 0
SMALL_B_LANE = GDN_HEADS
SMALL_DT_LANE = 2 * GDN_HEADS
CONV_HALO = 8


def _softplus(x):
    return jnp.maximum(x, 0.0) + jnp.log1p(jnp.exp(-jnp.abs(x)))


def _exact_dot(a_f32, b_01):
    return sum(jnp.dot(part, b_01, preferred_element_type=f32) for part in _split3(a_f32))


def _exact_dot_left(b_01, a_f32):
    return sum(jnp.dot(b_01, part, preferred_element_type=f32) for part in _split3(a_f32))


def _lane_vec(values, lane0):
    return jnp.zeros((1, LANE), f32).at[0, lane0:lane0 + values.shape[0]].set(values.astype(f32))


def _causal_conv_silu(x_ref, tail_ref, buf_ref, w_ref, bias):
    rows = x_ref.shape[0]
    buf_ref[pl.ds(0, CONV_HALO), :] = tail_ref[...]
    buf_ref[pl.ds(CONV_HALO, rows), :] = x_ref[...]
    tail_ref[...] = x_ref[pl.ds(rows - CONV_HALO, CONV_HALO), :]
    acc = bias
    for k in range(CONV_K):
        acc = acc + w_ref[k:k + 1, :] * buf_ref[pl.ds(CONV_HALO - (CONV_K - 1) + k, rows), :]
    return acc * jax.nn.sigmoid(acc)


SSM_PAIR = LANE // SSM_HEAD_DIM
SSM_HEADS_PER_GROUP = SSM_HEADS // SSM_GROUPS


def _ssd_kernel(xbc_ref, z_ref, sm_ref, cw_ref, cb_ref, dtb_ref, alog_ref, dvec_ref, ng_ref, tri_ref, exp_ref,
                o_ref, tail_ref, buf_ref, state_ref):
    L = SSM_CHUNK
    N = SSM_STATE

    xbc = _causal_conv_silu(xbc_ref, tail_ref, buf_ref, cw_ref, cb_ref[...])
    xs = xbc[:, :SSM_INNER]
    dt = _softplus(sm_ref[...] + dtb_ref[...])
    a = dt * (-jnp.exp(alog_ref[...]))
    tri = tri_ref[...]
    ac = _exact_dot_left(tri, a)
    ac_t = ac.T
    eac = jnp.exp(ac)
    w_t = jnp.exp(ac_t[:, L - 1:L] - ac_t)
    chunk_decay = jnp.exp(ac[L - 1:L, :])
    xdt = xs * _exact_dot(dt, exp_ref[...])

    row = lax.broadcasted_iota(jnp.int32, (L, L), 0)
    col = lax.broadcasted_iota(jnp.int32, (L, L), 1)
    causal = row >= col
    lane = lax.broadcasted_iota(jnp.int32, (L, LANE), 1)
    first_half = lane < SSM_HEAD_DIM

    ys = []
    for g in range(SSM_GROUPS):
        bg = xbc[:, SSM_INNER + g * N:SSM_INNER + (g + 1) * N]
        cg = xbc[:, SSM_INNER + SSM_GROUPS * N + g * N:SSM_INNER + SSM_GROUPS * N + (g + 1) * N]
        cb = _dot_nt(cg.astype(bf16), bg.astype(bf16))
        bg_t = bg.T
        for i in range(g * SSM_HEADS_PER_GROUP // SSM_PAIR, (g + 1) * SSM_HEADS_PER_GROUP // SSM_PAIR):
            xdt_pair = xdt[:, i * LANE:(i + 1) * LANE].astype(bf16)
            outs = []
            for hh in range(SSM_PAIR):
                h = SSM_PAIR * i + hh
                c = SMALL_DT_LANE + h
                seg = ac[:, c:c + 1] - ac_t[c:c + 1, :]
                sc = (cb * jnp.exp(jnp.where(causal, seg, NEG))).astype(bf16)
                y = jnp.dot(sc, xdt_pair, preferred_element_type=f32)
                s_in = state_ref[h]
                y += jnp.dot((cg * eac[:, c:c + 1]).astype(bf16), s_in.astype(bf16), preferred_element_type=f32)
                st = jnp.dot((bg_t * w_t[c:c + 1, :]).astype(bf16), xdt_pair, preferred_element_type=f32)
                state_ref[h] = s_in * chunk_decay[:, c:c + 1] + st
                outs.append(y)
            y_pair = jnp.where(first_half, outs[0], outs[1])
            sl = slice(i * LANE, (i + 1) * LANE)
            y_pair = y_pair + xs[:, sl] * dvec_ref[:, sl]
            z = z_ref[:, sl]
            ys.append(y_pair * (z * jax.nn.sigmoid(z)))

    per_group = len(ys) // SSM_GROUPS
    gw = SSM_INNER // SSM_GROUPS
    for g in range(SSM_GROUPS):
        tiles = ys[g * per_group:(g + 1) * per_group]
        ms = sum(jnp.sum(t * t, axis=-1, keepdims=True) for t in tiles) / gw
        scale = lax.rsqrt(ms + EPS)
        for k, t in enumerate(tiles):
            sl = slice(g * gw + k * LANE, g * gw + (k + 1) * LANE)
            o_ref[:, sl] = (t * scale * ng_ref[:, sl]).astype(o_ref.dtype)


GDN_STEP_ROWS = 128
GDN_QK_W = GDN_HEADS * GDN_DK
GDN_CONV_CH = GDN_HEADS * (2 * GDN_DK + GDN_DV)


def _split2(x):
    hi = x.astype(bf16)
    return hi, (x - hi.astype(f32)).astype(bf16)


def _dot_split(a_parts, b_parts):
    (ah, al), (bh, bl) = a_parts, b_parts
    return (jnp.dot(ah, bh, preferred_element_type=f32) + jnp.dot(ah, bl, preferred_element_type=f32)
            + jnp.dot(al, bh, preferred_element_type=f32))


def _unit_lower_solve_many(a_list, rhs_list):
    n = a_list[0].shape[0]
    ps = [_split2(-a) for a in a_list]
    xs = [rhs + _dot_split(p, _split2(rhs)) for p, rhs in zip(ps, rhs_list)]
    k = 2
    while k < n:
        ps = [_split2(_dot_split(p, p)) for p in ps]
        xs = [x + _dot_split(p, _split2(x)) for p, x in zip(ps, xs)]
        k *= 2
    return xs


def _gdn_kernel(qkv_ref, z_ref, sm_ref, cw_ref, dtb_ref, alog_ref, ng_ref, tri_ref, o_ref, tail_ref, buf_ref,
                state_ref):
    C = GDN_CHUNK

    qkv = _causal_conv_silu(qkv_ref, tail_ref, buf_ref, cw_ref, 0.0)
    sm = sm_ref[...]
    beta = jax.nn.sigmoid(sm)
    g_log = -jnp.exp(alog_ref[...]) * _softplus(sm + dtb_ref[...])
    gc = _exact_dot_left(tri_ref[...], g_log)
    gc_t = gc.T
    egc = jnp.exp(gc)

    row = lax.broadcasted_iota(jnp.int32, (C, C), 0)
    col = lax.broadcasted_iota(jnp.int32, (C, C), 1)
    causal = row >= col
    strict = row > col

    units = [(ch, h) for ch in range(GDN_STEP_ROWS // C) for h in range(GDN_HEADS)]
    a_list, rhs_list, pre = [], [], []
    for ch, h in units:
        rs = slice(ch * C, (ch + 1) * C)
        last = slice((ch + 1) * C - 1, (ch + 1) * C)
        q = qkv[rs, h * GDN_DK:(h + 1) * GDN_DK]
        k = qkv[rs, GDN_QK_W + h * GDN_DK:GDN_QK_W + (h + 1) * GDN_DK]
        v = qkv[rs, 2 * GDN_QK_W + h * GDN_DV:2 * GDN_QK_W + (h + 1) * GDN_DV]
        q = q * lax.rsqrt(jnp.sum(q * q, axis=-1, keepdims=True) + EPS) * (GDN_DK ** -0.5)
        k = k * lax.rsqrt(jnp.sum(k * k, axis=-1, keepdims=True) + EPS)
        ca = SMALL_A_LANE + h
        b = beta[rs, SMALL_B_LANE + h:SMALL_B_LANE + h + 1]
        g_col = gc[rs, ca:ca + 1]
        g_last = gc[last, ca:ca + 1]
        e_col = egc[rs, ca:ca + 1]
        decay = jnp.exp(jnp.where(causal, g_col - gc_t[ca:ca + 1, rs], NEG))
        kb = k * b
        k16 = k.astype(bf16)
        a_list.append(jnp.where(strict, _dot_nt(kb.astype(bf16), k16) * decay, 0.0))
        rhs_list.append(jnp.concatenate([v * b, kb * e_col], axis=1))
        attn = (_dot_nt(q.astype(bf16), k16) * decay).astype(bf16)
        q_dec = (q * e_col).astype(bf16)
        k_dec_t = (k * jnp.exp(g_last - g_col)).T.astype(bf16)
        pre.append((attn, q_dec, k_dec_t, jnp.exp(g_last)))

    sols = _unit_lower_solve_many(a_list, rhs_list)

    for (ch, h), sol, (attn, q_dec, k_dec_t, g_end) in zip(units, sols, pre):
        rs = slice(ch * C, (ch + 1) * C)
        u, w = sol[:, :GDN_DV], sol[:, GDN_DV:]
        s_in = state_ref[h]
        s16 = s_in.astype(bf16)
        v_new = u - jnp.dot(w.astype(bf16), s16, preferred_element_type=f32)
        v16 = v_new.astype(bf16)
        o = (jnp.dot(q_dec, s16, preferred_element_type=f32) + jnp.dot(attn, v16, preferred_element_type=f32))
        state_ref[h] = s_in * g_end + jnp.dot(k_dec_t, v16, preferred_element_type=f32)

        o = o * lax.rsqrt(jnp.mean(o * o, axis=-1, keepdims=True) + EPS) * ng_ref[...]
        z = z_ref[rs, h * GDN_DV:(h + 1) * GDN_DV]
        o_ref[rs, h * GDN_DV:(h + 1) * GDN_DV] = (o * (z * jax.nn.sigmoid(z))).astype(o_ref.dtype)


REC_ROWS = GDN_STEP_ROWS
assert REC_ROWS == SSM_CHUNK


def _rec_mixers_kernel(qkv_ref, za_ref, sm_ref, gcw_ref, gdtb_ref, galog_ref, gng_ref, gtri_ref,
                       xbc_ref, zc_ref, scw_ref, scb_ref, sdtb_ref, salog_ref, dvec_ref, sng_ref, stri_ref, exp_ref,
                       oa_ref, oc_ref, gtail_ref, gbuf_ref, gstate_ref, stail_ref, sbuf_ref, sstate_ref):
    @pl.when(pl.program_id(1) == 0)
    def _():
        for ref in (gtail_ref, gstate_ref, stail_ref, sstate_ref):
            ref[...] = jnp.zeros(ref.shape, f32)

    _gdn_kernel(qkv_ref, za_ref, sm_ref, gcw_ref, gdtb_ref, galog_ref, gng_ref, gtri_ref, oa_ref,
                gtail_ref, gbuf_ref, gstate_ref)
    _ssd_kernel(xbc_ref, zc_ref, sm_ref, scw_ref, scb_ref, sdtb_ref, salog_ref, dvec_ref, sng_ref, stri_ref,
                exp_ref, oc_ref, stail_ref, sbuf_ref, sstate_ref)


def _rec_mixers(proj, offs, gdn_conv, gdn_a_log, gdn_dt_bias, gdn_norm, ssm_conv_w, ssm_conv_b, ssm_dt_bias,
                ssm_a_log, ssm_d, ssm_norm, bsz, seq):
    m = bsz * seq
    R_ = REC_ROWS
    n_s = seq // R_
    idx = np.arange(R_)
    gtri = ((idx[:, None] >= idx[None, :]) & (idx[:, None] // GDN_CHUNK == idx[None, :] // GDN_CHUNK))
    stri = np.tril(np.ones((R_, R_), np.float32))
    expand = np.zeros((LANE, SSM_INNER), np.float32)
    for h in range(SSM_HEADS):
        expand[SMALL_DT_LANE + h, h * SSM_HEAD_DIM:(h + 1) * SSM_HEAD_DIM] = 1.0
    gz = GDN_HEADS * GDN_DV
    tok = lambda w, col: pl.BlockSpec((R_, w), lambda b, c: (b * n_s + c, col // w))
    const = lambda shape: pl.BlockSpec(shape, lambda b, c: (0, 0))
    return pl.pallas_call(
        _rec_mixers_kernel,
        grid=(bsz, n_s),
        in_specs=[
            tok(GDN_CONV_CH, offs["a_q"]), tok(gz, offs["a_z"]), tok(LANE, offs["small"]),
            const((CONV_K, GDN_CONV_CH)), const((1, LANE)), const((1, LANE)), const((1, GDN_DV)), const((R_, R_)),
            tok(SSM_CONV_CH, offs["c_xbc"]), tok(SSM_INNER, offs["c_z"]),
            const((CONV_K, SSM_CONV_CH)), const((1, SSM_CONV_CH)), const((1, LANE)), const((1, LANE)),
            const((1, SSM_INNER)), const((1, SSM_INNER)), const((R_, R_)), const((LANE, SSM_INNER)),
        ],
        out_specs=[pl.BlockSpec((R_, gz), lambda b, c: (b * n_s + c, 0)),
                   pl.BlockSpec((R_, SSM_INNER), lambda b, c: (b * n_s + c, 0))],
        out_shape=[jax.ShapeDtypeStruct((m, gz), bf16), jax.ShapeDtypeStruct((m, SSM_INNER), bf16)],
        scratch_shapes=[
            pltpu.VMEM((CONV_HALO, GDN_CONV_CH), f32),
            pltpu.VMEM((CONV_HALO + R_, GDN_CONV_CH), f32),
            pltpu.VMEM((GDN_HEADS, GDN_DK, GDN_DV), f32),
            pltpu.VMEM((CONV_HALO, SSM_CONV_CH), f32),
            pltpu.VMEM((CONV_HALO + R_, SSM_CONV_CH), f32),
            pltpu.VMEM((SSM_HEADS, SSM_STATE, LANE), f32),
        ],
        compiler_params=pltpu.CompilerParams(dimension_semantics=("arbitrary", "arbitrary"),
                                             vmem_limit_bytes=V7X_VMEM_LIMIT_BYTES),
        name="recurrent_mixers",
    )(proj, proj, proj, gdn_conv, _lane_vec(gdn_dt_bias, SMALL_A_LANE), _lane_vec(gdn_a_log, SMALL_A_LANE),
      gdn_norm.reshape(1, -1), jnp.asarray(gtri.astype(np.float32), bf16),
      proj, proj, ssm_conv_w, ssm_conv_b.reshape(1, -1), _lane_vec(ssm_dt_bias, SMALL_DT_LANE),
      _lane_vec(ssm_a_log, SMALL_DT_LANE), jnp.repeat(ssm_d, SSM_HEAD_DIM).reshape(1, -1),
      ssm_norm.reshape(1, -1), jnp.asarray(stri, bf16), jnp.asarray(expand, bf16))


_IN_NAMES = ("a_q", "a_k", "a_v", "a_z", "a_a", "a_b", "b_q", "b_kc", "b_vc", "b_ks", "b_vs", "b_kw", "b_vw",
             "b_g", "c_z", "c_xbc", "c_dt", "m_gate")


def _in_sizes(d_model):
    return (GDN_HEADS * GDN_DK, GDN_HEADS * GDN_DK, GDN_HEADS * GDN_DV, GDN_HEADS * GDN_DV, GDN_HEADS, GDN_HEADS,
            NSA_HEADS * NSA_DK, NSA_KV_GROUPS * NSA_DK, NSA_KV_GROUPS * NSA_DV, NSA_KV_GROUPS * NSA_DK,
            NSA_KV_GROUPS * NSA_DV, NSA_KV_GROUPS * NSA_DK, NSA_KV_GROUPS * NSA_DV, 3 * NSA_HEADS,
            SSM_INNER, SSM_CONV_CH, SSM_HEADS, 3 * d_model)


_BIG_ORDER = ("m_gate", "a_q", "a_k", "a_v", "c_xbc", "c_z", "a_z", "b_q", "b_kc", "b_vc", "b_ks", "b_vs",
              "b_kw", "b_vw")
_SMALL_ORDER = ("a_a", "a_b", "c_dt")


def _in_layout(d_model, tn=1024):
    sizes = dict(zip(_IN_NAMES, _in_sizes(d_model)))
    src_off = dict(zip(_IN_NAMES, np.cumsum((0,) + _in_sizes(d_model))[:-1].tolist()))
    cols, offs = [], {}

    def pad_to_lane():
        cols.extend([-1] * (_round_up(len(cols), LANE) - len(cols)))

    for name in _BIG_ORDER:
        offs[name] = len(cols)
        cols.extend(range(src_off[name], src_off[name] + sizes[name]))
    offs["small"] = len(cols)
    for name in _SMALL_ORDER:
        offs[name] = len(cols)
        cols.extend(range(src_off[name], src_off[name] + sizes[name]))
    pad_to_lane()
    offs["b_g"] = len(cols)
    per_group = 3 * NSA_HG
    for g in range(NSA_KV_GROUPS):
        cols.extend(range(src_off["b_g"] + g * per_group, src_off["b_g"] + (g + 1) * per_group))
        pad_to_lane()
    n_pad = _round_up(len(cols), tn)
    cols.extend([-1] * (n_pad - len(cols)))
    return np.asarray(cols, np.int32), offs, sizes, n_pad


def _col_runs(cols):
    runs, start = [], 0
    for i in range(1, len(cols) + 1):
        same_run = i < len(cols) and ((cols[i] < 0 and cols[i - 1] < 0)
                                      or (cols[i - 1] >= 0 and cols[i] == cols[i - 1] + 1))
        if not same_run:
            runs.append((start, None if cols[start] < 0 else int(cols[start]), i - start))
            start = i
    return runs


def _cast_cols_kernel(runs_per_out, w_ref, *o_refs):
    for runs, o_ref in zip(runs_per_out, o_refs):
        for dst, src, size in runs:
            if src is None:
                o_ref[:, dst:dst + size] = jnp.zeros((o_ref.shape[0], size), o_ref.dtype)
            else:
                o_ref[:, dst:dst + size] = w_ref[0, :, src:src + size].astype(o_ref.dtype)


def _cast_cols(w_stack, layer, runs_per_out, widths, *, tr=256):
    _, rows, n_src = w_stack.shape
    return pl.pallas_call(
        functools.partial(_cast_cols_kernel, runs_per_out),
        grid=(rows // tr,),
        in_specs=[pl.BlockSpec((1, tr, n_src), lambda i: (layer, i, 0))],
        out_specs=[pl.BlockSpec((tr, w), lambda i: (i, 0)) for w in widths],
        out_shape=[jax.ShapeDtypeStruct((rows, w), bf16) for w in widths],
        compiler_params=pltpu.CompilerParams(dimension_semantics=("parallel",),
                                             vmem_limit_bytes=V7X_VMEM_LIMIT_BYTES),
        name="weight_cast_cols",
    )(w_stack)


def _cast_rows_kernel(rows_src, w_ref, o_ref):
    tr = o_ref.shape[0]
    row = lax.broadcasted_iota(jnp.int32, o_ref.shape, 0) + pl.program_id(0) * tr
    o_ref[...] = jnp.where(row < rows_src, w_ref[0], 0.0).astype(o_ref.dtype)


def _cast_rows(w_stack, layer, rows_out, *, tr=512):
    _, rows_src, n = w_stack.shape
    return pl.pallas_call(
        functools.partial(_cast_rows_kernel, rows_src),
        grid=(rows_out // tr,),
        in_specs=[pl.BlockSpec((1, tr, n), lambda i: (layer, i, 0))],
        out_specs=pl.BlockSpec((tr, n), lambda i: (i, 0)),
        out_shape=jax.ShapeDtypeStruct((rows_out, n), bf16),
        compiler_params=pltpu.CompilerParams(dimension_semantics=("parallel",),
                                             vmem_limit_bytes=V7X_VMEM_LIMIT_BYTES),
        name="weight_cast_rows",
    )(w_stack)


def _prep_w_in(w_in, layer, cols):
    return _cast_cols(w_in, layer, [_col_runs(cols)], [len(cols)])[0]


def _prep_ffn(w_up, w_down, layer, tf=512):
    ff = w_down.shape[1]
    ffp = _round_up(ff, tf)
    half = lambda src: [(0, src, ff)] + ([(ff, None, ffp - ff)] if ffp > ff else [])
    wa, wb = _cast_cols(w_up, layer, [half(0), half(ff)], [ffp, ffp])
    return wa, wb, _cast_rows(w_down, layer, ffp)


def kernel(x, rel_table, g_ffn1, w_up1, w_down1, g_mix, w_in, gdn_conv, gdn_a_log, gdn_dt_bias, gdn_norm,
           nsa_q_norm, nsa_k_norm, nsa_pe_k, nsa_pe_v, nsa_w_ck, nsa_w_cv, ssm_conv_w, ssm_conv_b,
           ssm_dt_bias, ssm_a_log, ssm_d, ssm_norm, p_a, p_b, p_c, w_o, g_ffn2, w_up2, w_down2):
    bsz, seq, d = x.shape
    depth = w_in.shape[0]
    m = bsz * seq
    cols, offs, _, _ = _in_layout(d)
    x2 = x.reshape(m, d)
    nsa_consts = _nsa_constants(seq)

    for l in range(depth):
        wa, wb, wd = _prep_ffn(w_up1, w_down1, l)
        x2 = _ffn(x2, g_ffn1[l].reshape(1, d), wa, wb, wd)

        proj = _norm_matmul(x2, g_mix[l].reshape(1, d), _prep_w_in(w_in, l, cols))
        y_a, y_c = _rec_mixers(proj, offs, gdn_conv[l], gdn_a_log[l], gdn_dt_bias[l], gdn_norm[l], ssm_conv_w[l],
                               ssm_conv_b[l], ssm_dt_bias[l], ssm_a_log[l], ssm_d[l], ssm_norm[l], bsz, seq)
        y_b = _nsa(proj, offs, nsa_q_norm[l], nsa_k_norm[l], nsa_pe_k[l], nsa_pe_v[l], nsa_w_ck[l], nsa_w_cv[l],
                   rel_table, nsa_consts, bsz, seq)
        x2 = _merge(x2, y_a, y_b, y_c, proj, offs["m_gate"],
                    p_a[l].astype(bf16), p_b[l].astype(bf16), p_c[l].astype(bf16), w_o[l].astype(bf16))

        wa, wb, wd = _prep_ffn(w_up2, w_down2, l)
        x2 = _ffn(x2, g_ffn2[l].reshape(1, d), wa, wb, wd)
    return x2.reshape(bsz, seq, d)
```

```python
import functools
import math

import jax
import jax.numpy as jnp
import numpy as np
from jax import lax
from jax.experimental import pallas as pl
from jax.experimental.pallas import tpu as pltpu

EPS = 1e-6
CONV_K = 4

GDN_HEADS = 4
GDN_DK = 128
GDN_DV = 128
GDN_CHUNK = 64

NSA_HEADS = 4
NSA_KV_GROUPS = 2
NSA_DK = 128
NSA_DV = 128
CMP_BLOCK = 32
CMP_STRIDE = 16
SEL_BLOCK = 64
N_SELECT = 16
WINDOW = 512
Q_BLOCK = 128
SEL_OVERLAP_WEIGHTS = (1.0, 2.0, 2.0, 2.0, 1.0)
FORCE_SCORE = 1e4
NEG = -1e30

SSM_HEADS = 16
SSM_HEAD_DIM = 64
SSM_GROUPS = 2
SSM_STATE = 128
SSM_CHUNK = 128
SSM_INNER = SSM_HEADS * SSM_HEAD_DIM
SSM_CONV_CH = SSM_INNER + 2 * SSM_GROUPS * SSM_STATE

REL_BUCKETS = 32
REL_MAX_EXACT = 16
REL_MAX_DIST = 1024

V7X_VMEM_LIMIT_BYTES = 56 * 1024 * 1024
LANE = 128

bf16 = jnp.bfloat16
f32 = jnp.float32


def _round_up(n, m):
    return (n + m - 1) // m * m


def _ffn_kernel(x_ref, g_ref, wa_ref, wb_ref, wd_ref, o_ref, h_ref):
    j = pl.program_id(1)

    @pl.when(j == 0)
    def _():
        x = x_ref[...]
        h = x * lax.rsqrt(jnp.mean(x * x, axis=-1, keepdims=True) + EPS) * g_ref[...]
        h_ref[...] = h.astype(bf16)
        o_ref[...] = x

    h = h_ref[...]
    a = jnp.dot(h, wa_ref[...], preferred_element_type=f32)
    b = jnp.dot(h, wb_ref[...], preferred_element_type=f32)
    act = (0.5 * a * jax.nn.sigmoid(a) * b).astype(bf16)
    o_ref[...] += jnp.dot(act, wd_ref[...], preferred_element_type=f32)


def _ffn(x2, g, wa, wb, wd, *, tm=1024, tf=512):
    m, d = x2.shape
    ffp = wa.shape[1]
    return pl.pallas_call(
        _ffn_kernel,
        grid=(m // tm, ffp // tf),
        in_specs=[
            pl.BlockSpec((tm, d), lambda i, j: (i, 0)),
            pl.BlockSpec((1, d), lambda i, j: (0, 0)),
            pl.BlockSpec((d, tf), lambda i, j: (0, j)),
            pl.BlockSpec((d, tf), lambda i, j: (0, j)),
            pl.BlockSpec((tf, d), lambda i, j: (j, 0)),
        ],
        out_specs=pl.BlockSpec((tm, d), lambda i, j: (i, 0)),
        out_shape=jax.ShapeDtypeStruct((m, d), f32),
        scratch_shapes=[pltpu.VMEM((tm, d), bf16)],
        compiler_params=pltpu.CompilerParams(
            dimension_semantics=("parallel", "arbitrary"),
            vmem_limit_bytes=V7X_VMEM_LIMIT_BYTES),
        name="ffn_swiglu",
    )(x2, g, wa, wb, wd)


def _norm_matmul_kernel(x_ref, g_ref, w_ref, o_ref, h_ref):
    j = pl.program_id(1)

    @pl.when(j == 0)
    def _():
        x = x_ref[...]
        h = x * lax.rsqrt(jnp.mean(x * x, axis=-1, keepdims=True) + EPS) * g_ref[...]
        h_ref[...] = h.astype(bf16)

    o_ref[...] = jnp.dot(h_ref[...], w_ref[...], preferred_element_type=f32)


def _norm_matmul(x2, g, w, *, tm=1024, tn=1024):
    m, d = x2.shape
    n = w.shape[1]
    return pl.pallas_call(
        _norm_matmul_kernel,
        grid=(m // tm, n // tn),
        in_specs=[
            pl.BlockSpec((tm, d), lambda i, j: (i, 0)),
            pl.BlockSpec((1, d), lambda i, j: (0, 0)),
            pl.BlockSpec((d, tn), lambda i, j: (0, j)),
        ],
        out_specs=pl.BlockSpec((tm, tn), lambda i, j: (i, j)),
        out_shape=jax.ShapeDtypeStruct((m, n), f32),
        scratch_shapes=[pltpu.VMEM((tm, d), bf16)],
        compiler_params=pltpu.CompilerParams(
            dimension_semantics=("parallel", "arbitrary"),
            vmem_limit_bytes=V7X_VMEM_LIMIT_BYTES),
        name="norm_in_proj",
    )(x2, g, w)


def _merge_kernel(x_ref, ya_ref, yb_ref, yc_ref, ga_ref, gb_ref, gc_ref, pa_ref, pb_ref, pc_ref, wo_ref, o_ref):
    ma = jnp.dot(ya_ref[...].astype(bf16), pa_ref[...], preferred_element_type=f32)
    mb = jnp.dot(yb_ref[...].astype(bf16), pb_ref[...], preferred_element_type=f32)
    mc = jnp.dot(yc_ref[...].astype(bf16), pc_ref[...], preferred_element_type=f32)
    merged = (jax.nn.sigmoid(ga_ref[...]) * ma + jax.nn.sigmoid(gb_ref[...]) * mb
              + jax.nn.sigmoid(gc_ref[...]) * mc)
    o_ref[...] = x_ref[...] + jnp.dot(merged.astype(bf16), wo_ref[...], preferred_element_type=f32)


def _merge(x2, ya, yb, yc, gates, gate_col0, pa, pb, pc, wo, *, tm=256):
    m, d = x2.shape
    gb0 = gate_col0 // d
    const = dict(pipeline_mode=pl.Buffered(1))
    return pl.pallas_call(
        _merge_kernel,
        grid=(m // tm,),
        in_specs=[
            pl.BlockSpec((tm, d), lambda i: (i, 0)),
            pl.BlockSpec((tm, ya.shape[1]), lambda i: (i, 0)),
            pl.BlockSpec((tm, yb.shape[1]), lambda i: (i, 0)),
            pl.BlockSpec((tm, yc.shape[1]), lambda i: (i, 0)),
            pl.BlockSpec((tm, d), lambda i: (i, gb0)),
            pl.BlockSpec((tm, d), lambda i: (i, gb0 + 1)),
            pl.BlockSpec((tm, d), lambda i: (i, gb0 + 2)),
            pl.BlockSpec(pa.shape, lambda i: (0, 0), **const),
            pl.BlockSpec(pb.shape, lambda i: (0, 0), **const),
            pl.BlockSpec(pc.shape, lambda i: (0, 0), **const),
            pl.BlockSpec(wo.shape, lambda i: (0, 0), **const),
        ],
        out_specs=pl.BlockSpec((tm, d), lambda i: (i, 0)),
        out_shape=jax.ShapeDtypeStruct((m, d), f32),
        compiler_params=pltpu.CompilerParams(
            dimension_semantics=("parallel",),
            vmem_limit_bytes=V7X_VMEM_LIMIT_BYTES),
        name="merge_out_proj",
    )(x2, ya, yb, yc, gates, gates, gates, pa, pb, pc, wo)


NSA_HG = NSA_HEADS // NSA_KV_GROUPS
SEL_FAR_TILE = 512
SEL_FAR_SUB = 2
NEAR_TILES = 8
STRIP_W = NEAR_TILES * Q_BLOCK
STRIP_D0 = (NEAR_TILES - 1) * Q_BLOCK
WIN_TILES = WINDOW // Q_BLOCK + 1
CMP_PER_Q = Q_BLOCK // CMP_STRIDE
SEL_PER_Q = Q_BLOCK // SEL_BLOCK
MASK_PEN = -1e30


def _dot_nt(a, b):
    return lax.dot_general(a, b, (((1,), (1,)), ((), ())), preferred_element_type=f32)


def _lane_rms(x, gain):
    return x * lax.rsqrt(jnp.mean(x * x, axis=-1, keepdims=True) + EPS) * gain


def _nsa_prep_kernel(seq, q_ref, ks_ref, kw_ref, qg_ref, kg_ref, qo_ref, kao_ref, vso_ref, kwo_ref):
    tm = q_ref.shape[0]
    qg = qg_ref[...] * (NSA_DK ** -0.5)
    kg = kg_ref[...]
    for h in range(NSA_HEADS):
        sl = slice(h * NSA_DK, (h + 1) * NSA_DK)
        qo_ref[:, sl] = _lane_rms(q_ref[:, sl], qg).astype(bf16)
    kw_ = NSA_KV_GROUPS * NSA_DK
    tok = (lax.broadcasted_iota(jnp.int32, (tm, LANE), 0) + pl.program_id(0) * tm) % seq
    onehot = jnp.where(lax.broadcasted_iota(jnp.int32, (tm, LANE), 1) == tok // SEL_BLOCK, 1.0, 0.0).astype(bf16)
    for g in range(NSA_KV_GROUPS):
        sl = slice(g * NSA_DK, (g + 1) * NSA_DK)
        kao_ref[:, 2 * g * NSA_DK:(2 * g + 1) * NSA_DK] = _lane_rms(ks_ref[:, sl], kg).astype(bf16)
        kao_ref[:, (2 * g + 1) * NSA_DK:(2 * g + 2) * NSA_DK] = onehot
        kwo_ref[:, sl] = _lane_rms(kw_ref[:, sl], kg).astype(bf16)
    vso_ref[...] = ks_ref[:, kw_:].astype(bf16)
    kwo_ref[:, kw_:] = kw_ref[:, kw_:].astype(bf16)


def _nsa_prep(proj, offs, q_gain, k_gain, seq, *, tm=512):
    m = proj.shape[0]
    w = NSA_HEADS * NSA_DK
    assert seq // SEL_BLOCK <= LANE and NSA_DK == LANE, "selection blocks must fit one 128-lane one-hot"
    blk = lambda name: pl.BlockSpec((tm, w), lambda i, c=offs[name] // w: (i, c))
    widths = (w, 2 * NSA_KV_GROUPS * NSA_DK, NSA_KV_GROUPS * NSA_DV, w)
    return pl.pallas_call(
        functools.partial(_nsa_prep_kernel, seq),
        grid=(m // tm,),
        in_specs=[blk("b_q"), blk("b_ks"), blk("b_kw"),
                  pl.BlockSpec((1, NSA_DK), lambda i: (0, 0)), pl.BlockSpec((1, NSA_DK), lambda i: (0, 0))],
        out_specs=[pl.BlockSpec((tm, wd), lambda i: (i, 0)) for wd in widths],
        out_shape=[jax.ShapeDtypeStruct((m, wd), bf16) for wd in widths],
        compiler_params=pltpu.CompilerParams(dimension_semantics=("parallel",),
                                             vmem_limit_bytes=V7X_VMEM_LIMIT_BYTES),
        name="nsa_prep",
    )(proj, proj, proj, q_gain, k_gain)


def _nsa_compress_kernel(x_ref, w_ref, pe_ref, kg_ref, o_ref, hi_ref):
    j = pl.program_id(1)
    nc = o_ref.shape[2]
    lo = jnp.zeros((nc, NSA_DK), f32)
    hi = jnp.zeros((nc, NSA_DK), f32)
    for l in range(CMP_STRIDE):
        rows = x_ref[pl.ds(l, nc, stride=CMP_STRIDE), :]
        lo += jnp.dot((rows + pe_ref[0, l:l + 1, :]).astype(bf16), w_ref[0, l], preferred_element_type=f32)
        hi += jnp.dot((rows + pe_ref[0, CMP_STRIDE + l:CMP_STRIDE + l + 1, :]).astype(bf16),
                      w_ref[0, CMP_STRIDE + l], preferred_element_type=f32)
    hi_ref[pl.ds(0, nc), :] = hi
    hi_ref[pl.ds(nc, 8), :] = jnp.zeros((8, NSA_DK), f32)
    c = lo + hi_ref[pl.ds(1, nc), :]
    normed = _lane_rms(c, kg_ref[...])
    o_ref[0, 0] = jnp.where(j < NSA_KV_GROUPS, normed, c).astype(bf16)


def _nsa_compress(proj, offs, w_stack, pe_stack, k_gain, bsz, seq):
    nc = seq // CMP_STRIDE
    c0 = offs["b_kc"] // NSA_DK
    return pl.pallas_call(
        _nsa_compress_kernel,
        grid=(bsz, 2 * NSA_KV_GROUPS),
        in_specs=[
            pl.BlockSpec((seq, NSA_DK), lambda b, j: (b, c0 + j)),
            pl.BlockSpec((1, CMP_BLOCK, NSA_DK, NSA_DK), lambda b, j: (j // NSA_KV_GROUPS, 0, 0, 0)),
            pl.BlockSpec((1, CMP_BLOCK, NSA_DK), lambda b, j: (j // NSA_KV_GROUPS, 0, 0)),
            pl.BlockSpec((1, NSA_DK), lambda b, j: (0, 0)),
        ],
        out_specs=pl.BlockSpec((1, 1, nc, NSA_DK), lambda b, j: (b, j, 0, 0)),
        out_shape=jax.ShapeDtypeStruct((bsz, 2 * NSA_KV_GROUPS, nc, NSA_DK), bf16),
        scratch_shapes=[pltpu.VMEM((nc + 8, NSA_DK), f32)],
        compiler_params=pltpu.CompilerParams(dimension_semantics=("parallel", "parallel"),
                                             vmem_limit_bytes=V7X_VMEM_LIMIT_BYTES),
        name="nsa_compress",
    )(proj, w_stack, pe_stack, k_gain)


def _split3(x):
    hi = x.astype(bf16)
    r1 = x - hi.astype(f32)
    mid = r1.astype(bf16)
    lo = (r1 - mid.astype(f32)).astype(bf16)
    return hi, mid, lo


def _nsa_attn_kernel(tbl_ref, q_ref, cmp_ref, ka_ref, vs_ref, kwv_ref, mt_ref, bs_ref, bc_ref, gl0_ref, gl1_ref,
                     o_ref, strip_ref, sc_ref, score_ref, stage_ref, far_ref):
    qi = pl.program_id(1)
    nc = cmp_ref.shape[2]
    qb = Q_BLOCK
    groups = range(NSA_KV_GROUPS)
    gl_refs = (gl0_ref, gl1_ref)

    @pl.when(qi == 0)
    def _():
        def lookup(bucket):
            vals = [jnp.zeros(bucket.shape, f32) for _ in range(NSA_HEADS)]
            for k in range(REL_BUCKETS):
                eq = bucket == k
                vals = [jnp.where(eq, tbl_ref[h, k], v) for h, v in enumerate(vals)]
            return vals

        for c in range(NEAR_TILES):
            sl = slice(c * qb, (c + 1) * qb)
            for h, v in enumerate(lookup(bs_ref[:, sl])):
                strip_ref[h, :, sl] = v
        for h, v in enumerate(lookup(bc_ref[...])):
            d = v - tbl_ref[h, REL_BUCKETS - 1]
            hi = d.astype(bf16)
            sc_ref[h, 0] = hi
            sc_ref[h, 1] = (d - hi.astype(f32)).astype(bf16)

    far_bias = [tbl_ref[h, REL_BUCKETS - 1] for h in range(NSA_HEADS)]
    q = q_ref[...]
    q2 = [jnp.concatenate([q[:, (NSA_HG * g + hh) * NSA_DK:(NSA_HG * g + hh + 1) * NSA_DK]
                           for hh in range(NSA_HG)], axis=0) for g in groups]

    r2 = lax.broadcasted_iota(jnp.int32, (NSA_HG * qb, qb), 0) % qb
    c2 = lax.broadcasted_iota(jnp.int32, (NSA_HG * qb, qb), 1)

    def strip_bias(g, w):
        sl = slice(w * qb, (w + 1) * qb)
        return jnp.concatenate([strip_ref[NSA_HG * g + hh, :, sl] for hh in range(NSA_HG)], axis=0)

    def key_rows(w):
        kt = qi - (NEAR_TILES - 1) + w
        return kt, pl.ds(pl.multiple_of(jnp.maximum(kt, 0) * qb, qb), qb)

    def ka(g, rows):
        return ka_ref[rows, g * (NSA_DK + LANE):(g + 1) * (NSA_DK + LANE)]

    def vs(g, rows):
        return vs_ref[rows, g * NSA_DV:(g + 1) * NSA_DV]

    kw_cols = NSA_KV_GROUPS * NSA_DK

    o_win = []
    for g in groups:
        m_w = jnp.full((NSA_HG * qb, 1), NEG, f32)
        for w in range(NEAR_TILES - WIN_TILES, NEAR_TILES):
            kt, rows = key_rows(w)
            s = _dot_nt(q2[g], kwv_ref[rows, g * NSA_DK:(g + 1) * NSA_DK]) + strip_bias(g, w)
            if w == NEAR_TILES - WIN_TILES:
                valid = (c2 > r2) & (kt >= 0)
            elif w == NEAR_TILES - 1:
                valid = c2 <= r2
            else:
                valid = jnp.broadcast_to(kt >= 0, c2.shape)
            s = jnp.where(valid, s, NEG)
            stage_ref[g, w] = s
            m_w = jnp.maximum(m_w, jnp.max(s, axis=-1, keepdims=True))
        den = jnp.zeros((NSA_HG * qb, 1), f32)
        acc_w = jnp.zeros((NSA_HG * qb, NSA_DV), f32)
        for w in range(NEAR_TILES - WIN_TILES, NEAR_TILES):
            _, rows = key_rows(w)
            e = jnp.exp(stage_ref[g, w] - m_w)
            den = den + jnp.sum(e, axis=-1, keepdims=True)
            acc_w = acc_w + jnp.dot(e.astype(bf16), kwv_ref[rows, kw_cols + g * NSA_DV:kw_cols + (g + 1) * NSA_DV],
                                    preferred_element_type=f32)
        o_win.append(acc_w / den)

    r_i = lax.broadcasted_iota(jnp.int32, (qb, nc), 0)
    n_i = lax.broadcasted_iota(jnp.int32, (qb, nc), 1)
    ok_c = CMP_STRIDE * n_i <= qb * qi + r_i - (CMP_BLOCK - 1)
    shift = jnp.where(r_i + n_i == CMP_PER_Q * qi + CMP_PER_Q - 1, 1.0, 0.0).astype(bf16)
    mt = mt_ref[...]
    o_cmp, keys = [], []
    nblk = mt.shape[0]
    j_i = lax.broadcasted_iota(jnp.int32, (nblk, qb), 0)
    l_i = lax.broadcasted_iota(jnp.int32, (nblk, qb), 1)
    cur = SEL_PER_Q * qi + l_i // SEL_BLOCK
    ok_s = j_i <= cur
    forced = ok_s & ((j_i == 0) | (j_i >= cur - 1))
    for g in groups:
        kc = cmp_ref[0, g]
        vc = cmp_ref[0, NSA_KV_GROUPS + g]
        psum = jnp.zeros((qb, nc), f32)
        for hh in range(NSA_HG):
            h = NSA_HG * g + hh
            s = _dot_nt(q2[g][hh * qb:(hh + 1) * qb], kc)
            bias = (far_bias[h] + jnp.dot(sc_ref[h, 0], shift, preferred_element_type=f32)
                    + jnp.dot(sc_ref[h, 1], shift, preferred_element_type=f32))
            s = jnp.where(ok_c, s + bias, NEG)
            e = jnp.exp(s - jnp.max(s, axis=-1, keepdims=True))
            p = jnp.where(ok_c, e / jnp.sum(e, axis=-1, keepdims=True), 0.0)
            o_cmp.append(jnp.dot(p.astype(bf16), vc, preferred_element_type=f32))
            psum += p

        imp_t = sum(_dot_nt(mt, part) for part in _split3(psum))
        score = jnp.where(forced, FORCE_SCORE, jnp.where(ok_s, imp_t, -FORCE_SCORE))
        key = pltpu.bitcast(score, jnp.int32)
        score_ref[g] = key
        keys.append(key)

    half = nblk // 2

    def rank_half(lo):
        key_h = [k[lo:lo + half] for k in keys]
        key_h_m1 = [k - 1 for k in key_h]
        j_h = j_i[lo:lo + half]

        def rank_body(i, cnts):
            cnts = list(cnts)
            for u in range(SEL_PER_Q):
                jp = SEL_PER_Q * i + u
                later = j_h > jp
                for g in groups:
                    row = score_ref[g, pl.ds(jp, 1), :]
                    cnts[g] = cnts[g] + jnp.where(row > jnp.where(later, key_h_m1[g], key_h[g]), 1, 0)
            return tuple(cnts)

        return lax.fori_loop(0, qi + 1, rank_body,
                             tuple(jnp.zeros((half, qb), jnp.int32) for _ in groups))

    cnt_hi = lax.cond(SEL_PER_Q * qi + SEL_PER_Q > half, lambda: rank_half(half),
                      lambda: tuple(jnp.full((half, qb), N_SELECT, jnp.int32) for _ in groups))
    cnt_lo = rank_half(0)
    q_aug = []
    for g in groups:
        cnt = jnp.concatenate([cnt_lo[g], cnt_hi[g]], axis=0)
        pen = jnp.where(cnt < N_SELECT, 0.0, MASK_PEN).T.astype(bf16)
        q_aug.append(jnp.concatenate([q2[g], jnp.concatenate([pen] * NSA_HG, axis=0)], axis=1))

    row2 = lax.broadcasted_iota(jnp.int32, (NSA_HG * qb, 1), 0)
    far_col = [jnp.where(row2 < qb, far_bias[NSA_HG * g], far_bias[NSA_HG * g + 1]) for g in groups]
    n_far_keys = qb * jnp.maximum(qi - (NEAR_TILES - 1), 0)
    far_step = SEL_FAR_SUB * SEL_FAR_TILE
    c_far = lax.broadcasted_iota(jnp.int32, (NSA_HG * qb, SEL_FAR_TILE), 1)
    n_far_steps = (n_far_keys + far_step - 1) // far_step

    def far_rows(kt, sub):
        k0 = kt * far_step + sub * SEL_FAR_TILE
        return k0, pl.ds(pl.multiple_of(k0, SEL_FAR_TILE), SEL_FAR_TILE)

    def far_scores(g, kt):
        m_raw = jnp.full((NSA_HG * qb, 1), MASK_PEN, f32)
        for sub in range(SEL_FAR_SUB):
            k0, rows = far_rows(kt, sub)
            s = _dot_nt(q_aug[g], ka(g, rows)) + far_col[g]
            s = jnp.where(c_far < n_far_keys - k0, s, MASK_PEN)
            far_ref[g, kt % 2, :, sub * SEL_FAR_TILE:(sub + 1) * SEL_FAR_TILE] = s
            m_raw = jnp.maximum(m_raw, jnp.max(s, axis=-1, keepdims=True))
        return m_raw

    m_raw0 = [far_scores(g, 0) for g in groups]

    m_near = []
    for g in groups:
        m_n = jnp.full((NSA_HG * qb, 1), NEG, f32)
        for w in range(NEAR_TILES):
            kt, rows = key_rows(w)
            s = _dot_nt(q_aug[g], ka(g, rows)) + strip_bias(g, w)
            if w == NEAR_TILES - 1:
                s = jnp.where(c2 <= r2, s, NEG)
            else:
                s = jnp.where(kt >= 0, s, NEG)
            stage_ref[g, w] = s
            m_n = jnp.maximum(m_n, jnp.max(s, axis=-1, keepdims=True))
        m_near.append(m_n)

    def far_body(kt, carry):
        out = []
        for g in groups:
            m_old, l_old, acc_old, m_raw = carry[g]
            ss = [far_ref[g, kt % 2, :, sub * SEL_FAR_TILE:(sub + 1) * SEL_FAR_TILE] for sub in range(SEL_FAR_SUB)]
            m_new = jnp.maximum(m_old, m_raw)
            ps = [jnp.exp(s - m_new) for s in ss]
            alpha = jnp.exp(m_old - m_new)
            l_new = alpha * l_old + sum(jnp.sum(p, axis=-1, keepdims=True) for p in ps)
            acc_new = alpha * acc_old + sum(
                jnp.dot(p.astype(bf16), vs(g, far_rows(kt, sub)[1]), preferred_element_type=f32)
                for sub, p in enumerate(ps))
            m_raw_next = far_scores(g, jnp.minimum(kt + 1, n_far_steps - 1))
            out.append((m_new, l_new, acc_new, m_raw_next))
        return tuple(out)

    far = lax.fori_loop(
        0, n_far_steps, far_body,
        tuple((jnp.full((NSA_HG * qb, 1), MASK_PEN, f32), jnp.zeros((NSA_HG * qb, 1), f32),
               jnp.zeros((NSA_HG * qb, NSA_DV), f32), m_raw0[g]) for g in groups))

    for g in groups:
        m_far, l_far, acc_far, _ = far[g]
        m_sel = jnp.maximum(m_far, m_near[g])
        alpha = jnp.exp(m_far - m_sel)
        l_sel = alpha * l_far
        acc_sel = alpha * acc_far
        for w in range(NEAR_TILES):
            _, rows = key_rows(w)
            e = jnp.exp(stage_ref[g, w] - m_sel)
            l_sel = l_sel + jnp.sum(e, axis=-1, keepdims=True)
            acc_sel = acc_sel + jnp.dot(e.astype(bf16), vs(g, rows), preferred_element_type=f32)
        o_sel = acc_sel / l_sel

        gate = jax.nn.sigmoid(gl_refs[g][...])
        for hh in range(NSA_HG):
            h = NSA_HG * g + hh
            rs = slice(hh * qb, (hh + 1) * qb)
            o_ref[:, h * NSA_DV:(h + 1) * NSA_DV] = (gate[:, 3 * hh:3 * hh + 1] * o_cmp[h]
                                                     + gate[:, 3 * hh + 1:3 * hh + 2] * o_sel[rs]
                                                     + gate[:, 3 * hh + 2:3 * hh + 3] * o_win[g][rs])


def _rel_bucket(dist):
    n = jnp.maximum(dist, 0)
    nf = jnp.maximum(n, 1).astype(f32)
    large = REL_MAX_EXACT + (jnp.log(nf / REL_MAX_EXACT) / math.log(REL_MAX_DIST / REL_MAX_EXACT)
                             * (REL_BUCKETS - REL_MAX_EXACT)).astype(jnp.int32)
    large = jnp.minimum(large, REL_BUCKETS - 1)
    return jnp.where(n < REL_MAX_EXACT, n, large)


def _nsa_constants(seq):
    nc = seq // CMP_STRIDE
    r = jnp.arange(Q_BLOCK)[:, None]
    bs = _rel_bucket(STRIP_D0 + r - jnp.arange(STRIP_W)[None, :])
    bc = _rel_bucket(r + CMP_STRIDE * jnp.arange(LANE)[None, :] - (CMP_STRIDE * (CMP_PER_Q - 1) + CMP_BLOCK - 1))
    ratio = SEL_BLOCK // CMP_STRIDE
    mt = np.zeros((LANE, nc), np.float32)
    for j in range(LANE):
        for i, wt in enumerate(SEL_OVERLAP_WEIGHTS):
            mcol = ratio * j - 1 + i
            if 0 <= mcol < nc:
                mt[j, mcol] = wt
    return bs.astype(jnp.int32), bc.astype(jnp.int32), jnp.asarray(mt, bf16)


def _nsa_attention(qn, cmp_kv, kaug, vsel, kwvw, proj, offs, tbl, consts, bsz, seq):
    bs, bc, mt = consts
    assert seq % (SEL_FAR_SUB * SEL_FAR_TILE) == 0, "far steps must tile the sequence"
    m = bsz * seq
    n_qb = seq // Q_BLOCK
    nc = seq // CMP_STRIDE
    g_ = NSA_KV_GROUPS
    assert g_ == 2
    resident = lambda width: pl.BlockSpec((seq, width), lambda b, i: (b, 0))
    const2 = lambda shape: pl.BlockSpec(shape, lambda b, i: (0, 0))
    gcol = offs["b_g"] // LANE
    gate = lambda g: pl.BlockSpec((Q_BLOCK, LANE), lambda b, i: (b * n_qb + i, gcol + g))
    rows2 = NSA_HG * Q_BLOCK
    return pl.pallas_call(
        _nsa_attn_kernel,
        grid=(bsz, n_qb),
        in_specs=[
            pl.BlockSpec(memory_space=pltpu.SMEM),
            pl.BlockSpec((Q_BLOCK, NSA_HEADS * NSA_DK), lambda b, i: (b * n_qb + i, 0)),
            pl.BlockSpec((1, 2 * g_, nc, NSA_DK), lambda b, i: (b, 0, 0, 0)),
            resident(g_ * (NSA_DK + LANE)), resident(g_ * NSA_DV), resident(g_ * (NSA_DK + NSA_DV)),
            const2((LANE, nc)), const2((Q_BLOCK, STRIP_W)), const2((Q_BLOCK, LANE)),
            gate(0), gate(1),
        ],
        out_specs=pl.BlockSpec((Q_BLOCK, NSA_HEADS * NSA_DV), lambda b, i: (b * n_qb + i, 0)),
        out_shape=jax.ShapeDtypeStruct((m, NSA_HEADS * NSA_DV), f32),
        scratch_shapes=[
            pltpu.VMEM((NSA_HEADS, Q_BLOCK, STRIP_W), f32),
            pltpu.VMEM((NSA_HEADS, 2, Q_BLOCK, LANE), bf16),
            pltpu.VMEM((g_, LANE, Q_BLOCK), jnp.int32),
            pltpu.VMEM((g_, NEAR_TILES, rows2, Q_BLOCK), f32),
            pltpu.VMEM((g_, 2, rows2, SEL_FAR_SUB * SEL_FAR_TILE), f32),
        ],
        compiler_params=pltpu.CompilerParams(dimension_semantics=("arbitrary", "arbitrary"),
                                             vmem_limit_bytes=V7X_VMEM_LIMIT_BYTES),
        name="nsa_attention",
    )(tbl, qn, cmp_kv, kaug, vsel, kwvw, mt, bs, bc, proj, proj)


def _nsa(proj, offs, q_norm, k_norm, pe_k, pe_v, w_ck, w_cv, rel_table, consts, bsz, seq):
    q_gain = q_norm.reshape(1, NSA_DK)
    k_gain = k_norm.reshape(1, NSA_DK)
    qn, kaug, vsel, kwvw = _nsa_prep(proj, offs, q_gain, k_gain, seq)
    cmp_kv = _nsa_compress(proj, offs, jnp.stack([w_ck, w_cv]).astype(bf16), jnp.stack([pe_k, pe_v]), k_gain,
                           bsz, seq)
    return _nsa_attention(qn, cmp_kv, kaug, vsel, kwvw, proj, offs, rel_table.T, consts, bsz, seq)


SMALL_A_LANE = 0
SMALL_B_LANE = GDN_HEADS
SMALL_DT_LANE = 2 * GDN_HEADS
CONV_HALO = 8


def _softplus(x):
    return jnp.maximum(x, 0.0) + jnp.log1p(jnp.exp(-jnp.abs(x)))


def _exact_dot(a_f32, b_01):
    return sum(jnp.dot(part, b_01, preferred_element_type=f32) for part in _split3(a_f32))


def _exact_dot_left(b_01, a_f32):
    return sum(jnp.dot(b_01, part, preferred_element_type=f32) for part in _split3(a_f32))


def _lane_vec(values, lane0):
    return jnp.zeros((1, LANE), f32).at[0, lane0:lane0 + values.shape[0]].set(values.astype(f32))


def _causal_conv_silu(x_ref, tail_ref, buf_ref, w_ref, bias):
    rows = x_ref.shape[0]
    buf_ref[pl.ds(0, CONV_HALO), :] = tail_ref[...]
    buf_ref[pl.ds(CONV_HALO, rows), :] = x_ref[...]
    tail_ref[...] = x_ref[pl.ds(rows - CONV_HALO, CONV_HALO), :]
    acc = bias
    for k in range(CONV_K):
        acc = acc + w_ref[k:k + 1, :] * buf_ref[pl.ds(CONV_HALO - (CONV_K - 1) + k, rows), :]
    return acc * jax.nn.sigmoid(acc)


SSM_PAIR = LANE // SSM_HEAD_DIM
SSM_HEADS_PER_GROUP = SSM_HEADS // SSM_GROUPS


def _ssd_kernel(xbc_ref, z_ref, sm_ref, cw_ref, cb_ref, dtb_ref, alog_ref, dvec_ref, ng_ref, tri_ref, exp_ref,
                o_ref, tail_ref, buf_ref, state_ref):
    L = SSM_CHUNK
    N = SSM_STATE

    xbc = _causal_conv_silu(xbc_ref, tail_ref, buf_ref, cw_ref, cb_ref[...])
    xs = xbc[:, :SSM_INNER]
    dt = _softplus(sm_ref[...] + dtb_ref[...])
    a = dt * (-jnp.exp(alog_ref[...]))
    tri = tri_ref[...]
    ac = _exact_dot_left(tri, a)
    ac_t = ac.T
    eac = jnp.exp(ac)
    w_t = jnp.exp(ac_t[:, L - 1:L] - ac_t)
    chunk_decay = jnp.exp(ac[L - 1:L, :])
    xdt = xs * _exact_dot(dt, exp_ref[...])

    row = lax.broadcasted_iota(jnp.int32, (L, L), 0)
    col = lax.broadcasted_iota(jnp.int32, (L, L), 1)
    causal = row >= col
    lane = lax.broadcasted_iota(jnp.int32, (L, LANE), 1)
    first_half = lane < SSM_HEAD_DIM

    ys = []
    for g in range(SSM_GROUPS):
        bg = xbc[:, SSM_INNER + g * N:SSM_INNER + (g + 1) * N]
        cg = xbc[:, SSM_INNER + SSM_GROUPS * N + g * N:SSM_INNER + SSM_GROUPS * N + (g + 1) * N]
        cb = _dot_nt(cg.astype(bf16), bg.astype(bf16))
        bg_t = bg.T
        for i in range(g * SSM_HEADS_PER_GROUP // SSM_PAIR, (g + 1) * SSM_HEADS_PER_GROUP // SSM_PAIR):
            xdt_pair = xdt[:, i * LANE:(i + 1) * LANE].astype(bf16)
            outs = []
            for hh in range(SSM_PAIR):
                h = SSM_PAIR * i + hh
                c = SMALL_DT_LANE + h
                seg = ac[:, c:c + 1] - ac_t[c:c + 1, :]
                sc = (cb * jnp.exp(jnp.where(causal, seg, NEG))).astype(bf16)
                y = jnp.dot(sc, xdt_pair, preferred_element_type=f32)
                s_in = state_ref[h]
                y += jnp.dot((cg * eac[:, c:c + 1]).astype(bf16), s_in.astype(bf16), preferred_element_type=f32)
                st = jnp.dot((bg_t * w_t[c:c + 1, :]).astype(bf16), xdt_pair, preferred_element_type=f32)
                state_ref[h] = s_in * chunk_decay[:, c:c + 1] + st
                outs.append(y)
            y_pair = jnp.where(first_half, outs[0], outs[1])
            sl = slice(i * LANE, (i + 1) * LANE)
            y_pair = y_pair + xs[:, sl] * dvec_ref[:, sl]
            z = z_ref[:, sl]
            ys.append(y_pair * (z * jax.nn.sigmoid(z)))

    per_group = len(ys) // SSM_GROUPS
    gw = SSM_INNER // SSM_GROUPS
    for g in range(SSM_GROUPS):
        tiles = ys[g * per_group:(g + 1) * per_group]
        ms = sum(jnp.sum(t * t, axis=-1, keepdims=True) for t in tiles) / gw
        scale = lax.rsqrt(ms + EPS)
        for k, t in enumerate(tiles):
            sl = slice(g * gw + k * LANE, g * gw + (k + 1) * LANE)
            o_ref[:, sl] = (t * scale * ng_ref[:, sl]).astype(o_ref.dtype)


GDN_STEP_ROWS = 128
GDN_QK_W = GDN_HEADS * GDN_DK
GDN_CONV_CH = GDN_HEADS * (2 * GDN_DK + GDN_DV)


def _split2(x):
    hi = x.astype(bf16)
    return hi, (x - hi.astype(f32)).astype(bf16)


def _dot_split(a_parts, b_parts):
    (ah, al), (bh, bl) = a_parts, b_parts
    return (jnp.dot(ah, bh, preferred_element_type=f32) + jnp.dot(ah, bl, preferred_element_type=f32)
            + jnp.dot(al, bh, preferred_element_type=f32))


def _unit_lower_solve_many(a_list, rhs_list):
    n = a_list[0].shape[0]
    ps = [_split2(-a) for a in a_list]
    xs = [rhs + _dot_split(p, _split2(rhs)) for p, rhs in zip(ps, rhs_list)]
    k = 2
    while k < n:
        ps = [_split2(_dot_split(p, p)) for p in ps]
        xs = [x + _dot_split(p, _split2(x)) for p, x in zip(ps, xs)]
        k *= 2
    return xs


def _gdn_kernel(qkv_ref, z_ref, sm_ref, cw_ref, dtb_ref, alog_ref, ng_ref, tri_ref, o_ref, tail_ref, buf_ref,
                state_ref):
    C = GDN_CHUNK

    qkv = _causal_conv_silu(qkv_ref, tail_ref, buf_ref, cw_ref, 0.0)
    sm = sm_ref[...]
    beta = jax.nn.sigmoid(sm)
    g_log = -jnp.exp(alog_ref[...]) * _softplus(sm + dtb_ref[...])
    gc = _exact_dot_left(tri_ref[...], g_log)
    gc_t = gc.T
    egc = jnp.exp(gc)

    row = lax.broadcasted_iota(jnp.int32, (C, C), 0)
    col = lax.broadcasted_iota(jnp.int32, (C, C), 1)
    causal = row >= col
    strict = row > col

    units = [(ch, h) for ch in range(GDN_STEP_ROWS // C) for h in range(GDN_HEADS)]
    a_list, rhs_list, pre = [], [], []
    for ch, h in units:
        rs = slice(ch * C, (ch + 1) * C)
        last = slice((ch + 1) * C - 1, (ch + 1) * C)
        q = qkv[rs, h * GDN_DK:(h + 1) * GDN_DK]
        k = qkv[rs, GDN_QK_W + h * GDN_DK:GDN_QK_W + (h + 1) * GDN_DK]
        v = qkv[rs, 2 * GDN_QK_W + h * GDN_DV:2 * GDN_QK_W + (h + 1) * GDN_DV]
        q = q * lax.rsqrt(jnp.sum(q * q, axis=-1, keepdims=True) + EPS) * (GDN_DK ** -0.5)
        k = k * lax.rsqrt(jnp.sum(k * k, axis=-1, keepdims=True) + EPS)
        ca = SMALL_A_LANE + h
        b = beta[rs, SMALL_B_LANE + h:SMALL_B_LANE + h + 1]
        g_col = gc[rs, ca:ca + 1]
        g_last = gc[last, ca:ca + 1]
        e_col = egc[rs, ca:ca + 1]
        decay = jnp.exp(jnp.where(causal, g_col - gc_t[ca:ca + 1, rs], NEG))
        kb = k * b
        k16 = k.astype(bf16)
        a_list.append(jnp.where(strict, _dot_nt(kb.astype(bf16), k16) * decay, 0.0))
        rhs_list.append(jnp.concatenate([v * b, kb * e_col], axis=1))
        attn = (_dot_nt(q.astype(bf16), k16) * decay).astype(bf16)
        q_dec = (q * e_col).astype(bf16)
        k_dec_t = (k * jnp.exp(g_last - g_col)).T.astype(bf16)
        pre.append((attn, q_dec, k_dec_t, jnp.exp(g_last)))

    sols = _unit_lower_solve_many(a_list, rhs_list)

    for (ch, h), sol, (attn, q_dec, k_dec_t, g_end) in zip(units, sols, pre):
        rs = slice(ch * C, (ch + 1) * C)
        u, w = sol[:, :GDN_DV], sol[:, GDN_DV:]
        s_in = state_ref[h]
        s16 = s_in.astype(bf16)
        v_new = u - jnp.dot(w.astype(bf16), s16, preferred_element_type=f32)
        v16 = v_new.astype(bf16)
        o = (jnp.dot(q_dec, s16, preferred_element_type=f32) + jnp.dot(attn, v16, preferred_element_type=f32))
        state_ref[h] = s_in * g_end + jnp.dot(k_dec_t, v16, preferred_element_type=f32)

        o = o * lax.rsqrt(jnp.mean(o * o, axis=-1, keepdims=True) + EPS) * ng_ref[...]
        z = z_ref[rs, h * GDN_DV:(h + 1) * GDN_DV]
        o_ref[rs, h * GDN_DV:(h + 1) * GDN_DV] = (o * (z * jax.nn.sigmoid(z))).astype(o_ref.dtype)


REC_ROWS = GDN_STEP_ROWS
assert REC_ROWS == SSM_CHUNK


def _rec_mixers_kernel(qkv_ref, za_ref, sm_ref, gcw_ref, gdtb_ref, galog_ref, gng_ref, gtri_ref,
                       xbc_ref, zc_ref, scw_ref, scb_ref, sdtb_ref, salog_ref, dvec_ref, sng_ref, stri_ref, exp_ref,
                       oa_ref, oc_ref, gtail_ref, gbuf_ref, gstate_ref, stail_ref, sbuf_ref, sstate_ref):
    @pl.when(pl.program_id(1) == 0)
    def _():
        for ref in (gtail_ref, gstate_ref, stail_ref, sstate_ref):
            ref[...] = jnp.zeros(ref.shape, f32)

    _gdn_kernel(qkv_ref, za_ref, sm_ref, gcw_ref, gdtb_ref, galog_ref, gng_ref, gtri_ref, oa_ref,
                gtail_ref, gbuf_ref, gstate_ref)
    _ssd_kernel(xbc_ref, zc_ref, sm_ref, scw_ref, scb_ref, sdtb_ref, salog_ref, dvec_ref, sng_ref, stri_ref,
                exp_ref, oc_ref, stail_ref, sbuf_ref, sstate_ref)


def _rec_mixers(proj, offs, gdn_conv, gdn_a_log, gdn_dt_bias, gdn_norm, ssm_conv_w, ssm_conv_b, ssm_dt_bias,
                ssm_a_log, ssm_d, ssm_norm, bsz, seq):
    m = bsz * seq
    R_ = REC_ROWS
    n_s = seq // R_
    idx = np.arange(R_)
    gtri = ((idx[:, None] >= idx[None, :]) & (idx[:, None] // GDN_CHUNK == idx[None, :] // GDN_CHUNK))
    stri = np.tril(np.ones((R_, R_), np.float32))
    expand = np.zeros((LANE, SSM_INNER), np.float32)
    for h in range(SSM_HEADS):
        expand[SMALL_DT_LANE + h, h * SSM_HEAD_DIM:(h + 1) * SSM_HEAD_DIM] = 1.0
    gz = GDN_HEADS * GDN_DV
    tok = lambda w, col: pl.BlockSpec((R_, w), lambda b, c: (b * n_s + c, col // w))
    const = lambda shape: pl.BlockSpec(shape, lambda b, c: (0, 0))
    return pl.pallas_call(
        _rec_mixers_kernel,
        grid=(bsz, n_s),
        in_specs=[
            tok(GDN_CONV_CH, offs["a_q"]), tok(gz, offs["a_z"]), tok(LANE, offs["small"]),
            const((CONV_K, GDN_CONV_CH)), const((1, LANE)), const((1, LANE)), const((1, GDN_DV)), const((R_, R_)),
            tok(SSM_CONV_CH, offs["c_xbc"]), tok(SSM_INNER, offs["c_z"]),
            const((CONV_K, SSM_CONV_CH)), const((1, SSM_CONV_CH)), const((1, LANE)), const((1, LANE)),
            const((1, SSM_INNER)), const((1, SSM_INNER)), const((R_, R_)), const((LANE, SSM_INNER)),
        ],
        out_specs=[pl.BlockSpec((R_, gz), lambda b, c: (b * n_s + c, 0)),
                   pl.BlockSpec((R_, SSM_INNER), lambda b, c: (b * n_s + c, 0))],
        out_shape=[jax.ShapeDtypeStruct((m, gz), bf16), jax.ShapeDtypeStruct((m, SSM_INNER), bf16)],
        scratch_shapes=[
            pltpu.VMEM((CONV_HALO, GDN_CONV_CH), f32),
            pltpu.VMEM((CONV_HALO + R_, GDN_CONV_CH), f32),
            pltpu.VMEM((GDN_HEADS, GDN_DK, GDN_DV), f32),
            pltpu.VMEM((CONV_HALO, SSM_CONV_CH), f32),
            pltpu.VMEM((CONV_HALO + R_, SSM_CONV_CH), f32),
            pltpu.VMEM((SSM_HEADS, SSM_STATE, LANE), f32),
        ],
        compiler_params=pltpu.CompilerParams(dimension_semantics=("arbitrary", "arbitrary"),
                                             vmem_limit_bytes=V7X_VMEM_LIMIT_BYTES),
        name="recurrent_mixers",
    )(proj, proj, proj, gdn_conv, _lane_vec(gdn_dt_bias, SMALL_A_LANE), _lane_vec(gdn_a_log, SMALL_A_LANE),
      gdn_norm.reshape(1, -1), jnp.asarray(gtri.astype(np.float32), bf16),
      proj, proj, ssm_conv_w, ssm_conv_b.reshape(1, -1), _lane_vec(ssm_dt_bias, SMALL_DT_LANE),
      _lane_vec(ssm_a_log, SMALL_DT_LANE), jnp.repeat(ssm_d, SSM_HEAD_DIM).reshape(1, -1),
      ssm_norm.reshape(1, -1), jnp.asarray(stri, bf16), jnp.asarray(expand, bf16))


_IN_NAMES = ("a_q", "a_k", "a_v", "a_z", "a_a", "a_b", "b_q", "b_kc", "b_vc", "b_ks", "b_vs", "b_kw", "b_vw",
             "b_g", "c_z", "c_xbc", "c_dt", "m_gate")


def _in_sizes(d_model):
    return (GDN_HEADS * GDN_DK, GDN_HEADS * GDN_DK, GDN_HEADS * GDN_DV, GDN_HEADS * GDN_DV, GDN_HEADS, GDN_HEADS,
            NSA_HEADS * NSA_DK, NSA_KV_GROUPS * NSA_DK, NSA_KV_GROUPS * NSA_DV, NSA_KV_GROUPS * NSA_DK,
            NSA_KV_GROUPS * NSA_DV, NSA_KV_GROUPS * NSA_DK, NSA_KV_GROUPS * NSA_DV, 3 * NSA_HEADS,
            SSM_INNER, SSM_CONV_CH, SSM_HEADS, 3 * d_model)


_BIG_ORDER = ("m_gate", "a_q", "a_k", "a_v", "c_xbc", "c_z", "a_z", "b_q", "b_kc", "b_vc", "b_ks", "b_vs",
              "b_kw", "b_vw")
_SMALL_ORDER = ("a_a", "a_b", "c_dt")


def _in_layout(d_model, tn=1024):
    sizes = dict(zip(_IN_NAMES, _in_sizes(d_model)))
    src_off = dict(zip(_IN_NAMES, np.cumsum((0,) + _in_sizes(d_model))[:-1].tolist()))
    cols, offs = [], {}

    def pad_to_lane():
        cols.extend([-1] * (_round_up(len(cols), LANE) - len(cols)))

    for name in _BIG_ORDER:
        offs[name] = len(cols)
        cols.extend(range(src_off[name], src_off[name] + sizes[name]))
    offs["small"] = len(cols)
    for name in _SMALL_ORDER:
        offs[name] = len(cols)
        cols.extend(range(src_off[name], src_off[name] + sizes[name]))
    pad_to_lane()
    offs["b_g"] = len(cols)
    per_group = 3 * NSA_HG
    for g in range(NSA_KV_GROUPS):
        cols.extend(range(src_off["b_g"] + g * per_group, src_off["b_g"] + (g + 1) * per_group))
        pad_to_lane()
    n_pad = _round_up(len(cols), tn)
    cols.extend([-1] * (n_pad - len(cols)))
    return np.asarray(cols, np.int32), offs, sizes, n_pad


def _col_runs(cols):
    runs, start = [], 0
    for i in range(1, len(cols) + 1):
        same_run = i < len(cols) and ((cols[i] < 0 and cols[i - 1] < 0)
                                      or (cols[i - 1] >= 0 and cols[i] == cols[i - 1] + 1))
        if not same_run:
            runs.append((start, None if cols[start] < 0 else int(cols[start]), i - start))
            start = i
    return runs


def _cast_cols_kernel(runs_per_out, w_ref, *o_refs):
    for runs, o_ref in zip(runs_per_out, o_refs):
        for dst, src, size in runs:
            if src is None:
                o_ref[:, dst:dst + size] = jnp.zeros((o_ref.shape[0], size), o_ref.dtype)
            else:
                o_ref[:, dst:dst + size] = w_ref[0, :, src:src + size].astype(o_ref.dtype)


def _cast_cols(w_stack, layer, runs_per_out, widths, *, tr=256):
    _, rows, n_src = w_stack.shape
    return pl.pallas_call(
        functools.partial(_cast_cols_kernel, runs_per_out),
        grid=(rows // tr,),
        in_specs=[pl.BlockSpec((1, tr, n_src), lambda i: (layer, i, 0))],
        out_specs=[pl.BlockSpec((tr, w), lambda i: (i, 0)) for w in widths],
        out_shape=[jax.ShapeDtypeStruct((rows, w), bf16) for w in widths],
        compiler_params=pltpu.CompilerParams(dimension_semantics=("parallel",),
                                             vmem_limit_bytes=V7X_VMEM_LIMIT_BYTES),
        name="weight_cast_cols",
    )(w_stack)


def _cast_rows_kernel(rows_src, w_ref, o_ref):
    tr = o_ref.shape[0]
    row = lax.broadcasted_iota(jnp.int32, o_ref.shape, 0) + pl.program_id(0) * tr
    o_ref[...] = jnp.where(row < rows_src, w_ref[0], 0.0).astype(o_ref.dtype)


def _cast_rows(w_stack, layer, rows_out, *, tr=512):
    _, rows_src, n = w_stack.shape
    return pl.pallas_call(
        functools.partial(_cast_rows_kernel, rows_src),
        grid=(rows_out // tr,),
        in_specs=[pl.BlockSpec((1, tr, n), lambda i: (layer, i, 0))],
        out_specs=pl.BlockSpec((tr, n), lambda i: (i, 0)),
        out_shape=jax.ShapeDtypeStruct((rows_out, n), bf16),
        compiler_params=pltpu.CompilerParams(dimension_semantics=("parallel",),
                                             vmem_limit_bytes=V7X_VMEM_LIMIT_BYTES),
        name="weight_cast_rows",
    )(w_stack)


def _prep_w_in(w_in, layer, cols):
    return _cast_cols(w_in, layer, [_col_runs(cols)], [len(cols)])[0]


def _prep_ffn(w_up, w_down, layer, tf=512):
    ff = w_down.shape[1]
    ffp = _round_up(ff, tf)
    half = lambda src: [(0, src, ff)] + ([(ff, None, ffp - ff)] if ffp > ff else [])
    wa, wb = _cast_cols(w_up, layer, [half(0), half(ff)], [ffp, ffp])
    return wa, wb, _cast_rows(w_down, layer, ffp)


def kernel(x, rel_table, g_ffn1, w_up1, w_down1, g_mix, w_in, gdn_conv, gdn_a_log, gdn_dt_bias, gdn_norm,
           nsa_q_norm, nsa_k_norm, nsa_pe_k, nsa_pe_v, nsa_w_ck, nsa_w_cv, ssm_conv_w, ssm_conv_b,
           ssm_dt_bias, ssm_a_log, ssm_d, ssm_norm, p_a, p_b, p_c, w_o, g_ffn2, w_up2, w_down2):
    bsz, seq, d = x.shape
    depth = w_in.shape[0]
    m = bsz * seq
    cols, offs, _, _ = _in_layout(d)
    x2 = x.reshape(m, d)
    nsa_consts = _nsa_constants(seq)

    for l in range(depth):
        wa, wb, wd = _prep_ffn(w_up1, w_down1, l)
        x2 = _ffn(x2, g_ffn1[l].reshape(1, d), wa, wb, wd)

        proj = _norm_matmul(x2, g_mix[l].reshape(1, d), _prep_w_in(w_in, l, cols))
        y_a, y_c = _rec_mixers(proj, offs, gdn_conv[l], gdn_a_log[l], gdn_dt_bias[l], gdn_norm[l], ssm_conv_w[l],
                               ssm_conv_b[l], ssm_dt_bias[l], ssm_a_log[l], ssm_d[l], ssm_norm[l], bsz, seq)
        y_b = _nsa(proj, offs, nsa_q_norm[l], nsa_k_norm[l], nsa_pe_k[l], nsa_pe_v[l], nsa_w_ck[l], nsa_w_cv[l],
                   rel_table, nsa_consts, bsz, seq)
        x2 = _merge(x2, y_a, y_b, y_c, proj, offs["m_gate"],
                    p_a[l].astype(bf16), p_b[l].astype(bf16), p_c[l].astype(bf16), w_o[l].astype(bf16))

        wa, wb, wd = _prep_ffn(w_up2, w_down2, l)
        x2 = _ffn(x2, g_ffn2[l].reshape(1, d), wa, wb, wd)
    return x2.reshape(bsz, seq, d)
```

```python
import functools
import math

import jax
import jax.numpy as jnp
import numpy as np
from jax import lax
from jax.experimental import pallas as pl
from jax.experimental.pallas import tpu as pltpu

EPS = 1e-6
CONV_K = 4

GDN_HEADS = 4
GDN_DK = 128
GDN_DV = 128
GDN_CHUNK = 64

NSA_HEADS = 4
NSA_KV_GROUPS = 2
NSA_DK = 128
NSA_DV = 128
CMP_BLOCK = 32
CMP_STRIDE = 16
SEL_BLOCK = 64
N_SELECT = 16
WINDOW = 512
Q_BLOCK = 128
SEL_OVERLAP_WEIGHTS = (1.0, 2.0, 2.0, 2.0, 1.0)
FORCE_SCORE = 1e4
NEG = -1e30

SSM_HEADS = 16
SSM_HEAD_DIM = 64
SSM_GROUPS = 2
SSM_STATE = 128
SSM_CHUNK = 128
SSM_INNER = SSM_HEADS * SSM_HEAD_DIM
SSM_CONV_CH = SSM_INNER + 2 * SSM_GROUPS * SSM_STATE

REL_BUCKETS = 32
REL_MAX_EXACT = 16
REL_MAX_DIST = 1024

V7X_VMEM_LIMIT_BYTES = 56 * 1024 * 1024
LANE = 128

bf16 = jnp.bfloat16
f32 = jnp.float32


def _round_up(n, m):
    return (n + m - 1) // m * m


def _ffn_kernel(x_ref, g_ref, wa_ref, wb_ref, wd_ref, o_ref, h_ref):
    j = pl.program_id(1)

    @pl.when(j == 0)
    def _():
        x = x_ref[...]
        h = x * lax.rsqrt(jnp.mean(x * x, axis=-1, keepdims=True) + EPS) * g_ref[...]
        h_ref[...] = h.astype(bf16)
        o_ref[...] = x

    h = h_ref[...]
    a = jnp.dot(h, wa_ref[...], preferred_element_type=f32)
    b = jnp.dot(h, wb_ref[...], preferred_element_type=f32)
    act = (0.5 * a * jax.nn.sigmoid(a) * b).astype(bf16)
    o_ref[...] += jnp.dot(act, wd_ref[...], preferred_element_type=f32)


def _ffn(x2, g, wa, wb, wd, *, tm=1024, tf=512):
    m, d = x2.shape
    ffp = wa.shape[1]
    return pl.pallas_call(
        _ffn_kernel,
        grid=(m // tm, ffp // tf),
        in_specs=[
            pl.BlockSpec((tm, d), lambda i, j: (i, 0)),
            pl.BlockSpec((1, d), lambda i, j: (0, 0)),
            pl.BlockSpec((d, tf), lambda i, j: (0, j)),
            pl.BlockSpec((d, tf), lambda i, j: (0, j)),
            pl.BlockSpec((tf, d), lambda i, j: (j, 0)),
        ],
        out_specs=pl.BlockSpec((tm, d), lambda i, j: (i, 0)),
        out_shape=jax.ShapeDtypeStruct((m, d), f32),
        scratch_shapes=[pltpu.VMEM((tm, d), bf16)],
        compiler_params=pltpu.CompilerParams(
            dimension_semantics=("parallel", "arbitrary"),
            vmem_limit_bytes=V7X_VMEM_LIMIT_BYTES),
        name="ffn_swiglu",
    )(x2, g, wa, wb, wd)


def _norm_matmul_kernel(x_ref, g_ref, w_ref, o_ref, h_ref):
    j = pl.program_id(1)

    @pl.when(j == 0)
    def _():
        x = x_ref[...]
        h = x * lax.rsqrt(jnp.mean(x * x, axis=-1, keepdims=True) + EPS) * g_ref[...]
        h_ref[...] = h.astype(bf16)

    o_ref[...] = jnp.dot(h_ref[...], w_ref[...], preferred_element_type=f32)


def _norm_matmul(x2, g, w, *, tm=1024, tn=1024):
    m, d = x2.shape
    n = w.shape[1]
    return pl.pallas_call(
        _norm_matmul_kernel,
        grid=(m // tm, n // tn),
        in_specs=[
            pl.BlockSpec((tm, d), lambda i, j: (i, 0)),
            pl.BlockSpec((1, d), lambda i, j: (0, 0)),
            pl.BlockSpec((d, tn), lambda i, j: (0, j)),
        ],
        out_specs=pl.BlockSpec((tm, tn), lambda i, j: (i, j)),
        out_shape=jax.ShapeDtypeStruct((m, n), f32),
        scratch_shapes=[pltpu.VMEM((tm, d), bf16)],
        compiler_params=pltpu.CompilerParams(
            dimension_semantics=("parallel", "arbitrary"),
            vmem_limit_bytes=V7X_VMEM_LIMIT_BYTES),
        name="norm_in_proj",
    )(x2, g, w)


def _merge_kernel(x_ref, ya_ref, yb_ref, yc_ref, ga_ref, gb_ref, gc_ref, pa_ref, pb_ref, pc_ref, wo_ref, o_ref):
    ma = jnp.dot(ya_ref[...].astype(bf16), pa_ref[...], preferred_element_type=f32)
    mb = jnp.dot(yb_ref[...].astype(bf16), pb_ref[...], preferred_element_type=f32)
    mc = jnp.dot(yc_ref[...].astype(bf16), pc_ref[...], preferred_element_type=f32)
    merged = (jax.nn.sigmoid(ga_ref[...]) * ma + jax.nn.sigmoid(gb_ref[...]) * mb
              + jax.nn.sigmoid(gc_ref[...]) * mc)
    o_ref[...] = x_ref[...] + jnp.dot(merged.astype(bf16), wo_ref[...], preferred_element_type=f32)


def _merge(x2, ya, yb, yc, gates, gate_col0, pa, pb, pc, wo, *, tm=256):
    m, d = x2.shape
    gb0 = gate_col0 // d
    const = dict(pipeline_mode=pl.Buffered(1))
    return pl.pallas_call(
        _merge_kernel,
        grid=(m // tm,),
        in_specs=[
            pl.BlockSpec((tm, d), lambda i: (i, 0)),
            pl.BlockSpec((tm, ya.shape[1]), lambda i: (i, 0)),
            pl.BlockSpec((tm, yb.shape[1]), lambda i: (i, 0)),
            pl.BlockSpec((tm, yc.shape[1]), lambda i: (i, 0)),
            pl.BlockSpec((tm, d), lambda i: (i, gb0)),
            pl.BlockSpec((tm, d), lambda i: (i, gb0 + 1)),
            pl.BlockSpec((tm, d), lambda i: (i, gb0 + 2)),
            pl.BlockSpec(pa.shape, lambda i: (0, 0), **const),
            pl.BlockSpec(pb.shape, lambda i: (0, 0), **const),
            pl.BlockSpec(pc.shape, lambda i: (0, 0), **const),
            pl.BlockSpec(wo.shape, lambda i: (0, 0), **const),
        ],
        out_specs=pl.BlockSpec((tm, d), lambda i: (i, 0)),
        out_shape=jax.ShapeDtypeStruct((m, d), f32),
        compiler_params=pltpu.CompilerParams(
            dimension_semantics=("parallel",),
            vmem_limit_bytes=V7X_VMEM_LIMIT_BYTES),
        name="merge_out_proj",
    )(x2, ya, yb, yc, gates, gates, gates, pa, pb, pc, wo)


NSA_HG = NSA_HEADS // NSA_KV_GROUPS
SEL_FAR_TILE = 512
SEL_FAR_SUB = 2
NEAR_TILES = 8
STRIP_W = NEAR_TILES * Q_BLOCK
STRIP_D0 = (NEAR_TILES - 1) * Q_BLOCK
WIN_TILES = WINDOW // Q_BLOCK + 1
CMP_PER_Q = Q_BLOCK // CMP_STRIDE
SEL_PER_Q = Q_BLOCK // SEL_BLOCK
MASK_PEN = -1e30


def _dot_nt(a, b):
    return lax.dot_general(a, b, (((1,), (1,)), ((), ())), preferred_element_type=f32)


def _lane_rms(x, gain):
    return x * lax.rsqrt(jnp.mean(x * x, axis=-1, keepdims=True) + EPS) * gain


def _nsa_prep_kernel(seq, q_ref, ks_ref, kw_ref, qg_ref, kg_ref, qo_ref, kao_ref, vso_ref, kwo_ref):
    tm = q_ref.shape[0]
    qg = qg_ref[...] * (NSA_DK ** -0.5)
    kg = kg_ref[...]
    for h in range(NSA_HEADS):
        sl = slice(h * NSA_DK, (h + 1) * NSA_DK)
        qo_ref[:, sl] = _lane_rms(q_ref[:, sl], qg).astype(bf16)
    kw_ = NSA_KV_GROUPS * NSA_DK
    tok = (lax.broadcasted_iota(jnp.int32, (tm, LANE), 0) + pl.program_id(0) * tm) % seq
    onehot = jnp.where(lax.broadcasted_iota(jnp.int32, (tm, LANE), 1) == tok // SEL_BLOCK, 1.0, 0.0).astype(bf16)
    for g in range(NSA_KV_GROUPS):
        sl = slice(g * NSA_DK, (g + 1) * NSA_DK)
        kao_ref[:, 2 * g * NSA_DK:(2 * g + 1) * NSA_DK] = _lane_rms(ks_ref[:, sl], kg).astype(bf16)
        kao_ref[:, (2 * g + 1) * NSA_DK:(2 * g + 2) * NSA_DK] = onehot
        kwo_ref[:, sl] = _lane_rms(kw_ref[:, sl], kg).astype(bf16)
    vso_ref[...] = ks_ref[:, kw_:].astype(bf16)
    kwo_ref[:, kw_:] = kw_ref[:, kw_:].astype(bf16)


def _nsa_prep(proj, offs, q_gain, k_gain, seq, *, tm=512):
    m = proj.shape[0]
    w = NSA_HEADS * NSA_DK
    assert seq // SEL_BLOCK <= LANE and NSA_DK == LANE, "selection blocks must fit one 128-lane one-hot"
    blk = lambda name: pl.BlockSpec((tm, w), lambda i, c=offs[name] // w: (i, c))
    widths = (w, 2 * NSA_KV_GROUPS * NSA_DK, NSA_KV_GROUPS * NSA_DV, w)
    return pl.pallas_call(
        functools.partial(_nsa_prep_kernel, seq),
        grid=(m // tm,),
        in_specs=[blk("b_q"), blk("b_ks"), blk("b_kw"),
                  pl.BlockSpec((1, NSA_DK), lambda i: (0, 0)), pl.BlockSpec((1, NSA_DK), lambda i: (0, 0))],
        out_specs=[pl.BlockSpec((tm, wd), lambda i: (i, 0)) for wd in widths],
        out_shape=[jax.ShapeDtypeStruct((m, wd), bf16) for wd in widths],
        compiler_params=pltpu.CompilerParams(dimension_semantics=("parallel",),
                                             vmem_limit_bytes=V7X_VMEM_LIMIT_BYTES),
        name="nsa_prep",
    )(proj, proj, proj, q_gain, k_gain)


def _nsa_compress_kernel(x_ref, w_ref, pe_ref, kg_ref, o_ref, hi_ref):
    j = pl.program_id(1)
    nc = o_ref.shape[2]
    lo = jnp.zeros((nc, NSA_DK), f32)
    hi = jnp.zeros((nc, NSA_DK), f32)
    for l in range(CMP_STRIDE):
        rows = x_ref[pl.ds(l, nc, stride=CMP_STRIDE), :]
        lo += jnp.dot((rows + pe_ref[0, l:l + 1, :]).astype(bf16), w_ref[0, l], preferred_element_type=f32)
        hi += jnp.dot((rows + pe_ref[0, CMP_STRIDE + l:CMP_STRIDE + l + 1, :]).astype(bf16),
                      w_ref[0, CMP_STRIDE + l], preferred_element_type=f32)
    hi_ref[pl.ds(0, nc), :] = hi
    hi_ref[pl.ds(nc, 8), :] = jnp.zeros((8, NSA_DK), f32)
    c = lo + hi_ref[pl.ds(1, nc), :]
    normed = _lane_rms(c, kg_ref[...])
    o_ref[0, 0] = jnp.where(j < NSA_KV_GROUPS, normed, c).astype(bf16)


def _nsa_compress(proj, offs, w_stack, pe_stack, k_gain, bsz, seq):
    nc = seq // CMP_STRIDE
    c0 = offs["b_kc"] // NSA_DK
    return pl.pallas_call(
        _nsa_compress_kernel,
        grid=(bsz, 2 * NSA_KV_GROUPS),
        in_specs=[
            pl.BlockSpec((seq, NSA_DK), lambda b, j: (b, c0 + j)),
            pl.BlockSpec((1, CMP_BLOCK, NSA_DK, NSA_DK), lambda b, j: (j // NSA_KV_GROUPS, 0, 0, 0)),
            pl.BlockSpec((1, CMP_BLOCK, NSA_DK), lambda b, j: (j // NSA_KV_GROUPS, 0, 0)),
            pl.BlockSpec((1, NSA_DK), lambda b, j: (0, 0)),
        ],
        out_specs=pl.BlockSpec((1, 1, nc, NSA_DK), lambda b, j: (b, j, 0, 0)),
        out_shape=jax.ShapeDtypeStruct((bsz, 2 * NSA_KV_GROUPS, nc, NSA_DK), bf16),
        scratch_shapes=[pltpu.VMEM((nc + 8, NSA_DK), f32)],
        compiler_params=pltpu.CompilerParams(dimension_semantics=("parallel", "parallel"),
                                             vmem_limit_bytes=V7X_VMEM_LIMIT_BYTES),
        name="nsa_compress",
    )(proj, w_stack, pe_stack, k_gain)


def _split3(x):
    hi = x.astype(bf16)
    r1 = x - hi.astype(f32)
    mid = r1.astype(bf16)
    lo = (r1 - mid.astype(f32)).astype(bf16)
    return hi, mid, lo


def _nsa_attn_kernel(tbl_ref, q_ref, cmp_ref, ka_ref, vs_ref, kwv_ref, mt_ref, bs_ref, bc_ref, gl0_ref, gl1_ref,
                     o_ref, strip_ref, sc_ref, score_ref, stage_ref, far_ref):
    qi = pl.program_id(1)
    nc = cmp_ref.shape[2]
    qb = Q_BLOCK
    groups = range(NSA_KV_GROUPS)
    gl_refs = (gl0_ref, gl1_ref)

    @pl.when(qi == 0)
    def _():
        def lookup(bucket):
            vals = [jnp.zeros(bucket.shape, f32) for _ in range(NSA_HEADS)]
            for k in range(REL_BUCKETS):
                eq = bucket == k
                vals = [jnp.where(eq, tbl_ref[h, k], v) for h, v in enumerate(vals)]
            return vals

        for c in range(NEAR_TILES):
            sl = slice(c * qb, (c + 1) * qb)
            for h, v in enumerate(lookup(bs_ref[:, sl])):
                strip_ref[h, :, sl] = v
        for h, v in enumerate(lookup(bc_ref[...])):
            d = v - tbl_ref[h, REL_BUCKETS - 1]
            hi = d.astype(bf16)
            sc_ref[h, 0] = hi
            sc_ref[h, 1] = (d - hi.astype(f32)).astype(bf16)

    far_bias = [tbl_ref[h, REL_BUCKETS - 1] for h in range(NSA_HEADS)]
    q = q_ref[...]
    q2 = [jnp.concatenate([q[:, (NSA_HG * g + hh) * NSA_DK:(NSA_HG * g + hh + 1) * NSA_DK]
                           for hh in range(NSA_HG)], axis=0) for g in groups]

    r2 = lax.broadcasted_iota(jnp.int32, (NSA_HG * qb, qb), 0) % qb
    c2 = lax.broadcasted_iota(jnp.int32, (NSA_HG * qb, qb), 1)

    def strip_bias(g, w):
        sl = slice(w * qb, (w + 1) * qb)
        return jnp.concatenate([strip_ref[NSA_HG * g + hh, :, sl] for hh in range(NSA_HG)], axis=0)

    def key_rows(w):
        kt = qi - (NEAR_TILES - 1) + w
        return kt, pl.ds(pl.multiple_of(jnp.maximum(kt, 0) * qb, qb), qb)

    def ka(g, rows):
        return ka_ref[rows, g * (NSA_DK + LANE):(g + 1) * (NSA_DK + LANE)]

    def vs(g, rows):
        return vs_ref[rows, g * NSA_DV:(g + 1) * NSA_DV]

    kw_cols = NSA_KV_GROUPS * NSA_DK

    o_win = []
    for g in groups:
        m_w = jnp.full((NSA_HG * qb, 1), NEG, f32)
        for w in range(NEAR_TILES - WIN_TILES, NEAR_TILES):
            kt, rows = key_rows(w)
            s = _dot_nt(q2[g], kwv_ref[rows, g * NSA_DK:(g + 1) * NSA_DK]) + strip_bias(g, w)
            if w == NEAR_TILES - WIN_TILES:
                valid = (c2 > r2) & (kt >= 0)
            elif w == NEAR_TILES - 1:
                valid = c2 <= r2
            else:
                valid = jnp.broadcast_to(kt >= 0, c2.shape)
            s = jnp.where(valid, s, NEG)
            stage_ref[g, w] = s
            m_w = jnp.maximum(m_w, jnp.max(s, axis=-1, keepdims=True))
        den = jnp.zeros((NSA_HG * qb, 1), f32)
        acc_w = jnp.zeros((NSA_HG * qb, NSA_DV), f32)
        for w in range(NEAR_TILES - WIN_TILES, NEAR_TILES):
            _, rows = key_rows(w)
            e = jnp.exp(stage_ref[g, w] - m_w)
            den = den + jnp.sum(e, axis=-1, keepdims=True)
            acc_w = acc_w + jnp.dot(e.astype(bf16), kwv_ref[rows, kw_cols + g * NSA_DV:kw_cols + (g + 1) * NSA_DV],
                                    preferred_element_type=f32)
        o_win.append(acc_w / den)

    r_i = lax.broadcasted_iota(jnp.int32, (qb, nc), 0)
    n_i = lax.broadcasted_iota(jnp.int32, (qb, nc), 1)
    ok_c = CMP_STRIDE * n_i <= qb * qi + r_i - (CMP_BLOCK - 1)
    shift = jnp.where(r_i + n_i == CMP_PER_Q * qi + CMP_PER_Q - 1, 1.0, 0.0).astype(bf16)
    mt = mt_ref[...]
    o_cmp, keys = [], []
    nblk = mt.shape[0]
    j_i = lax.broadcasted_iota(jnp.int32, (nblk, qb), 0)
    l_i = lax.broadcasted_iota(jnp.int32, (nblk, qb), 1)
    cur = SEL_PER_Q * qi + l_i // SEL_BLOCK
    ok_s = j_i <= cur
    forced = ok_s & ((j_i == 0) | (j_i >= cur - 1))
    for g in groups:
        kc = cmp_ref[0, g]
        vc = cmp_ref[0, NSA_KV_GROUPS + g]
        psum = jnp.zeros((qb, nc), f32)
        for hh in range(NSA_HG):
            h = NSA_HG * g + hh
            s = _dot_nt(q2[g][hh * qb:(hh + 1) * qb], kc)
            bias = (far_bias[h] + jnp.dot(sc_ref[h, 0], shift, preferred_element_type=f32)
                    + jnp.dot(sc_ref[h, 1], shift, preferred_element_type=f32))
            s = jnp.where(ok_c, s + bias, NEG)
            e = jnp.exp(s - jnp.max(s, axis=-1, keepdims=True))
            p = jnp.where(ok_c, e / jnp.sum(e, axis=-1, keepdims=True), 0.0)
            o_cmp.append(jnp.dot(p.astype(bf16), vc, preferred_element_type=f32))
            psum += p

        imp_t = sum(_dot_nt(mt, part) for part in _split3(psum))
        score = jnp.where(forced, FORCE_SCORE, jnp.where(ok_s, imp_t, -FORCE_SCORE))
        key = pltpu.bitcast(score, jnp.int32)
        score_ref[g] = key
        keys.append(key)

    half = nblk // 2

    def rank_half(lo):
        key_h = [k[lo:lo + half] for k in keys]
        key_h_m1 = [k - 1 for k in key_h]
        j_h = j_i[lo:lo + half]

        def rank_body(i, cnts):
            cnts = list(cnts)
            for u in range(SEL_PER_Q):
                jp = SEL_PER_Q * i + u
                later = j_h > jp
                for g in groups:
                    row = score_ref[g, pl.ds(jp, 1), :]
                    cnts[g] = cnts[g] + jnp.where(row > jnp.where(later, key_h_m1[g], key_h[g]), 1, 0)
            return tuple(cnts)

        return lax.fori_loop(0, qi + 1, rank_body,
                             tuple(jnp.zeros((half, qb), jnp.int32) for _ in groups))

    cnt_hi = lax.cond(SEL_PER_Q * qi + SEL_PER_Q > half, lambda: rank_half(half),
                      lambda: tuple(jnp.full((half, qb), N_SELECT, jnp.int32) for _ in groups))
    cnt_lo = rank_half(0)
    q_aug = []
    for g in groups:
        cnt = jnp.concatenate([cnt_lo[g], cnt_hi[g]], axis=0)
        pen = jnp.where(cnt < N_SELECT, 0.0, MASK_PEN).T.astype(bf16)
        q_aug.append(jnp.concatenate([q2[g], jnp.concatenate([pen] * NSA_HG, axis=0)], axis=1))

    row2 = lax.broadcasted_iota(jnp.int32, (NSA_HG * qb, 1), 0)
    far_col = [jnp.where(row2 < qb, far_bias[NSA_HG * g], far_bias[NSA_HG * g + 1]) for g in groups]
    n_far_keys = qb * jnp.maximum(qi - (NEAR_TILES - 1), 0)
    far_step = SEL_FAR_SUB * SEL_FAR_TILE
    c_far = lax.broadcasted_iota(jnp.int32, (NSA_HG * qb, SEL_FAR_TILE), 1)
    n_far_steps = (n_far_keys + far_step - 1) // far_step

    def far_rows(kt, sub):
        k0 = kt * far_step + sub * SEL_FAR_TILE
        return k0, pl.ds(pl.multiple_of(k0, SEL_FAR_TILE), SEL_FAR_TILE)

    def far_scores(g, kt):
        m_raw = jnp.full((NSA_HG * qb, 1), MASK_PEN, f32)
        for sub in range(SEL_FAR_SUB):
            k0, rows = far_rows(kt, sub)
            s = _dot_nt(q_aug[g], ka(g, rows)) + far_col[g]
            s = jnp.where(c_far < n_far_keys - k0, s, MASK_PEN)
            far_ref[g, kt % 2, :, sub * SEL_FAR_TILE:(sub + 1) * SEL_FAR_TILE] = s
            m_raw = jnp.maximum(m_raw, jnp.max(s, axis=-1, keepdims=True))
        return m_raw

    m_raw0 = [far_scores(g, 0) for g in groups]

    m_near = []
    for g in groups:
        m_n = jnp.full((NSA_HG * qb, 1), NEG, f32)
        for w in range(NEAR_TILES):
            kt, rows = key_rows(w)
            s = _dot_nt(q_aug[g], ka(g, rows)) + strip_bias(g, w)
            if w == NEAR_TILES - 1:
                s = jnp.where(c2 <= r2, s, NEG)
            else:
                s = jnp.where(kt >= 0, s, NEG)
            stage_ref[g, w] = s
            m_n = jnp.maximum(m_n, jnp.max(s, axis=-1, keepdims=True))
        m_near.append(m_n)

    def far_loop(g):
        def far_body(kt, carry):
            m_old, l_old, acc_old, m_raw = carry
            ss = [far_ref[g, kt % 2, :, sub * SEL_FAR_TILE:(sub + 1) * SEL_FAR_TILE] for sub in range(SEL_FAR_SUB)]
            m_new = jnp.maximum(m_old, m_raw)
            ps = [jnp.exp(s - m_new) for s in ss]
            alpha = jnp.exp(m_old - m_new)
            l_new = alpha * l_old + sum(jnp.sum(p, axis=-1, keepdims=True) for p in ps)
            acc_new = alpha * acc_old + sum(
                jnp.dot(p.astype(bf16), vs(g, far_rows(kt, sub)[1]), preferred_element_type=f32)
                for sub, p in enumerate(ps))
            m_raw_next = far_scores(g, jnp.minimum(kt + 1, n_far_steps - 1))
            return m_new, l_new, acc_new, m_raw_next

        return lax.fori_loop(
            0, n_far_steps, far_body,
            (jnp.full((NSA_HG * qb, 1), MASK_PEN, f32), jnp.zeros((NSA_HG * qb, 1), f32),
             jnp.zeros((NSA_HG * qb, NSA_DV), f32), m_raw0[g]))

    far = [far_loop(g) for g in groups]

    for g in groups:
        m_far, l_far, acc_far, _ = far[g]
        m_sel = jnp.maximum(m_far, m_near[g])
        alpha = jnp.exp(m_far - m_sel)
        l_sel = alpha * l_far
        acc_sel = alpha * acc_far
        for w in range(NEAR_TILES):
            _, rows = key_rows(w)
            e = jnp.exp(stage_ref[g, w] - m_sel)
            l_sel = l_sel + jnp.sum(e, axis=-1, keepdims=True)
            acc_sel = acc_sel + jnp.dot(e.astype(bf16), vs(g, rows), preferred_element_type=f32)
        o_sel = acc_sel / l_sel

        gate = jax.nn.sigmoid(gl_refs[g][...])
        for hh in range(NSA_HG):
            h = NSA_HG * g + hh
            rs = slice(hh * qb, (hh + 1) * qb)
            o_ref[:, h * NSA_DV:(h + 1) * NSA_DV] = (gate[:, 3 * hh:3 * hh + 1] * o_cmp[h]
                                                     + gate[:, 3 * hh + 1:3 * hh + 2] * o_sel[rs]
                                                     + gate[:, 3 * hh + 2:3 * hh + 3] * o_win[g][rs])


def _rel_bucket(dist):
    n = jnp.maximum(dist, 0)
    nf = jnp.maximum(n, 1).astype(f32)
    large = REL_MAX_EXACT + (jnp.log(nf / REL_MAX_EXACT) / math.log(REL_MAX_DIST / REL_MAX_EXACT)
                             * (REL_BUCKETS - REL_MAX_EXACT)).astype(jnp.int32)
    large = jnp.minimum(large, REL_BUCKETS - 1)
    return jnp.where(n < REL_MAX_EXACT, n, large)


def _nsa_constants(seq):
    nc = seq // CMP_STRIDE
    r = jnp.arange(Q_BLOCK)[:, None]
    bs = _rel_bucket(STRIP_D0 + r - jnp.arange(STRIP_W)[None, :])
    bc = _rel_bucket(r + CMP_STRIDE * jnp.arange(LANE)[None, :] - (CMP_STRIDE * (CMP_PER_Q - 1) + CMP_BLOCK - 1))
    ratio = SEL_BLOCK // CMP_STRIDE
    mt = np.zeros((LANE, nc), np.float32)
    for j in range(LANE):
        for i, wt in enumerate(SEL_OVERLAP_WEIGHTS):
            mcol = ratio * j - 1 + i
            if 0 <= mcol < nc:
                mt[j, mcol] = wt
    return bs.astype(jnp.int32), bc.astype(jnp.int32), jnp.asarray(mt, bf16)


def _nsa_attention(qn, cmp_kv, kaug, vsel, kwvw, proj, offs, tbl, consts, bsz, seq):
    bs, bc, mt = consts
    assert seq % (SEL_FAR_SUB * SEL_FAR_TILE) == 0, "far steps must tile the sequence"
    m = bsz * seq
    n_qb = seq // Q_BLOCK
    nc = seq // CMP_STRIDE
    g_ = NSA_KV_GROUPS
    assert g_ == 2
    resident = lambda width: pl.BlockSpec((seq, width), lambda b, i: (b, 0))
    const2 = lambda shape: pl.BlockSpec(shape, lambda b, i: (0, 0))
    gcol = offs["b_g"] // LANE
    gate = lambda g: pl.BlockSpec((Q_BLOCK, LANE), lambda b, i: (b * n_qb + i, gcol + g))
    rows2 = NSA_HG * Q_BLOCK
    return pl.pallas_call(
        _nsa_attn_kernel,
        grid=(bsz, n_qb),
        in_specs=[
            pl.BlockSpec(memory_space=pltpu.SMEM),
            pl.BlockSpec((Q_BLOCK, NSA_HEADS * NSA_DK), lambda b, i: (b * n_qb + i, 0)),
            pl.BlockSpec((1, 2 * g_, nc, NSA_DK), lambda b, i: (b, 0, 0, 0)),
            resident(g_ * (NSA_DK + LANE)), resident(g_ * NSA_DV), resident(g_ * (NSA_DK + NSA_DV)),
            const2((LANE, nc)), const2((Q_BLOCK, STRIP_W)), const2((Q_BLOCK, LANE)),
            gate(0), gate(1),
        ],
        out_specs=pl.BlockSpec((Q_BLOCK, NSA_HEADS * NSA_DV), lambda b, i: (b * n_qb + i, 0)),
        out_shape=jax.ShapeDtypeStruct((m, NSA_HEADS * NSA_DV), f32),
        scratch_shapes=[
            pltpu.VMEM((NSA_HEADS, Q_BLOCK, STRIP_W), f32),
            pltpu.VMEM((NSA_HEADS, 2, Q_BLOCK, LANE), bf16),
            pltpu.VMEM((g_, LANE, Q_BLOCK), jnp.int32),
            pltpu.VMEM((g_, NEAR_TILES, rows2, Q_BLOCK), f32),
            pltpu.VMEM((g_, 2, rows2, SEL_FAR_SUB * SEL_FAR_TILE), f32),
        ],
        compiler_params=pltpu.CompilerParams(dimension_semantics=("arbitrary", "arbitrary"),
                                             vmem_limit_bytes=V7X_VMEM_LIMIT_BYTES),
        name="nsa_attention",
    )(tbl, qn, cmp_kv, kaug, vsel, kwvw, mt, bs, bc, proj, proj)


def _nsa(proj, offs, q_norm, k_norm, pe_k, pe_v, w_ck, w_cv, rel_table, consts, bsz, seq):
    q_gain = q_norm.reshape(1, NSA_DK)
    k_gain = k_norm.reshape(1, NSA_DK)
    qn, kaug, vsel, kwvw = _nsa_prep(proj, offs, q_gain, k_gain, seq)
    cmp_kv = _nsa_compress(proj, offs, jnp.stack([w_ck, w_cv]).astype(bf16), jnp.stack([pe_k, pe_v]), k_gain,
                           bsz, seq)
    return _nsa_attention(qn, cmp_kv, kaug, vsel, kwvw, proj, offs, rel_table.T, consts, bsz, seq)


SMALL_A_LANE = 0
SMALL_B_LANE = GDN_HEADS
SMALL_DT_LANE = 2 * GDN_HEADS
CONV_HALO = 8


def _softplus(x):
    return jnp.maximum(x, 0.0) + jnp.log1p(jnp.exp(-jnp.abs(x)))


def _exact_dot(a_f32, b_01):
    return sum(jnp.dot(part, b_01, preferred_element_type=f32) for part in _split3(a_f32))


def _exact_dot_left(b_01, a_f32):
    return sum(jnp.dot(b_01, part, preferred_element_type=f32) for part in _split3(a_f32))


def _lane_vec(values, lane0):
    return jnp.zeros((1, LANE), f32).at[0, lane0:lane0 + values.shape[0]].set(values.astype(f32))


def _causal_conv_silu(x_ref, tail_ref, buf_ref, w_ref, bias):
    rows = x_ref.shape[0]
    buf_ref[pl.ds(0, CONV_HALO), :] = tail_ref[...]
    buf_ref[pl.ds(CONV_HALO, rows), :] = x_ref[...]
    tail_ref[...] = x_ref[pl.ds(rows - CONV_HALO, CONV_HALO), :]
    acc = bias
    for k in range(CONV_K):
        acc = acc + w_ref[k:k + 1, :] * buf_ref[pl.ds(CONV_HALO - (CONV_K - 1) + k, rows), :]
    return acc * jax.nn.sigmoid(acc)


SSM_PAIR = LANE // SSM_HEAD_DIM
SSM_HEADS_PER_GROUP = SSM_HEADS // SSM_GROUPS


def _ssd_kernel(xbc_ref, z_ref, sm_ref, cw_ref, cb_ref, dtb_ref, alog_ref, dvec_ref, ng_ref, tri_ref, exp_ref,
                o_ref, tail_ref, buf_ref, state_ref):
    L = SSM_CHUNK
    N = SSM_STATE

    xbc = _causal_conv_silu(xbc_ref, tail_ref, buf_ref, cw_ref, cb_ref[...])
    xs = xbc[:, :SSM_INNER]
    dt = _softplus(sm_ref[...] + dtb_ref[...])
    a = dt * (-jnp.exp(alog_ref[...]))
    tri = tri_ref[...]
    ac = _exact_dot_left(tri, a)
    ac_t = ac.T
    eac = jnp.exp(ac)
    w_t = jnp.exp(ac_t[:, L - 1:L] - ac_t)
    chunk_decay = jnp.exp(ac[L - 1:L, :])
    xdt = xs * _exact_dot(dt, exp_ref[...])

    row = lax.broadcasted_iota(jnp.int32, (L, L), 0)
    col = lax.broadcasted_iota(jnp.int32, (L, L), 1)
    causal = row >= col
    lane = lax.broadcasted_iota(jnp.int32, (L, LANE), 1)
    first_half = lane < SSM_HEAD_DIM

    ys = []
    for g in range(SSM_GROUPS):
        bg = xbc[:, SSM_INNER + g * N:SSM_INNER + (g + 1) * N]
        cg = xbc[:, SSM_INNER + SSM_GROUPS * N + g * N:SSM_INNER + SSM_GROUPS * N + (g + 1) * N]
        cb = _dot_nt(cg.astype(bf16), bg.astype(bf16))
        bg_t = bg.T
        for i in range(g * SSM_HEADS_PER_GROUP // SSM_PAIR, (g + 1) * SSM_HEADS_PER_GROUP // SSM_PAIR):
            xdt_pair = xdt[:, i * LANE:(i + 1) * LANE].astype(bf16)
            outs = []
            for hh in range(SSM_PAIR):
                h = SSM_PAIR * i + hh
                c = SMALL_DT_LANE + h
                seg = ac[:, c:c + 1] - ac_t[c:c + 1, :]
                sc = (cb * jnp.exp(jnp.where(causal, seg, NEG))).astype(bf16)
                y = jnp.dot(sc, xdt_pair, preferred_element_type=f32)
                s_in = state_ref[h]
                y += jnp.dot((cg * eac[:, c:c + 1]).astype(bf16), s_in.astype(bf16), preferred_element_type=f32)
                st = jnp.dot((bg_t * w_t[c:c + 1, :]).astype(bf16), xdt_pair, preferred_element_type=f32)
                state_ref[h] = s_in * chunk_decay[:, c:c + 1] + st
                outs.append(y)
            y_pair = jnp.where(first_half, outs[0], outs[1])
            sl = slice(i * LANE, (i + 1) * LANE)
            y_pair = y_pair + xs[:, sl] * dvec_ref[:, sl]
            z = z_ref[:, sl]
            ys.append(y_pair * (z * jax.nn.sigmoid(z)))

    per_group = len(ys) // SSM_GROUPS
    gw = SSM_INNER // SSM_GROUPS
    for g in range(SSM_GROUPS):
        tiles = ys[g * per_group:(g + 1) * per_group]
        ms = sum(jnp.sum(t * t, axis=-1, keepdims=True) for t in tiles) / gw
        scale = lax.rsqrt(ms + EPS)
        for k, t in enumerate(tiles):
            sl = slice(g * gw + k * LANE, g * gw + (k + 1) * LANE)
            o_ref[:, sl] = (t * scale * ng_ref[:, sl]).astype(o_ref.dtype)


GDN_STEP_ROWS = 128
GDN_QK_W = GDN_HEADS * GDN_DK
GDN_CONV_CH = GDN_HEADS * (2 * GDN_DK + GDN_DV)


def _split2(x):
    hi = x.astype(bf16)
    return hi, (x - hi.astype(f32)).astype(bf16)


def _dot_split(a_parts, b_parts):
    (ah, al), (bh, bl) = a_parts, b_parts
    return (jnp.dot(ah, bh, preferred_element_type=f32) + jnp.dot(ah, bl, preferred_element_type=f32)
            + jnp.dot(al, bh, preferred_element_type=f32))


def _unit_lower_solve_many(a_list, rhs_list):
    n = a_list[0].shape[0]
    ps = [_split2(-a) for a in a_list]
    xs = [rhs + _dot_split(p, _split2(rhs)) for p, rhs in zip(ps, rhs_list)]
    k = 2
    while k < n:
        ps = [_split2(_dot_split(p, p)) for p in ps]
        xs = [x + _dot_split(p, _split2(x)) for p, x in zip(ps, xs)]
        k *= 2
    return xs


def _gdn_kernel(qkv_ref, z_ref, sm_ref, cw_ref, dtb_ref, alog_ref, ng_ref, tri_ref, o_ref, tail_ref, buf_ref,
                state_ref):
    C = GDN_CHUNK

    qkv = _causal_conv_silu(qkv_ref, tail_ref, buf_ref, cw_ref, 0.0)
    sm = sm_ref[...]
    beta = jax.nn.sigmoid(sm)
    g_log = -jnp.exp(alog_ref[...]) * _softplus(sm + dtb_ref[...])
    gc = _exact_dot_left(tri_ref[...], g_log)
    gc_t = gc.T
    egc = jnp.exp(gc)

    row = lax.broadcasted_iota(jnp.int32, (C, C), 0)
    col = lax.broadcasted_iota(jnp.int32, (C, C), 1)
    causal = row >= col
    strict = row > col

    units = [(ch, h) for ch in range(GDN_STEP_ROWS // C) for h in range(GDN_HEADS)]
    a_list, rhs_list, pre = [], [], []
    for ch, h in units:
        rs = slice(ch * C, (ch + 1) * C)
        last = slice((ch + 1) * C - 1, (ch + 1) * C)
        q = qkv[rs, h * GDN_DK:(h + 1) * GDN_DK]
        k = qkv[rs, GDN_QK_W + h * GDN_DK:GDN_QK_W + (h + 1) * GDN_DK]
        v = qkv[rs, 2 * GDN_QK_W + h * GDN_DV:2 * GDN_QK_W + (h + 1) * GDN_DV]
        q = q * lax.rsqrt(jnp.sum(q * q, axis=-1, keepdims=True) + EPS) * (GDN_DK ** -0.5)
        k = k * lax.rsqrt(jnp.sum(k * k, axis=-1, keepdims=True) + EPS)
        ca = SMALL_A_LANE + h
        b = beta[rs, SMALL_B_LANE + h:SMALL_B_LANE + h + 1]
        g_col = gc[rs, ca:ca + 1]
        g_last = gc[last, ca:ca + 1]
        e_col = egc[rs, ca:ca + 1]
        decay = jnp.exp(jnp.where(causal, g_col - gc_t[ca:ca + 1, rs], NEG))
        kb = k * b
        k16 = k.astype(bf16)
        a_list.append(jnp.where(strict, _dot_nt(kb.astype(bf16), k16) * decay, 0.0))
        rhs_list.append(jnp.concatenate([v * b, kb * e_col], axis=1))
        attn = (_dot_nt(q.astype(bf16), k16) * decay).astype(bf16)
        q_dec = (q * e_col).astype(bf16)
        k_dec_t = (k * jnp.exp(g_last - g_col)).T.astype(bf16)
        pre.append((attn, q_dec, k_dec_t, jnp.exp(g_last)))

    sols = _unit_lower_solve_many(a_list, rhs_list)

    for (ch, h), sol, (attn, q_dec, k_dec_t, g_end) in zip(units, sols, pre):
        rs = slice(ch * C, (ch + 1) * C)
        u, w = sol[:, :GDN_DV], sol[:, GDN_DV:]
        s_in = state_ref[h]
        s16 = s_in.astype(bf16)
        v_new = u - jnp.dot(w.astype(bf16), s16, preferred_element_type=f32)
        v16 = v_new.astype(bf16)
        o = (jnp.dot(q_dec, s16, preferred_element_type=f32) + jnp.dot(attn, v16, preferred_element_type=f32))
        state_ref[h] = s_in * g_end + jnp.dot(k_dec_t, v16, preferred_element_type=f32)

        o = o * lax.rsqrt(jnp.mean(o * o, axis=-1, keepdims=True) + EPS) * ng_ref[...]
        z = z_ref[rs, h * GDN_DV:(h + 1) * GDN_DV]
        o_ref[rs, h * GDN_DV:(h + 1) * GDN_DV] = (o * (z * jax.nn.sigmoid(z))).astype(o_ref.dtype)


REC_ROWS = GDN_STEP_ROWS
assert REC_ROWS == SSM_CHUNK


def _rec_mixers_kernel(qkv_ref, za_ref, sm_ref, gcw_ref, gdtb_ref, galog_ref, gng_ref, gtri_ref,
                       xbc_ref, zc_ref, scw_ref, scb_ref, sdtb_ref, salog_ref, dvec_ref, sng_ref, stri_ref, exp_ref,
                       oa_ref, oc_ref, gtail_ref, gbuf_ref, gstate_ref, stail_ref, sbuf_ref, sstate_ref):
    @pl.when(pl.program_id(1) == 0)
    def _():
        for ref in (gtail_ref, gstate_ref, stail_ref, sstate_ref):
            ref[...] = jnp.zeros(ref.shape, f32)

    _gdn_kernel(qkv_ref, za_ref, sm_ref, gcw_ref, gdtb_ref, galog_ref, gng_ref, gtri_ref, oa_ref,
                gtail_ref, gbuf_ref, gstate_ref)
    _ssd_kernel(xbc_ref, zc_ref, sm_ref, scw_ref, scb_ref, sdtb_ref, salog_ref, dvec_ref, sng_ref, stri_ref,
                exp_ref, oc_ref, stail_ref, sbuf_ref, sstate_ref)


def _rec_mixers(proj, offs, gdn_conv, gdn_a_log, gdn_dt_bias, gdn_norm, ssm_conv_w, ssm_conv_b, ssm_dt_bias,
                ssm_a_log, ssm_d, ssm_norm, bsz, seq):
    m = bsz * seq
    R_ = REC_ROWS
    n_s = seq // R_
    idx = np.arange(R_)
    gtri = ((idx[:, None] >= idx[None, :]) & (idx[:, None] // GDN_CHUNK == idx[None, :] // GDN_CHUNK))
    stri = np.tril(np.ones((R_, R_), np.float32))
    expand = np.zeros((LANE, SSM_INNER), np.float32)
    for h in range(SSM_HEADS):
        expand[SMALL_DT_LANE + h, h * SSM_HEAD_DIM:(h + 1) * SSM_HEAD_DIM] = 1.0
    gz = GDN_HEADS * GDN_DV
    tok = lambda w, col: pl.BlockSpec((R_, w), lambda b, c: (b * n_s + c, col // w))
    const = lambda shape: pl.BlockSpec(shape, lambda b, c: (0, 0))
    return pl.pallas_call(
        _rec_mixers_kernel,
        grid=(bsz, n_s),
        in_specs=[
            tok(GDN_CONV_CH, offs["a_q"]), tok(gz, offs["a_z"]), tok(LANE, offs["small"]),
            const((CONV_K, GDN_CONV_CH)), const((1, LANE)), const((1, LANE)), const((1, GDN_DV)), const((R_, R_)),
            tok(SSM_CONV_CH, offs["c_xbc"]), tok(SSM_INNER, offs["c_z"]),
            const((CONV_K, SSM_CONV_CH)), const((1, SSM_CONV_CH)), const((1, LANE)), const((1, LANE)),
            const((1, SSM_INNER)), const((1, SSM_INNER)), const((R_, R_)), const((LANE, SSM_INNER)),
        ],
        out_specs=[pl.BlockSpec((R_, gz), lambda b, c: (b * n_s + c, 0)),
                   pl.BlockSpec((R_, SSM_INNER), lambda b, c: (b * n_s + c, 0))],
        out_shape=[jax.ShapeDtypeStruct((m, gz), bf16), jax.ShapeDtypeStruct((m, SSM_INNER), bf16)],
        scratch_shapes=[
            pltpu.VMEM((CONV_HALO, GDN_CONV_CH), f32),
            pltpu.VMEM((CONV_HALO + R_, GDN_CONV_CH), f32),
            pltpu.VMEM((GDN_HEADS, GDN_DK, GDN_DV), f32),
            pltpu.VMEM((CONV_HALO, SSM_CONV_CH), f32),
            pltpu.VMEM((CONV_HALO + R_, SSM_CONV_CH), f32),
            pltpu.VMEM((SSM_HEADS, SSM_STATE, LANE), f32),
        ],
        compiler_params=pltpu.CompilerParams(dimension_semantics=("arbitrary", "arbitrary"),
                                             vmem_limit_bytes=V7X_VMEM_LIMIT_BYTES),
        name="recurrent_mixers",
    )(proj, proj, proj, gdn_conv, _lane_vec(gdn_dt_bias, SMALL_A_LANE), _lane_vec(gdn_a_log, SMALL_A_LANE),
      gdn_norm.reshape(1, -1), jnp.asarray(gtri.astype(np.float32), bf16),
      proj, proj, ssm_conv_w, ssm_conv_b.reshape(1, -1), _lane_vec(ssm_dt_bias, SMALL_DT_LANE),
      _lane_vec(ssm_a_log, SMALL_DT_LANE), jnp.repeat(ssm_d, SSM_HEAD_DIM).reshape(1, -1),
      ssm_norm.reshape(1, -1), jnp.asarray(stri, bf16), jnp.asarray(expand, bf16))


_IN_NAMES = ("a_q", "a_k", "a_v", "a_z", "a_a", "a_b", "b_q", "b_kc", "b_vc", "b_ks", "b_vs", "b_kw", "b_vw",
             "b_g", "c_z", "c_xbc", "c_dt", "m_gate")


def _in_sizes(d_model):
    return (GDN_HEADS * GDN_DK, GDN_HEADS * GDN_DK, GDN_HEADS * GDN_DV, GDN_HEADS * GDN_DV, GDN_HEADS, GDN_HEADS,
            NSA_HEADS * NSA_DK, NSA_KV_GROUPS * NSA_DK, NSA_KV_GROUPS * NSA_DV, NSA_KV_GROUPS * NSA_DK,
            NSA_KV_GROUPS * NSA_DV, NSA_KV_GROUPS * NSA_DK, NSA_KV_GROUPS * NSA_DV, 3 * NSA_HEADS,
            SSM_INNER, SSM_CONV_CH, SSM_HEADS, 3 * d_model)


_BIG_ORDER = ("m_gate", "a_q", "a_k", "a_v", "c_xbc", "c_z", "a_z", "b_q", "b_kc", "b_vc", "b_ks", "b_vs",
              "b_kw", "b_vw")
_SMALL_ORDER = ("a_a", "a_b", "c_dt")


def _in_layout(d_model, tn=1024):
    sizes = dict(zip(_IN_NAMES, _in_sizes(d_model)))
    src_off = dict(zip(_IN_NAMES, np.cumsum((0,) + _in_sizes(d_model))[:-1].tolist()))
    cols, offs = [], {}

    def pad_to_lane():
        cols.extend([-1] * (_round_up(len(cols), LANE) - len(cols)))

    for name in _BIG_ORDER:
        offs[name] = len(cols)
        cols.extend(range(src_off[name], src_off[name] + sizes[name]))
    offs["small"] = len(cols)
    for name in _SMALL_ORDER:
        offs[name] = len(cols)
        cols.extend(range(src_off[name], src_off[name] + sizes[name]))
    pad_to_lane()
    offs["b_g"] = len(cols)
    per_group = 3 * NSA_HG
    for g in range(NSA_KV_GROUPS):
        cols.extend(range(src_off["b_g"] + g * per_group, src_off["b_g"] + (g + 1) * per_group))
        pad_to_lane()
    n_pad = _round_up(len(cols), tn)
    cols.extend([-1] * (n_pad - len(cols)))
    return np.asarray(cols, np.int32), offs, sizes, n_pad


def _col_runs(cols):
    runs, start = [], 0
    for i in range(1, len(cols) + 1):
        same_run = i < len(cols) and ((cols[i] < 0 and cols[i - 1] < 0)
                                      or (cols[i - 1] >= 0 and cols[i] == cols[i - 1] + 1))
        if not same_run:
            runs.append((start, None if cols[start] < 0 else int(cols[start]), i - start))
            start = i
    return runs


def _cast_cols_kernel(runs_per_out, w_ref, *o_refs):
    for runs, o_ref in zip(runs_per_out, o_refs):
        for dst, src, size in runs:
            if src is None:
                o_ref[:, dst:dst + size] = jnp.zeros((o_ref.shape[0], size), o_ref.dtype)
            else:
                o_ref[:, dst:dst + size] = w_ref[0, :, src:src + size].astype(o_ref.dtype)


def _cast_cols(w_stack, layer, runs_per_out, widths, *, tr=256):
    _, rows, n_src = w_stack.shape
    return pl.pallas_call(
        functools.partial(_cast_cols_kernel, runs_per_out),
        grid=(rows // tr,),
        in_specs=[pl.BlockSpec((1, tr, n_src), lambda i: (layer, i, 0))],
        out_specs=[pl.BlockSpec((tr, w), lambda i: (i, 0)) for w in widths],
        out_shape=[jax.ShapeDtypeStruct((rows, w), bf16) for w in widths],
        compiler_params=pltpu.CompilerParams(dimension_semantics=("parallel",),
                                             vmem_limit_bytes=V7X_VMEM_LIMIT_BYTES),
        name="weight_cast_cols",
    )(w_stack)


def _cast_rows_kernel(rows_src, w_ref, o_ref):
    tr = o_ref.shape[0]
    row = lax.broadcasted_iota(jnp.int32, o_ref.shape, 0) + pl.program_id(0) * tr
    o_ref[...] = jnp.where(row < rows_src, w_ref[0], 0.0).astype(o_ref.dtype)


def _cast_rows(w_stack, layer, rows_out, *, tr=512):
    _, rows_src, n = w_stack.shape
    return pl.pallas_call(
        functools.partial(_cast_rows_kernel, rows_src),
        grid=(rows_out // tr,),
        in_specs=[pl.BlockSpec((1, tr, n), lambda i: (layer, i, 0))],
        out_specs=pl.BlockSpec((tr, n), lambda i: (i, 0)),
        out_shape=jax.ShapeDtypeStruct((rows_out, n), bf16),
        compiler_params=pltpu.CompilerParams(dimension_semantics=("parallel",),
                                             vmem_limit_bytes=V7X_VMEM_LIMIT_BYTES),
        name="weight_cast_rows",
    )(w_stack)


def _prep_w_in(w_in, layer, cols):
    return _cast_cols(w_in, layer, [_col_runs(cols)], [len(cols)])[0]


def _prep_ffn(w_up, w_down, layer, tf=512):
    ff = w_down.shape[1]
    ffp = _round_up(ff, tf)
    half = lambda src: [(0, src, ff)] + ([(ff, None, ffp - ff)] if ffp > ff else [])
    wa, wb = _cast_cols(w_up, layer, [half(0), half(ff)], [ffp, ffp])
    return wa, wb, _cast_rows(w_down, layer, ffp)


def kernel(x, rel_table, g_ffn1, w_up1, w_down1, g_mix, w_in, gdn_conv, gdn_a_log, gdn_dt_bias, gdn_norm,
           nsa_q_norm, nsa_k_norm, nsa_pe_k, nsa_pe_v, nsa_w_ck, nsa_w_cv, ssm_conv_w, ssm_conv_b,
           ssm_dt_bias, ssm_a_log, ssm_d, ssm_norm, p_a, p_b, p_c, w_o, g_ffn2, w_up2, w_down2):
    bsz, seq, d = x.shape
    depth = w_in.shape[0]
    m = bsz * seq
    cols, offs, _, _ = _in_layout(d)
    x2 = x.reshape(m, d)
    nsa_consts = _nsa_constants(seq)

    for l in range(depth):
        wa, wb, wd = _prep_ffn(w_up1, w_down1, l)
        x2 = _ffn(x2, g_ffn1[l].reshape(1, d), wa, wb, wd)

        proj = _norm_matmul(x2, g_mix[l].reshape(1, d), _prep_w_in(w_in, l, cols))
        y_a, y_c = _rec_mixers(proj, offs, gdn_conv[l], gdn_a_log[l], gdn_dt_bias[l], gdn_norm[l], ssm_conv_w[l],
                               ssm_conv_b[l], ssm_dt_bias[l], ssm_a_log[l], ssm_d[l], ssm_norm[l], bsz, seq)
        y_b = _nsa(proj, offs, nsa_q_norm[l], nsa_k_norm[l], nsa_pe_k[l], nsa_pe_v[l], nsa_w_ck[l], nsa_w_cv[l],
                   rel_table, nsa_consts, bsz, seq)
        x2 = _merge(x2, y_a, y_b, y_c, proj, offs["m_gate"],
                    p_a[l].astype(bf16), p_b[l].astype(bf16), p_c[l].astype(bf16), w_o[l].astype(bf16))

        wa, wb, wd = _prep_ffn(w_up2, w_down2, l)
        x2 = _ffn(x2, g_ffn2[l].reshape(1, d), wa, wb, wd)
    return x2.reshape(bsz, seq, d)
```

```python
import functools
import math

import jax
import jax.numpy as jnp
import numpy as np
from jax import lax
from jax.experimental import pallas as pl
from jax.experimental.pallas import tpu as pltpu

EPS = 1e-6
CONV_K = 4

GDN_HEADS = 4
GDN_DK = 128
GDN_DV = 128
GDN_CHUNK = 64

NSA_HEADS = 4
NSA_KV_GROUPS = 2
NSA_DK = 128
NSA_DV = 128
CMP_BLOCK = 32
CMP_STRIDE = 16
SEL_BLOCK = 64
N_SELECT = 16
WINDOW = 512
Q_BLOCK = 128
SEL_OVERLAP_WEIGHTS = (1.0, 2.0, 2.0, 2.0, 1.0)
FORCE_SCORE = 1e4
NEG = -1e30

SSM_HEADS = 16
SSM_HEAD_DIM = 64
SSM_GROUPS = 2
SSM_STATE = 128
SSM_CHUNK = 128
SSM_INNER = SSM_HEADS * SSM_HEAD_DIM
SSM_CONV_CH = SSM_INNER + 2 * SSM_GROUPS * SSM_STATE

REL_BUCKETS = 32
REL_MAX_EXACT = 16
REL_MAX_DIST = 1024

V7X_VMEM_LIMIT_BYTES = 56 * 1024 * 1024
LANE = 128

bf16 = jnp.bfloat16
f32 = jnp.float32


def _round_up(n, m):
    return (n + m - 1) // m * m


def _ffn_kernel(x_ref, g_ref, wa_ref, wb_ref, wd_ref, o_ref, h_ref):
    j = pl.program_id(1)

    @pl.when(j == 0)
    def _():
        x = x_ref[...]
        h = x * lax.rsqrt(jnp.mean(x * x, axis=-1, keepdims=True) + EPS) * g_ref[...]
        h_ref[...] = h.astype(bf16)
        o_ref[...] = x

    h = h_ref[...]
    a = jnp.dot(h, wa_ref[...], preferred_element_type=f32)
    b = jnp.dot(h, wb_ref[...], preferred_element_type=f32)
    act = (0.5 * a * jax.nn.sigmoid(a) * b).astype(bf16)
    o_ref[...] += jnp.dot(act, wd_ref[...], preferred_element_type=f32)


def _ffn(x2, g, wa, wb, wd, *, tm=1024, tf=512):
    m, d = x2.shape
    ffp = wa.shape[1]
    return pl.pallas_call(
        _ffn_kernel,
        grid=(m // tm, ffp // tf),
        in_specs=[
            pl.BlockSpec((tm, d), lambda i, j: (i, 0)),
            pl.BlockSpec((1, d), lambda i, j: (0, 0)),
            pl.BlockSpec((d, tf), lambda i, j: (0, j)),
            pl.BlockSpec((d, tf), lambda i, j: (0, j)),
            pl.BlockSpec((tf, d), lambda i, j: (j, 0)),
        ],
        out_specs=pl.BlockSpec((tm, d), lambda i, j: (i, 0)),
        out_shape=jax.ShapeDtypeStruct((m, d), f32),
        scratch_shapes=[pltpu.VMEM((tm, d), bf16)],
        compiler_params=pltpu.CompilerParams(
            dimension_semantics=("parallel", "arbitrary"),
            vmem_limit_bytes=V7X_VMEM_LIMIT_BYTES),
        name="ffn_swiglu",
    )(x2, g, wa, wb, wd)


def _norm_matmul_kernel(x_ref, g_ref, w_ref, o_ref, h_ref):
    j = pl.program_id(1)

    @pl.when(j == 0)
    def _():
        x = x_ref[...]
        h = x * lax.rsqrt(jnp.mean(x * x, axis=-1, keepdims=True) + EPS) * g_ref[...]
        h_ref[...] = h.astype(bf16)

    o_ref[...] = jnp.dot(h_ref[...], w_ref[...], preferred_element_type=f32)


def _norm_matmul(x2, g, w, *, tm=1024, tn=1024):
    m, d = x2.shape
    n = w.shape[1]
    return pl.pallas_call(
        _norm_matmul_kernel,
        grid=(m // tm, n // tn),
        in_specs=[
            pl.BlockSpec((tm, d), lambda i, j: (i, 0)),
            pl.BlockSpec((1, d), lambda i, j: (0, 0)),
            pl.BlockSpec((d, tn), lambda i, j: (0, j)),
        ],
        out_specs=pl.BlockSpec((tm, tn), lambda i, j: (i, j)),
        out_shape=jax.ShapeDtypeStruct((m, n), f32),
        scratch_shapes=[pltpu.VMEM((tm, d), bf16)],
        compiler_params=pltpu.CompilerParams(
            dimension_semantics=("parallel", "arbitrary"),
            vmem_limit_bytes=V7X_VMEM_LIMIT_BYTES),
        name="norm_in_proj",
    )(x2, g, w)


def _merge_kernel(x_ref, ya_ref, yb_ref, yc_ref, ga_ref, gb_ref, gc_ref, pa_ref, pb_ref, pc_ref, wo_ref, o_ref):
    ma = jnp.dot(ya_ref[...].astype(bf16), pa_ref[...], preferred_element_type=f32)
    mb = jnp.dot(yb_ref[...].astype(bf16), pb_ref[...], preferred_element_type=f32)
    mc = jnp.dot(yc_ref[...].astype(bf16), pc_ref[...], preferred_element_type=f32)
    merged = (jax.nn.sigmoid(ga_ref[...]) * ma + jax.nn.sigmoid(gb_ref[...]) * mb
              + jax.nn.sigmoid(gc_ref[...]) * mc)
    o_ref[...] = x_ref[...] + jnp.dot(merged.astype(bf16), wo_ref[...], preferred_element_type=f32)


def _merge(x2, ya, yb, yc, gates, gate_col0, pa, pb, pc, wo, *, tm=256):
    m, d = x2.shape
    gb0 = gate_col0 // d
    const = dict(pipeline_mode=pl.Buffered(1))
    return pl.pallas_call(
        _merge_kernel,
        grid=(m // tm,),
        in_specs=[
            pl.BlockSpec((tm, d), lambda i: (i, 0)),
            pl.BlockSpec((tm, ya.shape[1]), lambda i: (i, 0)),
            pl.BlockSpec((tm, yb.shape[1]), lambda i: (i, 0)),
            pl.BlockSpec((tm, yc.shape[1]), lambda i: (i, 0)),
            pl.BlockSpec((tm, d), lambda i: (i, gb0)),
            pl.BlockSpec((tm, d), lambda i: (i, gb0 + 1)),
            pl.BlockSpec((tm, d), lambda i: (i, gb0 + 2)),
            pl.BlockSpec(pa.shape, lambda i: (0, 0), **const),
            pl.BlockSpec(pb.shape, lambda i: (0, 0), **const),
            pl.BlockSpec(pc.shape, lambda i: (0, 0), **const),
            pl.BlockSpec(wo.shape, lambda i: (0, 0), **const),
        ],
        out_specs=pl.BlockSpec((tm, d), lambda i: (i, 0)),
        out_shape=jax.ShapeDtypeStruct((m, d), f32),
        compiler_params=pltpu.CompilerParams(
            dimension_semantics=("parallel",),
            vmem_limit_bytes=V7X_VMEM_LIMIT_BYTES),
        name="merge_out_proj",
    )(x2, ya, yb, yc, gates, gates, gates, pa, pb, pc, wo)


NSA_HG = NSA_HEADS // NSA_KV_GROUPS
SEL_FAR_TILE = 512
SEL_FAR_SUB = 2
NEAR_TILES = 8
STRIP_W = NEAR_TILES * Q_BLOCK
STRIP_D0 = (NEAR_TILES - 1) * Q_BLOCK
WIN_TILES = WINDOW // Q_BLOCK + 1
CMP_PER_Q = Q_BLOCK // CMP_STRIDE
SEL_PER_Q = Q_BLOCK // SEL_BLOCK
MASK_PEN = -1e30


def _dot_nt(a, b):
    return lax.dot_general(a, b, (((1,), (1,)), ((), ())), preferred_element_type=f32)


def _lane_rms(x, gain):
    return x * lax.rsqrt(jnp.mean(x * x, axis=-1, keepdims=True) + EPS) * gain


def _nsa_prep_kernel(seq, q_ref, ks_ref, kw_ref, qg_ref, kg_ref, qo_ref, kao_ref, vso_ref, kwo_ref):
    tm = q_ref.shape[0]
    qg = qg_ref[...] * (NSA_DK ** -0.5)
    kg = kg_ref[...]
    for h in range(NSA_HEADS):
        sl = slice(h * NSA_DK, (h + 1) * NSA_DK)
        qo_ref[:, sl] = _lane_rms(q_ref[:, sl], qg).astype(bf16)
    kw_ = NSA_KV_GROUPS * NSA_DK
    tok = (lax.broadcasted_iota(jnp.int32, (tm, LANE), 0) + pl.program_id(0) * tm) % seq
    onehot = jnp.where(lax.broadcasted_iota(jnp.int32, (tm, LANE), 1) == tok // SEL_BLOCK, 1.0, 0.0).astype(bf16)
    for g in range(NSA_KV_GROUPS):
        sl = slice(g * NSA_DK, (g + 1) * NSA_DK)
        kao_ref[:, 2 * g * NSA_DK:(2 * g + 1) * NSA_DK] = _lane_rms(ks_ref[:, sl], kg).astype(bf16)
        kao_ref[:, (2 * g + 1) * NSA_DK:(2 * g + 2) * NSA_DK] = onehot
        kwo_ref[:, sl] = _lane_rms(kw_ref[:, sl], kg).astype(bf16)
    vso_ref[...] = ks_ref[:, kw_:].astype(bf16)
    kwo_ref[:, kw_:] = kw_ref[:, kw_:].astype(bf16)


def _nsa_prep(proj, offs, q_gain, k_gain, seq, *, tm=512):
    m = proj.shape[0]
    w = NSA_HEADS * NSA_DK
    assert seq // SEL_BLOCK <= LANE and NSA_DK == LANE, "selection blocks must fit one 128-lane one-hot"
    blk = lambda name: pl.BlockSpec((tm, w), lambda i, c=offs[name] // w: (i, c))
    widths = (w, 2 * NSA_KV_GROUPS * NSA_DK, NSA_KV_GROUPS * NSA_DV, w)
    return pl.pallas_call(
        functools.partial(_nsa_prep_kernel, seq),
        grid=(m // tm,),
        in_specs=[blk("b_q"), blk("b_ks"), blk("b_kw"),
                  pl.BlockSpec((1, NSA_DK), lambda i: (0, 0)), pl.BlockSpec((1, NSA_DK), lambda i: (0, 0))],
        out_specs=[pl.BlockSpec((tm, wd), lambda i: (i, 0)) for wd in widths],
        out_shape=[jax.ShapeDtypeStruct((m, wd), bf16) for wd in widths],
        compiler_params=pltpu.CompilerParams(dimension_semantics=("parallel",),
                                             vmem_limit_bytes=V7X_VMEM_LIMIT_BYTES),
        name="nsa_prep",
    )(proj, proj, proj, q_gain, k_gain)


def _nsa_compress_kernel(x_ref, w_ref, pe_ref, kg_ref, o_ref, hi_ref):
    j = pl.program_id(1)
    nc = o_ref.shape[2]
    lo = jnp.zeros((nc, NSA_DK), f32)
    hi = jnp.zeros((nc, NSA_DK), f32)
    for l in range(CMP_STRIDE):
        rows = x_ref[pl.ds(l, nc, stride=CMP_STRIDE), :]
        lo += jnp.dot((rows + pe_ref[0, l:l + 1, :]).astype(bf16), w_ref[0, l], preferred_element_type=f32)
        hi += jnp.dot((rows + pe_ref[0, CMP_STRIDE + l:CMP_STRIDE + l + 1, :]).astype(bf16),
                      w_ref[0, CMP_STRIDE + l], preferred_element_type=f32)
    hi_ref[pl.ds(0, nc), :] = hi
    hi_ref[pl.ds(nc, 8), :] = jnp.zeros((8, NSA_DK), f32)
    c = lo + hi_ref[pl.ds(1, nc), :]
    normed = _lane_rms(c, kg_ref[...])
    o_ref[0, 0] = jnp.where(j < NSA_KV_GROUPS, normed, c).astype(bf16)


def _nsa_compress(proj, offs, w_stack, pe_stack, k_gain, bsz, seq):
    nc = seq // CMP_STRIDE
    c0 = offs["b_kc"] // NSA_DK
    return pl.pallas_call(
        _nsa_compress_kernel,
        grid=(bsz, 2 * NSA_KV_GROUPS),
        in_specs=[
            pl.BlockSpec((seq, NSA_DK), lambda b, j: (b, c0 + j)),
            pl.BlockSpec((1, CMP_BLOCK, NSA_DK, NSA_DK), lambda b, j: (j // NSA_KV_GROUPS, 0, 0, 0)),
            pl.BlockSpec((1, CMP_BLOCK, NSA_DK), lambda b, j: (j // NSA_KV_GROUPS, 0, 0)),
            pl.BlockSpec((1, NSA_DK), lambda b, j: (0, 0)),
        ],
        out_specs=pl.BlockSpec((1, 1, nc, NSA_DK), lambda b, j: (b, j, 0, 0)),
        out_shape=jax.ShapeDtypeStruct((bsz, 2 * NSA_KV_GROUPS, nc, NSA_DK), bf16),
        scratch_shapes=[pltpu.VMEM((nc + 8, NSA_DK), f32)],
        compiler_params=pltpu.CompilerParams(dimension_semantics=("parallel", "parallel"),
                                             vmem_limit_bytes=V7X_VMEM_LIMIT_BYTES),
        name="nsa_compress",
    )(proj, w_stack, pe_stack, k_gain)


def _split3(x):
    hi = x.astype(bf16)
    r1 = x - hi.astype(f32)
    mid = r1.astype(bf16)
    lo = (r1 - mid.astype(f32)).astype(bf16)
    return hi, mid, lo


def _nsa_attn_kernel(tbl_ref, q_ref, cmp_ref, ka_ref, vs_ref, kwv_ref, mt_ref, bs_ref, bc_ref, gl0_ref, gl1_ref,
                     o_ref, strip_ref, sc_ref, score_ref, stage_ref, far_ref):
    qi = pl.program_id(1)
    nc = cmp_ref.shape[2]
    qb = Q_BLOCK
    groups = range(NSA_KV_GROUPS)
    gl_refs = (gl0_ref, gl1_ref)

    @pl.when(qi == 0)
    def _():
        def lookup(bucket):
            vals = [jnp.zeros(bucket.shape, f32) for _ in range(NSA_HEADS)]
            for k in range(REL_BUCKETS):
                eq = bucket == k
                vals = [jnp.where(eq, tbl_ref[h, k], v) for h, v in enumerate(vals)]
            return vals

        for c in range(NEAR_TILES):
            sl = slice(c * qb, (c + 1) * qb)
            for h, v in enumerate(lookup(bs_ref[:, sl])):
                strip_ref[h, :, sl] = v
        for h, v in enumerate(lookup(bc_ref[...])):
            d = v - tbl_ref[h, REL_BUCKETS - 1]
            hi = d.astype(bf16)
            sc_ref[h, 0] = hi
            sc_ref[h, 1] = (d - hi.astype(f32)).astype(bf16)

    far_bias = [tbl_ref[h, REL_BUCKETS - 1] for h in range(NSA_HEADS)]
    q = q_ref[...]
    q2 = [jnp.concatenate([q[:, (NSA_HG * g + hh) * NSA_DK:(NSA_HG * g + hh + 1) * NSA_DK]
                           for hh in range(NSA_HG)], axis=0) for g in groups]

    r2 = lax.broadcasted_iota(jnp.int32, (NSA_HG * qb, qb), 0) % qb
    c2 = lax.broadcasted_iota(jnp.int32, (NSA_HG * qb, qb), 1)

    def strip_bias(g, w):
        sl = slice(w * qb, (w + 1) * qb)
        return jnp.concatenate([strip_ref[NSA_HG * g + hh, :, sl] for hh in range(NSA_HG)], axis=0)

    def key_rows(w):
        kt = qi - (NEAR_TILES - 1) + w
        return kt, pl.ds(pl.multiple_of(jnp.maximum(kt, 0) * qb, qb), qb)

    def ka(g, rows):
        return ka_ref[rows, g * (NSA_DK + LANE):(g + 1) * (NSA_DK + LANE)]

    def vs(g, rows):
        return vs_ref[rows, g * NSA_DV:(g + 1) * NSA_DV]

    kw_cols = NSA_KV_GROUPS * NSA_DK

    o_win = []
    for g in groups:
        m_w = jnp.full((NSA_HG * qb, 1), NEG, f32)
        for w in range(NEAR_TILES - WIN_TILES, NEAR_TILES):
            kt, rows = key_rows(w)
            s = _dot_nt(q2[g], kwv_ref[rows, g * NSA_DK:(g + 1) * NSA_DK]) + strip_bias(g, w)
            if w == NEAR_TILES - WIN_TILES:
                valid = (c2 > r2) & (kt >= 0)
            elif w == NEAR_TILES - 1:
                valid = c2 <= r2
            else:
                valid = jnp.broadcast_to(kt >= 0, c2.shape)
            s = jnp.where(valid, s, NEG)
            stage_ref[g, w] = s
            m_w = jnp.maximum(m_w, jnp.max(s, axis=-1, keepdims=True))
        den = jnp.zeros((NSA_HG * qb, 1), f32)
        acc_w = jnp.zeros((NSA_HG * qb, NSA_DV), f32)
        for w in range(NEAR_TILES - WIN_TILES, NEAR_TILES):
            _, rows = key_rows(w)
            e = jnp.exp(stage_ref[g, w] - m_w)
            den = den + jnp.sum(e, axis=-1, keepdims=True)
            acc_w = acc_w + jnp.dot(e.astype(bf16), kwv_ref[rows, kw_cols + g * NSA_DV:kw_cols + (g + 1) * NSA_DV],
                                    preferred_element_type=f32)
        o_win.append(acc_w / den)

    r_i = lax.broadcasted_iota(jnp.int32, (qb, nc), 0)
    n_i = lax.broadcasted_iota(jnp.int32, (qb, nc), 1)
    ok_c = CMP_STRIDE * n_i <= qb * qi + r_i - (CMP_BLOCK - 1)
    shift = jnp.where(r_i + n_i == CMP_PER_Q * qi + CMP_PER_Q - 1, 1.0, 0.0).astype(bf16)
    mt = mt_ref[...]
    o_cmp, keys = [], []
    nblk = mt.shape[0]
    j_i = lax.broadcasted_iota(jnp.int32, (nblk, qb), 0)
    l_i = lax.broadcasted_iota(jnp.int32, (nblk, qb), 1)
    cur = SEL_PER_Q * qi + l_i // SEL_BLOCK
    ok_s = j_i <= cur
    forced = ok_s & ((j_i == 0) | (j_i >= cur - 1))
    for g in groups:
        kc = cmp_ref[0, g]
        vc = cmp_ref[0, NSA_KV_GROUPS + g]
        psum = jnp.zeros((qb, nc), f32)
        for hh in range(NSA_HG):
            h = NSA_HG * g + hh
            s = _dot_nt(q2[g][hh * qb:(hh + 1) * qb], kc)
            bias = (far_bias[h] + jnp.dot(sc_ref[h, 0], shift, preferred_element_type=f32)
                    + jnp.dot(sc_ref[h, 1], shift, preferred_element_type=f32))
            s = jnp.where(ok_c, s + bias, NEG)
            e = jnp.exp(s - jnp.max(s, axis=-1, keepdims=True))
            p = jnp.where(ok_c, e / jnp.sum(e, axis=-1, keepdims=True), 0.0)
            o_cmp.append(jnp.dot(p.astype(bf16), vc, preferred_element_type=f32))
            psum += p

        imp_t = sum(_dot_nt(mt, part) for part in _split3(psum))
        score = jnp.where(forced, FORCE_SCORE, jnp.where(ok_s, imp_t, -FORCE_SCORE))
        key = pltpu.bitcast(score, jnp.int32)
        score_ref[g] = key
        keys.append(key)

    half = nblk // 2

    def rank_half(lo):
        key_h = [k[lo:lo + half] for k in keys]
        key_h_m1 = [k - 1 for k in key_h]
        j_h = j_i[lo:lo + half]

        def rank_body(i, cnts):
            cnts = list(cnts)
            for u in range(SEL_PER_Q):
                jp = SEL_PER_Q * i + u
                later = j_h > jp
                for g in groups:
                    row = score_ref[g, pl.ds(jp, 1), :]
                    cnts[g] = cnts[g] + jnp.where(row > jnp.where(later, key_h_m1[g], key_h[g]), 1, 0)
            return tuple(cnts)

        return lax.fori_loop(0, qi + 1, rank_body,
                             tuple(jnp.zeros((half, qb), jnp.int32) for _ in groups))

    cnt_hi = lax.cond(SEL_PER_Q * qi + SEL_PER_Q > half, lambda: rank_half(half),
                      lambda: tuple(jnp.full((half, qb), N_SELECT, jnp.int32) for _ in groups))
    cnt_lo = rank_half(0)
    q_aug = []
    for g in groups:
        cnt = jnp.concatenate([cnt_lo[g], cnt_hi[g]], axis=0)
        pen = jnp.where(cnt < N_SELECT, 0.0, MASK_PEN).T.astype(bf16)
        q_aug.append(jnp.concatenate([q2[g], jnp.concatenate([pen] * NSA_HG, axis=0)], axis=1))

    row2 = lax.broadcasted_iota(jnp.int32, (NSA_HG * qb, 1), 0)
    far_col = [jnp.where(row2 < qb, far_bias[NSA_HG * g], far_bias[NSA_HG * g + 1]) for g in groups]
    n_far_keys = qb * jnp.maximum(qi - (NEAR_TILES - 1), 0)
    far_step = SEL_FAR_SUB * SEL_FAR_TILE
    c_far = lax.broadcasted_iota(jnp.int32, (NSA_HG * qb, SEL_FAR_TILE), 1)
    n_far_steps = (n_far_keys + far_step - 1) // far_step

    def far_rows(kt, sub):
        k0 = kt * far_step + sub * SEL_FAR_TILE
        return k0, pl.ds(pl.multiple_of(k0, SEL_FAR_TILE), SEL_FAR_TILE)

    def far_scores(g, kt):
        m_raw = jnp.full((NSA_HG * qb, 1), MASK_PEN, f32)
        for sub in range(SEL_FAR_SUB):
            k0, rows = far_rows(kt, sub)
            s = _dot_nt(q_aug[g], ka(g, rows)) + far_col[g]
            s = jnp.where(c_far < n_far_keys - k0, s, MASK_PEN)
            far_ref[g, kt % 2, :, sub * SEL_FAR_TILE:(sub + 1) * SEL_FAR_TILE] = s
            m_raw = jnp.maximum(m_raw, jnp.max(s, axis=-1, keepdims=True))
        return m_raw

    m_raw0 = [far_scores(g, 0) for g in groups]

    m_near = []
    for g in groups:
        m_n = jnp.full((NSA_HG * qb, 1), NEG, f32)
        for w in range(NEAR_TILES):
            kt, rows = key_rows(w)
            s = _dot_nt(q_aug[g], ka(g, rows)) + strip_bias(g, w)
            if w == NEAR_TILES - 1:
                s = jnp.where(c2 <= r2, s, NEG)
            else:
                s = jnp.where(kt >= 0, s, NEG)
            stage_ref[g, w] = s
            m_n = jnp.maximum(m_n, jnp.max(s, axis=-1, keepdims=True))
        m_near.append(m_n)

    def far_loop(g):
        def far_body(kt, carry):
            m_old, l_old, acc_old, m_raw = carry
            ss = [far_ref[g, kt % 2, :, sub * SEL_FAR_TILE:(sub + 1) * SEL_FAR_TILE] for sub in range(SEL_FAR_SUB)]
            m_new = jnp.maximum(m_old, m_raw)
            ps = [jnp.exp(s - m_new) for s in ss]
            alpha = jnp.exp(m_old - m_new)
            l_new = alpha * l_old + sum(jnp.sum(p, axis=-1, keepdims=True) for p in ps)
            acc_new = alpha * acc_old + sum(
                jnp.dot(p.astype(bf16), vs(g, far_rows(kt, sub)[1]), preferred_element_type=f32)
                for sub, p in enumerate(ps))
            m_raw_next = far_scores(g, jnp.minimum(kt + 1, n_far_steps - 1))
            return m_new, l_new, acc_new, m_raw_next

        return lax.fori_loop(
            0, n_far_steps, far_body,
            (jnp.full((NSA_HG * qb, 1), MASK_PEN, f32), jnp.zeros((NSA_HG * qb, 1), f32),
             jnp.zeros((NSA_HG * qb, NSA_DV), f32), m_raw0[g]))

    far = [far_loop(g) for g in groups]

    for g in groups:
        m_far, l_far, acc_far, _ = far[g]
        m_sel = jnp.maximum(m_far, m_near[g])
        alpha = jnp.exp(m_far - m_sel)
        l_sel = alpha * l_far
        acc_sel = alpha * acc_far
        for w in range(NEAR_TILES):
            _, rows = key_rows(w)
            e = jnp.exp(stage_ref[g, w] - m_sel)
            l_sel = l_sel + jnp.sum(e, axis=-1, keepdims=True)
            acc_sel = acc_sel + jnp.dot(e.astype(bf16), vs(g, rows), preferred_element_type=f32)
        o_sel = acc_sel / l_sel

        gate = jax.nn.sigmoid(gl_refs[g][...])
        for hh in range(NSA_HG):
            h = NSA_HG * g + hh
            rs = slice(hh * qb, (hh + 1) * qb)
            o_ref[:, h * NSA_DV:(h + 1) * NSA_DV] = (gate[:, 3 * hh:3 * hh + 1] * o_cmp[h]
                                                     + gate[:, 3 * hh + 1:3 * hh + 2] * o_sel[rs]
                                                     + gate[:, 3 * hh + 2:3 * hh + 3] * o_win[g][rs]
                                                     ).astype(o_ref.dtype)


def _rel_bucket(dist):
    n = jnp.maximum(dist, 0)
    nf = jnp.maximum(n, 1).astype(f32)
    large = REL_MAX_EXACT + (jnp.log(nf / REL_MAX_EXACT) / math.log(REL_MAX_DIST / REL_MAX_EXACT)
                             * (REL_BUCKETS - REL_MAX_EXACT)).astype(jnp.int32)
    large = jnp.minimum(large, REL_BUCKETS - 1)
    return jnp.where(n < REL_MAX_EXACT, n, large)


def _nsa_constants(seq):
    nc = seq // CMP_STRIDE
    r = jnp.arange(Q_BLOCK)[:, None]
    bs = _rel_bucket(STRIP_D0 + r - jnp.arange(STRIP_W)[None, :])
    bc = _rel_bucket(r + CMP_STRIDE * jnp.arange(LANE)[None, :] - (CMP_STRIDE * (CMP_PER_Q - 1) + CMP_BLOCK - 1))
    ratio = SEL_BLOCK // CMP_STRIDE
    mt = np.zeros((LANE, nc), np.float32)
    for j in range(LANE):
        for i, wt in enumerate(SEL_OVERLAP_WEIGHTS):
            mcol = ratio * j - 1 + i
            if 0 <= mcol < nc:
                mt[j, mcol] = wt
    return bs.astype(jnp.int32), bc.astype(jnp.int32), jnp.asarray(mt, bf16)


def _nsa_attention(qn, cmp_kv, kaug, vsel, kwvw, proj, offs, tbl, consts, bsz, seq):
    bs, bc, mt = consts
    assert seq % (SEL_FAR_SUB * SEL_FAR_TILE) == 0, "far steps must tile the sequence"
    m = bsz * seq
    n_qb = seq // Q_BLOCK
    nc = seq // CMP_STRIDE
    g_ = NSA_KV_GROUPS
    assert g_ == 2
    resident = lambda width: pl.BlockSpec((seq, width), lambda b, i: (b, 0))
    const2 = lambda shape: pl.BlockSpec(shape, lambda b, i: (0, 0))
    gcol = offs["b_g"] // LANE
    gate = lambda g: pl.BlockSpec((Q_BLOCK, LANE), lambda b, i: (b * n_qb + i, gcol + g))
    rows2 = NSA_HG * Q_BLOCK
    return pl.pallas_call(
        _nsa_attn_kernel,
        grid=(bsz, n_qb),
        in_specs=[
            pl.BlockSpec(memory_space=pltpu.SMEM),
            pl.BlockSpec((Q_BLOCK, NSA_HEADS * NSA_DK), lambda b, i: (b * n_qb + i, 0)),
            pl.BlockSpec((1, 2 * g_, nc, NSA_DK), lambda b, i: (b, 0, 0, 0)),
            resident(g_ * (NSA_DK + LANE)), resident(g_ * NSA_DV), resident(g_ * (NSA_DK + NSA_DV)),
            const2((LANE, nc)), const2((Q_BLOCK, STRIP_W)), const2((Q_BLOCK, LANE)),
            gate(0), gate(1),
        ],
        out_specs=pl.BlockSpec((Q_BLOCK, NSA_HEADS * NSA_DV), lambda b, i: (b * n_qb + i, 0)),
        out_shape=jax.ShapeDtypeStruct((m, NSA_HEADS * NSA_DV), bf16),
        scratch_shapes=[
            pltpu.VMEM((NSA_HEADS, Q_BLOCK, STRIP_W), f32),
            pltpu.VMEM((NSA_HEADS, 2, Q_BLOCK, LANE), bf16),
            pltpu.VMEM((g_, LANE, Q_BLOCK), jnp.int32),
            pltpu.VMEM((g_, NEAR_TILES, rows2, Q_BLOCK), f32),
            pltpu.VMEM((g_, 2, rows2, SEL_FAR_SUB * SEL_FAR_TILE), f32),
        ],
        compiler_params=pltpu.CompilerParams(dimension_semantics=("arbitrary", "arbitrary"),
                                             vmem_limit_bytes=V7X_VMEM_LIMIT_BYTES),
        name="nsa_attention",
    )(tbl, qn, cmp_kv, kaug, vsel, kwvw, mt, bs, bc, proj, proj)


def _nsa(proj, offs, q_norm, k_norm, pe_k, pe_v, w_ck, w_cv, rel_table, consts, bsz, seq):
    q_gain = q_norm.reshape(1, NSA_DK)
    k_gain = k_norm.reshape(1, NSA_DK)
    qn, kaug, vsel, kwvw = _nsa_prep(proj, offs, q_gain, k_gain, seq)
    cmp_kv = _nsa_compress(proj, offs, jnp.stack([w_ck, w_cv]).astype(bf16), jnp.stack([pe_k, pe_v]), k_gain,
                           bsz, seq)
    return _nsa_attention(qn, cmp_kv, kaug, vsel, kwvw, proj, offs, rel_table.T, consts, bsz, seq)


SMALL_A_LANE = 0
SMALL_B_LANE = GDN_HEADS
SMALL_DT_LANE = 2 * GDN_HEADS
CONV_HALO = 8


def _softplus(x):
    return jnp.maximum(x, 0.0) + jnp.log1p(jnp.exp(-jnp.abs(x)))


def _exact_dot(a_f32, b_01):
    return sum(jnp.dot(part, b_01, preferred_element_type=f32) for part in _split3(a_f32))


def _exact_dot_left(b_01, a_f32):
    return sum(jnp.dot(b_01, part, preferred_element_type=f32) for part in _split3(a_f32))


def _lane_vec(values, lane0):
    return jnp.zeros((1, LANE), f32).at[0, lane0:lane0 + values.shape[0]].set(values.astype(f32))


def _causal_conv_silu(x_ref, tail_ref, buf_ref, w_ref, bias):
    rows = x_ref.shape[0]
    buf_ref[pl.ds(0, CONV_HALO), :] = tail_ref[...]
    buf_ref[pl.ds(CONV_HALO, rows), :] = x_ref[...]
    tail_ref[...] = x_ref[pl.ds(rows - CONV_HALO, CONV_HALO), :]
    acc = bias
    for k in range(CONV_K):
        acc = acc + w_ref[k:k + 1, :] * buf_ref[pl.ds(CONV_HALO - (CONV_K - 1) + k, rows), :]
    return acc * jax.nn.sigmoid(acc)


SSM_PAIR = LANE // SSM_HEAD_DIM
SSM_HEADS_PER_GROUP = SSM_HEADS // SSM_GROUPS


def _ssd_kernel(xbc_ref, z_ref, sm_ref, cw_ref, cb_ref, dtb_ref, alog_ref, dvec_ref, ng_ref, tri_ref, exp_ref,
                o_ref, tail_ref, buf_ref, state_ref):
    L = SSM_CHUNK
    N = SSM_STATE

    xbc = _causal_conv_silu(xbc_ref, tail_ref, buf_ref, cw_ref, cb_ref[...])
    xs = xbc[:, :SSM_INNER]
    dt = _softplus(sm_ref[...] + dtb_ref[...])
    a = dt * (-jnp.exp(alog_ref[...]))
    tri = tri_ref[...]
    ac = _exact_dot_left(tri, a)
    ac_t = ac.T
    eac = jnp.exp(ac)
    w_t = jnp.exp(ac_t[:, L - 1:L] - ac_t)
    chunk_decay = jnp.exp(ac[L - 1:L, :])
    xdt = xs * _exact_dot(dt, exp_ref[...])

    row = lax.broadcasted_iota(jnp.int32, (L, L), 0)
    col = lax.broadcasted_iota(jnp.int32, (L, L), 1)
    causal = row >= col
    lane = lax.broadcasted_iota(jnp.int32, (L, LANE), 1)
    first_half = lane < SSM_HEAD_DIM

    ys = []
    for g in range(SSM_GROUPS):
        bg = xbc[:, SSM_INNER + g * N:SSM_INNER + (g + 1) * N]
        cg = xbc[:, SSM_INNER + SSM_GROUPS * N + g * N:SSM_INNER + SSM_GROUPS * N + (g + 1) * N]
        cb = _dot_nt(cg.astype(bf16), bg.astype(bf16))
        bg_t = bg.T
        for i in range(g * SSM_HEADS_PER_GROUP // SSM_PAIR, (g + 1) * SSM_HEADS_PER_GROUP // SSM_PAIR):
            xdt_pair = xdt[:, i * LANE:(i + 1) * LANE].astype(bf16)
            outs = []
            for hh in range(SSM_PAIR):
                h = SSM_PAIR * i + hh
                c = SMALL_DT_LANE + h
                seg = ac[:, c:c + 1] - ac_t[c:c + 1, :]
                sc = (cb * jnp.exp(jnp.where(causal, seg, NEG))).astype(bf16)
                y = jnp.dot(sc, xdt_pair, preferred_element_type=f32)
                s_in = state_ref[h]
                y += jnp.dot((cg * eac[:, c:c + 1]).astype(bf16), s_in.astype(bf16), preferred_element_type=f32)
                st = jnp.dot((bg_t * w_t[c:c + 1, :]).astype(bf16), xdt_pair, preferred_element_type=f32)
                state_ref[h] = s_in * chunk_decay[:, c:c + 1] + st
                outs.append(y)
            y_pair = jnp.where(first_half, outs[0], outs[1])
            sl = slice(i * LANE, (i + 1) * LANE)
            y_pair = y_pair + xs[:, sl] * dvec_ref[:, sl]
            z = z_ref[:, sl]
            ys.append(y_pair * (z * jax.nn.sigmoid(z)))

    per_group = len(ys) // SSM_GROUPS
    gw = SSM_INNER // SSM_GROUPS
    for g in range(SSM_GROUPS):
        tiles = ys[g * per_group:(g + 1) * per_group]
        ms = sum(jnp.sum(t * t, axis=-1, keepdims=True) for t in tiles) / gw
        scale = lax.rsqrt(ms + EPS)
        for k, t in enumerate(tiles):
            sl = slice(g * gw + k * LANE, g * gw + (k + 1) * LANE)
            o_ref[:, sl] = (t * scale * ng_ref[:, sl]).astype(o_ref.dtype)


GDN_STEP_ROWS = 128
GDN_QK_W = GDN_HEADS * GDN_DK
GDN_CONV_CH = GDN_HEADS * (2 * GDN_DK + GDN_DV)


def _split2(x):
    hi = x.astype(bf16)
    return hi, (x - hi.astype(f32)).astype(bf16)


def _dot_split(a_parts, b_parts):
    (ah, al), (bh, bl) = a_parts, b_parts
    return (jnp.dot(ah, bh, preferred_element_type=f32) + jnp.dot(ah, bl, preferred_element_type=f32)
            + jnp.dot(al, bh, preferred_element_type=f32))


def _unit_lower_solve_many(a_list, rhs_list):
    n = a_list[0].shape[0]
    ps = [_split2(-a) for a in a_list]
    xs = [rhs + _dot_split(p, _split2(rhs)) for p, rhs in zip(ps, rhs_list)]
    k = 2
    while k < n:
        ps = [_split2(_dot_split(p, p)) for p in ps]
        xs = [x + _dot_split(p, _split2(x)) for p, x in zip(ps, xs)]
        k *= 2
    return xs


def _gdn_kernel(qkv_ref, z_ref, sm_ref, cw_ref, dtb_ref, alog_ref, ng_ref, tri_ref, o_ref, tail_ref, buf_ref,
                state_ref):
    C = GDN_CHUNK

    qkv = _causal_conv_silu(qkv_ref, tail_ref, buf_ref, cw_ref, 0.0)
    sm = sm_ref[...]
    beta = jax.nn.sigmoid(sm)
    g_log = -jnp.exp(alog_ref[...]) * _softplus(sm + dtb_ref[...])
    gc = _exact_dot_left(tri_ref[...], g_log)
    gc_t = gc.T
    egc = jnp.exp(gc)

    row = lax.broadcasted_iota(jnp.int32, (C, C), 0)
    col = lax.broadcasted_iota(jnp.int32, (C, C), 1)
    causal = row >= col
    strict = row > col

    units = [(ch, h) for ch in range(GDN_STEP_ROWS // C) for h in range(GDN_HEADS)]
    a_list, rhs_list, pre = [], [], []
    for ch, h in units:
        rs = slice(ch * C, (ch + 1) * C)
        last = slice((ch + 1) * C - 1, (ch + 1) * C)
        q = qkv[rs, h * GDN_DK:(h + 1) * GDN_DK]
        k = qkv[rs, GDN_QK_W + h * GDN_DK:GDN_QK_W + (h + 1) * GDN_DK]
        v = qkv[rs, 2 * GDN_QK_W + h * GDN_DV:2 * GDN_QK_W + (h + 1) * GDN_DV]
        q = q * lax.rsqrt(jnp.sum(q * q, axis=-1, keepdims=True) + EPS) * (GDN_DK ** -0.5)
        k = k * lax.rsqrt(jnp.sum(k * k, axis=-1, keepdims=True) + EPS)
        ca = SMALL_A_LANE + h
        b = beta[rs, SMALL_B_LANE + h:SMALL_B_LANE + h + 1]
        g_col = gc[rs, ca:ca + 1]
        g_last = gc[last, ca:ca + 1]
        e_col = egc[rs, ca:ca + 1]
        decay = jnp.exp(jnp.where(causal, g_col - gc_t[ca:ca + 1, rs], NEG))
        kb = k * b
        k16 = k.astype(bf16)
        a_list.append(jnp.where(strict, _dot_nt(kb.astype(bf16), k16) * decay, 0.0))
        rhs_list.append(jnp.concatenate([v * b, kb * e_col], axis=1))
        attn = (_dot_nt(q.astype(bf16), k16) * decay).astype(bf16)
        q_dec = (q * e_col).astype(bf16)
        k_dec_t = (k * jnp.exp(g_last - g_col)).T.astype(bf16)
        pre.append((attn, q_dec, k_dec_t, jnp.exp(g_last)))

    sols = _unit_lower_solve_many(a_list, rhs_list)

    for (ch, h), sol, (attn, q_dec, k_dec_t, g_end) in zip(units, sols, pre):
        rs = slice(ch * C, (ch + 1) * C)
        u, w = sol[:, :GDN_DV], sol[:, GDN_DV:]
        s_in = state_ref[h]
        s16 = s_in.astype(bf16)
        v_new = u - jnp.dot(w.astype(bf16), s16, preferred_element_type=f32)
        v16 = v_new.astype(bf16)
        o = (jnp.dot(q_dec, s16, preferred_element_type=f32) + jnp.dot(attn, v16, preferred_element_type=f32))
        state_ref[h] = s_in * g_end + jnp.dot(k_dec_t, v16, preferred_element_type=f32)

        o = o * lax.rsqrt(jnp.mean(o * o, axis=-1, keepdims=True) + EPS) * ng_ref[...]
        z = z_ref[rs, h * GDN_DV:(h + 1) * GDN_DV]
        o_ref[rs, h * GDN_DV:(h + 1) * GDN_DV] = (o * (z * jax.nn.sigmoid(z))).astype(o_ref.dtype)


REC_ROWS = GDN_STEP_ROWS
assert REC_ROWS == SSM_CHUNK


def _rec_mixers_kernel(qkv_ref, za_ref, sm_ref, gcw_ref, gdtb_ref, galog_ref, gng_ref, gtri_ref,
                       xbc_ref, zc_ref, scw_ref, scb_ref, sdtb_ref, salog_ref, dvec_ref, sng_ref, stri_ref, exp_ref,
                       oa_ref, oc_ref, gtail_ref, gbuf_ref, gstate_ref, stail_ref, sbuf_ref, sstate_ref):
    @pl.when(pl.program_id(1) == 0)
    def _():
        for ref in (gtail_ref, gstate_ref, stail_ref, sstate_ref):
            ref[...] = jnp.zeros(ref.shape, f32)

    _gdn_kernel(qkv_ref, za_ref, sm_ref, gcw_ref, gdtb_ref, galog_ref, gng_ref, gtri_ref, oa_ref,
                gtail_ref, gbuf_ref, gstate_ref)
    _ssd_kernel(xbc_ref, zc_ref, sm_ref, scw_ref, scb_ref, sdtb_ref, salog_ref, dvec_ref, sng_ref, stri_ref,
                exp_ref, oc_ref, stail_ref, sbuf_ref, sstate_ref)


def _rec_mixers(proj, offs, gdn_conv, gdn_a_log, gdn_dt_bias, gdn_norm, ssm_conv_w, ssm_conv_b, ssm_dt_bias,
                ssm_a_log, ssm_d, ssm_norm, bsz, seq):
    m = bsz * seq
    R_ = REC_ROWS
    n_s = seq // R_
    idx = np.arange(R_)
    gtri = ((idx[:, None] >= idx[None, :]) & (idx[:, None] // GDN_CHUNK == idx[None, :] // GDN_CHUNK))
    stri = np.tril(np.ones((R_, R_), np.float32))
    expand = np.zeros((LANE, SSM_INNER), np.float32)
    for h in range(SSM_HEADS):
        expand[SMALL_DT_LANE + h, h * SSM_HEAD_DIM:(h + 1) * SSM_HEAD_DIM] = 1.0
    gz = GDN_HEADS * GDN_DV
    tok = lambda w, col: pl.BlockSpec((R_, w), lambda b, c: (b * n_s + c, col // w))
    const = lambda shape: pl.BlockSpec(shape, lambda b, c: (0, 0))
    return pl.pallas_call(
        _rec_mixers_kernel,
        grid=(bsz, n_s),
        in_specs=[
            tok(GDN_CONV_CH, offs["a_q"]), tok(gz, offs["a_z"]), tok(LANE, offs["small"]),
            const((CONV_K, GDN_CONV_CH)), const((1, LANE)), const((1, LANE)), const((1, GDN_DV)), const((R_, R_)),
            tok(SSM_CONV_CH, offs["c_xbc"]), tok(SSM_INNER, offs["c_z"]),
            const((CONV_K, SSM_CONV_CH)), const((1, SSM_CONV_CH)), const((1, LANE)), const((1, LANE)),
            const((1, SSM_INNER)), const((1, SSM_INNER)), const((R_, R_)), const((LANE, SSM_INNER)),
        ],
        out_specs=[pl.BlockSpec((R_, gz), lambda b, c: (b * n_s + c, 0)),
                   pl.BlockSpec((R_, SSM_INNER), lambda b, c: (b * n_s + c, 0))],
        out_shape=[jax.ShapeDtypeStruct((m, gz), bf16), jax.ShapeDtypeStruct((m, SSM_INNER), bf16)],
        scratch_shapes=[
            pltpu.VMEM((CONV_HALO, GDN_CONV_CH), f32),
            pltpu.VMEM((CONV_HALO + R_, GDN_CONV_CH), f32),
            pltpu.VMEM((GDN_HEADS, GDN_DK, GDN_DV), f32),
            pltpu.VMEM((CONV_HALO, SSM_CONV_CH), f32),
            pltpu.VMEM((CONV_HALO + R_, SSM_CONV_CH), f32),
            pltpu.VMEM((SSM_HEADS, SSM_STATE, LANE), f32),
        ],
        compiler_params=pltpu.CompilerParams(dimension_semantics=("arbitrary", "arbitrary"),
                                             vmem_limit_bytes=V7X_VMEM_LIMIT_BYTES),
        name="recurrent_mixers",
    )(proj, proj, proj, gdn_conv, _lane_vec(gdn_dt_bias, SMALL_A_LANE), _lane_vec(gdn_a_log, SMALL_A_LANE),
      gdn_norm.reshape(1, -1), jnp.asarray(gtri.astype(np.float32), bf16),
      proj, proj, ssm_conv_w, ssm_conv_b.reshape(1, -1), _lane_vec(ssm_dt_bias, SMALL_DT_LANE),
      _lane_vec(ssm_a_log, SMALL_DT_LANE), jnp.repeat(ssm_d, SSM_HEAD_DIM).reshape(1, -1),
      ssm_norm.reshape(1, -1), jnp.asarray(stri, bf16), jnp.asarray(expand, bf16))


_IN_NAMES = ("a_q", "a_k", "a_v", "a_z", "a_a", "a_b", "b_q", "b_kc", "b_vc", "b_ks", "b_vs", "b_kw", "b_vw",
             "b_g", "c_z", "c_xbc", "c_dt", "m_gate")


def _in_sizes(d_model):
    return (GDN_HEADS * GDN_DK, GDN_HEADS * GDN_DK, GDN_HEADS * GDN_DV, GDN_HEADS * GDN_DV, GDN_HEADS, GDN_HEADS,
            NSA_HEADS * NSA_DK, NSA_KV_GROUPS * NSA_DK, NSA_KV_GROUPS * NSA_DV, NSA_KV_GROUPS * NSA_DK,
            NSA_KV_GROUPS * NSA_DV, NSA_KV_GROUPS * NSA_DK, NSA_KV_GROUPS * NSA_DV, 3 * NSA_HEADS,
            SSM_INNER, SSM_CONV_CH, SSM_HEADS, 3 * d_model)


_BIG_ORDER = ("m_gate", "a_q", "a_k", "a_v", "c_xbc", "c_z", "a_z", "b_q", "b_kc", "b_vc", "b_ks", "b_vs",
              "b_kw", "b_vw")
_SMALL_ORDER = ("a_a", "a_b", "c_dt")


def _in_layout(d_model, tn=1024):
    sizes = dict(zip(_IN_NAMES, _in_sizes(d_model)))
    src_off = dict(zip(_IN_NAMES, np.cumsum((0,) + _in_sizes(d_model))[:-1].tolist()))
    cols, offs = [], {}

    def pad_to_lane():
        cols.extend([-1] * (_round_up(len(cols), LANE) - len(cols)))

    for name in _BIG_ORDER:
        offs[name] = len(cols)
        cols.extend(range(src_off[name], src_off[name] + sizes[name]))
    offs["small"] = len(cols)
    for name in _SMALL_ORDER:
        offs[name] = len(cols)
        cols.extend(range(src_off[name], src_off[name] + sizes[name]))
    pad_to_lane()
    offs["b_g"] = len(cols)
    per_group = 3 * NSA_HG
    for g in range(NSA_KV_GROUPS):
        cols.extend(range(src_off["b_g"] + g * per_group, src_off["b_g"] + (g + 1) * per_group))
        pad_to_lane()
    n_pad = _round_up(len(cols), tn)
    cols.extend([-1] * (n_pad - len(cols)))
    return np.asarray(cols, np.int32), offs, sizes, n_pad


def _col_runs(cols):
    runs, start = [], 0
    for i in range(1, len(cols) + 1):
        same_run = i < len(cols) and ((cols[i] < 0 and cols[i - 1] < 0)
                                      or (cols[i - 1] >= 0 and cols[i] == cols[i - 1] + 1))
        if not same_run:
            runs.append((start, None if cols[start] < 0 else int(cols[start]), i - start))
            start = i
    return runs


def _cast_cols_kernel(runs_per_out, w_ref, *o_refs):
    for runs, o_ref in zip(runs_per_out, o_refs):
        for dst, src, size in runs:
            if src is None:
                o_ref[:, dst:dst + size] = jnp.zeros((o_ref.shape[0], size), o_ref.dtype)
            else:
                o_ref[:, dst:dst + size] = w_ref[0, :, src:src + size].astype(o_ref.dtype)


def _cast_cols(w_stack, layer, runs_per_out, widths, *, tr=256):
    _, rows, n_src = w_stack.shape
    return pl.pallas_call(
        functools.partial(_cast_cols_kernel, runs_per_out),
        grid=(rows // tr,),
        in_specs=[pl.BlockSpec((1, tr, n_src), lambda i: (layer, i, 0))],
        out_specs=[pl.BlockSpec((tr, w), lambda i: (i, 0)) for w in widths],
        out_shape=[jax.ShapeDtypeStruct((rows, w), bf16) for w in widths],
        compiler_params=pltpu.CompilerParams(dimension_semantics=("parallel",),
                                             vmem_limit_bytes=V7X_VMEM_LIMIT_BYTES),
        name="weight_cast_cols",
    )(w_stack)


def _cast_rows_kernel(rows_src, w_ref, o_ref):
    tr = o_ref.shape[0]
    row = lax.broadcasted_iota(jnp.int32, o_ref.shape, 0) + pl.program_id(0) * tr
    o_ref[...] = jnp.where(row < rows_src, w_ref[0], 0.0).astype(o_ref.dtype)


def _cast_rows(w_stack, layer, rows_out, *, tr=512):
    _, rows_src, n = w_stack.shape
    return pl.pallas_call(
        functools.partial(_cast_rows_kernel, rows_src),
        grid=(rows_out // tr,),
        in_specs=[pl.BlockSpec((1, tr, n), lambda i: (layer, i, 0))],
        out_specs=pl.BlockSpec((tr, n), lambda i: (i, 0)),
        out_shape=jax.ShapeDtypeStruct((rows_out, n), bf16),
        compiler_params=pltpu.CompilerParams(dimension_semantics=("parallel",),
                                             vmem_limit_bytes=V7X_VMEM_LIMIT_BYTES),
        name="weight_cast_rows",
    )(w_stack)


def _prep_w_in(w_in, layer, cols):
    return _cast_cols(w_in, layer, [_col_runs(cols)], [len(cols)])[0]


def _prep_ffn(w_up, w_down, layer, tf=512):
    ff = w_down.shape[1]
    ffp = _round_up(ff, tf)
    half = lambda src: [(0, src, ff)] + ([(ff, None, ffp - ff)] if ffp > ff else [])
    wa, wb = _cast_cols(w_up, layer, [half(0), half(ff)], [ffp, ffp])
    return wa, wb, _cast_rows(w_down, layer, ffp)


def kernel(x, rel_table, g_ffn1, w_up1, w_down1, g_mix, w_in, gdn_conv, gdn_a_log, gdn_dt_bias, gdn_norm,
           nsa_q_norm, nsa_k_norm, nsa_pe_k, nsa_pe_v, nsa_w_ck, nsa_w_cv, ssm_conv_w, ssm_conv_b,
           ssm_dt_bias, ssm_a_log, ssm_d, ssm_norm, p_a, p_b, p_c, w_o, g_ffn2, w_up2, w_down2):
    bsz, seq, d = x.shape
    depth = w_in.shape[0]
    m = bsz * seq
    cols, offs, _, _ = _in_layout(d)
    x2 = x.reshape(m, d)
    nsa_consts = _nsa_constants(seq)

    for l in range(depth):
        wa, wb, wd = _prep_ffn(w_up1, w_down1, l)
        x2 = _ffn(x2, g_ffn1[l].reshape(1, d), wa, wb, wd)

        proj = _norm_matmul(x2, g_mix[l].reshape(1, d), _prep_w_in(w_in, l, cols))
        y_a, y_c = _rec_mixers(proj, offs, gdn_conv[l], gdn_a_log[l], gdn_dt_bias[l], gdn_norm[l], ssm_conv_w[l],
                               ssm_conv_b[l], ssm_dt_bias[l], ssm_a_log[l], ssm_d[l], ssm_norm[l], bsz, seq)
        y_b = _nsa(proj, offs, nsa_q_norm[l], nsa_k_norm[l], nsa_pe_k[l], nsa_pe_v[l], nsa_w_ck[l], nsa_w_cv[l],
                   rel_table, nsa_consts, bsz, seq)
        x2 = _merge(x2, y_a, y_b, y_c, proj, offs["m_gate"],
                    p_a[l].astype(bf16), p_b[l].astype(bf16), p_c[l].astype(bf16), w_o[l].astype(bf16))

        wa, wb, wd = _prep_ffn(w_up2, w_down2, l)
        x2 = _ffn(x2, g_ffn2[l].reshape(1, d), wa, wb, wd)
    return x2.reshape(bsz, seq, d)
```
